```python
import math
import jax, jax.numpy as jnp
from jax import lax
import numpy as np

D_MODEL = 1024
BATCH = 2
SEQ = 8192
DEPTH = 1

CTX_LEN = 256
GRID_W = 64
D_MIX = D_MODEL
ATTN_WIDTH = D_MIX // 2
SSM_WIDTH = D_MIX - ATTN_WIDTH
HEAD_DIM = 64
N_Q_HEADS = ATTN_WIDTH // HEAD_DIM
N_KV_HEADS = N_Q_HEADS // 4
KV_WIDTH = N_KV_HEADS * HEAD_DIM
Q_BLOCK = 128
ATTN_SCALE = HEAD_DIM ** -0.5
ROPE_THETA = 10000.0
ROPE_HALF = HEAD_DIM // 2
ROPE_FREQS = ROPE_HALF // 2
SSM_GROUP = 16
N_SSM_GROUPS = SSM_WIDTH // SSM_GROUP
SSM_STATE = 64
N_DIRS = 2
NORM_EPS = 1e-6
Q_END = ATTN_WIDTH
K_END = Q_END + KV_WIDTH
V_END = K_END + KV_WIDTH
GA_END = V_END + ATTN_WIDTH
U_END = GA_END + SSM_WIDTH
IN_WIDTH = U_END + SSM_WIDTH

kernel_name = 'hymba_gqa_s5_diffusion_layer'


def rmsnorm(x, gain):
    x32 = x.astype(jnp.float32)
    y = x32 * lax.rsqrt(jnp.mean(x32 * x32, axis=-1, keepdims=True) + NORM_EPS)
    return (y * gain.astype(jnp.float32)).astype(x.dtype)


def adaln(cvec, w_ada, b_ada):
    m = jax.nn.silu(cvec) @ w_ada + b_ada
    return jnp.split(m, 3, axis=-1)


def modulate(h, shift, scale):
    return h * (1.0 + scale) + shift


def axial_rope_tables(row_pos, col_pos):
    inv_freq = ROPE_THETA ** (-jnp.arange(ROPE_FREQS, dtype=jnp.float32) / ROPE_FREQS)
    ang_r = row_pos[:, None] * inv_freq
    ang_c = col_pos[:, None] * inv_freq
    return (jnp.cos(ang_r)[:, None, :], jnp.sin(ang_r)[:, None, :],
            jnp.cos(ang_c)[:, None, :], jnp.sin(ang_c)[:, None, :])


def apply_axial_rope(x, rope):
    cos_r, sin_r, cos_c, sin_c = rope
    x32 = x.astype(jnp.float32)

    def rot(xh, cos, sin):
        x1, x2 = xh[..., :ROPE_FREQS], xh[..., ROPE_FREQS:]
        return jnp.concatenate([x1 * cos - x2 * sin, x1 * sin + x2 * cos], axis=-1)

    out = jnp.concatenate([rot(x32[..., :ROPE_HALF], cos_r, sin_r),
                           rot(x32[..., ROPE_HALF:], cos_c, sin_c)], axis=-1)
    return out.astype(x.dtype)


def mixer_inputs(h, w_in, q_gain, k_gain):
    b, l, _ = h.shape
    z = h @ w_in
    q = rmsnorm(z[..., :Q_END].reshape(b, l, N_Q_HEADS, HEAD_DIM), q_gain)
    k = rmsnorm(z[..., Q_END:K_END].reshape(b, l, N_KV_HEADS, HEAD_DIM), k_gain)
    v = z[..., K_END:V_END].reshape(b, l, N_KV_HEADS, HEAD_DIM)
    gate_a = z[..., V_END:GA_END]
    u = z[..., GA_END:U_END]
    gate_s = z[..., U_END:]
    return q, k, v, gate_a, u, gate_s


def _attend(qg, k, v):
    s = jnp.einsum('bqhgd,bkhd->bhgqk', qg, k).astype(jnp.float32) * ATTN_SCALE
    p = jax.nn.softmax(s, axis=-1).astype(v.dtype)
    return jnp.einsum('bhgqk,bkhd->bqhgd', p, v)


def latent_attention(q, k_lat, v_lat, k_ctx, v_ctx):
    b, l, hq, dh = q.shape
    n_blk = l // Q_BLOCK
    k_all = jnp.concatenate([k_ctx, k_lat], axis=1)
    v_all = jnp.concatenate([v_ctx, v_lat], axis=1)
    qb = q.reshape(b, n_blk, Q_BLOCK, N_KV_HEADS, hq // N_KV_HEADS, dh)
    qb = jnp.moveaxis(qb, 1, 0)
    out = lax.map(lambda q_blk: _attend(q_blk, k_all, v_all), qb)
    return jnp.moveaxis(out, 0, 1).reshape(b, l, hq * dh)


def context_attention(q, k, v):
    b, l, hq, dh = q.shape
    qg = q.reshape(b, l, N_KV_HEADS, hq // N_KV_HEADS, dh)
    return _attend(qg, k, v).reshape(b, l, hq * dh)


def s5_discretise(lam_re, lam_im, log_dt, b_re, b_im):
    lam_re = lam_re.astype(jnp.float32)
    lam_im = lam_im.astype(jnp.float32)
    dt = jnp.exp(log_dt.astype(jnp.float32))[:, None]
    mag = jnp.exp(dt * lam_re)
    ab_re, ab_im = mag * jnp.cos(dt * lam_im), mag * jnp.sin(dt * lam_im)
    nr, ni = ab_re - 1.0, ab_im
    den = lam_re * lam_re + lam_im * lam_im
    cr = (nr * lam_re + ni * lam_im) / den
    ci = (ni * lam_re - nr * lam_im) / den
    b_re = b_re.astype(jnp.float32)
    b_im = b_im.astype(jnp.float32)
    bb_re = cr[..., None] * b_re - ci[..., None] * b_im
    bb_im = cr[..., None] * b_im + ci[..., None] * b_re
    return ab_re, ab_im, bb_re, bb_im


def _complex_affine_combine(e1, e2):
    a1r, a1i, b1r, b1i = e1
    a2r, a2i, b2r, b2i = e2
    return (a2r * a1r - a2i * a1i,
            a2r * a1i + a2i * a1r,
            a2r * b1r - a2i * b1i + b2r,
            a2r * b1i + a2i * b1r + b2i)


def s5_states(u, h0_re, h0_im, ab_re, ab_im, bb_re, bb_im):
    bu_re = jnp.einsum('blgp,gnp->blgn', u, bb_re)
    bu_im = jnp.einsum('blgp,gnp->blgn', u, bb_im)
    bu_re = bu_re.at[:, 0].add(ab_re * h0_re - ab_im * h0_im)
    bu_im = bu_im.at[:, 0].add(ab_re * h0_im + ab_im * h0_re)
    a_re = jnp.broadcast_to(ab_re, bu_re.shape)
    a_im = jnp.broadcast_to(ab_im, bu_im.shape)
    _, _, h_re, h_im = lax.associative_scan(_complex_affine_combine,
                                            (a_re, a_im, bu_re, bu_im), axis=1)
    return h_re, h_im


def s5_readout(h_re, h_im, c_re, c_im):
    return (jnp.einsum('blgn,gpn->blgp', h_re, c_re.astype(jnp.float32))
            - jnp.einsum('blgn,gpn->blgp', h_im, c_im.astype(jnp.float32)))


def s5_glu(y, w_glu, b_glu):
    y = jax.nn.gelu(y)
    return y * jax.nn.sigmoid(y @ w_glu + b_glu)


def branch_merge(attn, gate_a, ssm, gate_s, w_out):
    merged = jnp.concatenate([attn * jax.nn.silu(gate_a),
                              ssm.astype(attn.dtype) * jax.nn.silu(gate_s)], axis=-1)
    return merged @ w_out


def hybrid_layer(x, ctx, c, c_ctx, w_ada, b_ada, pre_g, post_g, w_in, q_g, k_g,
                 lam_re, lam_im, log_dt, b_re, b_im, c_re, c_im, d_skip, w_glu, b_glu,
                 w_out, rope, update_ctx):
    bsz, n_lat, _ = x.shape
    n_ctx = ctx.shape[1]
    shift_l, scale_l, gate_l = adaln(c, w_ada, b_ada)
    shift_c, scale_c, gate_c = adaln(c_ctx, w_ada, b_ada)
    h_lat = modulate(rmsnorm(x, pre_g), shift_l[:, None], scale_l[:, None])
    h_ctx = modulate(rmsnorm(ctx, pre_g), shift_c, scale_c)
    q_l, k_l, v_l, ga_l, u_l, gs_l = mixer_inputs(h_lat, w_in, q_g, k_g)
    q_c, k_c, v_c, ga_c, u_c, gs_c = mixer_inputs(h_ctx, w_in, q_g, k_g)

    q_l = apply_axial_rope(q_l, rope)
    k_l = apply_axial_rope(k_l, rope)
    attn_l = latent_attention(q_l, k_l, v_l, k_c, v_c)

    d32 = d_skip.astype(jnp.float32)
    u_l_g = u_l.astype(jnp.float32).reshape(bsz, n_lat, N_SSM_GROUPS, SSM_GROUP)
    u_c_g = u_c.astype(jnp.float32).reshape(bsz, n_ctx, N_SSM_GROUPS, SSM_GROUP)
    zero = jnp.zeros((bsz, N_SSM_GROUPS, SSM_STATE), jnp.float32)
    y_l = d32 * u_l_g
    y_c = d32 * u_c_g
    for d in range(N_DIRS):
        def orient(a):
            return jnp.flip(a, axis=1) if d == 1 else a
        disc = s5_discretise(lam_re[d], lam_im[d], log_dt[d], b_re[d], b_im[d])
        hc_re, hc_im = s5_states(orient(u_c_g), zero, zero, *disc)
        hl_re, hl_im = s5_states(orient(u_l_g), hc_re[:, -1], hc_im[:, -1], *disc)
        y_l = y_l + orient(s5_readout(hl_re, hl_im, c_re[d], c_im[d]))
        if update_ctx:
            y_c = y_c + orient(s5_readout(hc_re, hc_im, c_re[d], c_im[d]))
    ssm_l = s5_glu(y_l.reshape(bsz, n_lat, SSM_WIDTH).astype(x.dtype), w_glu, b_glu)

    out_l = branch_merge(attn_l, ga_l, ssm_l, gs_l, w_out)
    x_new = x + gate_l[:, None] * rmsnorm(out_l, post_g)

    if update_ctx:
        attn_c = context_attention(q_c, k_c, v_c)
        ssm_c = s5_glu(y_c.reshape(bsz, n_ctx, SSM_WIDTH).astype(ctx.dtype), w_glu, b_glu)
        out_c = branch_merge(attn_c, ga_c, ssm_c, gs_c, w_out)
        ctx = ctx + gate_c * rmsnorm(out_c, post_g)
    return x_new, ctx


def setup_inputs(seed: int = 0) -> dict:
    key = jax.random.key(seed)
    ks = jax.random.split(key, 24)
    f32 = jnp.float32
    G, N, P = N_SSM_GROUPS, SSM_STATE, SSM_GROUP
    n_idx = jnp.arange(N, dtype=f32)
    return {
        'x': jax.random.normal(ks[0], (BATCH, SEQ, D_MODEL), f32),
        'c': jax.random.normal(ks[1], (BATCH, D_MODEL), f32),
        'ctx': jax.random.normal(ks[2], (BATCH, CTX_LEN, D_MODEL), f32),
        'c_ctx': jax.random.normal(ks[3], (D_MODEL,), f32),
        'w_ada': jax.random.normal(ks[4], (DEPTH, D_MODEL, 3 * D_MODEL), f32) * D_MODEL ** -0.5,
        'b_ada': jax.random.normal(ks[5], (DEPTH, 3 * D_MODEL), f32) * 0.01,
        'pre_norm': 1.0 + 0.02 * jax.random.normal(ks[6], (DEPTH, D_MODEL), f32),
        'post_norm': 1.0 + 0.02 * jax.random.normal(ks[7], (DEPTH, D_MODEL), f32),
        'w_in': jax.random.normal(ks[8], (DEPTH, D_MODEL, IN_WIDTH), f32) * D_MODEL ** -0.5,
        'q_norm': 1.0 + 0.02 * jax.random.normal(ks[9], (DEPTH, HEAD_DIM), f32),
        'k_norm': 1.0 + 0.02 * jax.random.normal(ks[10], (DEPTH, HEAD_DIM), f32),
        'ssm_lam_re': -0.5 + 0.01 * jax.random.normal(ks[11], (DEPTH, N_DIRS, G, N), f32),
        'ssm_lam_im': math.pi * n_idx + 0.01 * jax.random.normal(ks[12], (DEPTH, N_DIRS, G, N), f32),
        'ssm_log_dt': jax.random.uniform(ks[13], (DEPTH, N_DIRS, G), f32,
                                         minval=math.log(1e-3), maxval=math.log(1e-1)),
        'ssm_b_re': jax.random.normal(ks[14], (DEPTH, N_DIRS, G, N, P), f32) * (2 * P) ** -0.5,
        'ssm_b_im': jax.random.normal(ks[15], (DEPTH, N_DIRS, G, N, P), f32) * (2 * P) ** -0.5,
        'ssm_c_re': jax.random.normal(ks[16], (DEPTH, N_DIRS, G, P, N), f32) * N ** -0.5,
        'ssm_c_im': jax.random.normal(ks[17], (DEPTH, N_DIRS, G, P, N), f32) * N ** -0.5,
        'ssm_d': jax.random.normal(ks[18], (DEPTH, G, P), f32),
        'w_glu': jax.random.normal(ks[19], (DEPTH, SSM_WIDTH, SSM_WIDTH), f32) * SSM_WIDTH ** -0.5,
        'b_glu': jax.random.normal(ks[20], (DEPTH, SSM_WIDTH), f32) * 0.01,
        'w_out': jax.random.normal(ks[21], (DEPTH, D_MIX, D_MODEL), f32) * D_MIX ** -0.5,
    }


def reference(x, c, ctx, c_ctx, w_ada, b_ada, pre_norm, post_norm, w_in, q_norm, k_norm,
              ssm_lam_re, ssm_lam_im, ssm_log_dt, ssm_b_re, ssm_b_im, ssm_c_re, ssm_c_im,
              ssm_d, w_glu, b_glu, w_out):
    n_lat = x.shape[1]
    rows = n_lat // GRID_W
    row_pos = jnp.repeat(jnp.arange(rows, dtype=jnp.float32), GRID_W)
    col_pos = jnp.tile(jnp.arange(GRID_W, dtype=jnp.float32), rows)
    rope = axial_rope_tables(row_pos, col_pos)
    for i in range(DEPTH):
        x, ctx = hybrid_layer(x, ctx, c, c_ctx, w_ada[i], b_ada[i], pre_norm[i], post_norm[i],
                              w_in[i], q_norm[i], k_norm[i], ssm_lam_re[i], ssm_lam_im[i],
                              ssm_log_dt[i], ssm_b_re[i], ssm_b_im[i], ssm_c_re[i], ssm_c_im[i],
                              ssm_d[i], w_glu[i], b_glu[i], w_out[i], rope,
                              update_ctx=(i < DEPTH - 1))
    return x
```

```python
import functools
import math

import jax
import jax.numpy as jnp
from jax import lax
from jax.experimental import pallas as pl
from jax.experimental.pallas import tpu as pltpu

F32 = jnp.float32
BF16 = jnp.bfloat16

D_MODEL = 1024
HEAD_DIM = 64
N_Q_HEADS = 8
N_KV_HEADS = 2
ATTN_WIDTH = N_Q_HEADS * HEAD_DIM
KV_WIDTH = N_KV_HEADS * HEAD_DIM
SSM_WIDTH = 512
SSM_GROUP = 16
N_SSM_GROUPS = SSM_WIDTH // SSM_GROUP
SSM_STATE = 64
GRID_W = 64
ROPE_THETA = 10000.0
ROPE_FREQS = 16
NORM_EPS = 1e-6
ATTN_SCALE = HEAD_DIM ** -0.5
Q_END = ATTN_WIDTH
K_END = Q_END + KV_WIDTH
V_END = K_END + KV_WIDTH
GA_END = V_END + ATTN_WIDTH
U_END = GA_END + SSM_WIDTH
IN_WIDTH = U_END + SSM_WIDTH

CHUNK = 16
CHUNK_LANES = CHUNK * SSM_GROUP
LANES = 128
STATE_LANES = 2 * SSM_STATE
VMEM_LIMIT = 48 * 1024 * 1024
NEG_BIG = -1e30


def _cparams(*sem):
    return pltpu.CompilerParams(dimension_semantics=sem, vmem_limit_bytes=VMEM_LIMIT)


def _adaln_kernel(c_ref, w_ref, b_ref, o_ref):
    c = c_ref[...]
    s = c * jax.nn.sigmoid(c)
    o_ref[...] = jnp.dot(s, w_ref[...], preferred_element_type=F32,
                         precision=lax.Precision.HIGHEST) + b_ref[...]


def _adaln(cvecs, w_ada, b_ada):
    rows, d = cvecs.shape
    n = w_ada.shape[1]
    tn = 512
    return pl.pallas_call(
        _adaln_kernel,
        grid=(n // tn,),
        in_specs=[pl.BlockSpec((rows, d), lambda j: (0, 0)),
                  pl.BlockSpec((d, tn), lambda j: (0, j)),
                  pl.BlockSpec((1, tn), lambda j: (0, j))],
        out_specs=pl.BlockSpec((rows, tn), lambda j: (0, j)),
        out_shape=jax.ShapeDtypeStruct((rows, n), F32),
        compiler_params=_cparams("arbitrary"),
        name="adaln",
    )(cvecs, w_ada, b_ada.reshape(1, n))


def _head_mean_sq(z, bd):
    return jnp.dot((z * z).astype(BF16), bd, preferred_element_type=F32)


def _swap16(x):
    w = x.shape[1]
    lane = lax.broadcasted_iota(jnp.int32, x.shape, 1)
    return jnp.where((lane & 16) == 0, pltpu.roll(x, w - 16, 1), pltpu.roll(x, 16, 1))


def _rope(x, cos, sin_signed):
    cols = []
    for c in range(x.shape[1] // LANES):
        xc = x[:, c * LANES:(c + 1) * LANES]
        cols.append(xc * cos + _swap16(xc) * sin_signed)
    return cols[0] if len(cols) == 1 else jnp.concatenate(cols, axis=1)


def _silu(z):
    return z * jax.nn.sigmoid(z)


def _modulated_input(x_ref, sh_ref, sc_ref, pg_ref):
    x = x_ref[0]
    ms = jnp.mean(x * x, axis=-1, keepdims=True)
    xn = x * lax.rsqrt(ms + NORM_EPS) * pg_ref[...]
    return (xn * (1.0 + sc_ref[0]) + sh_ref[0]).astype(BF16)


def _inproj_latent_kernel(x_ref, sh_ref, sc_ref, pg_ref, w_ref, qg_ref, kg_ref, cos_ref, sin_ref,
                          bd_ref, q_o, k_o, v_o, ga_o, u_o, gs_o):
    h = _modulated_input(x_ref, sh_ref, sc_ref, pg_ref)

    def proj(a, b):
        return jnp.dot(h, w_ref[:, a:b], preferred_element_type=F32)

    cos = cos_ref[...]
    sin = sin_ref[...]
    zq = proj(0, Q_END)
    qn = zq * lax.rsqrt(_head_mean_sq(zq, bd_ref[...]) + NORM_EPS) * qg_ref[...]
    q_o[0] = (_rope(qn, cos, sin) * ATTN_SCALE).astype(BF16)
    zk = proj(Q_END, K_END)
    kn = zk * lax.rsqrt(_head_mean_sq(zk, bd_ref[:KV_WIDTH, :KV_WIDTH]) + NORM_EPS) * kg_ref[...]
    k_o[0] = _rope(kn, cos, sin).astype(BF16)
    v_o[0] = proj(K_END, V_END).astype(BF16)
    ga_o[0] = _silu(proj(V_END, GA_END)).astype(BF16)
    u_o[0] = proj(GA_END, U_END).astype(BF16)
    gs_o[0] = _silu(proj(U_END, IN_WIDTH)).astype(BF16)


def _inproj_context_kernel(x_ref, sh_ref, sc_ref, pg_ref, w_ref, kg_ref, bd_ref, k_o, v_o, u_o):
    h = _modulated_input(x_ref, sh_ref, sc_ref, pg_ref)

    def proj(a, b):
        return jnp.dot(h, w_ref[:, a:b], preferred_element_type=F32)

    zk = proj(Q_END, K_END)
    kn = zk * lax.rsqrt(_head_mean_sq(zk, bd_ref[:KV_WIDTH, :KV_WIDTH]) + NORM_EPS) * kg_ref[...]
    k_o[0] = kn.astype(BF16)
    v_o[0] = proj(K_END, V_END).astype(BF16)
    u_o[0] = proj(GA_END, U_END).astype(BF16)


def _row_spec(tm, width):
    return pl.BlockSpec((1, tm, width), lambda b, i: (b, i, 0))


def _const_spec(shape):
    return pl.BlockSpec(shape, lambda b, i: (0,) * len(shape))


def _batch_vec_spec(width):
    return pl.BlockSpec((1, 1, width), lambda b, i: (b, 0, 0))


def _inproj_latent(x, shift, scale, pre_g, w_bf, qg, kg, cos, sin, bd, tm):
    bsz, n, d = x.shape
    outs = [jax.ShapeDtypeStruct((bsz, n, w), BF16)
            for w in (ATTN_WIDTH, KV_WIDTH, KV_WIDTH, ATTN_WIDTH, SSM_WIDTH, SSM_WIDTH)]
    return pl.pallas_call(
        _inproj_latent_kernel,
        grid=(bsz, n // tm),
        in_specs=[_row_spec(tm, d), _batch_vec_spec(d), _batch_vec_spec(d), _const_spec((1, d)),
                  _const_spec((d, IN_WIDTH)), _const_spec((1, ATTN_WIDTH)), _const_spec((1, KV_WIDTH)),
                  pl.BlockSpec((tm, LANES), lambda b, i: (i, 0)),
                  pl.BlockSpec((tm, LANES), lambda b, i: (i, 0)),
                  _const_spec((ATTN_WIDTH, ATTN_WIDTH))],
        out_specs=[_row_spec(tm, o.shape[2]) for o in outs],
        out_shape=outs,
        compiler_params=_cparams("parallel", "parallel"),
        name="inproj_latent",
    )(x, shift, scale, pre_g, w_bf, qg, kg, cos, sin, bd)


def _inproj_context(ctx, shift, scale, pre_g, w_bf, kg, bd, tm):
    bsz, n, d = ctx.shape
    outs = [jax.ShapeDtypeStruct((bsz, n, w), BF16) for w in (KV_WIDTH, KV_WIDTH, SSM_WIDTH)]
    vec = pl.BlockSpec((1, 1, d), lambda b, i: (0, 0, 0))
    return pl.pallas_call(
        _inproj_context_kernel,
        grid=(bsz, n // tm),
        in_specs=[_row_spec(tm, d), vec, vec, _const_spec((1, d)), _const_spec((d, IN_WIDTH)),
                  _const_spec((1, KV_WIDTH)), _const_spec((ATTN_WIDTH, ATTN_WIDTH))],
        out_specs=[_row_spec(tm, o.shape[2]) for o in outs],
        out_shape=outs,
        compiler_params=_cparams("parallel", "parallel"),
        name="inproj_context",
    )(ctx, shift, scale, pre_g, w_bf, kg, bd)


def _attn_kernel(q_ref, k_ref, v_ref, o_ref, m_ref, l_ref, acc_ref, *, tk):
    tq = q_ref.shape[1]
    n_keys = k_ref.shape[1]
    group = N_Q_HEADS // N_KV_HEADS
    q = q_ref[0]
    lane = lax.broadcasted_iota(jnp.int32, (tq, LANES), 1)
    low = lane < HEAD_DIM

    q_wide = []
    for h in range(N_Q_HEADS):
        col = q[:, (h // 2) * LANES:(h // 2 + 1) * LANES].astype(F32)
        src_high = h % 2 == 1
        dst_high = h // group == 1
        if src_high != dst_high:
            col = pltpu.roll(col, HEAD_DIM, 1)
        q_wide.append(jnp.where(low != dst_high, col, 0.0).astype(BF16))

    m_ref[...] = jnp.full(m_ref.shape, NEG_BIG, F32)
    l_ref[...] = jnp.zeros(l_ref.shape, F32)
    acc_ref[...] = jnp.zeros(acc_ref.shape, F32)

    def body(t, carry):
        start = pl.multiple_of(t * tk, tk)
        kt = k_ref[0, pl.ds(start, tk), :]
        vt = v_ref[0, pl.ds(start, tk), :]
        for h in range(N_Q_HEADS):
            s = lax.dot_general(q_wide[h], kt, (((1,), (1,)), ((), ())),
                                preferred_element_type=F32)
            m_old = m_ref[h]
            m_new = jnp.maximum(m_old, jnp.max(s, axis=1, keepdims=True))
            alpha = jnp.exp(m_old - m_new)
            p = jnp.exp(s - m_new[:, :1])
            l_ref[h] = alpha * l_ref[h] + jnp.sum(p, axis=1, keepdims=True)
            acc_ref[h] = alpha * acc_ref[h] + jnp.dot(p.astype(BF16), vt,
                                                      preferred_element_type=F32)
            m_ref[h] = m_new
        return carry

    lax.fori_loop(0, n_keys // tk, body, 0)

    for c in range(N_Q_HEADS // 2):
        kv_high = (2 * c) // group == 1
        even = acc_ref[2 * c] / l_ref[2 * c]
        odd = acc_ref[2 * c + 1] / l_ref[2 * c + 1]
        if kv_high:
            even = pltpu.roll(even, HEAD_DIM, 1)
        else:
            odd = pltpu.roll(odd, HEAD_DIM, 1)
        o_ref[0, :, c * LANES:(c + 1) * LANES] = jnp.where(low, even, odd).astype(BF16)


def _attention(q, k_all, v_all, tq, tk):
    bsz, n, _ = q.shape
    n_keys = k_all.shape[1]
    kv_spec = pl.BlockSpec((1, n_keys, KV_WIDTH), lambda b, i: (b, 0, 0))
    return pl.pallas_call(
        functools.partial(_attn_kernel, tk=tk),
        grid=(bsz, n // tq),
        in_specs=[_row_spec(tq, ATTN_WIDTH), kv_spec, kv_spec],
        out_specs=_row_spec(tq, ATTN_WIDTH),
        out_shape=jax.ShapeDtypeStruct((bsz, n, ATTN_WIDTH), BF16),
        scratch_shapes=[pltpu.VMEM((N_Q_HEADS, tq, LANES), F32),
                        pltpu.VMEM((N_Q_HEADS, tq, LANES), F32),
                        pltpu.VMEM((N_Q_HEADS, tq, LANES), F32)],
        compiler_params=_cparams("parallel", "parallel"),
        name="attention",
    )(q, k_all, v_all)


def _scan_steps(n_rows):
    return max(1, math.ceil(math.log2(n_rows)))


def _ssm_prep_kernel(lre_ref, lim_ref, ldt_ref, bt_ref, bts_ref, c_ref, cs_ref, d_ref,
                     win_o, coutt_o, kcat_o, p_o, q_o, *, n_steps):
    lane = lax.broadcasted_iota(jnp.int32, (1, STATE_LANES), 1)
    sign_lo = jnp.where(lane < SSM_STATE, -1.0, 1.0)
    k_idx = lax.broadcasted_iota(jnp.int32, (CHUNK, STATE_LANES), 0).astype(F32)

    def outer(pw, pws, mat, mats):
        full = pw[:, None, :] * mat[None, :, :] + pws[:, None, :] * mats[None, :, :]
        return full.reshape(CHUNK * mat.shape[0], STATE_LANES)

    kms = []
    for d in range(2):
        lre = lre_ref[0, d:d + 1, :]
        lim = lim_ref[0, d:d + 1, :]
        dt = jnp.exp(ldt_ref[0, d:d + 1, :])

        def power(expo):
            mag = jnp.exp(expo * dt * lre)
            ang = expo * dt * lim
            return mag * jnp.cos(ang), mag * jnp.sin(ang) * sign_lo

        a_re, a_ims = power(jnp.ones((1, STATE_LANES), F32))
        a_im = a_ims * sign_lo
        nr, ni = a_re - 1.0, a_im
        den = lre * lre + lim * lim
        cr = (nr * lre + ni * lim) / den
        ci = (ni * lre - nr * lim) / den
        cis = ci * sign_lo
        bbar = cr * bt_ref[0, d] + cis * bts_ref[0, d]
        bbar_s = cr * bts_ref[0, d] - cis * bt_ref[0, d]
        cmat, cmat_s = c_ref[0, d], cs_ref[0, d]

        asc, ascs = power(k_idx)
        desc, descs = power(CHUNK - 1.0 - k_idx)
        if d == 0:
            state_in = outer(desc, descs, bbar, bbar_s)
            taps = outer(asc, ascs, bbar, bbar_s)
            ro, ros = power(k_idx + 1.0)
        else:
            state_in = outer(asc, ascs, bbar, bbar_s)
            taps = outer(desc, descs, bbar, bbar_s)
            ro, ros = power(CHUNK - k_idx)
        state_out = outer(ro, ros, cmat, cmat_s) * (-sign_lo)
        win_o[0, :, d * STATE_LANES:(d + 1) * STATE_LANES] = state_in
        coutt_o[0, :, d * STATE_LANES:(d + 1) * STATE_LANES] = state_out
        kms.append(lax.dot_general(taps, cmat * (-sign_lo), (((1,), (1,)), ((), ())),
                                   preferred_element_type=F32, precision=lax.Precision.HIGHEST))

        sr, sis = power(jnp.full((1, STATE_LANES), float(CHUNK), F32))
        for k in range(n_steps):
            p_o[0, k:k + 1, d * STATE_LANES:(d + 1) * STATE_LANES] = sr
            q_o[0, k:k + 1, d * STATE_LANES:(d + 1) * STATE_LANES] = sis
            si = sis * sign_lo
            sr, sis = sr * sr - si * si, 2.0 * sr * si * sign_lo

    km_f, km_b = kms
    rr = lax.broadcasted_iota(jnp.int32, (SSM_GROUP, SSM_GROUP), 0)
    cc = lax.broadcasted_iota(jnp.int32, (SSM_GROUP, SSM_GROUP), 1)
    skip = jnp.where(rr == cc, d_ref[0], 0.0)
    edge = (CHUNK - 1) * SSM_GROUP
    kcat_o[0, :edge, :] = km_b[:edge]
    kcat_o[0, edge:edge + SSM_GROUP, :] = km_b[edge:] + km_f[:SSM_GROUP] + skip
    kcat_o[0, edge + SSM_GROUP:, :] = km_f[SSM_GROUP:]


def _ssm_prep(lam_re, lam_im, log_dt, b_re, b_im, c_re, c_im, d_skip, n_steps):
    g = N_SSM_GROUPS
    dup = lambda a: jnp.concatenate([a, a], axis=-1)
    lre = dup(jnp.swapaxes(lam_re, 0, 1))
    lim = dup(jnp.swapaxes(lam_im, 0, 1))
    ldt = jnp.broadcast_to(jnp.swapaxes(log_dt, 0, 1)[..., None], (g, 2, STATE_LANES))
    btr = jnp.transpose(b_re, (1, 0, 3, 2))
    bti = jnp.transpose(b_im, (1, 0, 3, 2))
    bt, bts = jnp.concatenate([btr, bti], -1), jnp.concatenate([bti, btr], -1)
    cr, ci = jnp.swapaxes(c_re, 0, 1), jnp.swapaxes(c_im, 0, 1)
    cm, cms = jnp.concatenate([cr, ci], -1), jnp.concatenate([ci, cr], -1)
    dsk = d_skip.reshape(g, 1, SSM_GROUP)
    n_kcat = (2 * CHUNK - 1) * SSM_GROUP
    n_pad = 16
    vec = pl.BlockSpec((1, 2, STATE_LANES), lambda i: (i, 0, 0))
    mat = pl.BlockSpec((1, 2, SSM_GROUP, STATE_LANES), lambda i: (i, 0, 0, 0))
    sq = pl.BlockSpec((1, CHUNK_LANES, CHUNK_LANES), lambda i: (i, 0, 0))
    mult = pl.BlockSpec((1, n_pad, 2 * STATE_LANES), lambda i: (i, 0, 0))
    return pl.pallas_call(
        functools.partial(_ssm_prep_kernel, n_steps=n_steps),
        grid=(g,),
        in_specs=[vec, vec, vec, mat, mat, mat, mat,
                  pl.BlockSpec((1, 1, SSM_GROUP), lambda i: (i, 0, 0))],
        out_specs=[sq, sq, pl.BlockSpec((1, n_kcat, SSM_GROUP), lambda i: (i, 0, 0)), mult, mult],
        out_shape=[jax.ShapeDtypeStruct((g, CHUNK_LANES, CHUNK_LANES), F32),
                   jax.ShapeDtypeStruct((g, CHUNK_LANES, CHUNK_LANES), F32),
                   jax.ShapeDtypeStruct((g, n_kcat, SSM_GROUP), F32),
                   jax.ShapeDtypeStruct((g, n_pad, 2 * STATE_LANES), F32),
                   jax.ShapeDtypeStruct((g, n_pad, 2 * STATE_LANES), F32)],
        compiler_params=_cparams("parallel"),
        name="ssm_prep",
    )(lre, lim, ldt, bt, bts, cm, cms, dsk)


def _ssm_kernel(u_ref, win_ref, m_ref, cout_ref, p_ref, q_ref, y_o, *, n_ctx_rows, n_steps):
    u = u_ref[0, 0]
    rows = u.shape[0]
    x = jnp.dot(u, win_ref[0], preferred_element_type=F32)
    xf = x[:, :STATE_LANES]
    xb = jnp.concatenate([x[n_ctx_rows:, STATE_LANES:], x[:n_ctx_rows, STATE_LANES:]], axis=0)
    row = lax.broadcasted_iota(jnp.int32, (rows, STATE_LANES), 0)
    for k in range(n_steps):
        s = 1 << k
        pf, qf = p_ref[0, k:k + 1, :STATE_LANES], q_ref[0, k:k + 1, :STATE_LANES]
        pb, qb = p_ref[0, k:k + 1, STATE_LANES:], q_ref[0, k:k + 1, STATE_LANES:]
        sf = jnp.where(row >= s, pltpu.roll(xf, s, 0), 0.0)
        xf = xf + pf * sf + qf * pltpu.roll(sf, SSM_STATE, 1)
        sb = jnp.where(row < rows - s, pltpu.roll(xb, rows - s, 0), 0.0)
        xb = xb + pb * sb + qb * pltpu.roll(sb, SSM_STATE, 1)
    prev_f = pltpu.roll(xf, 1, 0)[n_ctx_rows:]
    next_b = pltpu.roll(xb, rows - 1, 0)[:rows - n_ctx_rows]
    st = jnp.concatenate([prev_f, next_b], axis=1).astype(BF16)
    y = jnp.dot(u[n_ctx_rows:], m_ref[0], preferred_element_type=F32)
    y = y + jnp.dot(st, cout_ref[0], preferred_element_type=F32)
    y_o[0, 0] = y.astype(BF16)


def _ssm(u_cf, win, m, cout, p, q, n_ctx_rows, n_steps):
    bsz, g, rows, _ = u_cf.shape
    sq = pl.BlockSpec((1, CHUNK_LANES, CHUNK_LANES), lambda b, i: (i, 0, 0))
    mult = pl.BlockSpec((1, p.shape[1], p.shape[2]), lambda b, i: (i, 0, 0))
    return pl.pallas_call(
        functools.partial(_ssm_kernel, n_ctx_rows=n_ctx_rows, n_steps=n_steps),
        grid=(bsz, g),
        in_specs=[pl.BlockSpec((1, 1, rows, CHUNK_LANES), lambda b, i: (b, i, 0, 0)),
                  sq, sq, sq, mult, mult],
        out_specs=pl.BlockSpec((1, 1, rows - n_ctx_rows, CHUNK_LANES), lambda b, i: (b, i, 0, 0)),
        out_shape=jax.ShapeDtypeStruct((bsz, g, rows - n_ctx_rows, CHUNK_LANES), BF16),
        compiler_params=_cparams("parallel", "parallel"),
        name="ssm",
    )(u_cf, win, m, cout, p, q)


def _final_kernel(x_ref, at_ref, ga_ref, y_ref, gs_ref, gate_ref, wglu_ref, bglu_ref, wout_ref,
                  pg_ref, o_ref):
    y = jax.nn.gelu(y_ref[0].astype(F32))
    t = jnp.dot(y.astype(BF16), wglu_ref[...], preferred_element_type=F32) + bglu_ref[...]
    ssm = y * jax.nn.sigmoid(t)
    a = (at_ref[0].astype(F32) * ga_ref[0].astype(F32)).astype(BF16)
    s = (ssm * gs_ref[0].astype(F32)).astype(BF16)
    out = jnp.dot(a, wout_ref[:ATTN_WIDTH, :], preferred_element_type=F32)
    out = out + jnp.dot(s, wout_ref[ATTN_WIDTH:, :], preferred_element_type=F32)
    ms = jnp.mean(out * out, axis=-1, keepdims=True)
    r = out * lax.rsqrt(ms + NORM_EPS) * pg_ref[...]
    o_ref[0] = x_ref[0] + gate_ref[0] * r


def _final(x, attn, ga, y, gs, gate, wglu_bf, b_glu, wout_bf, post_g, tm):
    bsz, n, d = x.shape
    return pl.pallas_call(
        _final_kernel,
        grid=(bsz, n // tm),
        in_specs=[_row_spec(tm, d), _row_spec(tm, ATTN_WIDTH), _row_spec(tm, ATTN_WIDTH),
                  _row_spec(tm, SSM_WIDTH), _row_spec(tm, SSM_WIDTH), _batch_vec_spec(d),
                  _const_spec((SSM_WIDTH, SSM_WIDTH)), _const_spec((1, SSM_WIDTH)),
                  _const_spec((d, d)), _const_spec((1, d))],
        out_specs=_row_spec(tm, d),
        out_shape=jax.ShapeDtypeStruct((bsz, n, d), F32),
        compiler_params=_cparams("parallel", "parallel"),
        name="final",
    )(x, attn, ga, y, gs, gate, wglu_bf, b_glu, wout_bf, post_g)


def _rope_tables(n_lat):
    t = jnp.arange(n_lat, dtype=jnp.int32)
    row_pos = (t // GRID_W).astype(F32)
    col_pos = (t % GRID_W).astype(F32)
    inv_freq = ROPE_THETA ** (-jnp.arange(ROPE_FREQS, dtype=F32) / ROPE_FREQS)
    ang_r = row_pos[:, None] * inv_freq
    ang_c = col_pos[:, None] * inv_freq
    cos = jnp.concatenate([jnp.cos(ang_r)] * 2 + [jnp.cos(ang_c)] * 2, axis=1)
    sin = jnp.concatenate([-jnp.sin(ang_r), jnp.sin(ang_r), -jnp.sin(ang_c), jnp.sin(ang_c)], axis=1)
    return jnp.tile(cos, (1, LANES // HEAD_DIM)), jnp.tile(sin, (1, LANES // HEAD_DIM))


def _toeplitz(kcat):
    j = jnp.arange(CHUNK)
    lag = j[None, :] - j[:, None] + CHUNK - 1
    blocks = kcat[:, lag]
    return jnp.transpose(blocks, (0, 1, 3, 2, 4)).reshape(N_SSM_GROUPS, CHUNK_LANES, CHUNK_LANES)


def _to_chunk_major(u):
    bsz, n, _ = u.shape
    r = u.reshape(bsz, n // CHUNK, CHUNK, N_SSM_GROUPS, SSM_GROUP)
    return jnp.transpose(r, (0, 3, 1, 2, 4)).reshape(bsz, N_SSM_GROUPS, n // CHUNK, CHUNK_LANES)


def _from_chunk_major(y):
    bsz, g, rows, _ = y.shape
    r = y.reshape(bsz, g, rows, CHUNK, SSM_GROUP)
    return jnp.transpose(r, (0, 2, 3, 1, 4)).reshape(bsz, rows * CHUNK, SSM_WIDTH)


def _layer(x, ctx, c, c_ctx, w_ada, b_ada, pre_g, post_g, w_in, q_g, k_g, lam_re, lam_im, log_dt,
           b_re, b_im, c_re, c_im, d_skip, w_glu, b_glu, w_out):
    bsz, n_lat, d = x.shape
    n_ctx = ctx.shape[1]
    assert n_lat % 512 == 0 and n_ctx % 256 == 0 and n_lat % GRID_W == 0

    cvecs = jnp.zeros((8, d), F32).at[:bsz].set(c).at[bsz].set(c_ctx)
    mod = _adaln(cvecs, w_ada, b_ada)
    shift, scale, gate = (mod[:, i * d:(i + 1) * d] for i in range(3))
    lat = lambda a: a[:bsz].reshape(bsz, 1, d)
    cvec = lambda a: a[bsz].reshape(1, 1, d)

    w_bf = w_in.astype(BF16)
    head = jnp.arange(ATTN_WIDTH) // HEAD_DIM
    bd = (head[:, None] == head[None, :]).astype(BF16) * (1.0 / HEAD_DIM)
    qg = jnp.tile(q_g, N_Q_HEADS).reshape(1, ATTN_WIDTH)
    kg = jnp.tile(k_g, N_KV_HEADS).reshape(1, KV_WIDTH)
    cos, sin = _rope_tables(n_lat)
    pre = pre_g.reshape(1, d)

    q, k_l, v_l, ga, u_l, gs = _inproj_latent(x, lat(shift), lat(scale), pre, w_bf, qg, kg,
                                              cos, sin, bd, tm=512)
    k_c, v_c, u_c = _inproj_context(ctx, cvec(shift), cvec(scale), pre, w_bf, kg, bd, tm=256)

    k_all = jnp.concatenate([k_c, k_l], axis=1)
    v_all = jnp.concatenate([v_c, v_l], axis=1)
    attn = _attention(q, k_all, v_all, tq=256, tk=256)

    rows = (n_ctx + n_lat) // CHUNK
    n_steps = _scan_steps(rows)
    win, coutt, kcat, p, qm = _ssm_prep(lam_re, lam_im, log_dt, b_re, b_im, c_re, c_im, d_skip, n_steps)
    m_op = _toeplitz(kcat.reshape(N_SSM_GROUPS, 2 * CHUNK - 1, SSM_GROUP, SSM_GROUP))
    u_cf = _to_chunk_major(jnp.concatenate([u_c, u_l], axis=1))
    y_cf = _ssm(u_cf, win.astype(BF16), m_op.astype(BF16), jnp.swapaxes(coutt, 1, 2).astype(BF16),
                p, qm, n_ctx // CHUNK, n_steps)
    y = _from_chunk_major(y_cf)

    return _final(x, attn, ga, y, gs, lat(gate), w_glu.astype(BF16), b_glu.reshape(1, SSM_WIDTH),
                  w_out.astype(BF16), post_g.reshape(1, d), tm=512)


def kernel(x, c, ctx, c_ctx, w_ada, b_ada, pre_norm, post_norm, w_in, q_norm, k_norm, ssm_lam_re,
           ssm_lam_im, ssm_log_dt, ssm_b_re, ssm_b_im, ssm_c_re, ssm_c_im, ssm_d, w_glu, b_glu, w_out):
    depth = w_ada.shape[0]
    assert depth == 1, "context stream update between layers is not implemented"
    return _layer(x, ctx, c, c_ctx, w_ada[0], b_ada[0], pre_norm[0], post_norm[0], w_in[0], q_norm[0],
                  k_norm[0], ssm_lam_re[0], ssm_lam_im[0], ssm_log_dt[0], ssm_b_re[0], ssm_b_im[0],
                  ssm_c_re[0], ssm_c_im[0], ssm_d[0], w_glu[0], b_glu[0], w_out[0])
```

```python
import functools
import math

import jax
import jax.numpy as jnp
from jax import lax
from jax.experimental import pallas as pl
from jax.experimental.pallas import tpu as pltpu

F32 = jnp.float32
BF16 = jnp.bfloat16

D_MODEL = 1024
HEAD_DIM = 64
N_Q_HEADS = 8
N_KV_HEADS = 2
ATTN_WIDTH = N_Q_HEADS * HEAD_DIM
KV_WIDTH = N_KV_HEADS * HEAD_DIM
SSM_WIDTH = 512
SSM_GROUP = 16
N_SSM_GROUPS = SSM_WIDTH // SSM_GROUP
SSM_STATE = 64
GRID_W = 64
ROPE_THETA = 10000.0
ROPE_FREQS = 16
NORM_EPS = 1e-6
ATTN_SCALE = HEAD_DIM ** -0.5
LOG2_E = math.log2(math.e)
Q_END = ATTN_WIDTH
K_END = Q_END + KV_WIDTH
V_END = K_END + KV_WIDTH
GA_END = V_END + ATTN_WIDTH
U_END = GA_END + SSM_WIDTH
IN_WIDTH = U_END + SSM_WIDTH

CHUNK = 16
CHUNK_LANES = CHUNK * SSM_GROUP
LANES = 128
STATE_LANES = 2 * SSM_STATE
VMEM_LIMIT = 48 * 1024 * 1024
NEG_BIG = -1e30


def _cparams(*sem):
    return pltpu.CompilerParams(dimension_semantics=sem, vmem_limit_bytes=VMEM_LIMIT)


def _adaln_kernel(c_ref, w_ref, b_ref, o_ref):
    c = c_ref[...]
    s = c * jax.nn.sigmoid(c)
    o_ref[...] = jnp.dot(s, w_ref[...], preferred_element_type=F32,
                         precision=lax.Precision.HIGHEST) + b_ref[...]


def _adaln(cvecs, w_ada, b_ada):
    rows, d = cvecs.shape
    n = w_ada.shape[1]
    tn = 512
    return pl.pallas_call(
        _adaln_kernel,
        grid=(n // tn,),
        in_specs=[pl.BlockSpec((rows, d), lambda j: (0, 0)),
                  pl.BlockSpec((d, tn), lambda j: (0, j)),
                  pl.BlockSpec((1, tn), lambda j: (0, j))],
        out_specs=pl.BlockSpec((rows, tn), lambda j: (0, j)),
        out_shape=jax.ShapeDtypeStruct((rows, n), F32),
        compiler_params=_cparams("arbitrary"),
        name="adaln",
    )(cvecs, w_ada, b_ada.reshape(1, n))


def _head_mean_sq(z, bd):
    return jnp.dot((z * z).astype(BF16), bd, preferred_element_type=F32)


def _swap16(x):
    w = x.shape[1]
    lane = lax.broadcasted_iota(jnp.int32, x.shape, 1)
    return jnp.where((lane & 16) == 0, pltpu.roll(x, w - 16, 1), pltpu.roll(x, 16, 1))


def _rope(x, cos, sin_signed):
    cols = []
    for c in range(x.shape[1] // LANES):
        xc = x[:, c * LANES:(c + 1) * LANES]
        cols.append(xc * cos + _swap16(xc) * sin_signed)
    return cols[0] if len(cols) == 1 else jnp.concatenate(cols, axis=1)


def _silu(z):
    return z * jax.nn.sigmoid(z)


def _modulated_input(x_ref, sh_ref, sc_ref, pg_ref):
    x = x_ref[0]
    ms = jnp.mean(x * x, axis=-1, keepdims=True)
    xn = x * lax.rsqrt(ms + NORM_EPS) * pg_ref[...]
    return (xn * (1.0 + sc_ref[0]) + sh_ref[0]).astype(BF16)


def _inproj_latent_kernel(x_ref, sh_ref, sc_ref, pg_ref, w_ref, qg_ref, kg_ref, cos_ref, sin_ref,
                          bd_ref, q_o, k_o, v_o, ga_o, u_o, gs_o):
    h = _modulated_input(x_ref, sh_ref, sc_ref, pg_ref)

    def proj(a, b):
        return jnp.dot(h, w_ref[:, a:b], preferred_element_type=F32)

    cos = cos_ref[...]
    sin = sin_ref[...]
    zq = proj(0, Q_END)
    qn = zq * lax.rsqrt(_head_mean_sq(zq, bd_ref[...]) + NORM_EPS) * qg_ref[...]
    q_o[0] = (_rope(qn, cos, sin) * (ATTN_SCALE * LOG2_E)).astype(BF16)
    zk = proj(Q_END, K_END)
    kn = zk * lax.rsqrt(_head_mean_sq(zk, bd_ref[:KV_WIDTH, :KV_WIDTH]) + NORM_EPS) * kg_ref[...]
    k_o[0] = _rope(kn, cos, sin).astype(BF16)
    v_o[0] = proj(K_END, V_END).astype(BF16)
    ga_o[0] = _silu(proj(V_END, GA_END)).astype(BF16)
    u_o[0] = proj(GA_END, U_END).astype(BF16)
    gs_o[0] = _silu(proj(U_END, IN_WIDTH)).astype(BF16)


def _inproj_context_kernel(x_ref, sh_ref, sc_ref, pg_ref, w_ref, kg_ref, bd_ref, k_o, v_o, u_o):
    h = _modulated_input(x_ref, sh_ref, sc_ref, pg_ref)

    def proj(a, b):
        return jnp.dot(h, w_ref[:, a:b], preferred_element_type=F32)

    zk = proj(Q_END, K_END)
    kn = zk * lax.rsqrt(_head_mean_sq(zk, bd_ref[:KV_WIDTH, :KV_WIDTH]) + NORM_EPS) * kg_ref[...]
    k_o[0] = kn.astype(BF16)
    v_o[0] = proj(K_END, V_END).astype(BF16)
    u_o[0] = proj(GA_END, U_END).astype(BF16)


def _row_spec(tm, width):
    return pl.BlockSpec((1, tm, width), lambda b, i: (b, i, 0))


def _const_spec(shape):
    return pl.BlockSpec(shape, lambda b, i: (0,) * len(shape))


def _batch_vec_spec(width):
    return pl.BlockSpec((1, 1, width), lambda b, i: (b, 0, 0))


def _inproj_latent(x, shift, scale, pre_g, w_bf, qg, kg, cos, sin, bd, tm):
    bsz, n, d = x.shape
    outs = [jax.ShapeDtypeStruct((bsz, n, w), BF16)
            for w in (ATTN_WIDTH, KV_WIDTH, KV_WIDTH, ATTN_WIDTH, SSM_WIDTH, SSM_WIDTH)]
    return pl.pallas_call(
        _inproj_latent_kernel,
        grid=(bsz, n // tm),
        in_specs=[_row_spec(tm, d), _batch_vec_spec(d), _batch_vec_spec(d), _const_spec((1, d)),
                  _const_spec((d, IN_WIDTH)), _const_spec((1, ATTN_WIDTH)), _const_spec((1, KV_WIDTH)),
                  pl.BlockSpec((tm, LANES), lambda b, i: (i, 0)),
                  pl.BlockSpec((tm, LANES), lambda b, i: (i, 0)),
                  _const_spec((ATTN_WIDTH, ATTN_WIDTH))],
        out_specs=[_row_spec(tm, o.shape[2]) for o in outs],
        out_shape=outs,
        compiler_params=_cparams("parallel", "parallel"),
        name="inproj_latent",
    )(x, shift, scale, pre_g, w_bf, qg, kg, cos, sin, bd)


def _inproj_context(ctx, shift, scale, pre_g, w_bf, kg, bd, tm):
    bsz, n, d = ctx.shape
    outs = [jax.ShapeDtypeStruct((bsz, n, w), BF16) for w in (KV_WIDTH, KV_WIDTH, SSM_WIDTH)]
    vec = pl.BlockSpec((1, 1, d), lambda b, i: (0, 0, 0))
    return pl.pallas_call(
        _inproj_context_kernel,
        grid=(bsz, n // tm),
        in_specs=[_row_spec(tm, d), vec, vec, _const_spec((1, d)), _const_spec((d, IN_WIDTH)),
                  _const_spec((1, KV_WIDTH)), _const_spec((ATTN_WIDTH, ATTN_WIDTH))],
        out_specs=[_row_spec(tm, o.shape[2]) for o in outs],
        out_shape=outs,
        compiler_params=_cparams("parallel", "parallel"),
        name="inproj_context",
    )(ctx, shift, scale, pre_g, w_bf, kg, bd)


V_ROWS = HEAD_DIM + 16


def _attn_kernel(q_ref, k_ref, vt_ref, o_ref, *, tk, heads_per_pass):
    tq = q_ref.shape[1]
    n_keys = k_ref.shape[1]
    group = N_Q_HEADS // N_KV_HEADS
    q = q_ref[0]
    low = lax.broadcasted_iota(jnp.int32, (tq, LANES), 1) < HEAD_DIM

    def widen(h):
        col = q[:, (h // 2) * LANES:(h // 2 + 1) * LANES].astype(F32)
        dst_high = h // group == 1
        if (h % 2 == 1) != dst_high:
            col = pltpu.roll(col, HEAD_DIM, 1)
        return jnp.where(low != dst_high, col, 0.0).astype(BF16)

    outs = []
    for h0 in range(0, N_Q_HEADS, heads_per_pass):
        heads = list(range(h0, h0 + heads_per_pass))
        q_wide = [widen(h) for h in heads]

        def body(t, carry, heads=heads, q_wide=q_wide):
            start = pl.multiple_of(t * tk, tk)
            kt = k_ref[0, pl.ds(start, tk), :]
            scores = [lax.dot_general(kt, qw, (((1,), (1,)), ((), ())), preferred_element_type=F32)
                      for qw in q_wide]
            new = []
            for i, h in enumerate(heads):
                m_old, acc = carry[i]
                vt = vt_ref[0, h // group, :, pl.ds(start, tk)]
                s = scores[i]
                m_new = jnp.maximum(m_old, jnp.max(s, axis=0, keepdims=True))
                p = jnp.exp2(s - m_new).astype(BF16)
                acc = jnp.exp2(m_old - m_new) * acc + jnp.dot(vt, p, preferred_element_type=F32)
                new.append((m_new, acc))
            return tuple(new)

        init = tuple((jnp.full((1, tq), NEG_BIG, F32), jnp.zeros((V_ROWS, tq), F32)) for _ in heads)
        for _, acc in lax.fori_loop(0, n_keys // tk, body, init):
            outs.append(acc[:HEAD_DIM] / acc[HEAD_DIM:HEAD_DIM + 1])

    for c in range(N_Q_HEADS // 2):
        pair = jnp.concatenate([outs[2 * c], outs[2 * c + 1]], axis=0)
        o_ref[0, :, c * LANES:(c + 1) * LANES] = pair.T.astype(BF16)


def _attention(q, k_all, vt_aug, tq, tk, heads_per_pass):
    bsz, n, _ = q.shape
    n_keys = k_all.shape[1]
    return pl.pallas_call(
        functools.partial(_attn_kernel, tk=tk, heads_per_pass=heads_per_pass),
        grid=(bsz, n // tq),
        in_specs=[_row_spec(tq, ATTN_WIDTH),
                  pl.BlockSpec((1, n_keys, KV_WIDTH), lambda b, i: (b, 0, 0)),
                  pl.BlockSpec((1, N_KV_HEADS, V_ROWS, n_keys), lambda b, i: (b, 0, 0, 0))],
        out_specs=_row_spec(tq, ATTN_WIDTH),
        out_shape=jax.ShapeDtypeStruct((bsz, n, ATTN_WIDTH), BF16),
        compiler_params=_cparams("parallel", "parallel"),
        name="attention",
    )(q, k_all, vt_aug)


def _values_transposed(v_all):
    bsz, n_keys, _ = v_all.shape
    vt = jnp.transpose(v_all.reshape(bsz, n_keys, N_KV_HEADS, HEAD_DIM), (0, 2, 3, 1))
    ones = jnp.ones((bsz, N_KV_HEADS, 1, n_keys), v_all.dtype)
    zeros = jnp.zeros((bsz, N_KV_HEADS, V_ROWS - HEAD_DIM - 1, n_keys), v_all.dtype)
    return jnp.concatenate([vt, ones, zeros], axis=2)


def _scan_steps(n_rows):
    return max(1, math.ceil(math.log2(n_rows)))


def _ssm_prep_kernel(lre_ref, lim_ref, ldt_ref, bt_ref, bts_ref, c_ref, cs_ref, d_ref,
                     win_o, coutt_o, kcat_o, p_o, q_o, *, n_steps):
    lane = lax.broadcasted_iota(jnp.int32, (1, STATE_LANES), 1)
    sign_lo = jnp.where(lane < SSM_STATE, -1.0, 1.0)
    k_idx = lax.broadcasted_iota(jnp.int32, (CHUNK, STATE_LANES), 0).astype(F32)

    def outer(pw, pws, mat, mats):
        full = pw[:, None, :] * mat[None, :, :] + pws[:, None, :] * mats[None, :, :]
        return full.reshape(CHUNK * mat.shape[0], STATE_LANES)

    kms = []
    for d in range(2):
        lre = lre_ref[0, d:d + 1, :]
        lim = lim_ref[0, d:d + 1, :]
        dt = jnp.exp(ldt_ref[0, d:d + 1, :])

        def power(expo):
            mag = jnp.exp(expo * dt * lre)
            ang = expo * dt * lim
            return mag * jnp.cos(ang), mag * jnp.sin(ang) * sign_lo

        a_re, a_ims = power(jnp.ones((1, STATE_LANES), F32))
        a_im = a_ims * sign_lo
        nr, ni = a_re - 1.0, a_im
        den = lre * lre + lim * lim
        cr = (nr * lre + ni * lim) / den
        ci = (ni * lre - nr * lim) / den
        cis = ci * sign_lo
        bbar = cr * bt_ref[0, d] + cis * bts_ref[0, d]
        bbar_s = cr * bts_ref[0, d] - cis * bt_ref[0, d]
        cmat, cmat_s = c_ref[0, d], cs_ref[0, d]

        asc, ascs = power(k_idx)
        desc, descs = power(CHUNK - 1.0 - k_idx)
        if d == 0:
            state_in = outer(desc, descs, bbar, bbar_s)
            taps = outer(asc, ascs, bbar, bbar_s)
            ro, ros = power(k_idx + 1.0)
        else:
            state_in = outer(asc, ascs, bbar, bbar_s)
            taps = outer(desc, descs, bbar, bbar_s)
            ro, ros = power(CHUNK - k_idx)
        state_out = outer(ro, ros, cmat, cmat_s) * (-sign_lo)
        win_o[0, :, d * STATE_LANES:(d + 1) * STATE_LANES] = state_in
        coutt_o[0, :, d * STATE_LANES:(d + 1) * STATE_LANES] = state_out
        kms.append(lax.dot_general(taps, cmat * (-sign_lo), (((1,), (1,)), ((), ())),
                                   preferred_element_type=F32, precision=lax.Precision.HIGHEST))

        sr, sis = power(jnp.full((1, STATE_LANES), float(CHUNK), F32))
        for k in range(n_steps):
            p_o[0, k:k + 1, d * STATE_LANES:(d + 1) * STATE_LANES] = sr
            q_o[0, k:k + 1, d * STATE_LANES:(d + 1) * STATE_LANES] = sis
            si = sis * sign_lo
            sr, sis = sr * sr - si * si, 2.0 * sr * si * sign_lo

    km_f, km_b = kms
    rr = lax.broadcasted_iota(jnp.int32, (SSM_GROUP, SSM_GROUP), 0)
    cc = lax.broadcasted_iota(jnp.int32, (SSM_GROUP, SSM_GROUP), 1)
    skip = jnp.where(rr == cc, d_ref[0], 0.0)
    edge = (CHUNK - 1) * SSM_GROUP
    kcat_o[0, :edge, :] = km_b[:edge]
    kcat_o[0, edge:edge + SSM_GROUP, :] = km_b[edge:] + km_f[:SSM_GROUP] + skip
    kcat_o[0, edge + SSM_GROUP:, :] = km_f[SSM_GROUP:]


def _ssm_prep(lam_re, lam_im, log_dt, b_re, b_im, c_re, c_im, d_skip, n_steps):
    g = N_SSM_GROUPS
    dup = lambda a: jnp.concatenate([a, a], axis=-1)
    lre = dup(jnp.swapaxes(lam_re, 0, 1))
    lim = dup(jnp.swapaxes(lam_im, 0, 1))
    ldt = jnp.broadcast_to(jnp.swapaxes(log_dt, 0, 1)[..., None], (g, 2, STATE_LANES))
    btr = jnp.transpose(b_re, (1, 0, 3, 2))
    bti = jnp.transpose(b_im, (1, 0, 3, 2))
    bt, bts = jnp.concatenate([btr, bti], -1), jnp.concatenate([bti, btr], -1)
    cr, ci = jnp.swapaxes(c_re, 0, 1), jnp.swapaxes(c_im, 0, 1)
    cm, cms = jnp.concatenate([cr, ci], -1), jnp.concatenate([ci, cr], -1)
    dsk = d_skip.reshape(g, 1, SSM_GROUP)
    n_kcat = (2 * CHUNK - 1) * SSM_GROUP
    n_pad = 16
    vec = pl.BlockSpec((1, 2, STATE_LANES), lambda i: (i, 0, 0))
    mat = pl.BlockSpec((1, 2, SSM_GROUP, STATE_LANES), lambda i: (i, 0, 0, 0))
    sq = pl.BlockSpec((1, CHUNK_LANES, CHUNK_LANES), lambda i: (i, 0, 0))
    mult = pl.BlockSpec((1, n_pad, 2 * STATE_LANES), lambda i: (i, 0, 0))
    return pl.pallas_call(
        functools.partial(_ssm_prep_kernel, n_steps=n_steps),
        grid=(g,),
        in_specs=[vec, vec, vec, mat, mat, mat, mat,
                  pl.BlockSpec((1, 1, SSM_GROUP), lambda i: (i, 0, 0))],
        out_specs=[sq, sq, pl.BlockSpec((1, n_kcat, SSM_GROUP), lambda i: (i, 0, 0)), mult, mult],
        out_shape=[jax.ShapeDtypeStruct((g, CHUNK_LANES, CHUNK_LANES), F32),
                   jax.ShapeDtypeStruct((g, CHUNK_LANES, CHUNK_LANES), F32),
                   jax.ShapeDtypeStruct((g, n_kcat, SSM_GROUP), F32),
                   jax.ShapeDtypeStruct((g, n_pad, 2 * STATE_LANES), F32),
                   jax.ShapeDtypeStruct((g, n_pad, 2 * STATE_LANES), F32)],
        compiler_params=_cparams("parallel"),
        name="ssm_prep",
    )(lre, lim, ldt, bt, bts, cm, cms, dsk)


def _ssm_kernel(u_ref, win_ref, m_ref, cout_ref, p_ref, q_ref, y_o, *, n_ctx_rows, n_steps):
    u = u_ref[0, 0]
    rows = u.shape[0]
    x = jnp.dot(u, win_ref[0], preferred_element_type=F32)
    xf = x[:, :STATE_LANES]
    xb = jnp.concatenate([x[n_ctx_rows:, STATE_LANES:], x[:n_ctx_rows, STATE_LANES:]], axis=0)
    row = lax.broadcasted_iota(jnp.int32, (rows, STATE_LANES), 0)
    for k in range(n_steps):
        s = 1 << k
        pf, qf = p_ref[0, k:k + 1, :STATE_LANES], q_ref[0, k:k + 1, :STATE_LANES]
        pb, qb = p_ref[0, k:k + 1, STATE_LANES:], q_ref[0, k:k + 1, STATE_LANES:]
        sf = jnp.where(row >= s, pltpu.roll(xf, s, 0), 0.0)
        xf = xf + pf * sf + qf * pltpu.roll(sf, SSM_STATE, 1)
        sb = jnp.where(row < rows - s, pltpu.roll(xb, rows - s, 0), 0.0)
        xb = xb + pb * sb + qb * pltpu.roll(sb, SSM_STATE, 1)
    prev_f = pltpu.roll(xf, 1, 0)[n_ctx_rows:]
    next_b = pltpu.roll(xb, rows - 1, 0)[:rows - n_ctx_rows]
    st = jnp.concatenate([prev_f, next_b], axis=1).astype(BF16)
    y = jnp.dot(u[n_ctx_rows:], m_ref[0], preferred_element_type=F32)
    y = y + jnp.dot(st, cout_ref[0], preferred_element_type=F32)
    y_o[0, 0] = y.astype(BF16)


def _ssm(u_cf, win, m, cout, p, q, n_ctx_rows, n_steps):
    bsz, g, rows, _ = u_cf.shape
    sq = pl.BlockSpec((1, CHUNK_LANES, CHUNK_LANES), lambda b, i: (i, 0, 0))
    mult = pl.BlockSpec((1, p.shape[1], p.shape[2]), lambda b, i: (i, 0, 0))
    return pl.pallas_call(
        functools.partial(_ssm_kernel, n_ctx_rows=n_ctx_rows, n_steps=n_steps),
        grid=(bsz, g),
        in_specs=[pl.BlockSpec((1, 1, rows, CHUNK_LANES), lambda b, i: (b, i, 0, 0)),
                  sq, sq, sq, mult, mult],
        out_specs=pl.BlockSpec((1, 1, rows - n_ctx_rows, CHUNK_LANES), lambda b, i: (b, i, 0, 0)),
        out_shape=jax.ShapeDtypeStruct((bsz, g, rows - n_ctx_rows, CHUNK_LANES), BF16),
        compiler_params=_cparams("parallel", "parallel"),
        name="ssm",
    )(u_cf, win, m, cout, p, q)


def _final_kernel(x_ref, at_ref, ga_ref, y_ref, gs_ref, gate_ref, wglu_ref, bglu_ref, wout_ref,
                  pg_ref, o_ref):
    y = jax.nn.gelu(y_ref[0].astype(F32))
    t = jnp.dot(y.astype(BF16), wglu_ref[...], preferred_element_type=F32) + bglu_ref[...]
    ssm = y * jax.nn.sigmoid(t)
    a = (at_ref[0].astype(F32) * ga_ref[0].astype(F32)).astype(BF16)
    s = (ssm * gs_ref[0].astype(F32)).astype(BF16)
    out = jnp.dot(a, wout_ref[:ATTN_WIDTH, :], preferred_element_type=F32)
    out = out + jnp.dot(s, wout_ref[ATTN_WIDTH:, :], preferred_element_type=F32)
    ms = jnp.mean(out * out, axis=-1, keepdims=True)
    r = out * lax.rsqrt(ms + NORM_EPS) * pg_ref[...]
    o_ref[0] = x_ref[0] + gate_ref[0] * r


def _final(x, attn, ga, y, gs, gate, wglu_bf, b_glu, wout_bf, post_g, tm):
    bsz, n, d = x.shape
    return pl.pallas_call(
        _final_kernel,
        grid=(bsz, n // tm),
        in_specs=[_row_spec(tm, d), _row_spec(tm, ATTN_WIDTH), _row_spec(tm, ATTN_WIDTH),
                  _row_spec(tm, SSM_WIDTH), _row_spec(tm, SSM_WIDTH), _batch_vec_spec(d),
                  _const_spec((SSM_WIDTH, SSM_WIDTH)), _const_spec((1, SSM_WIDTH)),
                  _const_spec((d, d)), _const_spec((1, d))],
        out_specs=_row_spec(tm, d),
        out_shape=jax.ShapeDtypeStruct((bsz, n, d), F32),
        compiler_params=_cparams("parallel", "parallel"),
        name="final",
    )(x, attn, ga, y, gs, gate, wglu_bf, b_glu, wout_bf, post_g)


def _rope_tables(n_lat):
    t = jnp.arange(n_lat, dtype=jnp.int32)
    row_pos = (t // GRID_W).astype(F32)
    col_pos = (t % GRID_W).astype(F32)
    inv_freq = ROPE_THETA ** (-jnp.arange(ROPE_FREQS, dtype=F32) / ROPE_FREQS)
    ang_r = row_pos[:, None] * inv_freq
    ang_c = col_pos[:, None] * inv_freq
    cos = jnp.concatenate([jnp.cos(ang_r)] * 2 + [jnp.cos(ang_c)] * 2, axis=1)
    sin = jnp.concatenate([-jnp.sin(ang_r), jnp.sin(ang_r), -jnp.sin(ang_c), jnp.sin(ang_c)], axis=1)
    return jnp.tile(cos, (1, LANES // HEAD_DIM)), jnp.tile(sin, (1, LANES // HEAD_DIM))


def _toeplitz(kcat):
    j = jnp.arange(CHUNK)
    lag = j[None, :] - j[:, None] + CHUNK - 1
    blocks = kcat[:, lag]
    return jnp.transpose(blocks, (0, 1, 3, 2, 4)).reshape(N_SSM_GROUPS, CHUNK_LANES, CHUNK_LANES)


def _to_chunk_major(u):
    bsz, n, _ = u.shape
    r = u.reshape(bsz, n // CHUNK, CHUNK, N_SSM_GROUPS, SSM_GROUP)
    return jnp.transpose(r, (0, 3, 1, 2, 4)).reshape(bsz, N_SSM_GROUPS, n // CHUNK, CHUNK_LANES)


def _from_chunk_major(y):
    bsz, g, rows, _ = y.shape
    r = y.reshape(bsz, g, rows, CHUNK, SSM_GROUP)
    return jnp.transpose(r, (0, 2, 3, 1, 4)).reshape(bsz, rows * CHUNK, SSM_WIDTH)


def _layer(x, ctx, c, c_ctx, w_ada, b_ada, pre_g, post_g, w_in, q_g, k_g, lam_re, lam_im, log_dt,
           b_re, b_im, c_re, c_im, d_skip, w_glu, b_glu, w_out):
    bsz, n_lat, d = x.shape
    n_ctx = ctx.shape[1]
    assert n_lat % 512 == 0 and n_ctx % 256 == 0 and n_lat % GRID_W == 0

    cvecs = jnp.zeros((8, d), F32).at[:bsz].set(c).at[bsz].set(c_ctx)
    mod = _adaln(cvecs, w_ada, b_ada)
    shift, scale, gate = (mod[:, i * d:(i + 1) * d] for i in range(3))
    lat = lambda a: a[:bsz].reshape(bsz, 1, d)
    cvec = lambda a: a[bsz].reshape(1, 1, d)

    w_bf = w_in.astype(BF16)
    head = jnp.arange(ATTN_WIDTH) // HEAD_DIM
    bd = (head[:, None] == head[None, :]).astype(BF16) * (1.0 / HEAD_DIM)
    qg = jnp.tile(q_g, N_Q_HEADS).reshape(1, ATTN_WIDTH)
    kg = jnp.tile(k_g, N_KV_HEADS).reshape(1, KV_WIDTH)
    cos, sin = _rope_tables(n_lat)
    pre = pre_g.reshape(1, d)

    q, k_l, v_l, ga, u_l, gs = _inproj_latent(x, lat(shift), lat(scale), pre, w_bf, qg, kg,
                                              cos, sin, bd, tm=512)
    k_c, v_c, u_c = _inproj_context(ctx, cvec(shift), cvec(scale), pre, w_bf, kg, bd, tm=256)

    k_all = jnp.concatenate([k_c, k_l], axis=1)
    v_all = jnp.concatenate([v_c, v_l], axis=1)
    attn = _attention(q, k_all, _values_transposed(v_all), tq=256, tk=768, heads_per_pass=8)

    rows = (n_ctx + n_lat) // CHUNK
    n_steps = _scan_steps(rows)
    win, coutt, kcat, p, qm = _ssm_prep(lam_re, lam_im, log_dt, b_re, b_im, c_re, c_im, d_skip, n_steps)
    m_op = _toeplitz(kcat.reshape(N_SSM_GROUPS, 2 * CHUNK - 1, SSM_GROUP, SSM_GROUP))
    u_cf = _to_chunk_major(jnp.concatenate([u_c, u_l], axis=1))
    y_cf = _ssm(u_cf, win.astype(BF16), m_op.astype(BF16), jnp.swapaxes(coutt, 1, 2).astype(BF16),
                p, qm, n_ctx // CHUNK, n_steps)
    y = _from_chunk_major(y_cf)

    return _final(x, attn, ga, y, gs, lat(gate), w_glu.astype(BF16), b_glu.reshape(1, SSM_WIDTH),
                  w_out.astype(BF16), post_g.reshape(1, d), tm=512)


def kernel(x, c, ctx, c_ctx, w_ada, b_ada, pre_norm, post_norm, w_in, q_norm, k_norm, ssm_lam_re,
           ssm_lam_im, ssm_log_dt, ssm_b_re, ssm_b_im, ssm_c_re, ssm_c_im, ssm_d, w_glu, b_glu, w_out):
    depth = w_ada.shape[0]
    assert depth == 1, "context stream update between layers is not implemented"
    return _layer(x, ctx, c, c_ctx, w_ada[0], b_ada[0], pre_norm[0], post_norm[0], w_in[0], q_norm[0],
                  k_norm[0], ssm_lam_re[0], ssm_lam_im[0], ssm_log_dt[0], ssm_b_re[0], ssm_b_im[0],
                  ssm_c_re[0], ssm_c_im[0], ssm_d[0], w_glu[0], b_glu[0], w_out[0])
```

```python
import functools
import math

import jax
import jax.numpy as jnp
from jax import lax
from jax.experimental import pallas as pl
from jax.experimental.pallas import tpu as pltpu

F32 = jnp.float32
BF16 = jnp.bfloat16

D_MODEL = 1024
HEAD_DIM = 64
N_Q_HEADS = 8
N_KV_HEADS = 2
ATTN_WIDTH = N_Q_HEADS * HEAD_DIM
KV_WIDTH = N_KV_HEADS * HEAD_DIM
SSM_WIDTH = 512
SSM_GROUP = 16
N_SSM_GROUPS = SSM_WIDTH // SSM_GROUP
SSM_STATE = 64
GRID_W = 64
ROPE_THETA = 10000.0
ROPE_FREQS = 16
NORM_EPS = 1e-6
ATTN_SCALE = HEAD_DIM ** -0.5
LOG2_E = math.log2(math.e)
Q_END = ATTN_WIDTH
K_END = Q_END + KV_WIDTH
V_END = K_END + KV_WIDTH
GA_END = V_END + ATTN_WIDTH
U_END = GA_END + SSM_WIDTH
IN_WIDTH = U_END + SSM_WIDTH

CHUNK = 16
CHUNK_LANES = CHUNK * SSM_GROUP
LANES = 128
STATE_LANES = 2 * SSM_STATE
VMEM_LIMIT = 48 * 1024 * 1024
NEG_BIG = -1e30


def _cparams(*sem):
    return pltpu.CompilerParams(dimension_semantics=sem, vmem_limit_bytes=VMEM_LIMIT)


def _adaln_kernel(c_ref, w_ref, b_ref, o_ref):
    c = c_ref[...]
    s = c * jax.nn.sigmoid(c)
    o_ref[...] = jnp.dot(s, w_ref[...], preferred_element_type=F32,
                         precision=lax.Precision.HIGHEST) + b_ref[...]


def _adaln(cvecs, w_ada, b_ada):
    rows, d = cvecs.shape
    n = w_ada.shape[1]
    tn = 512
    return pl.pallas_call(
        _adaln_kernel,
        grid=(n // tn,),
        in_specs=[pl.BlockSpec((rows, d), lambda j: (0, 0)),
                  pl.BlockSpec((d, tn), lambda j: (0, j)),
                  pl.BlockSpec((1, tn), lambda j: (0, j))],
        out_specs=pl.BlockSpec((rows, tn), lambda j: (0, j)),
        out_shape=jax.ShapeDtypeStruct((rows, n), F32),
        compiler_params=_cparams("arbitrary"),
        name="adaln",
    )(cvecs, w_ada, b_ada.reshape(1, n))


def _head_mean_sq(z, bd):
    return jnp.dot((z * z).astype(BF16), bd, preferred_element_type=F32)


def _swap16(x):
    w = x.shape[1]
    lane = lax.broadcasted_iota(jnp.int32, x.shape, 1)
    return jnp.where((lane & 16) == 0, pltpu.roll(x, w - 16, 1), pltpu.roll(x, 16, 1))


def _rope(x, cos, sin_signed):
    cols = []
    for c in range(x.shape[1] // LANES):
        xc = x[:, c * LANES:(c + 1) * LANES]
        cols.append(xc * cos + _swap16(xc) * sin_signed)
    return cols[0] if len(cols) == 1 else jnp.concatenate(cols, axis=1)


def _silu(z):
    return z * jax.nn.sigmoid(z)


def _modulated_input(x_ref, sh_ref, sc_ref, pg_ref):
    x = x_ref[0]
    ms = jnp.mean(x * x, axis=-1, keepdims=True)
    xn = x * lax.rsqrt(ms + NORM_EPS) * pg_ref[...]
    return (xn * (1.0 + sc_ref[0]) + sh_ref[0]).astype(BF16)


def _inproj_latent_kernel(x_ref, sh_ref, sc_ref, pg_ref, w_ref, qg_ref, kg_ref, cos_ref, sin_ref,
                          bd_ref, q_o, k_o, v_o, ga_o, u_o, gs_o):
    h = _modulated_input(x_ref, sh_ref, sc_ref, pg_ref)

    def proj(a, b):
        return jnp.dot(h, w_ref[:, a:b], preferred_element_type=F32)

    cos = cos_ref[...]
    sin = sin_ref[...]
    zq = proj(0, Q_END)
    qn = zq * lax.rsqrt(_head_mean_sq(zq, bd_ref[...]) + NORM_EPS) * qg_ref[...]
    q_o[0] = (_rope(qn, cos, sin) * (ATTN_SCALE * LOG2_E)).astype(BF16)
    zk = proj(Q_END, K_END)
    kn = zk * lax.rsqrt(_head_mean_sq(zk, bd_ref[:KV_WIDTH, :KV_WIDTH]) + NORM_EPS) * kg_ref[...]
    k_o[0] = _rope(kn, cos, sin).astype(BF16)
    v_o[0] = proj(K_END, V_END).astype(BF16)
    ga_o[0] = _silu(proj(V_END, GA_END)).astype(BF16)
    u_o[0] = proj(GA_END, U_END).astype(BF16)
    gs_o[0] = _silu(proj(U_END, IN_WIDTH)).astype(BF16)


def _inproj_context_kernel(x_ref, sh_ref, sc_ref, pg_ref, w_ref, kg_ref, bd_ref, k_o, v_o, u_o):
    h = _modulated_input(x_ref, sh_ref, sc_ref, pg_ref)

    def proj(a, b):
        return jnp.dot(h, w_ref[:, a:b], preferred_element_type=F32)

    zk = proj(Q_END, K_END)
    kn = zk * lax.rsqrt(_head_mean_sq(zk, bd_ref[:KV_WIDTH, :KV_WIDTH]) + NORM_EPS) * kg_ref[...]
    k_o[0] = kn.astype(BF16)
    v_o[0] = proj(K_END, V_END).astype(BF16)
    u_o[0] = proj(GA_END, U_END).astype(BF16)


def _row_spec(tm, width):
    return pl.BlockSpec((1, tm, width), lambda b, i: (b, i, 0))


def _const_spec(shape):
    return pl.BlockSpec(shape, lambda b, i: (0,) * len(shape))


def _batch_vec_spec(width):
    return pl.BlockSpec((1, 1, width), lambda b, i: (b, 0, 0))


def _inproj_latent(x, shift, scale, pre_g, w_bf, qg, kg, cos, sin, bd, tm):
    bsz, n, d = x.shape
    outs = [jax.ShapeDtypeStruct((bsz, n, w), BF16)
            for w in (ATTN_WIDTH, KV_WIDTH, KV_WIDTH, ATTN_WIDTH, SSM_WIDTH, SSM_WIDTH)]
    return pl.pallas_call(
        _inproj_latent_kernel,
        grid=(bsz, n // tm),
        in_specs=[_row_spec(tm, d), _batch_vec_spec(d), _batch_vec_spec(d), _const_spec((1, d)),
                  _const_spec((d, IN_WIDTH)), _const_spec((1, ATTN_WIDTH)), _const_spec((1, KV_WIDTH)),
                  pl.BlockSpec((tm, LANES), lambda b, i: (i, 0)),
                  pl.BlockSpec((tm, LANES), lambda b, i: (i, 0)),
                  _const_spec((ATTN_WIDTH, ATTN_WIDTH))],
        out_specs=[_row_spec(tm, o.shape[2]) for o in outs],
        out_shape=outs,
        compiler_params=_cparams("parallel", "parallel"),
        name="inproj_latent",
    )(x, shift, scale, pre_g, w_bf, qg, kg, cos, sin, bd)


def _inproj_context(ctx, shift, scale, pre_g, w_bf, kg, bd, tm):
    bsz, n, d = ctx.shape
    outs = [jax.ShapeDtypeStruct((bsz, n, w), BF16) for w in (KV_WIDTH, KV_WIDTH, SSM_WIDTH)]
    vec = pl.BlockSpec((1, 1, d), lambda b, i: (0, 0, 0))
    return pl.pallas_call(
        _inproj_context_kernel,
        grid=(bsz, n // tm),
        in_specs=[_row_spec(tm, d), vec, vec, _const_spec((1, d)), _const_spec((d, IN_WIDTH)),
                  _const_spec((1, KV_WIDTH)), _const_spec((ATTN_WIDTH, ATTN_WIDTH))],
        out_specs=[_row_spec(tm, o.shape[2]) for o in outs],
        out_shape=outs,
        compiler_params=_cparams("parallel", "parallel"),
        name="inproj_context",
    )(ctx, shift, scale, pre_g, w_bf, kg, bd)


V_ROWS = HEAD_DIM + 16


def _attn_kernel(q_ref, k_ref, vt_ref, o_ref, *, tk, sub, lookahead, heads_per_pass):
    tq = q_ref.shape[1]
    n_keys = k_ref.shape[1]
    group = N_Q_HEADS // N_KV_HEADS
    q = q_ref[0]
    low = lax.broadcasted_iota(jnp.int32, (tq, LANES), 1) < HEAD_DIM

    def widen(h):
        col = q[:, (h // 2) * LANES:(h // 2 + 1) * LANES].astype(F32)
        dst_high = h // group == 1
        if (h % 2 == 1) != dst_high:
            col = pltpu.roll(col, HEAD_DIM, 1)
        return jnp.where(low != dst_high, col, 0.0).astype(BF16)

    outs = []
    for h0 in range(0, N_Q_HEADS, heads_per_pass):
        heads = list(range(h0, h0 + heads_per_pass))
        q_wide = [widen(h) for h in heads]

        def body(t, carry, heads=heads, q_wide=q_wide):
            tasks = [(j, i) for j in range(tk // sub) for i in range(len(heads))]
            state = list(carry)
            scores = {}

            def keys_at(j):
                return pl.ds(pl.multiple_of(t * tk + j * sub, sub), sub)

            def issue(n):
                j, i = tasks[n]
                scores[n] = lax.dot_general(k_ref[0, keys_at(j), :], q_wide[i], (((1,), (1,)), ((), ())),
                                            preferred_element_type=F32)

            def consume(n):
                j, i = tasks[n]
                s = scores.pop(n)
                m_old, acc = state[i]
                vt = vt_ref[0, heads[i] // group, :, keys_at(j)]
                m_new = jnp.maximum(m_old, jnp.max(s, axis=0, keepdims=True))
                p = jnp.exp2((s - m_new).astype(BF16))
                acc = jnp.exp2(m_old - m_new) * acc + jnp.dot(vt, p, preferred_element_type=F32)
                state[i] = (m_new, acc)

            for n in range(len(tasks) + lookahead):
                if n < len(tasks):
                    issue(n)
                if n >= lookahead:
                    consume(n - lookahead)
            return tuple(state)

        init = tuple((jnp.full((1, tq), NEG_BIG, F32), jnp.zeros((V_ROWS, tq), F32)) for _ in heads)
        for _, acc in lax.fori_loop(0, n_keys // tk, body, init):
            outs.append(acc[:HEAD_DIM] / acc[HEAD_DIM:HEAD_DIM + 1])

    for c in range(N_Q_HEADS // 2):
        pair = jnp.concatenate([outs[2 * c], outs[2 * c + 1]], axis=0)
        o_ref[0, :, c * LANES:(c + 1) * LANES] = pair.T.astype(BF16)


def _attention(q, k_all, vt_aug, tq, tk, sub, lookahead, heads_per_pass):
    bsz, n, _ = q.shape
    n_keys = k_all.shape[1]
    assert n_keys % tk == 0 and tk % sub == 0 and n % tq == 0
    return pl.pallas_call(
        functools.partial(_attn_kernel, tk=tk, sub=sub, lookahead=lookahead,
                          heads_per_pass=heads_per_pass),
        grid=(bsz, n // tq),
        in_specs=[_row_spec(tq, ATTN_WIDTH),
                  pl.BlockSpec((1, n_keys, KV_WIDTH), lambda b, i: (b, 0, 0)),
                  pl.BlockSpec((1, N_KV_HEADS, V_ROWS, n_keys), lambda b, i: (b, 0, 0, 0))],
        out_specs=_row_spec(tq, ATTN_WIDTH),
        out_shape=jax.ShapeDtypeStruct((bsz, n, ATTN_WIDTH), BF16),
        compiler_params=_cparams("parallel", "parallel"),
        name="attention",
    )(q, k_all, vt_aug)


def _values_transposed(v_all):
    bsz, n_keys, _ = v_all.shape
    vt = jnp.transpose(v_all.reshape(bsz, n_keys, N_KV_HEADS, HEAD_DIM), (0, 2, 3, 1))
    ones = jnp.ones((bsz, N_KV_HEADS, 1, n_keys), v_all.dtype)
    zeros = jnp.zeros((bsz, N_KV_HEADS, V_ROWS - HEAD_DIM - 1, n_keys), v_all.dtype)
    return jnp.concatenate([vt, ones, zeros], axis=2)


def _scan_steps(n_rows):
    return max(1, math.ceil(math.log2(n_rows)))


def _ssm_prep_kernel(lre_ref, lim_ref, ldt_ref, bt_ref, bts_ref, c_ref, cs_ref, d_ref,
                     win_o, coutt_o, kcat_o, p_o, q_o, *, n_steps):
    lane = lax.broadcasted_iota(jnp.int32, (1, STATE_LANES), 1)
    sign_lo = jnp.where(lane < SSM_STATE, -1.0, 1.0)
    k_idx = lax.broadcasted_iota(jnp.int32, (CHUNK, STATE_LANES), 0).astype(F32)

    def outer(pw, pws, mat, mats):
        full = pw[:, None, :] * mat[None, :, :] + pws[:, None, :] * mats[None, :, :]
        return full.reshape(CHUNK * mat.shape[0], STATE_LANES)

    kms = []
    for d in range(2):
        lre = lre_ref[0, d:d + 1, :]
        lim = lim_ref[0, d:d + 1, :]
        dt = jnp.exp(ldt_ref[0, d:d + 1, :])

        def power(expo):
            mag = jnp.exp(expo * dt * lre)
            ang = expo * dt * lim
            return mag * jnp.cos(ang), mag * jnp.sin(ang) * sign_lo

        a_re, a_ims = power(jnp.ones((1, STATE_LANES), F32))
        a_im = a_ims * sign_lo
        nr, ni = a_re - 1.0, a_im
        den = lre * lre + lim * lim
        cr = (nr * lre + ni * lim) / den
        ci = (ni * lre - nr * lim) / den
        cis = ci * sign_lo
        bbar = cr * bt_ref[0, d] + cis * bts_ref[0, d]
        bbar_s = cr * bts_ref[0, d] - cis * bt_ref[0, d]
        cmat, cmat_s = c_ref[0, d], cs_ref[0, d]

        asc, ascs = power(k_idx)
        desc, descs = power(CHUNK - 1.0 - k_idx)
        if d == 0:
            state_in = outer(desc, descs, bbar, bbar_s)
            taps = outer(asc, ascs, bbar, bbar_s)
            ro, ros = power(k_idx + 1.0)
        else:
            state_in = outer(asc, ascs, bbar, bbar_s)
            taps = outer(desc, descs, bbar, bbar_s)
            ro, ros = power(CHUNK - k_idx)
        state_out = outer(ro, ros, cmat, cmat_s) * (-sign_lo)
        win_o[0, :, d * STATE_LANES:(d + 1) * STATE_LANES] = state_in
        coutt_o[0, :, d * STATE_LANES:(d + 1) * STATE_LANES] = state_out
        kms.append(lax.dot_general(taps, cmat * (-sign_lo), (((1,), (1,)), ((), ())),
                                   preferred_element_type=F32, precision=lax.Precision.HIGHEST))

        sr, sis = power(jnp.full((1, STATE_LANES), float(CHUNK), F32))
        for k in range(n_steps):
            p_o[0, k:k + 1, d * STATE_LANES:(d + 1) * STATE_LANES] = sr
            q_o[0, k:k + 1, d * STATE_LANES:(d + 1) * STATE_LANES] = sis
            si = sis * sign_lo
            sr, sis = sr * sr - si * si, 2.0 * sr * si * sign_lo

    km_f, km_b = kms
    rr = lax.broadcasted_iota(jnp.int32, (SSM_GROUP, SSM_GROUP), 0)
    cc = lax.broadcasted_iota(jnp.int32, (SSM_GROUP, SSM_GROUP), 1)
    skip = jnp.where(rr == cc, d_ref[0], 0.0)
    edge = (CHUNK - 1) * SSM_GROUP
    kcat_o[0, :edge, :] = km_b[:edge]
    kcat_o[0, edge:edge + SSM_GROUP, :] = km_b[edge:] + km_f[:SSM_GROUP] + skip
    kcat_o[0, edge + SSM_GROUP:, :] = km_f[SSM_GROUP:]


def _ssm_prep(lam_re, lam_im, log_dt, b_re, b_im, c_re, c_im, d_skip, n_steps):
    g = N_SSM_GROUPS
    dup = lambda a: jnp.concatenate([a, a], axis=-1)
    lre = dup(jnp.swapaxes(lam_re, 0, 1))
    lim = dup(jnp.swapaxes(lam_im, 0, 1))
    ldt = jnp.broadcast_to(jnp.swapaxes(log_dt, 0, 1)[..., None], (g, 2, STATE_LANES))
    btr = jnp.transpose(b_re, (1, 0, 3, 2))
    bti = jnp.transpose(b_im, (1, 0, 3, 2))
    bt, bts = jnp.concatenate([btr, bti], -1), jnp.concatenate([bti, btr], -1)
    cr, ci = jnp.swapaxes(c_re, 0, 1), jnp.swapaxes(c_im, 0, 1)
    cm, cms = jnp.concatenate([cr, ci], -1), jnp.concatenate([ci, cr], -1)
    dsk = d_skip.reshape(g, 1, SSM_GROUP)
    n_kcat = (2 * CHUNK - 1) * SSM_GROUP
    n_pad = 16
    vec = pl.BlockSpec((1, 2, STATE_LANES), lambda i: (i, 0, 0))
    mat = pl.BlockSpec((1, 2, SSM_GROUP, STATE_LANES), lambda i: (i, 0, 0, 0))
    sq = pl.BlockSpec((1, CHUNK_LANES, CHUNK_LANES), lambda i: (i, 0, 0))
    mult = pl.BlockSpec((1, n_pad, 2 * STATE_LANES), lambda i: (i, 0, 0))
    return pl.pallas_call(
        functools.partial(_ssm_prep_kernel, n_steps=n_steps),
        grid=(g,),
        in_specs=[vec, vec, vec, mat, mat, mat, mat,
                  pl.BlockSpec((1, 1, SSM_GROUP), lambda i: (i, 0, 0))],
        out_specs=[sq, sq, pl.BlockSpec((1, n_kcat, SSM_GROUP), lambda i: (i, 0, 0)), mult, mult],
        out_shape=[jax.ShapeDtypeStruct((g, CHUNK_LANES, CHUNK_LANES), F32),
                   jax.ShapeDtypeStruct((g, CHUNK_LANES, CHUNK_LANES), F32),
                   jax.ShapeDtypeStruct((g, n_kcat, SSM_GROUP), F32),
                   jax.ShapeDtypeStruct((g, n_pad, 2 * STATE_LANES), F32),
                   jax.ShapeDtypeStruct((g, n_pad, 2 * STATE_LANES), F32)],
        compiler_params=_cparams("parallel"),
        name="ssm_prep",
    )(lre, lim, ldt, bt, bts, cm, cms, dsk)


def _ssm_kernel(u_ref, win_ref, m_ref, cout_ref, p_ref, q_ref, y_o, *, n_ctx_rows, n_steps):
    u = u_ref[0, 0]
    rows = u.shape[0]
    x = jnp.dot(u, win_ref[0], preferred_element_type=F32)
    xf = x[:, :STATE_LANES]
    xb = jnp.concatenate([x[n_ctx_rows:, STATE_LANES:], x[:n_ctx_rows, STATE_LANES:]], axis=0)
    row = lax.broadcasted_iota(jnp.int32, (rows, STATE_LANES), 0)
    for k in range(n_steps):
        s = 1 << k
        pf, qf = p_ref[0, k:k + 1, :STATE_LANES], q_ref[0, k:k + 1, :STATE_LANES]
        pb, qb = p_ref[0, k:k + 1, STATE_LANES:], q_ref[0, k:k + 1, STATE_LANES:]
        sf = jnp.where(row >= s, pltpu.roll(xf, s, 0), 0.0)
        xf = xf + pf * sf + qf * pltpu.roll(sf, SSM_STATE, 1)
        sb = jnp.where(row < rows - s, pltpu.roll(xb, rows - s, 0), 0.0)
        xb = xb + pb * sb + qb * pltpu.roll(sb, SSM_STATE, 1)
    prev_f = pltpu.roll(xf, 1, 0)[n_ctx_rows:]
    next_b = pltpu.roll(xb, rows - 1, 0)[:rows - n_ctx_rows]
    st = jnp.concatenate([prev_f, next_b], axis=1).astype(BF16)
    y = jnp.dot(u[n_ctx_rows:], m_ref[0], preferred_element_type=F32)
    y = y + jnp.dot(st, cout_ref[0], preferred_element_type=F32)
    y_o[0, 0] = y.astype(BF16)


def _ssm(u_cf, win, m, cout, p, q, n_ctx_rows, n_steps):
    bsz, g, rows, _ = u_cf.shape
    sq = pl.BlockSpec((1, CHUNK_LANES, CHUNK_LANES), lambda b, i: (i, 0, 0))
    mult = pl.BlockSpec((1, p.shape[1], p.shape[2]), lambda b, i: (i, 0, 0))
    return pl.pallas_call(
        functools.partial(_ssm_kernel, n_ctx_rows=n_ctx_rows, n_steps=n_steps),
        grid=(bsz, g),
        in_specs=[pl.BlockSpec((1, 1, rows, CHUNK_LANES), lambda b, i: (b, i, 0, 0)),
                  sq, sq, sq, mult, mult],
        out_specs=pl.BlockSpec((1, 1, rows - n_ctx_rows, CHUNK_LANES), lambda b, i: (b, i, 0, 0)),
        out_shape=jax.ShapeDtypeStruct((bsz, g, rows - n_ctx_rows, CHUNK_LANES), BF16),
        compiler_params=_cparams("parallel", "parallel"),
        name="ssm",
    )(u_cf, win, m, cout, p, q)


def _final_kernel(x_ref, at_ref, ga_ref, y_ref, gs_ref, gate_ref, wglu_ref, bglu_ref, wout_ref,
                  pg_ref, o_ref):
    y = jax.nn.gelu(y_ref[0].astype(F32))
    t = jnp.dot(y.astype(BF16), wglu_ref[...], preferred_element_type=F32) + bglu_ref[...]
    ssm = y * jax.nn.sigmoid(t)
    a = (at_ref[0].astype(F32) * ga_ref[0].astype(F32)).astype(BF16)
    s = (ssm * gs_ref[0].astype(F32)).astype(BF16)
    out = jnp.dot(a, wout_ref[:ATTN_WIDTH, :], preferred_element_type=F32)
    out = out + jnp.dot(s, wout_ref[ATTN_WIDTH:, :], preferred_element_type=F32)
    ms = jnp.mean(out * out, axis=-1, keepdims=True)
    r = out * lax.rsqrt(ms + NORM_EPS) * pg_ref[...]
    o_ref[0] = x_ref[0] + gate_ref[0] * r


def _final(x, attn, ga, y, gs, gate, wglu_bf, b_glu, wout_bf, post_g, tm):
    bsz, n, d = x.shape
    return pl.pallas_call(
        _final_kernel,
        grid=(bsz, n // tm),
        in_specs=[_row_spec(tm, d), _row_spec(tm, ATTN_WIDTH), _row_spec(tm, ATTN_WIDTH),
                  _row_spec(tm, SSM_WIDTH), _row_spec(tm, SSM_WIDTH), _batch_vec_spec(d),
                  _const_spec((SSM_WIDTH, SSM_WIDTH)), _const_spec((1, SSM_WIDTH)),
                  _const_spec((d, d)), _const_spec((1, d))],
        out_specs=_row_spec(tm, d),
        out_shape=jax.ShapeDtypeStruct((bsz, n, d), F32),
        compiler_params=_cparams("parallel", "parallel"),
        name="final",
    )(x, attn, ga, y, gs, gate, wglu_bf, b_glu, wout_bf, post_g)


def _rope_tables(n_lat):
    t = jnp.arange(n_lat, dtype=jnp.int32)
    row_pos = (t // GRID_W).astype(F32)
    col_pos = (t % GRID_W).astype(F32)
    inv_freq = ROPE_THETA ** (-jnp.arange(ROPE_FREQS, dtype=F32) / ROPE_FREQS)
    ang_r = row_pos[:, None] * inv_freq
    ang_c = col_pos[:, None] * inv_freq
    cos = jnp.concatenate([jnp.cos(ang_r)] * 2 + [jnp.cos(ang_c)] * 2, axis=1)
    sin = jnp.concatenate([-jnp.sin(ang_r), jnp.sin(ang_r), -jnp.sin(ang_c), jnp.sin(ang_c)], axis=1)
    return jnp.tile(cos, (1, LANES // HEAD_DIM)), jnp.tile(sin, (1, LANES // HEAD_DIM))


def _toeplitz(kcat):
    j = jnp.arange(CHUNK)
    lag = j[None, :] - j[:, None] + CHUNK - 1
    blocks = kcat[:, lag]
    return jnp.transpose(blocks, (0, 1, 3, 2, 4)).reshape(N_SSM_GROUPS, CHUNK_LANES, CHUNK_LANES)


def _to_chunk_major(u):
    bsz, n, _ = u.shape
    r = u.reshape(bsz, n // CHUNK, CHUNK, N_SSM_GROUPS, SSM_GROUP)
    return jnp.transpose(r, (0, 3, 1, 2, 4)).reshape(bsz, N_SSM_GROUPS, n // CHUNK, CHUNK_LANES)


def _from_chunk_major(y):
    bsz, g, rows, _ = y.shape
    r = y.reshape(bsz, g, rows, CHUNK, SSM_GROUP)
    return jnp.transpose(r, (0, 2, 3, 1, 4)).reshape(bsz, rows * CHUNK, SSM_WIDTH)


def _layer(x, ctx, c, c_ctx, w_ada, b_ada, pre_g, post_g, w_in, q_g, k_g, lam_re, lam_im, log_dt,
           b_re, b_im, c_re, c_im, d_skip, w_glu, b_glu, w_out):
    bsz, n_lat, d = x.shape
    n_ctx = ctx.shape[1]
    assert n_lat % 512 == 0 and n_ctx % 256 == 0 and n_lat % GRID_W == 0

    cvecs = jnp.zeros((8, d), F32).at[:bsz].set(c).at[bsz].set(c_ctx)
    mod = _adaln(cvecs, w_ada, b_ada)
    shift, scale, gate = (mod[:, i * d:(i + 1) * d] for i in range(3))
    lat = lambda a: a[:bsz].reshape(bsz, 1, d)
    cvec = lambda a: a[bsz].reshape(1, 1, d)

    w_bf = w_in.astype(BF16)
    head = jnp.arange(ATTN_WIDTH) // HEAD_DIM
    bd = (head[:, None] == head[None, :]).astype(BF16) * (1.0 / HEAD_DIM)
    qg = jnp.tile(q_g, N_Q_HEADS).reshape(1, ATTN_WIDTH)
    kg = jnp.tile(k_g, N_KV_HEADS).reshape(1, KV_WIDTH)
    cos, sin = _rope_tables(n_lat)
    pre = pre_g.reshape(1, d)

    q, k_l, v_l, ga, u_l, gs = _inproj_latent(x, lat(shift), lat(scale), pre, w_bf, qg, kg,
                                              cos, sin, bd, tm=512)
    k_c, v_c, u_c = _inproj_context(ctx, cvec(shift), cvec(scale), pre, w_bf, kg, bd, tm=256)

    k_all = jnp.concatenate([k_c, k_l], axis=1)
    v_all = jnp.concatenate([v_c, v_l], axis=1)
    attn = _attention(q, k_all, _values_transposed(v_all), tq=256, tk=2816, sub=256, lookahead=5,
                      heads_per_pass=8)

    rows = (n_ctx + n_lat) // CHUNK
    n_steps = _scan_steps(rows)
    win, coutt, kcat, p, qm = _ssm_prep(lam_re, lam_im, log_dt, b_re, b_im, c_re, c_im, d_skip, n_steps)
    m_op = _toeplitz(kcat.reshape(N_SSM_GROUPS, 2 * CHUNK - 1, SSM_GROUP, SSM_GROUP))
    u_cf = _to_chunk_major(jnp.concatenate([u_c, u_l], axis=1))
    y_cf = _ssm(u_cf, win.astype(BF16), m_op.astype(BF16), jnp.swapaxes(coutt, 1, 2).astype(BF16),
                p, qm, n_ctx // CHUNK, n_steps)
    y = _from_chunk_major(y_cf)

    return _final(x, attn, ga, y, gs, lat(gate), w_glu.astype(BF16), b_glu.reshape(1, SSM_WIDTH),
                  w_out.astype(BF16), post_g.reshape(1, d), tm=512)


def kernel(x, c, ctx, c_ctx, w_ada, b_ada, pre_norm, post_norm, w_in, q_norm, k_norm, ssm_lam_re,
           ssm_lam_im, ssm_log_dt, ssm_b_re, ssm_b_im, ssm_c_re, ssm_c_im, ssm_d, w_glu, b_glu, w_out):
    depth = w_ada.shape[0]
    assert depth == 1, "context stream update between layers is not implemented"
    return _layer(x, ctx, c, c_ctx, w_ada[0], b_ada[0], pre_norm[0], post_norm[0], w_in[0], q_norm[0],
                  k_norm[0], ssm_lam_re[0], ssm_lam_im[0], ssm_log_dt[0], ssm_b_re[0], ssm_b_im[0],
                  ssm_c_re[0], ssm_c_im[0], ssm_d[0], w_glu[0], b_glu[0], w_out[0])
```

```python
import functools
import math

import jax
import jax.numpy as jnp
from jax import lax
from jax.experimental import pallas as pl
from jax.experimental.pallas import tpu as pltpu

F32 = jnp.float32
BF16 = jnp.bfloat16

D_MODEL = 1024
HEAD_DIM = 64
N_Q_HEADS = 8
N_KV_HEADS = 2
ATTN_WIDTH = N_Q_HEADS * HEAD_DIM
KV_WIDTH = N_KV_HEADS * HEAD_DIM
SSM_WIDTH = 512
SSM_GROUP = 16
N_SSM_GROUPS = SSM_WIDTH // SSM_GROUP
SSM_STATE = 64
GRID_W = 64
ROPE_THETA = 10000.0
ROPE_FREQS = 16
NORM_EPS = 1e-6
ATTN_SCALE = HEAD_DIM ** -0.5
LOG2_E = math.log2(math.e)
Q_END = ATTN_WIDTH
K_END = Q_END + KV_WIDTH
V_END = K_END + KV_WIDTH
GA_END = V_END + ATTN_WIDTH
U_END = GA_END + SSM_WIDTH
IN_WIDTH = U_END + SSM_WIDTH

CHUNK = 16
CHUNK_LANES = CHUNK * SSM_GROUP
LANES = 128
GROUPS_PER_VREG = LANES // SSM_GROUP
STATE_LANES = 2 * SSM_STATE
V_ROWS = HEAD_DIM + 16
VMEM_LIMIT = 48 * 1024 * 1024
NEG_BIG = -1e30


def _cparams(*sem):
    return pltpu.CompilerParams(dimension_semantics=sem, vmem_limit_bytes=VMEM_LIMIT)


def _row_spec(tm, width):
    return pl.BlockSpec((1, tm, width), lambda b, i: (b, i, 0))


def _const_spec(shape):
    return pl.BlockSpec(shape, lambda b, i: (0,) * len(shape))


def _batch_vec_spec(width):
    return pl.BlockSpec((1, 1, width), lambda b, i: (b, 0, 0))


def _adaln_kernel(c_ref, w_ref, b_ref, o_ref):
    c = c_ref[...]
    s = c * jax.nn.sigmoid(c)
    o_ref[...] = jnp.dot(s, w_ref[...], preferred_element_type=F32,
                         precision=lax.Precision.HIGHEST) + b_ref[...]


def _adaln(cvecs, w_ada, b_ada):
    rows, d = cvecs.shape
    n = w_ada.shape[1]
    tn = 512
    return pl.pallas_call(
        _adaln_kernel,
        grid=(n // tn,),
        in_specs=[pl.BlockSpec((rows, d), lambda j: (0, 0)),
                  pl.BlockSpec((d, tn), lambda j: (0, j)),
                  pl.BlockSpec((1, tn), lambda j: (0, j))],
        out_specs=pl.BlockSpec((rows, tn), lambda j: (0, j)),
        out_shape=jax.ShapeDtypeStruct((rows, n), F32),
        compiler_params=_cparams("arbitrary"),
        name="adaln",
    )(cvecs, w_ada, b_ada.reshape(1, n))


def _lane_group(rows):
    return lax.broadcasted_iota(jnp.int32, (rows, LANES), 1) // SSM_GROUP


def _to_chunk_lanes(u_scr, u_o):
    rt = u_o.shape[2]
    grp = _lane_group(rt)
    for col in range(SSM_WIDTH // LANES):
        rolled = []
        for step in range(CHUNK):
            s = u_scr[col, pl.ds(step, rt, stride=CHUNK), :]
            rolled.append([s if k == 0 else pltpu.roll(s, k * SSM_GROUP, 1)
                           for k in range(GROUPS_PER_VREG)])
        for g_lo in range(GROUPS_PER_VREG):
            for half in range(CHUNK // GROUPS_PER_VREG):
                out = None
                for s8 in range(GROUPS_PER_VREG):
                    piece = rolled[half * GROUPS_PER_VREG + s8][(s8 - g_lo) % GROUPS_PER_VREG]
                    out = piece if out is None else jnp.where(grp == s8, piece, out)
                u_o[0, col * GROUPS_PER_VREG + g_lo, :, half * LANES:(half + 1) * LANES] = out.astype(BF16)


def _from_chunk_lanes(y_ref, y_scr):
    rt = y_ref.shape[2]
    grp = _lane_group(rt)
    for col in range(SSM_WIDTH // LANES):
        rolled = {}
        for g_lo in range(GROUPS_PER_VREG):
            for half in range(CHUNK // GROUPS_PER_VREG):
                s = y_ref[0, col * GROUPS_PER_VREG + g_lo, :, half * LANES:(half + 1) * LANES].astype(F32)
                rolled[g_lo, half] = [s if k == 0 else pltpu.roll(s, k * SSM_GROUP, 1)
                                      for k in range(GROUPS_PER_VREG)]
        for step in range(CHUNK):
            half, s8 = divmod(step, GROUPS_PER_VREG)
            out = None
            for g_lo in range(GROUPS_PER_VREG):
                piece = rolled[g_lo, half][(g_lo - s8) % GROUPS_PER_VREG]
                out = piece if out is None else jnp.where(grp == g_lo, piece, out)
            y_scr[col, pl.ds(step, rt, stride=CHUNK), :] = out


def _head_mean_sq(z, bd):
    return jnp.dot((z * z).astype(BF16), bd, preferred_element_type=F32)


def _swap16(x):
    w = x.shape[1]
    lane = lax.broadcasted_iota(jnp.int32, x.shape, 1)
    return jnp.where((lane & 16) == 0, pltpu.roll(x, w - 16, 1), pltpu.roll(x, 16, 1))


def _rope(x, cos, sin_signed):
    cols = []
    for c in range(x.shape[1] // LANES):
        xc = x[:, c * LANES:(c + 1) * LANES]
        cols.append(xc * cos + _swap16(xc) * sin_signed)
    return cols[0] if len(cols) == 1 else jnp.concatenate(cols, axis=1)


def _silu(z):
    return z * jax.nn.sigmoid(z)


def _inproj_kernel(x_ref, c_ref, shl_ref, scl_ref, shc_ref, scc_ref, pg_ref, w_ref, qg_ref, kg_ref,
                   cos_ref, sin_ref, bd_ref, q_o, k_o, vt_o, ga_o, u_o, gs_o, v_scr, u_scr, *, n_ctx_tiles):
    is_ctx = pl.program_id(1) < n_ctx_tiles
    x = jnp.where(is_ctx, c_ref[0], x_ref[0])
    shift = jnp.where(is_ctx, shc_ref[0], shl_ref[0])
    scale = jnp.where(is_ctx, scc_ref[0], scl_ref[0])
    ms = jnp.mean(x * x, axis=-1, keepdims=True)
    xn = x * lax.rsqrt(ms + NORM_EPS) * pg_ref[...]
    h = (xn * (1.0 + scale) + shift).astype(BF16)

    def proj(a, b):
        return jnp.dot(h, w_ref[:, a:b], preferred_element_type=F32)

    cos = cos_ref[...]
    sin = sin_ref[...]
    zq = proj(0, Q_END)
    qn = zq * lax.rsqrt(_head_mean_sq(zq, bd_ref[...]) + NORM_EPS) * qg_ref[...]
    q_o[0] = (_rope(qn, cos, sin) * (ATTN_SCALE * LOG2_E)).astype(BF16)
    zk = proj(Q_END, K_END)
    kn = zk * lax.rsqrt(_head_mean_sq(zk, bd_ref[:KV_WIDTH, :KV_WIDTH]) + NORM_EPS) * kg_ref[...]
    k_o[0] = _rope(kn, cos, sin).astype(BF16)

    v_scr[...] = proj(K_END, V_END)
    vt = v_scr[...].T.astype(BF16)
    tm = vt.shape[1]
    ones_row = lax.broadcasted_iota(jnp.int32, (V_ROWS - HEAD_DIM, tm), 0) == 0
    for j in range(N_KV_HEADS):
        vt_o[0, j, :HEAD_DIM, :] = vt[j * HEAD_DIM:(j + 1) * HEAD_DIM]
        vt_o[0, j, HEAD_DIM:, :] = jnp.where(ones_row, 1.0, 0.0).astype(BF16)

    ga_o[0] = _silu(proj(V_END, GA_END)).astype(BF16)
    u = proj(GA_END, U_END)
    for col in range(SSM_WIDTH // LANES):
        u_scr[col] = u[:, col * LANES:(col + 1) * LANES]
    _to_chunk_lanes(u_scr, u_o)
    gs_o[0] = _silu(proj(U_END, IN_WIDTH)).astype(BF16)


def _inproj(x, ctx, shift_l, scale_l, shift_c, scale_c, pre_g, w_bf, qg, kg, cos, sin, bd, tm):
    bsz, n_lat, d = x.shape
    n_ctx = ctx.shape[1]
    assert n_ctx % tm == 0 and n_lat % tm == 0 and tm % (CHUNK * 16) == 0
    nct = n_ctx // tm
    n_tot = n_ctx + n_lat
    lat_rows = lambda w: pl.BlockSpec((1, tm, w), lambda b, i: (b, jnp.maximum(i - nct, 0), 0))
    all_rows = lambda w: pl.BlockSpec((1, tm, w), lambda b, i: (b, i, 0))
    ctx_vec = pl.BlockSpec((1, 1, d), lambda b, i: (0, 0, 0))
    table = pl.BlockSpec((tm, LANES), lambda b, i: (i, 0))
    return pl.pallas_call(
        functools.partial(_inproj_kernel, n_ctx_tiles=nct),
        grid=(bsz, n_tot // tm),
        in_specs=[lat_rows(d),
                  pl.BlockSpec((1, tm, d), lambda b, i: (b, jnp.minimum(i, nct - 1), 0)),
                  _batch_vec_spec(d), _batch_vec_spec(d), ctx_vec, ctx_vec, _const_spec((1, d)),
                  _const_spec((d, IN_WIDTH)), _const_spec((1, ATTN_WIDTH)), _const_spec((1, KV_WIDTH)),
                  table, table, _const_spec((ATTN_WIDTH, ATTN_WIDTH))],
        out_specs=[lat_rows(ATTN_WIDTH), all_rows(KV_WIDTH),
                   pl.BlockSpec((1, N_KV_HEADS, V_ROWS, tm), lambda b, i: (b, 0, 0, i)),
                   lat_rows(ATTN_WIDTH),
                   pl.BlockSpec((1, N_SSM_GROUPS, tm // CHUNK, CHUNK_LANES), lambda b, i: (b, 0, i, 0)),
                   lat_rows(SSM_WIDTH)],
        out_shape=[jax.ShapeDtypeStruct((bsz, n_lat, ATTN_WIDTH), BF16),
                   jax.ShapeDtypeStruct((bsz, n_tot, KV_WIDTH), BF16),
                   jax.ShapeDtypeStruct((bsz, N_KV_HEADS, V_ROWS, n_tot), BF16),
                   jax.ShapeDtypeStruct((bsz, n_lat, ATTN_WIDTH), BF16),
                   jax.ShapeDtypeStruct((bsz, N_SSM_GROUPS, n_tot // CHUNK, CHUNK_LANES), BF16),
                   jax.ShapeDtypeStruct((bsz, n_lat, SSM_WIDTH), BF16)],
        scratch_shapes=[pltpu.VMEM((tm, KV_WIDTH), F32), pltpu.VMEM((SSM_WIDTH // LANES, tm, LANES), F32)],
        compiler_params=_cparams("parallel", "arbitrary"),
        name="inproj",
    )(x, ctx, shift_l, scale_l, shift_c, scale_c, pre_g, w_bf, qg, kg, cos, sin, bd)


def _attn_kernel(q_ref, k_ref, vt_ref, o_ref, *, tk, sub, lookahead, heads_per_pass):
    tq = q_ref.shape[1]
    n_keys = k_ref.shape[1]
    group = N_Q_HEADS // N_KV_HEADS
    q = q_ref[0]
    low = lax.broadcasted_iota(jnp.int32, (tq, LANES), 1) < HEAD_DIM

    def widen(h):
        col = q[:, (h // 2) * LANES:(h // 2 + 1) * LANES].astype(F32)
        dst_high = h // group == 1
        if (h % 2 == 1) != dst_high:
            col = pltpu.roll(col, HEAD_DIM, 1)
        return jnp.where(low != dst_high, col, 0.0).astype(BF16)

    outs = []
    for h0 in range(0, N_Q_HEADS, heads_per_pass):
        heads = list(range(h0, h0 + heads_per_pass))
        q_wide = [widen(h) for h in heads]

        def body(t, carry, heads=heads, q_wide=q_wide):
            tasks = [(j, i) for j in range(tk // sub) for i in range(len(heads))]
            state = list(carry)
            scores = {}

            def keys_at(j):
                return pl.ds(pl.multiple_of(t * tk + j * sub, sub), sub)

            def issue(n):
                j, i = tasks[n]
                scores[n] = lax.dot_general(k_ref[0, keys_at(j), :], q_wide[i], (((1,), (1,)), ((), ())),
                                            preferred_element_type=F32)

            def consume(n):
                j, i = tasks[n]
                s = scores.pop(n)
                m_old, acc = state[i]
                vt = vt_ref[0, heads[i] // group, :, keys_at(j)]
                m_new = jnp.maximum(m_old, jnp.max(s, axis=0, keepdims=True))
                p = jnp.exp2((s - m_new).astype(BF16))
                acc = jnp.exp2(m_old - m_new) * acc + jnp.dot(vt, p, preferred_element_type=F32)
                state[i] = (m_new, acc)

            for n in range(len(tasks) + lookahead):
                if n < len(tasks):
                    issue(n)
                if n >= lookahead:
                    consume(n - lookahead)
            return tuple(state)

        init = tuple((jnp.full((1, tq), NEG_BIG, F32), jnp.zeros((V_ROWS, tq), F32)) for _ in heads)
        for _, acc in lax.fori_loop(0, n_keys // tk, body, init):
            outs.append(acc[:HEAD_DIM] / acc[HEAD_DIM:HEAD_DIM + 1])

    for c in range(N_Q_HEADS // 2):
        pair = jnp.concatenate([outs[2 * c], outs[2 * c + 1]], axis=0)
        o_ref[0, :, c * LANES:(c + 1) * LANES] = pair.T.astype(BF16)


def _attention(q, k_all, vt_aug, tq, tk, sub, lookahead, heads_per_pass):
    bsz, n, _ = q.shape
    n_keys = k_all.shape[1]
    assert n_keys % tk == 0 and tk % sub == 0 and n % tq == 0
    return pl.pallas_call(
        functools.partial(_attn_kernel, tk=tk, sub=sub, lookahead=lookahead,
                          heads_per_pass=heads_per_pass),
        grid=(bsz, n // tq),
        in_specs=[_row_spec(tq, ATTN_WIDTH),
                  pl.BlockSpec((1, n_keys, KV_WIDTH), lambda b, i: (b, 0, 0)),
                  pl.BlockSpec((1, N_KV_HEADS, V_ROWS, n_keys), lambda b, i: (b, 0, 0, 0))],
        out_specs=_row_spec(tq, ATTN_WIDTH),
        out_shape=jax.ShapeDtypeStruct((bsz, n, ATTN_WIDTH), BF16),
        compiler_params=_cparams("parallel", "parallel"),
        name="attention",
    )(q, k_all, vt_aug)


def _scan_steps(n_rows):
    return max(1, math.ceil(math.log2(n_rows)))


def _ssm_prep_kernel(lre_ref, lim_ref, ldt_ref, bt_ref, bts_ref, c_ref, cs_ref, d_ref,
                     win_o, coutt_o, kcat_o, p_o, q_o, *, n_steps):
    lane = lax.broadcasted_iota(jnp.int32, (1, STATE_LANES), 1)
    sign_lo = jnp.where(lane < SSM_STATE, -1.0, 1.0)
    k_idx = lax.broadcasted_iota(jnp.int32, (CHUNK, STATE_LANES), 0).astype(F32)

    def outer(pw, pws, mat, mats):
        full = pw[:, None, :] * mat[None, :, :] + pws[:, None, :] * mats[None, :, :]
        return full.reshape(CHUNK * mat.shape[0], STATE_LANES)

    kms = []
    for d in range(2):
        lre = lre_ref[0, d:d + 1, :]
        lim = lim_ref[0, d:d + 1, :]
        dt = jnp.exp(ldt_ref[0, d:d + 1, :])

        def power(expo):
            mag = jnp.exp(expo * dt * lre)
            ang = expo * dt * lim
            return mag * jnp.cos(ang), mag * jnp.sin(ang) * sign_lo

        a_re, a_ims = power(jnp.ones((1, STATE_LANES), F32))
        a_im = a_ims * sign_lo
        nr, ni = a_re - 1.0, a_im
        den = lre * lre + lim * lim
        cr = (nr * lre + ni * lim) / den
        ci = (ni * lre - nr * lim) / den
        cis = ci * sign_lo
        bbar = cr * bt_ref[0, d] + cis * bts_ref[0, d]
        bbar_s = cr * bts_ref[0, d] - cis * bt_ref[0, d]
        cmat, cmat_s = c_ref[0, d], cs_ref[0, d]

        asc, ascs = power(k_idx)
        desc, descs = power(CHUNK - 1.0 - k_idx)
        if d == 0:
            state_in = outer(desc, descs, bbar, bbar_s)
            taps = outer(asc, ascs, bbar, bbar_s)
            ro, ros = power(k_idx + 1.0)
        else:
            state_in = outer(asc, ascs, bbar, bbar_s)
            taps = outer(desc, descs, bbar, bbar_s)
            ro, ros = power(CHUNK - k_idx)
        state_out = outer(ro, ros, cmat, cmat_s) * (-sign_lo)
        win_o[0, :, d * STATE_LANES:(d + 1) * STATE_LANES] = state_in
        coutt_o[0, :, d * STATE_LANES:(d + 1) * STATE_LANES] = state_out
        kms.append(lax.dot_general(taps, cmat * (-sign_lo), (((1,), (1,)), ((), ())),
                                   preferred_element_type=F32, precision=lax.Precision.HIGHEST))

        sr, sis = power(jnp.full((1, STATE_LANES), float(CHUNK), F32))
        for k in range(n_steps):
            p_o[0, k:k + 1, d * STATE_LANES:(d + 1) * STATE_LANES] = sr
            q_o[0, k:k + 1, d * STATE_LANES:(d + 1) * STATE_LANES] = sis
            si = sis * sign_lo
            sr, sis = sr * sr - si * si, 2.0 * sr * si * sign_lo

    km_f, km_b = kms
    rr = lax.broadcasted_iota(jnp.int32, (SSM_GROUP, SSM_GROUP), 0)
    cc = lax.broadcasted_iota(jnp.int32, (SSM_GROUP, SSM_GROUP), 1)
    skip = jnp.where(rr == cc, d_ref[0], 0.0)
    edge = (CHUNK - 1) * SSM_GROUP
    kcat_o[0, :edge, :] = km_b[:edge]
    kcat_o[0, edge:edge + SSM_GROUP, :] = km_b[edge:] + km_f[:SSM_GROUP] + skip
    kcat_o[0, edge + SSM_GROUP:, :] = km_f[SSM_GROUP:]


def _ssm_prep(lam_re, lam_im, log_dt, b_re, b_im, c_re, c_im, d_skip, n_steps):
    g = N_SSM_GROUPS
    dup = lambda a: jnp.concatenate([a, a], axis=-1)
    lre = dup(jnp.swapaxes(lam_re, 0, 1))
    lim = dup(jnp.swapaxes(lam_im, 0, 1))
    ldt = jnp.broadcast_to(jnp.swapaxes(log_dt, 0, 1)[..., None], (g, 2, STATE_LANES))
    btr = jnp.transpose(b_re, (1, 0, 3, 2))
    bti = jnp.transpose(b_im, (1, 0, 3, 2))
    bt, bts = jnp.concatenate([btr, bti], -1), jnp.concatenate([bti, btr], -1)
    cr, ci = jnp.swapaxes(c_re, 0, 1), jnp.swapaxes(c_im, 0, 1)
    cm, cms = jnp.concatenate([cr, ci], -1), jnp.concatenate([ci, cr], -1)
    dsk = d_skip.reshape(g, 1, SSM_GROUP)
    n_kcat = (2 * CHUNK - 1) * SSM_GROUP
    n_pad = 16
    vec = pl.BlockSpec((1, 2, STATE_LANES), lambda i: (i, 0, 0))
    mat = pl.BlockSpec((1, 2, SSM_GROUP, STATE_LANES), lambda i: (i, 0, 0, 0))
    sq = pl.BlockSpec((1, CHUNK_LANES, CHUNK_LANES), lambda i: (i, 0, 0))
    mult = pl.BlockSpec((1, n_pad, 2 * STATE_LANES), lambda i: (i, 0, 0))
    return pl.pallas_call(
        functools.partial(_ssm_prep_kernel, n_steps=n_steps),
        grid=(g,),
        in_specs=[vec, vec, vec, mat, mat, mat, mat,
                  pl.BlockSpec((1, 1, SSM_GROUP), lambda i: (i, 0, 0))],
        out_specs=[sq, sq, pl.BlockSpec((1, n_kcat, SSM_GROUP), lambda i: (i, 0, 0)), mult, mult],
        out_shape=[jax.ShapeDtypeStruct((g, CHUNK_LANES, CHUNK_LANES), F32),
                   jax.ShapeDtypeStruct((g, CHUNK_LANES, CHUNK_LANES), F32),
                   jax.ShapeDtypeStruct((g, n_kcat, SSM_GROUP), F32),
                   jax.ShapeDtypeStruct((g, n_pad, 2 * STATE_LANES), F32),
                   jax.ShapeDtypeStruct((g, n_pad, 2 * STATE_LANES), F32)],
        compiler_params=_cparams("parallel"),
        name="ssm_prep",
    )(lre, lim, ldt, bt, bts, cm, cms, dsk)


def _ssm_kernel(u_ref, win_ref, m_ref, cout_ref, p_ref, q_ref, y_o, *, n_ctx_rows, n_steps):
    u = u_ref[0, 0]
    rows = u.shape[0]
    x = jnp.dot(u, win_ref[0], preferred_element_type=F32)
    xf = x[:, :STATE_LANES]
    xb = jnp.concatenate([x[n_ctx_rows:, STATE_LANES:], x[:n_ctx_rows, STATE_LANES:]], axis=0)
    row = lax.broadcasted_iota(jnp.int32, (rows, STATE_LANES), 0)
    for k in range(n_steps):
        s = 1 << k
        pf, qf = p_ref[0, k:k + 1, :STATE_LANES], q_ref[0, k:k + 1, :STATE_LANES]
        pb, qb = p_ref[0, k:k + 1, STATE_LANES:], q_ref[0, k:k + 1, STATE_LANES:]
        sf = jnp.where(row >= s, pltpu.roll(xf, s, 0), 0.0)
        xf = xf + pf * sf + qf * pltpu.roll(sf, SSM_STATE, 1)
        sb = jnp.where(row < rows - s, pltpu.roll(xb, rows - s, 0), 0.0)
        xb = xb + pb * sb + qb * pltpu.roll(sb, SSM_STATE, 1)
    prev_f = pltpu.roll(xf, 1, 0)[n_ctx_rows:]
    next_b = pltpu.roll(xb, rows - 1, 0)[:rows - n_ctx_rows]
    st = jnp.concatenate([prev_f, next_b], axis=1).astype(BF16)
    y = jnp.dot(u[n_ctx_rows:], m_ref[0], preferred_element_type=F32)
    y = y + jnp.dot(st, cout_ref[0], preferred_element_type=F32)
    y_o[0, 0] = y.astype(BF16)


def _ssm(u_cf, win, m, cout, p, q, n_ctx_rows, n_steps):
    bsz, g, rows, _ = u_cf.shape
    sq = pl.BlockSpec((1, CHUNK_LANES, CHUNK_LANES), lambda b, i: (i, 0, 0))
    mult = pl.BlockSpec((1, p.shape[1], p.shape[2]), lambda b, i: (i, 0, 0))
    return pl.pallas_call(
        functools.partial(_ssm_kernel, n_ctx_rows=n_ctx_rows, n_steps=n_steps),
        grid=(bsz, g),
        in_specs=[pl.BlockSpec((1, 1, rows, CHUNK_LANES), lambda b, i: (b, i, 0, 0)),
                  sq, sq, sq, mult, mult],
        out_specs=pl.BlockSpec((1, 1, rows - n_ctx_rows, CHUNK_LANES), lambda b, i: (b, i, 0, 0)),
        out_shape=jax.ShapeDtypeStruct((bsz, g, rows - n_ctx_rows, CHUNK_LANES), BF16),
        compiler_params=_cparams("parallel", "parallel"),
        name="ssm",
    )(u_cf, win, m, cout, p, q)


def _final_kernel(x_ref, at_ref, ga_ref, y_ref, gs_ref, gate_ref, wglu_ref, bglu_ref, wout_ref,
                  pg_ref, o_ref, y_scr):
    _from_chunk_lanes(y_ref, y_scr)
    y = jax.nn.gelu(jnp.concatenate([y_scr[col] for col in range(SSM_WIDTH // LANES)], axis=1))
    t = jnp.dot(y.astype(BF16), wglu_ref[...], preferred_element_type=F32) + bglu_ref[...]
    ssm = y * jax.nn.sigmoid(t)
    a = (at_ref[0].astype(F32) * ga_ref[0].astype(F32)).astype(BF16)
    s = (ssm * gs_ref[0].astype(F32)).astype(BF16)
    out = jnp.dot(a, wout_ref[:ATTN_WIDTH, :], preferred_element_type=F32)
    out = out + jnp.dot(s, wout_ref[ATTN_WIDTH:, :], preferred_element_type=F32)
    ms = jnp.mean(out * out, axis=-1, keepdims=True)
    r = out * lax.rsqrt(ms + NORM_EPS) * pg_ref[...]
    o_ref[0] = x_ref[0] + gate_ref[0] * r


def _final(x, attn, ga, y_cf, gs, gate, wglu_bf, b_glu, wout_bf, post_g, tm):
    bsz, n, d = x.shape
    assert n % tm == 0 and tm % (CHUNK * 16) == 0
    return pl.pallas_call(
        _final_kernel,
        grid=(bsz, n // tm),
        in_specs=[_row_spec(tm, d), _row_spec(tm, ATTN_WIDTH), _row_spec(tm, ATTN_WIDTH),
                  pl.BlockSpec((1, N_SSM_GROUPS, tm // CHUNK, CHUNK_LANES), lambda b, i: (b, 0, i, 0)),
                  _row_spec(tm, SSM_WIDTH), _batch_vec_spec(d),
                  _const_spec((SSM_WIDTH, SSM_WIDTH)), _const_spec((1, SSM_WIDTH)),
                  _const_spec((d, d)), _const_spec((1, d))],
        out_specs=_row_spec(tm, d),
        out_shape=jax.ShapeDtypeStruct((bsz, n, d), F32),
        scratch_shapes=[pltpu.VMEM((SSM_WIDTH // LANES, tm, LANES), F32)],
        compiler_params=_cparams("parallel", "parallel"),
        name="final",
    )(x, attn, ga, y_cf, gs, gate, wglu_bf, b_glu, wout_bf, post_g)


def _rope_tables(n_ctx, n_lat):
    t = jnp.arange(n_lat, dtype=jnp.int32)
    row_pos = (t // GRID_W).astype(F32)
    col_pos = (t % GRID_W).astype(F32)
    inv_freq = ROPE_THETA ** (-jnp.arange(ROPE_FREQS, dtype=F32) / ROPE_FREQS)
    ang_r = row_pos[:, None] * inv_freq
    ang_c = col_pos[:, None] * inv_freq
    cos = jnp.concatenate([jnp.cos(ang_r)] * 2 + [jnp.cos(ang_c)] * 2, axis=1)
    sin = jnp.concatenate([-jnp.sin(ang_r), jnp.sin(ang_r), -jnp.sin(ang_c), jnp.sin(ang_c)], axis=1)
    cos = jnp.concatenate([jnp.ones((n_ctx, HEAD_DIM), F32), cos], axis=0)
    sin = jnp.concatenate([jnp.zeros((n_ctx, HEAD_DIM), F32), sin], axis=0)
    return jnp.tile(cos, (1, LANES // HEAD_DIM)), jnp.tile(sin, (1, LANES // HEAD_DIM))


def _toeplitz(kcat):
    j = jnp.arange(CHUNK)
    lag = j[None, :] - j[:, None] + CHUNK - 1
    blocks = kcat[:, lag]
    return jnp.transpose(blocks, (0, 1, 3, 2, 4)).reshape(N_SSM_GROUPS, CHUNK_LANES, CHUNK_LANES)


def _layer(x, ctx, c, c_ctx, w_ada, b_ada, pre_g, post_g, w_in, q_g, k_g, lam_re, lam_im, log_dt,
           b_re, b_im, c_re, c_im, d_skip, w_glu, b_glu, w_out):
    bsz, n_lat, d = x.shape
    n_ctx = ctx.shape[1]
    assert n_lat % GRID_W == 0

    cvecs = jnp.zeros((8, d), F32).at[:bsz].set(c).at[bsz].set(c_ctx)
    mod = _adaln(cvecs, w_ada, b_ada)
    shift, scale, gate = (mod[:, i * d:(i + 1) * d] for i in range(3))
    lat = lambda a: a[:bsz].reshape(bsz, 1, d)
    cvec = lambda a: a[bsz].reshape(1, 1, d)

    head = jnp.arange(ATTN_WIDTH) // HEAD_DIM
    bd = (head[:, None] == head[None, :]).astype(BF16) * (1.0 / HEAD_DIM)
    qg = jnp.tile(q_g, N_Q_HEADS).reshape(1, ATTN_WIDTH)
    kg = jnp.tile(k_g, N_KV_HEADS).reshape(1, KV_WIDTH)
    cos, sin = _rope_tables(n_ctx, n_lat)

    q, k_all, vt_aug, ga, u_cf, gs = _inproj(x, ctx, lat(shift), lat(scale), cvec(shift), cvec(scale),
                                             pre_g.reshape(1, d), w_in.astype(BF16), qg, kg, cos, sin,
                                             bd, tm=256)
    attn = _attention(q, k_all, vt_aug, tq=256, tk=2816, sub=256, lookahead=5, heads_per_pass=8)

    rows = (n_ctx + n_lat) // CHUNK
    n_steps = _scan_steps(rows)
    win, coutt, kcat, p, qm = _ssm_prep(lam_re, lam_im, log_dt, b_re, b_im, c_re, c_im, d_skip, n_steps)
    m_op = _toeplitz(kcat.reshape(N_SSM_GROUPS, 2 * CHUNK - 1, SSM_GROUP, SSM_GROUP))
    y_cf = _ssm(u_cf, win.astype(BF16), m_op.astype(BF16), jnp.swapaxes(coutt, 1, 2).astype(BF16),
                p, qm, n_ctx // CHUNK, n_steps)

    return _final(x, attn, ga, y_cf, gs, lat(gate), w_glu.astype(BF16), b_glu.reshape(1, SSM_WIDTH),
                  w_out.astype(BF16), post_g.reshape(1, d), tm=512)


def kernel(x, c, ctx, c_ctx, w_ada, b_ada, pre_norm, post_norm, w_in, q_norm, k_norm, ssm_lam_re,
           ssm_lam_im, ssm_log_dt, ssm_b_re, ssm_b_im, ssm_c_re, ssm_c_im, ssm_d, w_glu, b_glu, w_out):
    depth = w_ada.shape[0]
    assert depth == 1, "context stream update between layers is not implemented"
    return _layer(x, ctx, c, c_ctx, w_ada[0], b_ada[0], pre_norm[0], post_norm[0], w_in[0], q_norm[0],
                  k_norm[0], ssm_lam_re[0], ssm_lam_im[0], ssm_log_dt[0], ssm_b_re[0], ssm_b_im[0],
                  ssm_c_re[0], ssm_c_im[0], ssm_d[0], w_glu[0], b_glu[0], w_out[0])
```

```python
import functools
import math

import jax
import jax.numpy as jnp
from jax import lax
from jax.experimental import pallas as pl
from jax.experimental.pallas import tpu as pltpu

F32 = jnp.float32
BF16 = jnp.bfloat16

D_MODEL = 1024
HEAD_DIM = 64
N_Q_HEADS = 8
N_KV_HEADS = 2
ATTN_WIDTH = N_Q_HEADS * HEAD_DIM
KV_WIDTH = N_KV_HEADS * HEAD_DIM
SSM_WIDTH = 512
SSM_GROUP = 16
N_SSM_GROUPS = SSM_WIDTH // SSM_GROUP
SSM_STATE = 64
GRID_W = 64
ROPE_THETA = 10000.0
ROPE_FREQS = 16
NORM_EPS = 1e-6
ATTN_SCALE = HEAD_DIM ** -0.5
LOG2_E = math.log2(math.e)
Q_END = ATTN_WIDTH
K_END = Q_END + KV_WIDTH
V_END = K_END + KV_WIDTH
GA_END = V_END + ATTN_WIDTH
U_END = GA_END + SSM_WIDTH
IN_WIDTH = U_END + SSM_WIDTH

CHUNK = 16
CHUNK_LANES = CHUNK * SSM_GROUP
LANES = 128
GROUPS_PER_VREG = LANES // SSM_GROUP
STATE_LANES = 2 * SSM_STATE
V_ROWS = HEAD_DIM + 16
VMEM_LIMIT = 48 * 1024 * 1024
NEG_BIG = -1e30


def _cparams(*sem):
    return pltpu.CompilerParams(dimension_semantics=sem, vmem_limit_bytes=VMEM_LIMIT)


def _row_spec(tm, width):
    return pl.BlockSpec((1, tm, width), lambda b, i: (b, i, 0))


def _const_spec(shape):
    return pl.BlockSpec(shape, lambda b, i: (0,) * len(shape))


def _batch_vec_spec(width):
    return pl.BlockSpec((1, 1, width), lambda b, i: (b, 0, 0))


def _adaln_kernel(c_ref, w_ref, b_ref, o_ref):
    c = c_ref[...]
    s = c * jax.nn.sigmoid(c)
    o_ref[...] = jnp.dot(s, w_ref[...], preferred_element_type=F32,
                         precision=lax.Precision.HIGHEST) + b_ref[...]


def _adaln(cvecs, w_ada, b_ada):
    rows, d = cvecs.shape
    n = w_ada.shape[1]
    tn = 512
    return pl.pallas_call(
        _adaln_kernel,
        grid=(n // tn,),
        in_specs=[pl.BlockSpec((rows, d), lambda j: (0, 0)),
                  pl.BlockSpec((d, tn), lambda j: (0, j)),
                  pl.BlockSpec((1, tn), lambda j: (0, j))],
        out_specs=pl.BlockSpec((rows, tn), lambda j: (0, j)),
        out_shape=jax.ShapeDtypeStruct((rows, n), F32),
        compiler_params=_cparams("arbitrary"),
        name="adaln",
    )(cvecs, w_ada, b_ada.reshape(1, n))


def _lane_group(rows):
    return lax.broadcasted_iota(jnp.int32, (rows, LANES), 1) // SSM_GROUP


def _to_chunk_lanes(u_scr, u_o):
    rt = u_o.shape[2]
    grp = _lane_group(rt)
    for col in range(SSM_WIDTH // LANES):
        rolled = []
        for step in range(CHUNK):
            s = u_scr[col, pl.ds(step, rt, stride=CHUNK), :]
            rolled.append([s if k == 0 else pltpu.roll(s, k * SSM_GROUP, 1)
                           for k in range(GROUPS_PER_VREG)])
        for g_lo in range(GROUPS_PER_VREG):
            for half in range(CHUNK // GROUPS_PER_VREG):
                out = None
                for s8 in range(GROUPS_PER_VREG):
                    piece = rolled[half * GROUPS_PER_VREG + s8][(s8 - g_lo) % GROUPS_PER_VREG]
                    out = piece if out is None else jnp.where(grp == s8, piece, out)
                u_o[0, col * GROUPS_PER_VREG + g_lo, :, half * LANES:(half + 1) * LANES] = out.astype(BF16)


def _from_chunk_lanes(y_ref, y_scr):
    rt = y_ref.shape[2]
    grp = _lane_group(rt)
    for col in range(SSM_WIDTH // LANES):
        rolled = {}
        for g_lo in range(GROUPS_PER_VREG):
            for half in range(CHUNK // GROUPS_PER_VREG):
                s = y_ref[0, col * GROUPS_PER_VREG + g_lo, :, half * LANES:(half + 1) * LANES].astype(F32)
                rolled[g_lo, half] = [s if k == 0 else pltpu.roll(s, k * SSM_GROUP, 1)
                                      for k in range(GROUPS_PER_VREG)]
        for step in range(CHUNK):
            half, s8 = divmod(step, GROUPS_PER_VREG)
            out = None
            for g_lo in range(GROUPS_PER_VREG):
                piece = rolled[g_lo, half][(g_lo - s8) % GROUPS_PER_VREG]
                out = piece if out is None else jnp.where(grp == g_lo, piece, out)
            y_scr[col, pl.ds(step, rt, stride=CHUNK), :] = out


def _head_mean_sq(z, bd):
    return jnp.dot((z * z).astype(BF16), bd, preferred_element_type=F32)


def _swap16(x):
    w = x.shape[1]
    lane = lax.broadcasted_iota(jnp.int32, x.shape, 1)
    return jnp.where((lane & 16) == 0, pltpu.roll(x, w - 16, 1), pltpu.roll(x, 16, 1))


def _rope(x, cos, sin_signed):
    cols = []
    for c in range(x.shape[1] // LANES):
        xc = x[:, c * LANES:(c + 1) * LANES]
        cols.append(xc * cos + _swap16(xc) * sin_signed)
    return cols[0] if len(cols) == 1 else jnp.concatenate(cols, axis=1)


def _silu(z):
    return z * jax.nn.sigmoid(z)


def _inproj_kernel(x_ref, c_ref, shl_ref, scl_ref, shc_ref, scc_ref, pg_ref, w_ref, qg_ref, kg_ref,
                   cos_ref, sin_ref, bd_ref, q_o, k_o, vt_o, ga_o, u_o, gs_o, v_scr, u_scr, *, n_ctx_tiles):
    is_ctx = pl.program_id(1) < n_ctx_tiles
    x = jnp.where(is_ctx, c_ref[0], x_ref[0])
    shift = jnp.where(is_ctx, shc_ref[0], shl_ref[0])
    scale = jnp.where(is_ctx, scc_ref[0], scl_ref[0])
    ms = jnp.mean(x * x, axis=-1, keepdims=True)
    xn = x * lax.rsqrt(ms + NORM_EPS) * pg_ref[...]
    h = (xn * (1.0 + scale) + shift).astype(BF16)

    def proj(a, b):
        return jnp.dot(h, w_ref[:, a:b], preferred_element_type=F32)

    cos = cos_ref[...]
    sin = sin_ref[...]
    zq = proj(0, Q_END)
    qn = zq * lax.rsqrt(_head_mean_sq(zq, bd_ref[...]) + NORM_EPS) * qg_ref[...]
    q_o[0] = (_rope(qn, cos, sin) * (ATTN_SCALE * LOG2_E)).astype(BF16)
    zk = proj(Q_END, K_END)
    kn = zk * lax.rsqrt(_head_mean_sq(zk, bd_ref[:KV_WIDTH, :KV_WIDTH]) + NORM_EPS) * kg_ref[...]
    k_o[0] = _rope(kn, cos, sin).astype(BF16)

    v_scr[...] = proj(K_END, V_END)
    vt = v_scr[...].T.astype(BF16)
    tm = vt.shape[1]
    ones_row = lax.broadcasted_iota(jnp.int32, (V_ROWS - HEAD_DIM, tm), 0) == 0
    for j in range(N_KV_HEADS):
        vt_o[0, j, :HEAD_DIM, :] = vt[j * HEAD_DIM:(j + 1) * HEAD_DIM]
        vt_o[0, j, HEAD_DIM:, :] = jnp.where(ones_row, 1.0, 0.0).astype(BF16)

    ga_o[0] = _silu(proj(V_END, GA_END)).astype(BF16)
    u = proj(GA_END, U_END)
    for col in range(SSM_WIDTH // LANES):
        u_scr[col] = u[:, col * LANES:(col + 1) * LANES]
    _to_chunk_lanes(u_scr, u_o)
    gs_o[0] = _silu(proj(U_END, IN_WIDTH)).astype(BF16)


def _inproj(x, ctx, shift_l, scale_l, shift_c, scale_c, pre_g, w_bf, qg, kg, cos, sin, bd, tm):
    bsz, n_lat, d = x.shape
    n_ctx = ctx.shape[1]
    assert n_ctx % tm == 0 and n_lat % tm == 0 and tm % (CHUNK * 16) == 0
    nct = n_ctx // tm
    n_tot = n_ctx + n_lat
    lat_rows = lambda w: pl.BlockSpec((1, tm, w), lambda b, i: (b, jnp.maximum(i - nct, 0), 0))
    all_rows = lambda w: pl.BlockSpec((1, tm, w), lambda b, i: (b, i, 0))
    ctx_vec = pl.BlockSpec((1, 1, d), lambda b, i: (0, 0, 0))
    table = pl.BlockSpec((tm, LANES), lambda b, i: (i, 0))
    return pl.pallas_call(
        functools.partial(_inproj_kernel, n_ctx_tiles=nct),
        grid=(bsz, n_tot // tm),
        in_specs=[lat_rows(d),
                  pl.BlockSpec((1, tm, d), lambda b, i: (b, jnp.minimum(i, nct - 1), 0)),
                  _batch_vec_spec(d), _batch_vec_spec(d), ctx_vec, ctx_vec, _const_spec((1, d)),
                  _const_spec((d, IN_WIDTH)), _const_spec((1, ATTN_WIDTH)), _const_spec((1, KV_WIDTH)),
                  table, table, _const_spec((ATTN_WIDTH, ATTN_WIDTH))],
        out_specs=[lat_rows(ATTN_WIDTH), all_rows(KV_WIDTH),
                   pl.BlockSpec((1, N_KV_HEADS, V_ROWS, tm), lambda b, i: (b, 0, 0, i)),
                   lat_rows(ATTN_WIDTH),
                   pl.BlockSpec((1, N_SSM_GROUPS, tm // CHUNK, CHUNK_LANES), lambda b, i: (b, 0, i, 0)),
                   lat_rows(SSM_WIDTH)],
        out_shape=[jax.ShapeDtypeStruct((bsz, n_lat, ATTN_WIDTH), BF16),
                   jax.ShapeDtypeStruct((bsz, n_tot, KV_WIDTH), BF16),
                   jax.ShapeDtypeStruct((bsz, N_KV_HEADS, V_ROWS, n_tot), BF16),
                   jax.ShapeDtypeStruct((bsz, n_lat, ATTN_WIDTH), BF16),
                   jax.ShapeDtypeStruct((bsz, N_SSM_GROUPS, n_tot // CHUNK, CHUNK_LANES), BF16),
                   jax.ShapeDtypeStruct((bsz, n_lat, SSM_WIDTH), BF16)],
        scratch_shapes=[pltpu.VMEM((tm, KV_WIDTH), F32), pltpu.VMEM((SSM_WIDTH // LANES, tm, LANES), F32)],
        compiler_params=_cparams("parallel", "arbitrary"),
        name="inproj",
    )(x, ctx, shift_l, scale_l, shift_c, scale_c, pre_g, w_bf, qg, kg, cos, sin, bd)


def _attn_kernel(q_ref, k_ref, vt_ref, o_ref, *, tk, sub, lookahead, heads_per_pass):
    tq = q_ref.shape[1]
    n_keys = k_ref.shape[1]
    group = N_Q_HEADS // N_KV_HEADS
    q = q_ref[0]
    low = lax.broadcasted_iota(jnp.int32, (tq, LANES), 1) < HEAD_DIM

    def widen(h):
        col = q[:, (h // 2) * LANES:(h // 2 + 1) * LANES].astype(F32)
        dst_high = h // group == 1
        if (h % 2 == 1) != dst_high:
            col = pltpu.roll(col, HEAD_DIM, 1)
        return jnp.where(low != dst_high, col, 0.0).astype(BF16)

    outs = []
    for h0 in range(0, N_Q_HEADS, heads_per_pass):
        heads = list(range(h0, h0 + heads_per_pass))
        q_wide = [widen(h) for h in heads]

        def body(t, carry, heads=heads, q_wide=q_wide):
            tasks = [(j, i) for j in range(tk // sub) for i in range(len(heads))]
            state = list(carry)
            scores = {}

            def keys_at(j):
                return pl.ds(pl.multiple_of(t * tk + j * sub, sub), sub)

            def issue(n):
                j, i = tasks[n]
                scores[n] = lax.dot_general(k_ref[0, keys_at(j), :], q_wide[i], (((1,), (1,)), ((), ())),
                                            preferred_element_type=F32)

            def consume(n):
                j, i = tasks[n]
                s = scores.pop(n)
                m_old, acc = state[i]
                vt = vt_ref[0, heads[i] // group, :, keys_at(j)]
                m_new = jnp.maximum(m_old, jnp.max(s, axis=0, keepdims=True))
                p = jnp.exp2((s - m_new).astype(BF16))
                acc = jnp.exp2(m_old - m_new) * acc + jnp.dot(vt, p, preferred_element_type=F32)
                state[i] = (m_new, acc)

            for n in range(len(tasks) + lookahead):
                if n < len(tasks):
                    issue(n)
                if n >= lookahead:
                    consume(n - lookahead)
            return tuple(state)

        init = tuple((jnp.full((1, tq), NEG_BIG, F32), jnp.zeros((V_ROWS, tq), F32)) for _ in heads)
        for _, acc in lax.fori_loop(0, n_keys // tk, body, init):
            outs.append(acc[:HEAD_DIM] / acc[HEAD_DIM:HEAD_DIM + 1])

    for c in range(N_Q_HEADS // 2):
        pair = jnp.concatenate([outs[2 * c], outs[2 * c + 1]], axis=0)
        o_ref[0, :, c * LANES:(c + 1) * LANES] = pair.T.astype(BF16)


def _attention(q, k_all, vt_aug, tq, tk, sub, lookahead, heads_per_pass):
    bsz, n, _ = q.shape
    n_keys = k_all.shape[1]
    assert n_keys % tk == 0 and tk % sub == 0 and n % tq == 0
    return pl.pallas_call(
        functools.partial(_attn_kernel, tk=tk, sub=sub, lookahead=lookahead,
                          heads_per_pass=heads_per_pass),
        grid=(bsz, n // tq),
        in_specs=[_row_spec(tq, ATTN_WIDTH),
                  pl.BlockSpec((1, n_keys, KV_WIDTH), lambda b, i: (b, 0, 0)),
                  pl.BlockSpec((1, N_KV_HEADS, V_ROWS, n_keys), lambda b, i: (b, 0, 0, 0))],
        out_specs=_row_spec(tq, ATTN_WIDTH),
        out_shape=jax.ShapeDtypeStruct((bsz, n, ATTN_WIDTH), BF16),
        compiler_params=_cparams("parallel", "parallel"),
        name="attention",
    )(q, k_all, vt_aug)


def _scan_steps(n_rows):
    return max(1, math.ceil(math.log2(n_rows)))


def _ssm_prep_kernel(lre_ref, lim_ref, ldt_ref, bt_ref, bts_ref, c_ref, cs_ref, d_ref,
                     win_o, coutt_o, kcat_o, p_o, q_o, *, n_steps):
    lane = lax.broadcasted_iota(jnp.int32, (1, STATE_LANES), 1)
    sign_lo = jnp.where(lane < SSM_STATE, -1.0, 1.0)
    k_idx = lax.broadcasted_iota(jnp.int32, (CHUNK, STATE_LANES), 0).astype(F32)

    def outer(pw, pws, mat, mats):
        full = pw[:, None, :] * mat[None, :, :] + pws[:, None, :] * mats[None, :, :]
        return full.reshape(CHUNK * mat.shape[0], STATE_LANES)

    kms = []
    for d in range(2):
        lre = lre_ref[0, d:d + 1, :]
        lim = lim_ref[0, d:d + 1, :]
        dt = jnp.exp(ldt_ref[0, d:d + 1, :])

        def power(expo):
            mag = jnp.exp(expo * dt * lre)
            ang = expo * dt * lim
            return mag * jnp.cos(ang), mag * jnp.sin(ang) * sign_lo

        a_re, a_ims = power(jnp.ones((1, STATE_LANES), F32))
        a_im = a_ims * sign_lo
        nr, ni = a_re - 1.0, a_im
        den = lre * lre + lim * lim
        cr = (nr * lre + ni * lim) / den
        ci = (ni * lre - nr * lim) / den
        cis = ci * sign_lo
        bbar = cr * bt_ref[0, d] + cis * bts_ref[0, d]
        bbar_s = cr * bts_ref[0, d] - cis * bt_ref[0, d]
        cmat, cmat_s = c_ref[0, d], cs_ref[0, d]

        asc, ascs = power(k_idx)
        desc, descs = power(CHUNK - 1.0 - k_idx)
        if d == 0:
            state_in = outer(desc, descs, bbar, bbar_s)
            taps = outer(asc, ascs, bbar, bbar_s)
            ro, ros = power(k_idx + 1.0)
        else:
            state_in = outer(asc, ascs, bbar, bbar_s)
            taps = outer(desc, descs, bbar, bbar_s)
            ro, ros = power(CHUNK - k_idx)
        state_out = outer(ro, ros, cmat, cmat_s) * (-sign_lo)
        win_o[0, :, d * STATE_LANES:(d + 1) * STATE_LANES] = state_in
        coutt_o[0, :, d * STATE_LANES:(d + 1) * STATE_LANES] = state_out
        kms.append(lax.dot_general(taps, cmat * (-sign_lo), (((1,), (1,)), ((), ())),
                                   preferred_element_type=F32, precision=lax.Precision.HIGHEST))

        sr, sis = power(jnp.full((1, STATE_LANES), float(CHUNK), F32))
        for k in range(n_steps):
            p_o[0, k:k + 1, d * STATE_LANES:(d + 1) * STATE_LANES] = sr
            q_o[0, k:k + 1, d * STATE_LANES:(d + 1) * STATE_LANES] = sis
            si = sis * sign_lo
            sr, sis = sr * sr - si * si, 2.0 * sr * si * sign_lo

    km_f, km_b = kms
    rr = lax.broadcasted_iota(jnp.int32, (SSM_GROUP, SSM_GROUP), 0)
    cc = lax.broadcasted_iota(jnp.int32, (SSM_GROUP, SSM_GROUP), 1)
    skip = jnp.where(rr == cc, d_ref[0], 0.0)
    edge = (CHUNK - 1) * SSM_GROUP
    kcat_o[0, :edge, :] = km_b[:edge]
    kcat_o[0, edge:edge + SSM_GROUP, :] = km_b[edge:] + km_f[:SSM_GROUP] + skip
    kcat_o[0, edge + SSM_GROUP:, :] = km_f[SSM_GROUP:]


def _ssm_prep(lam_re, lam_im, log_dt, b_re, b_im, c_re, c_im, d_skip, n_steps):
    g = N_SSM_GROUPS
    dup = lambda a: jnp.concatenate([a, a], axis=-1)
    lre = dup(jnp.swapaxes(lam_re, 0, 1))
    lim = dup(jnp.swapaxes(lam_im, 0, 1))
    ldt = jnp.broadcast_to(jnp.swapaxes(log_dt, 0, 1)[..., None], (g, 2, STATE_LANES))
    btr = jnp.transpose(b_re, (1, 0, 3, 2))
    bti = jnp.transpose(b_im, (1, 0, 3, 2))
    bt, bts = jnp.concatenate([btr, bti], -1), jnp.concatenate([bti, btr], -1)
    cr, ci = jnp.swapaxes(c_re, 0, 1), jnp.swapaxes(c_im, 0, 1)
    cm, cms = jnp.concatenate([cr, ci], -1), jnp.concatenate([ci, cr], -1)
    dsk = d_skip.reshape(g, 1, SSM_GROUP)
    n_kcat = (2 * CHUNK - 1) * SSM_GROUP
    n_pad = 16
    vec = pl.BlockSpec((1, 2, STATE_LANES), lambda i: (i, 0, 0))
    mat = pl.BlockSpec((1, 2, SSM_GROUP, STATE_LANES), lambda i: (i, 0, 0, 0))
    sq = pl.BlockSpec((1, CHUNK_LANES, CHUNK_LANES), lambda i: (i, 0, 0))
    mult = pl.BlockSpec((1, n_pad, 2 * STATE_LANES), lambda i: (i, 0, 0))
    return pl.pallas_call(
        functools.partial(_ssm_prep_kernel, n_steps=n_steps),
        grid=(g,),
        in_specs=[vec, vec, vec, mat, mat, mat, mat,
                  pl.BlockSpec((1, 1, SSM_GROUP), lambda i: (i, 0, 0))],
        out_specs=[sq, sq, pl.BlockSpec((1, n_kcat, SSM_GROUP), lambda i: (i, 0, 0)), mult, mult],
        out_shape=[jax.ShapeDtypeStruct((g, CHUNK_LANES, CHUNK_LANES), F32),
                   jax.ShapeDtypeStruct((g, CHUNK_LANES, CHUNK_LANES), F32),
                   jax.ShapeDtypeStruct((g, n_kcat, SSM_GROUP), F32),
                   jax.ShapeDtypeStruct((g, n_pad, 2 * STATE_LANES), F32),
                   jax.ShapeDtypeStruct((g, n_pad, 2 * STATE_LANES), F32)],
        compiler_params=_cparams("parallel"),
        name="ssm_prep",
    )(lre, lim, ldt, bt, bts, cm, cms, dsk)


def _shift_rows(x, s, rows, row, down):
    if s % 8 == 0:
        pad = jnp.zeros((s, x.shape[1]), x.dtype)
        return (jnp.concatenate([pad, x[:rows - s]], axis=0) if down
                else jnp.concatenate([x[s:], pad], axis=0))
    if down:
        return jnp.where(row >= s, pltpu.roll(x, s, 0), 0.0)
    return jnp.where(row < rows - s, pltpu.roll(x, rows - s, 0), 0.0)


def _ssm_kernel(u_ref, win_ref, m_ref, cout_ref, mult_ref, y_o, *, n_ctx_rows, n_steps):
    u0, u1 = u_ref[0, 0], u_ref[0, 1]
    rows = u0.shape[0]
    n_lat_rows = rows - n_ctx_rows
    x = jnp.dot(jnp.concatenate([u0, u1], axis=1), win_ref[0], preferred_element_type=F32)
    re_f, im_f = x[:, :LANES], x[:, LANES:2 * LANES]
    to_rev = lambda a: jnp.concatenate([a[n_ctx_rows:], a[:n_ctx_rows]], axis=0)
    re_b, im_b = to_rev(x[:, 2 * LANES:3 * LANES]), to_rev(x[:, 3 * LANES:])
    row = lax.broadcasted_iota(jnp.int32, (rows, LANES), 0)
    for k in range(n_steps):
        s = 1 << k
        ar_f, ai_f = mult_ref[0, k:k + 1, :LANES], mult_ref[0, k:k + 1, LANES:2 * LANES]
        ar_b, ai_b = mult_ref[0, k:k + 1, 2 * LANES:3 * LANES], mult_ref[0, k:k + 1, 3 * LANES:]
        sr, si = _shift_rows(re_f, s, rows, row, True), _shift_rows(im_f, s, rows, row, True)
        re_f, im_f = re_f + ar_f * sr - ai_f * si, im_f + ar_f * si + ai_f * sr
        sr, si = _shift_rows(re_b, s, rows, row, False), _shift_rows(im_b, s, rows, row, False)
        re_b, im_b = re_b + ar_b * sr - ai_b * si, im_b + ar_b * si + ai_b * sr
    prev = lambda a: pltpu.roll(a, 1, 0)[n_ctx_rows:]
    nxt = lambda a: pltpu.roll(a, rows - 1, 0)[:n_lat_rows]
    st = jnp.concatenate([prev(re_f), prev(im_f), nxt(re_b), nxt(im_b)], axis=1).astype(BF16)
    y = jnp.dot(st, cout_ref[0], preferred_element_type=F32)
    y_o[0, 0] = (y[:, :CHUNK_LANES]
                 + jnp.dot(u0[n_ctx_rows:], m_ref[0], preferred_element_type=F32)).astype(BF16)
    y_o[0, 1] = (y[:, CHUNK_LANES:]
                 + jnp.dot(u1[n_ctx_rows:], m_ref[1], preferred_element_type=F32)).astype(BF16)


def _pair_operators(win, cout, p, q):
    g = N_SSM_GROUPS
    eye = jnp.eye(2, dtype=win.dtype)
    win5 = win.reshape(g // 2, 2, CHUNK_LANES, 4, SSM_STATE)
    win_pair = jnp.einsum('pgrbn,gh->pgrbhn', win5, eye).reshape(g // 2, 2 * CHUNK_LANES, 4 * LANES)
    cout5 = cout.reshape(g // 2, 2, 4, SSM_STATE, CHUNK_LANES)
    cout_pair = jnp.einsum('pgbnc,gh->pbhngc', cout5, eye).reshape(g // 2, 4 * LANES, 2 * CHUNK_LANES)
    parts = [p[..., :SSM_STATE], q[..., SSM_STATE:STATE_LANES],
             p[..., STATE_LANES:STATE_LANES + SSM_STATE], q[..., STATE_LANES + SSM_STATE:]]
    pair = lambda a: jnp.transpose(a.reshape(g // 2, 2, a.shape[1], SSM_STATE), (0, 2, 1, 3)).reshape(
        g // 2, a.shape[1], LANES)
    mult = jnp.concatenate([pair(a) for a in parts], axis=-1)
    return win_pair, cout_pair, mult


def _ssm(u_cf, win_pair, m, cout_pair, mult, n_ctx_rows, n_steps):
    bsz, g, rows, _ = u_cf.shape
    pair_spec = lambda a: pl.BlockSpec((1,) + a.shape[1:], lambda b, i: (i, 0, 0))
    return pl.pallas_call(
        functools.partial(_ssm_kernel, n_ctx_rows=n_ctx_rows, n_steps=n_steps),
        grid=(bsz, g // 2),
        in_specs=[pl.BlockSpec((1, 2, rows, CHUNK_LANES), lambda b, i: (b, i, 0, 0)),
                  pair_spec(win_pair),
                  pl.BlockSpec((2, CHUNK_LANES, CHUNK_LANES), lambda b, i: (i, 0, 0)),
                  pair_spec(cout_pair), pair_spec(mult)],
        out_specs=pl.BlockSpec((1, 2, rows - n_ctx_rows, CHUNK_LANES), lambda b, i: (b, i, 0, 0)),
        out_shape=jax.ShapeDtypeStruct((bsz, g, rows - n_ctx_rows, CHUNK_LANES), BF16),
        compiler_params=_cparams("parallel", "parallel"),
        name="ssm",
    )(u_cf, win_pair, m, cout_pair, mult)


def _final_kernel(x_ref, at_ref, ga_ref, y_ref, gs_ref, gate_ref, wglu_ref, bglu_ref, wout_ref,
                  pg_ref, o_ref, y_scr):
    _from_chunk_lanes(y_ref, y_scr)
    y = jax.nn.gelu(jnp.concatenate([y_scr[col] for col in range(SSM_WIDTH // LANES)], axis=1))
    t = jnp.dot(y.astype(BF16), wglu_ref[...], preferred_element_type=F32) + bglu_ref[...]
    ssm = y * jax.nn.sigmoid(t)
    a = (at_ref[0].astype(F32) * ga_ref[0].astype(F32)).astype(BF16)
    s = (ssm * gs_ref[0].astype(F32)).astype(BF16)
    out = jnp.dot(a, wout_ref[:ATTN_WIDTH, :], preferred_element_type=F32)
    out = out + jnp.dot(s, wout_ref[ATTN_WIDTH:, :], preferred_element_type=F32)
    ms = jnp.mean(out * out, axis=-1, keepdims=True)
    r = out * lax.rsqrt(ms + NORM_EPS) * pg_ref[...]
    o_ref[0] = x_ref[0] + gate_ref[0] * r


def _final(x, attn, ga, y_cf, gs, gate, wglu_bf, b_glu, wout_bf, post_g, tm):
    bsz, n, d = x.shape
    assert n % tm == 0 and tm % (CHUNK * 16) == 0
    return pl.pallas_call(
        _final_kernel,
        grid=(bsz, n // tm),
        in_specs=[_row_spec(tm, d), _row_spec(tm, ATTN_WIDTH), _row_spec(tm, ATTN_WIDTH),
                  pl.BlockSpec((1, N_SSM_GROUPS, tm // CHUNK, CHUNK_LANES), lambda b, i: (b, 0, i, 0)),
                  _row_spec(tm, SSM_WIDTH), _batch_vec_spec(d),
                  _const_spec((SSM_WIDTH, SSM_WIDTH)), _const_spec((1, SSM_WIDTH)),
                  _const_spec((d, d)), _const_spec((1, d))],
        out_specs=_row_spec(tm, d),
        out_shape=jax.ShapeDtypeStruct((bsz, n, d), F32),
        scratch_shapes=[pltpu.VMEM((SSM_WIDTH // LANES, tm, LANES), F32)],
        compiler_params=_cparams("parallel", "parallel"),
        name="final",
    )(x, attn, ga, y_cf, gs, gate, wglu_bf, b_glu, wout_bf, post_g)


def _rope_tables(n_ctx, n_lat):
    t = jnp.arange(n_lat, dtype=jnp.int32)
    row_pos = (t // GRID_W).astype(F32)
    col_pos = (t % GRID_W).astype(F32)
    inv_freq = ROPE_THETA ** (-jnp.arange(ROPE_FREQS, dtype=F32) / ROPE_FREQS)
    ang_r = row_pos[:, None] * inv_freq
    ang_c = col_pos[:, None] * inv_freq
    cos = jnp.concatenate([jnp.cos(ang_r)] * 2 + [jnp.cos(ang_c)] * 2, axis=1)
    sin = jnp.concatenate([-jnp.sin(ang_r), jnp.sin(ang_r), -jnp.sin(ang_c), jnp.sin(ang_c)], axis=1)
    cos = jnp.concatenate([jnp.ones((n_ctx, HEAD_DIM), F32), cos], axis=0)
    sin = jnp.concatenate([jnp.zeros((n_ctx, HEAD_DIM), F32), sin], axis=0)
    return jnp.tile(cos, (1, LANES // HEAD_DIM)), jnp.tile(sin, (1, LANES // HEAD_DIM))


def _toeplitz(kcat):
    j = jnp.arange(CHUNK)
    lag = j[None, :] - j[:, None] + CHUNK - 1
    blocks = kcat[:, lag]
    return jnp.transpose(blocks, (0, 1, 3, 2, 4)).reshape(N_SSM_GROUPS, CHUNK_LANES, CHUNK_LANES)


def _layer(x, ctx, c, c_ctx, w_ada, b_ada, pre_g, post_g, w_in, q_g, k_g, lam_re, lam_im, log_dt,
           b_re, b_im, c_re, c_im, d_skip, w_glu, b_glu, w_out):
    bsz, n_lat, d = x.shape
    n_ctx = ctx.shape[1]
    assert n_lat % GRID_W == 0

    cvecs = jnp.zeros((8, d), F32).at[:bsz].set(c).at[bsz].set(c_ctx)
    mod = _adaln(cvecs, w_ada, b_ada)
    shift, scale, gate = (mod[:, i * d:(i + 1) * d] for i in range(3))
    lat = lambda a: a[:bsz].reshape(bsz, 1, d)
    cvec = lambda a: a[bsz].reshape(1, 1, d)

    head = jnp.arange(ATTN_WIDTH) // HEAD_DIM
    bd = (head[:, None] == head[None, :]).astype(BF16) * (1.0 / HEAD_DIM)
    qg = jnp.tile(q_g, N_Q_HEADS).reshape(1, ATTN_WIDTH)
    kg = jnp.tile(k_g, N_KV_HEADS).reshape(1, KV_WIDTH)
    cos, sin = _rope_tables(n_ctx, n_lat)

    q, k_all, vt_aug, ga, u_cf, gs = _inproj(x, ctx, lat(shift), lat(scale), cvec(shift), cvec(scale),
                                             pre_g.reshape(1, d), w_in.astype(BF16), qg, kg, cos, sin,
                                             bd, tm=256)
    attn = _attention(q, k_all, vt_aug, tq=256, tk=2816, sub=256, lookahead=5, heads_per_pass=8)

    rows = (n_ctx + n_lat) // CHUNK
    n_steps = _scan_steps(rows)
    win, coutt, kcat, p, qm = _ssm_prep(lam_re, lam_im, log_dt, b_re, b_im, c_re, c_im, d_skip, n_steps)
    m_op = _toeplitz(kcat.reshape(N_SSM_GROUPS, 2 * CHUNK - 1, SSM_GROUP, SSM_GROUP))
    win_pair, cout_pair, mult = _pair_operators(win, jnp.swapaxes(coutt, 1, 2), p, qm)
    y_cf = _ssm(u_cf, win_pair.astype(BF16), m_op.astype(BF16), cout_pair.astype(BF16), mult,
                n_ctx // CHUNK, n_steps)

    return _final(x, attn, ga, y_cf, gs, lat(gate), w_glu.astype(BF16), b_glu.reshape(1, SSM_WIDTH),
                  w_out.astype(BF16), post_g.reshape(1, d), tm=512)


def kernel(x, c, ctx, c_ctx, w_ada, b_ada, pre_norm, post_norm, w_in, q_norm, k_norm, ssm_lam_re,
           ssm_lam_im, ssm_log_dt, ssm_b_re, ssm_b_im, ssm_c_re, ssm_c_im, ssm_d, w_glu, b_glu, w_out):
    depth = w_ada.shape[0]
    assert depth == 1, "context stream update between layers is not implemented"
    return _layer(x, ctx, c, c_ctx, w_ada[0], b_ada[0], pre_norm[0], post_norm[0], w_in[0], q_norm[0],
                  k_norm[0], ssm_lam_re[0], ssm_lam_im[0], ssm_log_dt[0], ssm_b_re[0], ssm_b_im[0],
                  ssm_c_re[0], ssm_c_im[0], ssm_d[0], w_glu[0], b_glu[0], w_out[0])
```

```python
import functools
import math

import jax
import jax.numpy as jnp
from jax import lax
from jax.experimental import pallas as pl
from jax.experimental.pallas import tpu as pltpu

F32 = jnp.float32
BF16 = jnp.bfloat16

D_MODEL = 1024
HEAD_DIM = 64
N_Q_HEADS = 8
N_KV_HEADS = 2
ATTN_WIDTH = N_Q_HEADS * HEAD_DIM
KV_WIDTH = N_KV_HEADS * HEAD_DIM
SSM_WIDTH = 512
SSM_GROUP = 16
N_SSM_GROUPS = SSM_WIDTH // SSM_GROUP
SSM_STATE = 64
GRID_W = 64
ROPE_THETA = 10000.0
ROPE_FREQS = 16
NORM_EPS = 1e-6
ATTN_SCALE = HEAD_DIM ** -0.5
LOG2_E = math.log2(math.e)
Q_END = ATTN_WIDTH
K_END = Q_END + KV_WIDTH
V_END = K_END + KV_WIDTH
GA_END = V_END + ATTN_WIDTH
U_END = GA_END + SSM_WIDTH
IN_WIDTH = U_END + SSM_WIDTH

CHUNK = 16
CHUNK_LANES = CHUNK * SSM_GROUP
LANES = 128
GROUPS_PER_VREG = LANES // SSM_GROUP
STATE_LANES = 2 * SSM_STATE
V_ROWS = HEAD_DIM + 16
VMEM_LIMIT = 48 * 1024 * 1024
NEG_BIG = -1e30


def _cparams(*sem):
    return pltpu.CompilerParams(dimension_semantics=sem, vmem_limit_bytes=VMEM_LIMIT)


def _row_spec(tm, width):
    return pl.BlockSpec((1, tm, width), lambda b, i: (b, i, 0))


def _const_spec(shape):
    return pl.BlockSpec(shape, lambda b, i: (0,) * len(shape))


def _batch_vec_spec(width):
    return pl.BlockSpec((1, 1, width), lambda b, i: (b, 0, 0))


def _adaln_kernel(c_ref, w_ref, b_ref, o_ref):
    c = c_ref[...]
    s = c * jax.nn.sigmoid(c)
    o_ref[...] = jnp.dot(s, w_ref[...], preferred_element_type=F32,
                         precision=lax.Precision.HIGHEST) + b_ref[...]


def _adaln(cvecs, w_ada, b_ada):
    rows, d = cvecs.shape
    n = w_ada.shape[1]
    tn = 512
    return pl.pallas_call(
        _adaln_kernel,
        grid=(n // tn,),
        in_specs=[pl.BlockSpec((rows, d), lambda j: (0, 0)),
                  pl.BlockSpec((d, tn), lambda j: (0, j)),
                  pl.BlockSpec((1, tn), lambda j: (0, j))],
        out_specs=pl.BlockSpec((rows, tn), lambda j: (0, j)),
        out_shape=jax.ShapeDtypeStruct((rows, n), F32),
        compiler_params=_cparams("arbitrary"),
        name="adaln",
    )(cvecs, w_ada, b_ada.reshape(1, n))


def _lane_group(rows):
    return lax.broadcasted_iota(jnp.int32, (rows, LANES), 1) // SSM_GROUP


def _to_chunk_lanes(u_scr, u_o):
    rt = u_o.shape[2]
    grp = _lane_group(rt)
    for col in range(SSM_WIDTH // LANES):
        rolled = []
        for step in range(CHUNK):
            s = u_scr[col, pl.ds(step, rt, stride=CHUNK), :]
            rolled.append([s if k == 0 else pltpu.roll(s, k * SSM_GROUP, 1)
                           for k in range(GROUPS_PER_VREG)])
        for g_lo in range(GROUPS_PER_VREG):
            for half in range(CHUNK // GROUPS_PER_VREG):
                out = None
                for s8 in range(GROUPS_PER_VREG):
                    piece = rolled[half * GROUPS_PER_VREG + s8][(s8 - g_lo) % GROUPS_PER_VREG]
                    out = piece if out is None else jnp.where(grp == s8, piece, out)
                u_o[0, col * GROUPS_PER_VREG + g_lo, :, half * LANES:(half + 1) * LANES] = out.astype(BF16)


def _from_chunk_lanes(y_ref, y_scr):
    rt = y_ref.shape[2]
    grp = _lane_group(rt)
    for col in range(SSM_WIDTH // LANES):
        rolled = {}
        for g_lo in range(GROUPS_PER_VREG):
            for half in range(CHUNK // GROUPS_PER_VREG):
                s = y_ref[0, col * GROUPS_PER_VREG + g_lo, :, half * LANES:(half + 1) * LANES].astype(F32)
                rolled[g_lo, half] = [s if k == 0 else pltpu.roll(s, k * SSM_GROUP, 1)
                                      for k in range(GROUPS_PER_VREG)]
        for step in range(CHUNK):
            half, s8 = divmod(step, GROUPS_PER_VREG)
            out = None
            for g_lo in range(GROUPS_PER_VREG):
                piece = rolled[g_lo, half][(g_lo - s8) % GROUPS_PER_VREG]
                out = piece if out is None else jnp.where(grp == g_lo, piece, out)
            y_scr[col, pl.ds(step, rt, stride=CHUNK), :] = out


def _head_mean_sq(z, bd):
    return jnp.dot((z * z).astype(BF16), bd, preferred_element_type=F32)


def _swap16(x):
    w = x.shape[1]
    lane = lax.broadcasted_iota(jnp.int32, x.shape, 1)
    return jnp.where((lane & 16) == 0, pltpu.roll(x, w - 16, 1), pltpu.roll(x, 16, 1))


def _rope(x, cos, sin_signed):
    cols = []
    for c in range(x.shape[1] // LANES):
        xc = x[:, c * LANES:(c + 1) * LANES]
        cols.append(xc * cos + _swap16(xc) * sin_signed)
    return cols[0] if len(cols) == 1 else jnp.concatenate(cols, axis=1)


def _silu(z):
    return z * jax.nn.sigmoid(z)


def _inproj_kernel(x_ref, c_ref, shl_ref, scl_ref, shc_ref, scc_ref, pg_ref, w_ref, qg_ref, kg_ref,
                   cos_ref, sin_ref, bd_ref, q_o, k_o, vt_o, ga_o, u_o, gs_o, v_scr, u_scr, *, n_ctx_tiles):
    is_ctx = pl.program_id(1) < n_ctx_tiles
    x = jnp.where(is_ctx, c_ref[0], x_ref[0])
    shift = jnp.where(is_ctx, shc_ref[0], shl_ref[0])
    scale = jnp.where(is_ctx, scc_ref[0], scl_ref[0])
    ms = jnp.mean(x * x, axis=-1, keepdims=True)
    xn = x * lax.rsqrt(ms + NORM_EPS) * pg_ref[...]
    h = (xn * (1.0 + scale) + shift).astype(BF16)

    def proj(a, b):
        return jnp.dot(h, w_ref[:, a:b], preferred_element_type=F32)

    cos = cos_ref[...]
    sin = sin_ref[...]
    zq = proj(0, Q_END)
    qn = zq * lax.rsqrt(_head_mean_sq(zq, bd_ref[...]) + NORM_EPS) * qg_ref[...]
    q_o[0] = (_rope(qn, cos, sin) * (ATTN_SCALE * LOG2_E)).astype(BF16)
    zk = proj(Q_END, K_END)
    kn = zk * lax.rsqrt(_head_mean_sq(zk, bd_ref[:KV_WIDTH, :KV_WIDTH]) + NORM_EPS) * kg_ref[...]
    k_o[0] = _rope(kn, cos, sin).astype(BF16)

    v_scr[...] = proj(K_END, V_END)
    vt = v_scr[...].T.astype(BF16)
    tm = vt.shape[1]
    ones_row = lax.broadcasted_iota(jnp.int32, (V_ROWS - HEAD_DIM, tm), 0) == 0
    for j in range(N_KV_HEADS):
        vt_o[0, j, :HEAD_DIM, :] = vt[j * HEAD_DIM:(j + 1) * HEAD_DIM]
        vt_o[0, j, HEAD_DIM:, :] = jnp.where(ones_row, 1.0, 0.0).astype(BF16)

    ga_o[0] = _silu(proj(V_END, GA_END)).astype(BF16)
    u = proj(GA_END, U_END)
    for col in range(SSM_WIDTH // LANES):
        u_scr[col] = u[:, col * LANES:(col + 1) * LANES]
    _to_chunk_lanes(u_scr, u_o)
    gs_o[0] = _silu(proj(U_END, IN_WIDTH)).astype(BF16)


def _inproj(x, ctx, shift_l, scale_l, shift_c, scale_c, pre_g, w_bf, qg, kg, cos, sin, bd, tm):
    bsz, n_lat, d = x.shape
    n_ctx = ctx.shape[1]
    assert n_ctx % tm == 0 and n_lat % tm == 0 and tm % (CHUNK * 16) == 0
    nct = n_ctx // tm
    n_tot = n_ctx + n_lat
    lat_rows = lambda w: pl.BlockSpec((1, tm, w), lambda b, i: (b, jnp.maximum(i - nct, 0), 0))
    all_rows = lambda w: pl.BlockSpec((1, tm, w), lambda b, i: (b, i, 0))
    ctx_vec = pl.BlockSpec((1, 1, d), lambda b, i: (0, 0, 0))
    table = pl.BlockSpec((tm, LANES), lambda b, i: (i, 0))
    return pl.pallas_call(
        functools.partial(_inproj_kernel, n_ctx_tiles=nct),
        grid=(bsz, n_tot // tm),
        in_specs=[lat_rows(d),
                  pl.BlockSpec((1, tm, d), lambda b, i: (b, jnp.minimum(i, nct - 1), 0)),
                  _batch_vec_spec(d), _batch_vec_spec(d), ctx_vec, ctx_vec, _const_spec((1, d)),
                  _const_spec((d, IN_WIDTH)), _const_spec((1, ATTN_WIDTH)), _const_spec((1, KV_WIDTH)),
                  table, table, _const_spec((ATTN_WIDTH, ATTN_WIDTH))],
        out_specs=[lat_rows(ATTN_WIDTH), all_rows(KV_WIDTH),
                   pl.BlockSpec((1, N_KV_HEADS, V_ROWS, tm), lambda b, i: (b, 0, 0, i)),
                   lat_rows(ATTN_WIDTH),
                   pl.BlockSpec((1, N_SSM_GROUPS, tm // CHUNK, CHUNK_LANES), lambda b, i: (b, 0, i, 0)),
                   lat_rows(SSM_WIDTH)],
        out_shape=[jax.ShapeDtypeStruct((bsz, n_lat, ATTN_WIDTH), BF16),
                   jax.ShapeDtypeStruct((bsz, n_tot, KV_WIDTH), BF16),
                   jax.ShapeDtypeStruct((bsz, N_KV_HEADS, V_ROWS, n_tot), BF16),
                   jax.ShapeDtypeStruct((bsz, n_lat, ATTN_WIDTH), BF16),
                   jax.ShapeDtypeStruct((bsz, N_SSM_GROUPS, n_tot // CHUNK, CHUNK_LANES), BF16),
                   jax.ShapeDtypeStruct((bsz, n_lat, SSM_WIDTH), BF16)],
        scratch_shapes=[pltpu.VMEM((tm, KV_WIDTH), F32), pltpu.VMEM((SSM_WIDTH // LANES, tm, LANES), F32)],
        compiler_params=_cparams("parallel", "arbitrary"),
        name="inproj",
    )(x, ctx, shift_l, scale_l, shift_c, scale_c, pre_g, w_bf, qg, kg, cos, sin, bd)


def _attn_kernel(q_ref, k_ref, vt_ref, o_ref, *, tk, sub, lookahead, heads_per_pass):
    tq = q_ref.shape[1]
    n_keys = k_ref.shape[1]
    group = N_Q_HEADS // N_KV_HEADS
    q = q_ref[0]
    low = lax.broadcasted_iota(jnp.int32, (tq, LANES), 1) < HEAD_DIM

    def widen(h):
        col = q[:, (h // 2) * LANES:(h // 2 + 1) * LANES].astype(F32)
        dst_high = h // group == 1
        if (h % 2 == 1) != dst_high:
            col = pltpu.roll(col, HEAD_DIM, 1)
        return jnp.where(low != dst_high, col, 0.0).astype(BF16)

    outs = []
    for h0 in range(0, N_Q_HEADS, heads_per_pass):
        heads = list(range(h0, h0 + heads_per_pass))
        q_wide = [widen(h) for h in heads]

        def body(t, carry, heads=heads, q_wide=q_wide):
            tasks = [(j, i) for j in range(tk // sub) for i in range(len(heads))]
            state = list(carry)
            scores = {}

            def keys_at(j):
                return pl.ds(pl.multiple_of(t * tk + j * sub, sub), sub)

            def issue(n):
                j, i = tasks[n]
                scores[n] = lax.dot_general(k_ref[0, keys_at(j), :], q_wide[i], (((1,), (1,)), ((), ())),
                                            preferred_element_type=F32)

            def consume(n):
                j, i = tasks[n]
                s = scores.pop(n)
                m_old, acc = state[i]
                vt = vt_ref[0, heads[i] // group, :, keys_at(j)]
                m_new = jnp.maximum(m_old, jnp.max(s, axis=0, keepdims=True))
                p = jnp.exp2((s - m_new).astype(BF16))
                acc = jnp.exp2(m_old - m_new) * acc + jnp.dot(vt, p, preferred_element_type=F32)
                state[i] = (m_new, acc)

            for n in range(len(tasks) + lookahead):
                if n < len(tasks):
                    issue(n)
                if n >= lookahead:
                    consume(n - lookahead)
            return tuple(state)

        init = tuple((jnp.full((1, tq), NEG_BIG, F32), jnp.zeros((V_ROWS, tq), F32)) for _ in heads)
        for _, acc in lax.fori_loop(0, n_keys // tk, body, init):
            outs.append(acc[:HEAD_DIM] / acc[HEAD_DIM:HEAD_DIM + 1])

    for c in range(N_Q_HEADS // 2):
        pair = jnp.concatenate([outs[2 * c], outs[2 * c + 1]], axis=0)
        o_ref[0, :, c * LANES:(c + 1) * LANES] = pair.T.astype(BF16)


def _attention(q, k_all, vt_aug, tq, tk, sub, lookahead, heads_per_pass):
    bsz, n, _ = q.shape
    n_keys = k_all.shape[1]
    assert n_keys % tk == 0 and tk % sub == 0 and n % tq == 0
    return pl.pallas_call(
        functools.partial(_attn_kernel, tk=tk, sub=sub, lookahead=lookahead,
                          heads_per_pass=heads_per_pass),
        grid=(bsz, n // tq),
        in_specs=[_row_spec(tq, ATTN_WIDTH),
                  pl.BlockSpec((1, n_keys, KV_WIDTH), lambda b, i: (b, 0, 0)),
                  pl.BlockSpec((1, N_KV_HEADS, V_ROWS, n_keys), lambda b, i: (b, 0, 0, 0))],
        out_specs=_row_spec(tq, ATTN_WIDTH),
        out_shape=jax.ShapeDtypeStruct((bsz, n, ATTN_WIDTH), BF16),
        compiler_params=_cparams("parallel", "parallel"),
        name="attention",
    )(q, k_all, vt_aug)


def _scan_steps(n_rows):
    return max(1, math.ceil(math.log2(n_rows)))


def _ssm_prep_kernel(lre_ref, lim_ref, ldt_ref, bt_ref, bts_ref, c_ref, cs_ref, d_ref,
                     win_o, m_o, cout_o, mult_o, *, n_steps):
    lane = lax.broadcasted_iota(jnp.int32, (1, STATE_LANES), 1)
    low = lane < SSM_STATE
    sign_lo = jnp.where(low, -1.0, 1.0)
    k_idx = lax.broadcasted_iota(jnp.int32, (CHUNK, STATE_LANES), 0).astype(F32)
    step_of_row = lax.broadcasted_iota(jnp.int32, (CHUNK_LANES, STATE_LANES), 0) // SSM_GROUP
    nt = (((1,), (1,)), ((), ()))

    def outer(pw, pws, mat, mats):
        full = pw[:, None, :] * mat[None, :, :] + pws[:, None, :] * mats[None, :, :]
        return full.reshape(CHUNK * mat.shape[0], STATE_LANES)

    state_in, state_out, mults = {}, {}, {}
    for gi in range(2):
        taps_lo, taps_hi = None, None
        for d in range(2):
            lre = lre_ref[gi, d:d + 1, :]
            lim = lim_ref[gi, d:d + 1, :]
            dt = jnp.exp(ldt_ref[gi, d:d + 1, :])

            def power(expo, lre=lre, lim=lim, dt=dt):
                mag = jnp.exp(expo * dt * lre)
                ang = expo * dt * lim
                return mag * jnp.cos(ang), mag * jnp.sin(ang) * sign_lo

            a_re, a_ims = power(jnp.ones((1, STATE_LANES), F32))
            a_im = a_ims * sign_lo
            nr, ni = a_re - 1.0, a_im
            den = lre * lre + lim * lim
            cr = (nr * lre + ni * lim) / den
            ci = (ni * lre - nr * lim) / den
            cis = ci * sign_lo
            bbar = cr * bt_ref[gi, d] + cis * bts_ref[gi, d]
            bbar_s = cr * bts_ref[gi, d] - cis * bt_ref[gi, d]
            cmat, cmat_s = c_ref[gi, d], cs_ref[gi, d]

            asc, ascs = power(k_idx)
            desc, descs = power(CHUNK - 1.0 - k_idx)
            nxt, nxts = power(k_idx + 1.0)
            conj_c = lambda pw, pws: outer(pw, pws, cmat, cmat_s) * (-sign_lo)
            if d == 0:
                state_in[gi, d] = outer(desc, descs, bbar, bbar_s)
                state_out[gi, d] = conj_c(nxt, nxts)
                lag0 = jnp.where(step_of_row == CHUNK - 1, conj_c(desc, descs), 0.0)
                lags = jnp.where(step_of_row < CHUNK - 1, conj_c(nxt, nxts), 0.0)
                hi = lax.dot_general(bbar, lags, nt, preferred_element_type=F32,
                                     precision=lax.Precision.HIGHEST)
                lo = lax.dot_general(bbar, lag0, nt, preferred_element_type=F32,
                                     precision=lax.Precision.HIGHEST)
                taps_hi = hi
                taps_lo = lo if taps_lo is None else taps_lo + lo
            else:
                state_in[gi, d] = outer(asc, ascs, bbar, bbar_s)
                rev, revs = power(CHUNK - k_idx)
                state_out[gi, d] = conj_c(rev, revs)
                lo = lax.dot_general(bbar, conj_c(desc, descs), nt, preferred_element_type=F32,
                                     precision=lax.Precision.HIGHEST)
                taps_lo = lo if taps_lo is None else taps_lo + lo

            sr, sis = power(jnp.full((1, STATE_LANES), float(CHUNK), F32))
            chain = []
            for k in range(n_steps):
                chain.append((sr, sis))
                si = sis * sign_lo
                sr, sis = sr * sr - si * si, 2.0 * sr * si * sign_lo
            mults[gi, d] = chain

        qq = lax.broadcasted_iota(jnp.int32, (SSM_GROUP, CHUNK_LANES), 0)
        ll = lax.broadcasted_iota(jnp.int32, (SSM_GROUP, CHUNK_LANES), 1)
        taps_lo = taps_lo + jnp.where(ll == qq + (CHUNK - 1) * SSM_GROUP, d_ref[gi], 0.0)
        taps = jnp.concatenate([taps_lo, taps_hi], axis=1)
        for j in range(CHUNK):
            off = (CHUNK - 1 - j) * SSM_GROUP
            m_o[gi, j * SSM_GROUP:(j + 1) * SSM_GROUP, :] = taps[:, off:off + CHUNK_LANES].astype(BF16)

    def swap(x):
        return pltpu.roll(x, SSM_STATE, 1)

    mult_o[...] = jnp.zeros(mult_o.shape, F32)
    wide_low = lax.broadcasted_iota(jnp.int32, (CHUNK_LANES, STATE_LANES), 1) < SSM_STATE
    zeros_c = jnp.zeros((SSM_STATE, CHUNK_LANES), F32)
    for d in range(2):
        s0, s1 = state_in[0, d], state_in[1, d]
        re_blk = jnp.concatenate([jnp.where(wide_low, s0, 0.0), jnp.where(wide_low, 0.0, swap(s1))], axis=0)
        im_blk = jnp.concatenate([jnp.where(wide_low, swap(s0), 0.0), jnp.where(wide_low, 0.0, s1)], axis=0)
        win_o[0, :, (2 * d) * LANES:(2 * d + 1) * LANES] = re_blk.astype(BF16)
        win_o[0, :, (2 * d + 1) * LANES:(2 * d + 2) * LANES] = im_blk.astype(BF16)
        t0, t1 = state_out[0, d].T, state_out[1, d].T
        for part in range(2):
            rows = slice(part * SSM_STATE, (part + 1) * SSM_STATE)
            blk = jnp.concatenate([jnp.concatenate([t0[rows], zeros_c], axis=1),
                                   jnp.concatenate([zeros_c, t1[rows]], axis=1)], axis=0)
            cout_o[0, (2 * d + part) * LANES:(2 * d + part + 1) * LANES, :] = blk.astype(BF16)
        for k in range(n_steps):
            (sr0, sis0), (sr1, sis1) = mults[0, d][k], mults[1, d][k]
            mult_o[0, k:k + 1, (2 * d) * LANES:(2 * d + 1) * LANES] = jnp.where(low, sr0, sr1)
            mult_o[0, k:k + 1, (2 * d + 1) * LANES:(2 * d + 2) * LANES] = jnp.where(low, -sis0, sis1)


def _ssm_prep(lam_re, lam_im, log_dt, b_re, b_im, c_re, c_im, d_skip, n_steps):
    g = N_SSM_GROUPS
    dup = lambda a: jnp.concatenate([a, a], axis=-1)
    lre = dup(jnp.swapaxes(lam_re, 0, 1))
    lim = dup(jnp.swapaxes(lam_im, 0, 1))
    ldt = jnp.broadcast_to(jnp.swapaxes(log_dt, 0, 1)[..., None], (g, 2, STATE_LANES))
    btr = jnp.transpose(b_re, (1, 0, 3, 2))
    bti = jnp.transpose(b_im, (1, 0, 3, 2))
    bt, bts = jnp.concatenate([btr, bti], -1), jnp.concatenate([bti, btr], -1)
    cr, ci = jnp.swapaxes(c_re, 0, 1), jnp.swapaxes(c_im, 0, 1)
    cm, cms = jnp.concatenate([cr, ci], -1), jnp.concatenate([ci, cr], -1)
    dsk = jnp.pad(d_skip, ((0, 0), (CHUNK_LANES - SSM_GROUP, 0))).reshape(g, 1, CHUNK_LANES)
    n_pad = 16
    vec = pl.BlockSpec((2, 2, STATE_LANES), lambda i: (i, 0, 0))
    mat = pl.BlockSpec((2, 2, SSM_GROUP, STATE_LANES), lambda i: (i, 0, 0, 0))
    pair_sq = pl.BlockSpec((1, 2 * CHUNK_LANES, 4 * LANES), lambda i: (i, 0, 0))
    return pl.pallas_call(
        functools.partial(_ssm_prep_kernel, n_steps=n_steps),
        grid=(g // 2,),
        in_specs=[vec, vec, vec, mat, mat, mat, mat,
                  pl.BlockSpec((2, 1, CHUNK_LANES), lambda i: (i, 0, 0))],
        out_specs=[pair_sq, pl.BlockSpec((2, CHUNK_LANES, CHUNK_LANES), lambda i: (i, 0, 0)), pair_sq,
                   pl.BlockSpec((1, n_pad, 4 * LANES), lambda i: (i, 0, 0))],
        out_shape=[jax.ShapeDtypeStruct((g // 2, 2 * CHUNK_LANES, 4 * LANES), BF16),
                   jax.ShapeDtypeStruct((g, CHUNK_LANES, CHUNK_LANES), BF16),
                   jax.ShapeDtypeStruct((g // 2, 4 * LANES, 2 * CHUNK_LANES), BF16),
                   jax.ShapeDtypeStruct((g // 2, n_pad, 4 * LANES), F32)],
        compiler_params=_cparams("parallel"),
        name="ssm_prep",
    )(lre, lim, ldt, bt, bts, cm, cms, dsk)


def _shift_rows(x, s, rows, row, down):
    if s % 8 == 0:
        pad = jnp.zeros((s, x.shape[1]), x.dtype)
        return (jnp.concatenate([pad, x[:rows - s]], axis=0) if down
                else jnp.concatenate([x[s:], pad], axis=0))
    if down:
        return jnp.where(row >= s, pltpu.roll(x, s, 0), 0.0)
    return jnp.where(row < rows - s, pltpu.roll(x, rows - s, 0), 0.0)


def _ssm_kernel(u_ref, win_ref, m_ref, cout_ref, mult_ref, y_o, *, n_ctx_rows, n_steps):
    u0, u1 = u_ref[0, 0], u_ref[0, 1]
    rows = u0.shape[0]
    n_lat_rows = rows - n_ctx_rows
    x = jnp.dot(jnp.concatenate([u0, u1], axis=1), win_ref[0], preferred_element_type=F32)
    re_f, im_f = x[:, :LANES], x[:, LANES:2 * LANES]
    to_rev = lambda a: jnp.concatenate([a[n_ctx_rows:], a[:n_ctx_rows]], axis=0)
    re_b, im_b = to_rev(x[:, 2 * LANES:3 * LANES]), to_rev(x[:, 3 * LANES:])
    row = lax.broadcasted_iota(jnp.int32, (rows, LANES), 0)
    for k in range(n_steps):
        s = 1 << k
        ar_f, ai_f = mult_ref[0, k:k + 1, :LANES], mult_ref[0, k:k + 1, LANES:2 * LANES]
        ar_b, ai_b = mult_ref[0, k:k + 1, 2 * LANES:3 * LANES], mult_ref[0, k:k + 1, 3 * LANES:]
        sr, si = _shift_rows(re_f, s, rows, row, True), _shift_rows(im_f, s, rows, row, True)
        re_f, im_f = re_f + ar_f * sr - ai_f * si, im_f + ar_f * si + ai_f * sr
        sr, si = _shift_rows(re_b, s, rows, row, False), _shift_rows(im_b, s, rows, row, False)
        re_b, im_b = re_b + ar_b * sr - ai_b * si, im_b + ar_b * si + ai_b * sr
    prev = lambda a: pltpu.roll(a, 1, 0)[n_ctx_rows:]
    nxt = lambda a: pltpu.roll(a, rows - 1, 0)[:n_lat_rows]
    st = jnp.concatenate([prev(re_f), prev(im_f), nxt(re_b), nxt(im_b)], axis=1).astype(BF16)
    y = jnp.dot(st, cout_ref[0], preferred_element_type=F32)
    y_o[0, 0] = (y[:, :CHUNK_LANES]
                 + jnp.dot(u0[n_ctx_rows:], m_ref[0], preferred_element_type=F32)).astype(BF16)
    y_o[0, 1] = (y[:, CHUNK_LANES:]
                 + jnp.dot(u1[n_ctx_rows:], m_ref[1], preferred_element_type=F32)).astype(BF16)


def _ssm(u_cf, win_pair, m, cout_pair, mult, n_ctx_rows, n_steps):
    bsz, g, rows, _ = u_cf.shape
    pair_spec = lambda a: pl.BlockSpec((1,) + a.shape[1:], lambda b, i: (i, 0, 0))
    return pl.pallas_call(
        functools.partial(_ssm_kernel, n_ctx_rows=n_ctx_rows, n_steps=n_steps),
        grid=(bsz, g // 2),
        in_specs=[pl.BlockSpec((1, 2, rows, CHUNK_LANES), lambda b, i: (b, i, 0, 0)),
                  pair_spec(win_pair),
                  pl.BlockSpec((2, CHUNK_LANES, CHUNK_LANES), lambda b, i: (i, 0, 0)),
                  pair_spec(cout_pair), pair_spec(mult)],
        out_specs=pl.BlockSpec((1, 2, rows - n_ctx_rows, CHUNK_LANES), lambda b, i: (b, i, 0, 0)),
        out_shape=jax.ShapeDtypeStruct((bsz, g, rows - n_ctx_rows, CHUNK_LANES), BF16),
        compiler_params=_cparams("parallel", "parallel"),
        name="ssm",
    )(u_cf, win_pair, m, cout_pair, mult)


def _final_kernel(x_ref, at_ref, ga_ref, y_ref, gs_ref, gate_ref, wglu_ref, bglu_ref, wout_ref,
                  pg_ref, o_ref, y_scr):
    _from_chunk_lanes(y_ref, y_scr)
    y = jax.nn.gelu(jnp.concatenate([y_scr[col] for col in range(SSM_WIDTH // LANES)], axis=1))
    t = jnp.dot(y.astype(BF16), wglu_ref[...], preferred_element_type=F32) + bglu_ref[...]
    ssm = y * jax.nn.sigmoid(t)
    a = (at_ref[0].astype(F32) * ga_ref[0].astype(F32)).astype(BF16)
    s = (ssm * gs_ref[0].astype(F32)).astype(BF16)
    out = jnp.dot(a, wout_ref[:ATTN_WIDTH, :], preferred_element_type=F32)
    out = out + jnp.dot(s, wout_ref[ATTN_WIDTH:, :], preferred_element_type=F32)
    ms = jnp.mean(out * out, axis=-1, keepdims=True)
    r = out * lax.rsqrt(ms + NORM_EPS) * pg_ref[...]
    o_ref[0] = x_ref[0] + gate_ref[0] * r


def _final(x, attn, ga, y_cf, gs, gate, wglu_bf, b_glu, wout_bf, post_g, tm):
    bsz, n, d = x.shape
    assert n % tm == 0 and tm % (CHUNK * 16) == 0
    return pl.pallas_call(
        _final_kernel,
        grid=(bsz, n // tm),
        in_specs=[_row_spec(tm, d), _row_spec(tm, ATTN_WIDTH), _row_spec(tm, ATTN_WIDTH),
                  pl.BlockSpec((1, N_SSM_GROUPS, tm // CHUNK, CHUNK_LANES), lambda b, i: (b, 0, i, 0)),
                  _row_spec(tm, SSM_WIDTH), _batch_vec_spec(d),
                  _const_spec((SSM_WIDTH, SSM_WIDTH)), _const_spec((1, SSM_WIDTH)),
                  _const_spec((d, d)), _const_spec((1, d))],
        out_specs=_row_spec(tm, d),
        out_shape=jax.ShapeDtypeStruct((bsz, n, d), F32),
        scratch_shapes=[pltpu.VMEM((SSM_WIDTH // LANES, tm, LANES), F32)],
        compiler_params=_cparams("parallel", "parallel"),
        name="final",
    )(x, attn, ga, y_cf, gs, gate, wglu_bf, b_glu, wout_bf, post_g)


def _rope_tables(n_ctx, n_lat):
    t = jnp.arange(n_lat, dtype=jnp.int32)
    row_pos = (t // GRID_W).astype(F32)
    col_pos = (t % GRID_W).astype(F32)
    inv_freq = ROPE_THETA ** (-jnp.arange(ROPE_FREQS, dtype=F32) / ROPE_FREQS)
    ang_r = row_pos[:, None] * inv_freq
    ang_c = col_pos[:, None] * inv_freq
    cos = jnp.concatenate([jnp.cos(ang_r)] * 2 + [jnp.cos(ang_c)] * 2, axis=1)
    sin = jnp.concatenate([-jnp.sin(ang_r), jnp.sin(ang_r), -jnp.sin(ang_c), jnp.sin(ang_c)], axis=1)
    cos = jnp.concatenate([jnp.ones((n_ctx, HEAD_DIM), F32), cos], axis=0)
    sin = jnp.concatenate([jnp.zeros((n_ctx, HEAD_DIM), F32), sin], axis=0)
    return jnp.tile(cos, (1, LANES // HEAD_DIM)), jnp.tile(sin, (1, LANES // HEAD_DIM))


def _layer(x, ctx, c, c_ctx, w_ada, b_ada, pre_g, post_g, w_in, q_g, k_g, lam_re, lam_im, log_dt,
           b_re, b_im, c_re, c_im, d_skip, w_glu, b_glu, w_out):
    bsz, n_lat, d = x.shape
    n_ctx = ctx.shape[1]
    assert n_lat % GRID_W == 0

    cvecs = jnp.zeros((8, d), F32).at[:bsz].set(c).at[bsz].set(c_ctx)
    mod = _adaln(cvecs, w_ada, b_ada)
    shift, scale, gate = (mod[:, i * d:(i + 1) * d] for i in range(3))
    lat = lambda a: a[:bsz].reshape(bsz, 1, d)
    cvec = lambda a: a[bsz].reshape(1, 1, d)

    head = jnp.arange(ATTN_WIDTH) // HEAD_DIM
    bd = (head[:, None] == head[None, :]).astype(BF16) * (1.0 / HEAD_DIM)
    qg = jnp.tile(q_g, N_Q_HEADS).reshape(1, ATTN_WIDTH)
    kg = jnp.tile(k_g, N_KV_HEADS).reshape(1, KV_WIDTH)
    cos, sin = _rope_tables(n_ctx, n_lat)

    q, k_all, vt_aug, ga, u_cf, gs = _inproj(x, ctx, lat(shift), lat(scale), cvec(shift), cvec(scale),
                                             pre_g.reshape(1, d), w_in.astype(BF16), qg, kg, cos, sin,
                                             bd, tm=256)
    attn = _attention(q, k_all, vt_aug, tq=256, tk=2816, sub=256, lookahead=5, heads_per_pass=8)

    rows = (n_ctx + n_lat) // CHUNK
    n_steps = _scan_steps(rows)
    win_pair, m_op, cout_pair, mult = _ssm_prep(lam_re, lam_im, log_dt, b_re, b_im, c_re, c_im, d_skip,
                                                n_steps)
    y_cf = _ssm(u_cf, win_pair, m_op, cout_pair, mult, n_ctx // CHUNK, n_steps)

    return _final(x, attn, ga, y_cf, gs, lat(gate), w_glu.astype(BF16), b_glu.reshape(1, SSM_WIDTH),
                  w_out.astype(BF16), post_g.reshape(1, d), tm=512)


def kernel(x, c, ctx, c_ctx, w_ada, b_ada, pre_norm, post_norm, w_in, q_norm, k_norm, ssm_lam_re,
           ssm_lam_im, ssm_log_dt, ssm_b_re, ssm_b_im, ssm_c_re, ssm_c_im, ssm_d, w_glu, b_glu, w_out):
    depth = w_ada.shape[0]
    assert depth == 1, "context stream update between layers is not implemented"
    return _layer(x, ctx, c, c_ctx, w_ada[0], b_ada[0], pre_norm[0], post_norm[0], w_in[0], q_norm[0],
                  k_norm[0], ssm_lam_re[0], ssm_lam_im[0], ssm_log_dt[0], ssm_b_re[0], ssm_b_im[0],
                  ssm_c_re[0], ssm_c_im[0], ssm_d[0], w_glu[0], b_glu[0], w_out[0])
```

```python
import functools
import math

import jax
import jax.numpy as jnp
from jax import lax
from jax.experimental import pallas as pl
from jax.experimental.pallas import tpu as pltpu

F32 = jnp.float32
BF16 = jnp.bfloat16

D_MODEL = 1024
HEAD_DIM = 64
N_Q_HEADS = 8
N_KV_HEADS = 2
ATTN_WIDTH = N_Q_HEADS * HEAD_DIM
KV_WIDTH = N_KV_HEADS * HEAD_DIM
SSM_WIDTH = 512
SSM_GROUP = 16
N_SSM_GROUPS = SSM_WIDTH // SSM_GROUP
SSM_STATE = 64
GRID_W = 64
ROPE_THETA = 10000.0
ROPE_FREQS = 16
NORM_EPS = 1e-6
ATTN_SCALE = HEAD_DIM ** -0.5
LOG2_E = math.log2(math.e)
Q_END = ATTN_WIDTH
K_END = Q_END + KV_WIDTH
V_END = K_END + KV_WIDTH
GA_END = V_END + ATTN_WIDTH
U_END = GA_END + SSM_WIDTH
IN_WIDTH = U_END + SSM_WIDTH

CHUNK = 16
CHUNK_LANES = CHUNK * SSM_GROUP
LANES = 128
GROUPS_PER_VREG = LANES // SSM_GROUP
STATE_LANES = 2 * SSM_STATE
SUBLANES = 8
POWER_ROW0 = 16
V_ROWS = HEAD_DIM + 16
VMEM_LIMIT = 48 * 1024 * 1024
NEG_BIG = -1e30


def _cparams(*sem):
    return pltpu.CompilerParams(dimension_semantics=sem, vmem_limit_bytes=VMEM_LIMIT)


def _row_spec(tm, width):
    return pl.BlockSpec((1, tm, width), lambda b, i: (b, i, 0))


def _const_spec(shape):
    return pl.BlockSpec(shape, lambda b, i: (0,) * len(shape))


def _batch_vec_spec(width):
    return pl.BlockSpec((1, 1, width), lambda b, i: (b, 0, 0))


def _adaln_kernel(c_ref, w_ref, b_ref, o_ref):
    c = c_ref[...]
    s = c * jax.nn.sigmoid(c)
    o_ref[...] = jnp.dot(s, w_ref[...], preferred_element_type=F32,
                         precision=lax.Precision.HIGHEST) + b_ref[...]


def _adaln(cvecs, w_ada, b_ada):
    rows, d = cvecs.shape
    n = w_ada.shape[1]
    tn = 512
    return pl.pallas_call(
        _adaln_kernel,
        grid=(n // tn,),
        in_specs=[pl.BlockSpec((rows, d), lambda j: (0, 0)),
                  pl.BlockSpec((d, tn), lambda j: (0, j)),
                  pl.BlockSpec((1, tn), lambda j: (0, j))],
        out_specs=pl.BlockSpec((rows, tn), lambda j: (0, j)),
        out_shape=jax.ShapeDtypeStruct((rows, n), F32),
        compiler_params=_cparams("arbitrary"),
        name="adaln",
    )(cvecs, w_ada, b_ada.reshape(1, n))


def _lane_group(rows):
    return lax.broadcasted_iota(jnp.int32, (rows, LANES), 1) // SSM_GROUP


def _to_chunk_lanes(u_scr, u_o):
    rt = u_o.shape[2]
    grp = _lane_group(rt)
    for col in range(SSM_WIDTH // LANES):
        rolled = []
        for step in range(CHUNK):
            s = u_scr[col, pl.ds(step, rt, stride=CHUNK), :]
            rolled.append([s if k == 0 else pltpu.roll(s, k * SSM_GROUP, 1)
                           for k in range(GROUPS_PER_VREG)])
        for g_lo in range(GROUPS_PER_VREG):
            for half in range(CHUNK // GROUPS_PER_VREG):
                out = None
                for s8 in range(GROUPS_PER_VREG):
                    piece = rolled[half * GROUPS_PER_VREG + s8][(s8 - g_lo) % GROUPS_PER_VREG]
                    out = piece if out is None else jnp.where(grp == s8, piece, out)
                u_o[0, col * GROUPS_PER_VREG + g_lo, :, half * LANES:(half + 1) * LANES] = out.astype(BF16)


def _from_chunk_lanes(y_ref, y_scr, r0, rt):
    grp = _lane_group(rt)
    for col in range(SSM_WIDTH // LANES):
        rolled = {}
        for g_lo in range(GROUPS_PER_VREG):
            for half in range(CHUNK // GROUPS_PER_VREG):
                s = y_ref[0, col * GROUPS_PER_VREG + g_lo, r0:r0 + rt,
                          half * LANES:(half + 1) * LANES].astype(F32)
                rolled[g_lo, half] = [s if k == 0 else pltpu.roll(s, k * SSM_GROUP, 1)
                                      for k in range(GROUPS_PER_VREG)]
        for step in range(CHUNK):
            half, s8 = divmod(step, GROUPS_PER_VREG)
            out = None
            for g_lo in range(GROUPS_PER_VREG):
                piece = rolled[g_lo, half][(g_lo - s8) % GROUPS_PER_VREG]
                out = piece if out is None else jnp.where(grp == g_lo, piece, out)
            y_scr[col, pl.ds(r0 * CHUNK + step, rt, stride=CHUNK), :] = out


def _head_mean_sq(z, bd):
    return jnp.dot((z * z).astype(BF16), bd, preferred_element_type=F32)


def _swap16(x):
    w = x.shape[1]
    lane = lax.broadcasted_iota(jnp.int32, x.shape, 1)
    return jnp.where((lane & 16) == 0, pltpu.roll(x, w - 16, 1), pltpu.roll(x, 16, 1))


def _rope(x, cos, sin_signed):
    cols = []
    for c in range(x.shape[1] // LANES):
        xc = x[:, c * LANES:(c + 1) * LANES]
        cols.append(xc * cos + _swap16(xc) * sin_signed)
    return cols[0] if len(cols) == 1 else jnp.concatenate(cols, axis=1)


def _silu(z):
    return z * jax.nn.sigmoid(z)


def _inproj_kernel(x_ref, c_ref, shl_ref, scl_ref, shc_ref, scc_ref, pg_ref, w_ref, qg_ref, kg_ref,
                   cos_ref, sin_ref, bd_ref, q_o, k_o, vt_o, ga_o, u_o, gs_o, v_scr, u_scr, *, n_ctx_tiles):
    is_ctx = pl.program_id(1) < n_ctx_tiles
    x = jnp.where(is_ctx, c_ref[0], x_ref[0])
    shift = jnp.where(is_ctx, shc_ref[0], shl_ref[0])
    scale = jnp.where(is_ctx, scc_ref[0], scl_ref[0])
    ms = jnp.mean(x * x, axis=-1, keepdims=True)
    xn = x * lax.rsqrt(ms + NORM_EPS) * pg_ref[...]
    h = (xn * (1.0 + scale) + shift).astype(BF16)

    def proj(a, b):
        return jnp.dot(h, w_ref[:, a:b], preferred_element_type=F32)

    u = proj(GA_END, U_END)
    for col in range(SSM_WIDTH // LANES):
        u_scr[col] = u[:, col * LANES:(col + 1) * LANES]
    _to_chunk_lanes(u_scr, u_o)

    cos = cos_ref[...]
    sin = sin_ref[...]
    zq = proj(0, Q_END)
    qn = zq * lax.rsqrt(_head_mean_sq(zq, bd_ref[...]) + NORM_EPS) * qg_ref[...]
    q_o[0] = (_rope(qn, cos, sin) * (ATTN_SCALE * LOG2_E)).astype(BF16)
    zk = proj(Q_END, K_END)
    kn = zk * lax.rsqrt(_head_mean_sq(zk, bd_ref[:KV_WIDTH, :KV_WIDTH]) + NORM_EPS) * kg_ref[...]
    k_o[0] = _rope(kn, cos, sin).astype(BF16)

    v_scr[...] = proj(K_END, V_END)
    vt = v_scr[...].T.astype(BF16)
    tm = vt.shape[1]
    ones_row = lax.broadcasted_iota(jnp.int32, (V_ROWS - HEAD_DIM, tm), 0) == 0
    for j in range(N_KV_HEADS):
        vt_o[0, j, :HEAD_DIM, :] = vt[j * HEAD_DIM:(j + 1) * HEAD_DIM]
        vt_o[0, j, HEAD_DIM:, :] = jnp.where(ones_row, 1.0, 0.0).astype(BF16)

    ga_o[0] = _silu(proj(V_END, GA_END)).astype(BF16)
    gs_o[0] = _silu(proj(U_END, IN_WIDTH)).astype(BF16)


def _inproj(x, ctx, shift_l, scale_l, shift_c, scale_c, pre_g, w_bf, qg, kg, cos, sin, bd, tm):
    bsz, n_lat, d = x.shape
    n_ctx = ctx.shape[1]
    assert n_ctx % tm == 0 and n_lat % tm == 0 and tm % (CHUNK * 16) == 0
    nct = n_ctx // tm
    n_tot = n_ctx + n_lat
    lat_rows = lambda w: pl.BlockSpec((1, tm, w), lambda b, i: (b, jnp.maximum(i - nct, 0), 0))
    all_rows = lambda w: pl.BlockSpec((1, tm, w), lambda b, i: (b, i, 0))
    ctx_vec = pl.BlockSpec((1, 1, d), lambda b, i: (0, 0, 0))
    table = pl.BlockSpec((tm, LANES), lambda b, i: (i, 0))
    return pl.pallas_call(
        functools.partial(_inproj_kernel, n_ctx_tiles=nct),
        grid=(bsz, n_tot // tm),
        in_specs=[lat_rows(d),
                  pl.BlockSpec((1, tm, d), lambda b, i: (b, jnp.minimum(i, nct - 1), 0)),
                  _batch_vec_spec(d), _batch_vec_spec(d), ctx_vec, ctx_vec, _const_spec((1, d)),
                  _const_spec((d, IN_WIDTH)), _const_spec((1, ATTN_WIDTH)), _const_spec((1, KV_WIDTH)),
                  table, table, _const_spec((ATTN_WIDTH, ATTN_WIDTH))],
        out_specs=[lat_rows(ATTN_WIDTH), all_rows(KV_WIDTH),
                   pl.BlockSpec((1, N_KV_HEADS, V_ROWS, tm), lambda b, i: (b, 0, 0, i)),
                   lat_rows(ATTN_WIDTH),
                   pl.BlockSpec((1, N_SSM_GROUPS, tm // CHUNK, CHUNK_LANES), lambda b, i: (b, 0, i, 0)),
                   lat_rows(SSM_WIDTH)],
        out_shape=[jax.ShapeDtypeStruct((bsz, n_lat, ATTN_WIDTH), BF16),
                   jax.ShapeDtypeStruct((bsz, n_tot, KV_WIDTH), BF16),
                   jax.ShapeDtypeStruct((bsz, N_KV_HEADS, V_ROWS, n_tot), BF16),
                   jax.ShapeDtypeStruct((bsz, n_lat, ATTN_WIDTH), BF16),
                   jax.ShapeDtypeStruct((bsz, N_SSM_GROUPS, n_tot // CHUNK, CHUNK_LANES), BF16),
                   jax.ShapeDtypeStruct((bsz, n_lat, SSM_WIDTH), BF16)],
        scratch_shapes=[pltpu.VMEM((tm, KV_WIDTH), F32), pltpu.VMEM((SSM_WIDTH // LANES, tm, LANES), F32)],
        compiler_params=_cparams("parallel", "arbitrary"),
        name="inproj",
    )(x, ctx, shift_l, scale_l, shift_c, scale_c, pre_g, w_bf, qg, kg, cos, sin, bd)


def _attn_kernel(q_ref, k_ref, vt_ref, o_ref, *, tk, sub, lookahead, heads_per_pass):
    tq = q_ref.shape[1]
    n_keys = k_ref.shape[1]
    group = N_Q_HEADS // N_KV_HEADS
    q = q_ref[0]
    low = lax.broadcasted_iota(jnp.int32, (tq, LANES), 1) < HEAD_DIM

    def widen(h):
        col = q[:, (h // 2) * LANES:(h // 2 + 1) * LANES].astype(F32)
        dst_high = h // group == 1
        if (h % 2 == 1) != dst_high:
            col = pltpu.roll(col, HEAD_DIM, 1)
        return jnp.where(low != dst_high, col, 0.0).astype(BF16)

    outs = []
    for h0 in range(0, N_Q_HEADS, heads_per_pass):
        heads = list(range(h0, h0 + heads_per_pass))
        q_wide = [widen(h) for h in heads]

        def body(t, carry, heads=heads, q_wide=q_wide):
            tasks = [(j, i) for j in range(tk // sub) for i in range(len(heads))]
            state = list(carry)
            scores = {}

            def keys_at(j):
                return pl.ds(pl.multiple_of(t * tk + j * sub, sub), sub)

            def issue(n):
                j, i = tasks[n]
                scores[n] = lax.dot_general(k_ref[0, keys_at(j), :], q_wide[i], (((1,), (1,)), ((), ())),
                                            preferred_element_type=F32)

            def consume(n):
                j, i = tasks[n]
                s = scores.pop(n)
                m_old, acc = state[i]
                vt = vt_ref[0, heads[i] // group, :, keys_at(j)]
                m_new = jnp.maximum(m_old, jnp.max(s, axis=0, keepdims=True))
                p = jnp.exp2((s - m_new).astype(BF16))
                acc = jnp.exp2(m_old - m_new) * acc + jnp.dot(vt, p, preferred_element_type=F32)
                state[i] = (m_new, acc)

            for n in range(len(tasks) + lookahead):
                if n < len(tasks):
                    issue(n)
                if n >= lookahead:
                    consume(n - lookahead)
            return tuple(state)

        init = tuple((jnp.full((1, tq), NEG_BIG, F32), jnp.zeros((V_ROWS, tq), F32)) for _ in heads)
        for _, acc in lax.fori_loop(0, n_keys // tk, body, init):
            outs.append(acc[:HEAD_DIM] / acc[HEAD_DIM:HEAD_DIM + 1])

    for c in range(N_Q_HEADS // 2):
        pair = jnp.concatenate([outs[2 * c], outs[2 * c + 1]], axis=0)
        o_ref[0, :, c * LANES:(c + 1) * LANES] = pair.T.astype(BF16)


def _attention(q, k_all, vt_aug, tq, tk, sub, lookahead, heads_per_pass):
    bsz, n, _ = q.shape
    n_keys = k_all.shape[1]
    assert n_keys % tk == 0 and tk % sub == 0 and n % tq == 0
    return pl.pallas_call(
        functools.partial(_attn_kernel, tk=tk, sub=sub, lookahead=lookahead,
                          heads_per_pass=heads_per_pass),
        grid=(bsz, n // tq),
        in_specs=[_row_spec(tq, ATTN_WIDTH),
                  pl.BlockSpec((1, n_keys, KV_WIDTH), lambda b, i: (b, 0, 0)),
                  pl.BlockSpec((1, N_KV_HEADS, V_ROWS, n_keys), lambda b, i: (b, 0, 0, 0))],
        out_specs=_row_spec(tq, ATTN_WIDTH),
        out_shape=jax.ShapeDtypeStruct((bsz, n, ATTN_WIDTH), BF16),
        compiler_params=_cparams("parallel", "parallel"),
        name="attention",
    )(q, k_all, vt_aug)


def _scan_steps(n_rows):
    return max(1, math.ceil(math.log2(n_rows)))


def _ssm_prep_kernel(lre_ref, lim_ref, ldt_ref, bt_ref, bts_ref, c_ref, cs_ref, d_ref,
                     win_o, m_o, cout_o, mult_o, *, n_steps):
    lane = lax.broadcasted_iota(jnp.int32, (1, STATE_LANES), 1)
    low = lane < SSM_STATE
    sign_lo = jnp.where(low, -1.0, 1.0)
    k_idx = lax.broadcasted_iota(jnp.int32, (CHUNK, STATE_LANES), 0).astype(F32)
    step_of_row = lax.broadcasted_iota(jnp.int32, (CHUNK_LANES, STATE_LANES), 0) // SSM_GROUP
    nt = (((1,), (1,)), ((), ()))

    def outer(pw, pws, mat, mats):
        full = pw[:, None, :] * mat[None, :, :] + pws[:, None, :] * mats[None, :, :]
        return full.reshape(CHUNK * mat.shape[0], STATE_LANES)

    state_in, state_out, mults, tables = {}, {}, {}, {}
    for gi in range(2):
        taps_lo, taps_hi = None, None
        for d in range(2):
            lre = lre_ref[gi, d:d + 1, :]
            lim = lim_ref[gi, d:d + 1, :]
            dt = jnp.exp(ldt_ref[gi, d:d + 1, :])

            def power(expo, lre=lre, lim=lim, dt=dt):
                mag = jnp.exp(expo * dt * lre)
                ang = expo * dt * lim
                return mag * jnp.cos(ang), mag * jnp.sin(ang) * sign_lo

            a_re, a_ims = power(jnp.ones((1, STATE_LANES), F32))
            a_im = a_ims * sign_lo
            nr, ni = a_re - 1.0, a_im
            den = lre * lre + lim * lim
            cr = (nr * lre + ni * lim) / den
            ci = (ni * lre - nr * lim) / den
            cis = ci * sign_lo
            bbar = cr * bt_ref[gi, d] + cis * bts_ref[gi, d]
            bbar_s = cr * bts_ref[gi, d] - cis * bt_ref[gi, d]
            cmat, cmat_s = c_ref[gi, d], cs_ref[gi, d]

            asc, ascs = power(k_idx)
            desc, descs = power(CHUNK - 1.0 - k_idx)
            nxt, nxts = power(k_idx + 1.0)
            conj_c = lambda pw, pws: outer(pw, pws, cmat, cmat_s) * (-sign_lo)
            if d == 0:
                state_in[gi, d] = outer(desc, descs, bbar, bbar_s)
                state_out[gi, d] = conj_c(nxt, nxts)
                lag0 = jnp.where(step_of_row == CHUNK - 1, conj_c(desc, descs), 0.0)
                lags = jnp.where(step_of_row < CHUNK - 1, conj_c(nxt, nxts), 0.0)
                hi = lax.dot_general(bbar, lags, nt, preferred_element_type=F32,
                                     precision=lax.Precision.HIGHEST)
                lo = lax.dot_general(bbar, lag0, nt, preferred_element_type=F32,
                                     precision=lax.Precision.HIGHEST)
                taps_hi = hi
                taps_lo = lo if taps_lo is None else taps_lo + lo
            else:
                state_in[gi, d] = outer(asc, ascs, bbar, bbar_s)
                rev, revs = power(CHUNK - k_idx)
                state_out[gi, d] = conj_c(rev, revs)
                lo = lax.dot_general(bbar, conj_c(desc, descs), nt, preferred_element_type=F32,
                                     precision=lax.Precision.HIGHEST)
                taps_lo = lo if taps_lo is None else taps_lo + lo

            sr, sis = power(jnp.full((1, STATE_LANES), float(CHUNK), F32))
            chain = []
            for k in range(n_steps):
                chain.append((sr, sis))
                si = sis * sign_lo
                sr, sis = sr * sr - si * si, 2.0 * sr * si * sign_lo
            mults[gi, d] = chain
            r_idx = lax.broadcasted_iota(jnp.int32, (SUBLANES, STATE_LANES), 0).astype(F32)
            tables[gi, d] = power(CHUNK * (r_idx + 1.0) if d == 0 else CHUNK * (SUBLANES - r_idx))

        qq = lax.broadcasted_iota(jnp.int32, (SSM_GROUP, CHUNK_LANES), 0)
        ll = lax.broadcasted_iota(jnp.int32, (SSM_GROUP, CHUNK_LANES), 1)
        taps_lo = taps_lo + jnp.where(ll == qq + (CHUNK - 1) * SSM_GROUP, d_ref[gi], 0.0)
        taps = jnp.concatenate([taps_lo, taps_hi], axis=1)
        for j in range(CHUNK):
            off = (CHUNK - 1 - j) * SSM_GROUP
            m_o[gi, j * SSM_GROUP:(j + 1) * SSM_GROUP, :] = taps[:, off:off + CHUNK_LANES].astype(BF16)

    def swap(x):
        return pltpu.roll(x, SSM_STATE, 1)

    mult_o[...] = jnp.zeros(mult_o.shape, F32)
    wide_low = lax.broadcasted_iota(jnp.int32, (CHUNK_LANES, STATE_LANES), 1) < SSM_STATE
    zeros_c = jnp.zeros((SSM_STATE, CHUNK_LANES), F32)
    for d in range(2):
        s0, s1 = state_in[0, d], state_in[1, d]
        re_blk = jnp.concatenate([jnp.where(wide_low, s0, 0.0), jnp.where(wide_low, 0.0, swap(s1))], axis=0)
        im_blk = jnp.concatenate([jnp.where(wide_low, swap(s0), 0.0), jnp.where(wide_low, 0.0, s1)], axis=0)
        win_o[0, :, (2 * d) * LANES:(2 * d + 1) * LANES] = re_blk.astype(BF16)
        win_o[0, :, (2 * d + 1) * LANES:(2 * d + 2) * LANES] = im_blk.astype(BF16)
        t0, t1 = state_out[0, d].T, state_out[1, d].T
        for part in range(2):
            rows = slice(part * SSM_STATE, (part + 1) * SSM_STATE)
            blk = jnp.concatenate([jnp.concatenate([t0[rows], zeros_c], axis=1),
                                   jnp.concatenate([zeros_c, t1[rows]], axis=1)], axis=0)
            cout_o[0, (2 * d + part) * LANES:(2 * d + part + 1) * LANES, :] = blk.astype(BF16)
        for k in range(n_steps):
            (sr0, sis0), (sr1, sis1) = mults[0, d][k], mults[1, d][k]
            mult_o[0, k:k + 1, (2 * d) * LANES:(2 * d + 1) * LANES] = jnp.where(low, sr0, sr1)
            mult_o[0, k:k + 1, (2 * d + 1) * LANES:(2 * d + 2) * LANES] = jnp.where(low, -sis0, sis1)
        (tr0, tis0), (tr1, tis1) = tables[0, d], tables[1, d]
        mult_o[0, POWER_ROW0:POWER_ROW0 + SUBLANES, (2 * d) * LANES:(2 * d + 1) * LANES] = jnp.where(low, tr0, tr1)
        mult_o[0, POWER_ROW0:POWER_ROW0 + SUBLANES, (2 * d + 1) * LANES:(2 * d + 2) * LANES] = jnp.where(
            low, -tis0, tis1)


def _ssm_prep(lam_re, lam_im, log_dt, b_re, b_im, c_re, c_im, d_skip, n_steps):
    g = N_SSM_GROUPS
    dup = lambda a: jnp.concatenate([a, a], axis=-1)
    lre = dup(jnp.swapaxes(lam_re, 0, 1))
    lim = dup(jnp.swapaxes(lam_im, 0, 1))
    ldt = jnp.broadcast_to(jnp.swapaxes(log_dt, 0, 1)[..., None], (g, 2, STATE_LANES))
    btr = jnp.transpose(b_re, (1, 0, 3, 2))
    bti = jnp.transpose(b_im, (1, 0, 3, 2))
    bt, bts = jnp.concatenate([btr, bti], -1), jnp.concatenate([bti, btr], -1)
    cr, ci = jnp.swapaxes(c_re, 0, 1), jnp.swapaxes(c_im, 0, 1)
    cm, cms = jnp.concatenate([cr, ci], -1), jnp.concatenate([ci, cr], -1)
    dsk = jnp.pad(d_skip, ((0, 0), (CHUNK_LANES - SSM_GROUP, 0))).reshape(g, 1, CHUNK_LANES)
    n_pad = POWER_ROW0 + SUBLANES
    vec = pl.BlockSpec((2, 2, STATE_LANES), lambda i: (i, 0, 0))
    mat = pl.BlockSpec((2, 2, SSM_GROUP, STATE_LANES), lambda i: (i, 0, 0, 0))
    pair_sq = pl.BlockSpec((1, 2 * CHUNK_LANES, 4 * LANES), lambda i: (i, 0, 0))
    return pl.pallas_call(
        functools.partial(_ssm_prep_kernel, n_steps=n_steps),
        grid=(g // 2,),
        in_specs=[vec, vec, vec, mat, mat, mat, mat,
                  pl.BlockSpec((2, 1, CHUNK_LANES), lambda i: (i, 0, 0))],
        out_specs=[pair_sq, pl.BlockSpec((2, CHUNK_LANES, CHUNK_LANES), lambda i: (i, 0, 0)), pair_sq,
                   pl.BlockSpec((1, n_pad, 4 * LANES), lambda i: (i, 0, 0))],
        out_shape=[jax.ShapeDtypeStruct((g // 2, 2 * CHUNK_LANES, 4 * LANES), BF16),
                   jax.ShapeDtypeStruct((g, CHUNK_LANES, CHUNK_LANES), BF16),
                   jax.ShapeDtypeStruct((g // 2, 4 * LANES, 2 * CHUNK_LANES), BF16),
                   jax.ShapeDtypeStruct((g // 2, n_pad, 4 * LANES), F32)],
        compiler_params=_cparams("parallel"),
        name="ssm_prep",
    )(lre, lim, ldt, bt, bts, cm, cms, dsk)


def _complex_step(re, im, sr, si, ar, ai):
    return re + ar * sr - ai * si, im + ar * si + ai * sr


def _ssm_kernel(u_ref, win_ref, m_ref, cout_ref, mult_ref, y_o, loc_scr, car_scr, *, n_ctx_rows, n_steps):
    u0, u1 = u_ref[0, 0], u_ref[0, 1]
    rows = u0.shape[0]
    n_lat_rows = rows - n_ctx_rows
    n_blk = rows // SUBLANES
    n_blk_pad = -(-n_blk // SUBLANES) * SUBLANES
    x = jnp.dot(jnp.concatenate([u0, u1], axis=1), win_ref[0], preferred_element_type=F32)
    to_rev = lambda a: jnp.concatenate([a[n_ctx_rows:], a[:n_ctx_rows]], axis=0)
    state = [[x[:, :LANES], x[:, LANES:2 * LANES]],
             [to_rev(x[:, 2 * LANES:3 * LANES]), to_rev(x[:, 3 * LANES:])]]
    in_blk = lax.broadcasted_iota(jnp.int32, (rows, LANES), 0) & (SUBLANES - 1)
    blk = lax.broadcasted_iota(jnp.int32, (n_blk_pad, LANES), 0)

    def mult(k0, k1, d):
        return (mult_ref[0, k0:k1, 2 * d * LANES:(2 * d + 1) * LANES],
                mult_ref[0, k0:k1, (2 * d + 1) * LANES:(2 * d + 2) * LANES])

    def shifted(a, s, idx, n, down):
        if down:
            return jnp.where(idx >= s, pltpu.roll(a, s, 0), 0.0)
        return jnp.where(idx < n - s, pltpu.roll(a, a.shape[0] - s, 0), 0.0)

    tile = lambda t: jnp.broadcast_to(t[None], (n_blk, SUBLANES, LANES)).reshape(rows, LANES)
    sub_row = lax.broadcasted_iota(jnp.int32, (SUBLANES, LANES), 0)
    local_steps = SUBLANES.bit_length() - 1
    for d, down in ((0, True), (1, False)):
        re, im = state[d]
        for k in range(local_steps):
            s = 1 << k
            keep = sub_row >= s if down else sub_row < SUBLANES - s
            ar, ai = (tile(jnp.where(keep, m, 0.0)) for m in mult(k, k + 1, d))
            shift = s if down else rows - s
            re, im = _complex_step(re, im, pltpu.roll(re, shift, 0), pltpu.roll(im, shift, 0), ar, ai)
        loc_scr[2 * d], loc_scr[2 * d + 1] = re, im
        state[d] = [re, im]

        end_row = SUBLANES - 1 if down else 0
        pad = jnp.zeros((n_blk_pad - n_blk, LANES), F32)
        ere = jnp.concatenate([loc_scr[2 * d, pl.ds(end_row, n_blk, stride=SUBLANES), :], pad], axis=0)
        eim = jnp.concatenate([loc_scr[2 * d + 1, pl.ds(end_row, n_blk, stride=SUBLANES), :], pad], axis=0)
        for k in range(local_steps, n_steps):
            ar, ai = mult(k, k + 1, d)
            s = 1 << (k - local_steps)
            ere, eim = _complex_step(ere, eim, shifted(ere, s, blk, n_blk_pad, down),
                                     shifted(eim, s, blk, n_blk_pad, down), ar, ai)
        for c, ends in enumerate((ere, eim)):
            entering = shifted(ends, 1, blk, n_blk_pad, down)[:n_blk]
            for r in range(SUBLANES):
                car_scr[2 * d + c, pl.ds(r, n_blk, stride=SUBLANES), :] = entering

    for d in range(2):
        tre, tim = mult(POWER_ROW0, POWER_ROW0 + SUBLANES, d)
        re, im = state[d]
        state[d] = list(_complex_step(re, im, car_scr[2 * d], car_scr[2 * d + 1], tile(tre), tile(tim)))

    prev = lambda a: pltpu.roll(a, 1, 0)[n_ctx_rows:]
    nxt = lambda a: pltpu.roll(a, rows - 1, 0)[:n_lat_rows]
    st = jnp.concatenate([prev(state[0][0]), prev(state[0][1]), nxt(state[1][0]), nxt(state[1][1])],
                         axis=1).astype(BF16)
    y = jnp.dot(st, cout_ref[0], preferred_element_type=F32)
    y_o[0, 0] = (y[:, :CHUNK_LANES]
                 + jnp.dot(u0[n_ctx_rows:], m_ref[0], preferred_element_type=F32)).astype(BF16)
    y_o[0, 1] = (y[:, CHUNK_LANES:]
                 + jnp.dot(u1[n_ctx_rows:], m_ref[1], preferred_element_type=F32)).astype(BF16)


def _ssm(u_cf, win_pair, m, cout_pair, mult, n_ctx_rows, n_steps):
    bsz, g, rows, _ = u_cf.shape
    assert rows % SUBLANES == 0 and n_ctx_rows % 16 == 0
    pair_spec = lambda a: pl.BlockSpec((1,) + a.shape[1:], lambda b, i: (i, 0, 0))
    return pl.pallas_call(
        functools.partial(_ssm_kernel, n_ctx_rows=n_ctx_rows, n_steps=n_steps),
        grid=(bsz, g // 2),
        in_specs=[pl.BlockSpec((1, 2, rows, CHUNK_LANES), lambda b, i: (b, i, 0, 0)),
                  pair_spec(win_pair),
                  pl.BlockSpec((2, CHUNK_LANES, CHUNK_LANES), lambda b, i: (i, 0, 0)),
                  pair_spec(cout_pair), pair_spec(mult)],
        out_specs=pl.BlockSpec((1, 2, rows - n_ctx_rows, CHUNK_LANES), lambda b, i: (b, i, 0, 0)),
        out_shape=jax.ShapeDtypeStruct((bsz, g, rows - n_ctx_rows, CHUNK_LANES), BF16),
        scratch_shapes=[pltpu.VMEM((4, rows, LANES), F32), pltpu.VMEM((4, rows, LANES), F32)],
        compiler_params=_cparams("parallel", "parallel"),
        name="ssm",
    )(u_cf, win_pair, m, cout_pair, mult)


def _final_kernel(x_ref, at_ref, ga_ref, y_ref, gs_ref, gate_ref, wglu_ref, bglu_ref, wout_ref,
                  pg_ref, o_ref, y_scr):
    _from_chunk_lanes(y_ref, y_scr, 0, y_ref.shape[2])
    y = jax.nn.gelu(jnp.concatenate([y_scr[col] for col in range(SSM_WIDTH // LANES)], axis=1))
    t = jnp.dot(y.astype(BF16), wglu_ref[...], preferred_element_type=F32) + bglu_ref[...]
    s = (y * jax.nn.sigmoid(t) * gs_ref[0].astype(F32)).astype(BF16)
    a = at_ref[0] * ga_ref[0]
    out = jnp.dot(jnp.concatenate([a, s], axis=1), wout_ref[...], preferred_element_type=F32)
    ms = jnp.mean(out * out, axis=-1, keepdims=True)
    o_ref[0] = x_ref[0] + out * lax.rsqrt(ms + NORM_EPS) * (gate_ref[0] * pg_ref[...])


def _final(x, attn, ga, y_cf, gs, gate, wglu_bf, b_glu, wout_bf, post_g, tm):
    bsz, n, d = x.shape
    assert n % tm == 0 and tm % (CHUNK * 16) == 0
    return pl.pallas_call(
        _final_kernel,
        grid=(bsz, n // tm),
        in_specs=[_row_spec(tm, d), _row_spec(tm, ATTN_WIDTH), _row_spec(tm, ATTN_WIDTH),
                  pl.BlockSpec((1, N_SSM_GROUPS, tm // CHUNK, CHUNK_LANES), lambda b, i: (b, 0, i, 0)),
                  _row_spec(tm, SSM_WIDTH), _batch_vec_spec(d),
                  _const_spec((SSM_WIDTH, SSM_WIDTH)), _const_spec((1, SSM_WIDTH)),
                  _const_spec((d, d)), _const_spec((1, d))],
        out_specs=_row_spec(tm, d),
        out_shape=jax.ShapeDtypeStruct((bsz, n, d), F32),
        scratch_shapes=[pltpu.VMEM((SSM_WIDTH // LANES, tm, LANES), F32)],
        compiler_params=_cparams("parallel", "parallel"),
        name="final",
    )(x, attn, ga, y_cf, gs, gate, wglu_bf, b_glu, wout_bf, post_g)


def _rope_tables(n_ctx, n_lat):
    t = jnp.arange(n_lat, dtype=jnp.int32)
    row_pos = (t // GRID_W).astype(F32)
    col_pos = (t % GRID_W).astype(F32)
    inv_freq = ROPE_THETA ** (-jnp.arange(ROPE_FREQS, dtype=F32) / ROPE_FREQS)
    ang_r = row_pos[:, None] * inv_freq
    ang_c = col_pos[:, None] * inv_freq
    cos = jnp.concatenate([jnp.cos(ang_r)] * 2 + [jnp.cos(ang_c)] * 2, axis=1)
    sin = jnp.concatenate([-jnp.sin(ang_r), jnp.sin(ang_r), -jnp.sin(ang_c), jnp.sin(ang_c)], axis=1)
    cos = jnp.concatenate([jnp.ones((n_ctx, HEAD_DIM), F32), cos], axis=0)
    sin = jnp.concatenate([jnp.zeros((n_ctx, HEAD_DIM), F32), sin], axis=0)
    return jnp.tile(cos, (1, LANES // HEAD_DIM)), jnp.tile(sin, (1, LANES // HEAD_DIM))


def _layer(x, ctx, c, c_ctx, w_ada, b_ada, pre_g, post_g, w_in, q_g, k_g, lam_re, lam_im, log_dt,
           b_re, b_im, c_re, c_im, d_skip, w_glu, b_glu, w_out):
    bsz, n_lat, d = x.shape
    n_ctx = ctx.shape[1]
    assert n_lat % GRID_W == 0

    cvecs = jnp.zeros((8, d), F32).at[:bsz].set(c).at[bsz].set(c_ctx)
    mod = _adaln(cvecs, w_ada, b_ada)
    shift, scale, gate = (mod[:, i * d:(i + 1) * d] for i in range(3))
    lat = lambda a: a[:bsz].reshape(bsz, 1, d)
    cvec = lambda a: a[bsz].reshape(1, 1, d)

    head = jnp.arange(ATTN_WIDTH) // HEAD_DIM
    bd = (head[:, None] == head[None, :]).astype(BF16) * (1.0 / HEAD_DIM)
    qg = jnp.tile(q_g, N_Q_HEADS).reshape(1, ATTN_WIDTH)
    kg = jnp.tile(k_g, N_KV_HEADS).reshape(1, KV_WIDTH)
    cos, sin = _rope_tables(n_ctx, n_lat)

    q, k_all, vt_aug, ga, u_cf, gs = _inproj(x, ctx, lat(shift), lat(scale), cvec(shift), cvec(scale),
                                             pre_g.reshape(1, d), w_in.astype(BF16), qg, kg, cos, sin,
                                             bd, tm=256)
    attn = _attention(q, k_all, vt_aug, tq=256, tk=2816, sub=256, lookahead=5, heads_per_pass=8)

    rows = (n_ctx + n_lat) // CHUNK
    n_steps = _scan_steps(rows)
    win_pair, m_op, cout_pair, mult = _ssm_prep(lam_re, lam_im, log_dt, b_re, b_im, c_re, c_im, d_skip,
                                                n_steps)
    y_cf = _ssm(u_cf, win_pair, m_op, cout_pair, mult, n_ctx // CHUNK, n_steps)

    return _final(x, attn, ga, y_cf, gs, lat(gate), w_glu.astype(BF16), b_glu.reshape(1, SSM_WIDTH),
                  w_out.astype(BF16), post_g.reshape(1, d), tm=512)


def kernel(x, c, ctx, c_ctx, w_ada, b_ada, pre_norm, post_norm, w_in, q_norm, k_norm, ssm_lam_re,
           ssm_lam_im, ssm_log_dt, ssm_b_re, ssm_b_im, ssm_c_re, ssm_c_im, ssm_d, w_glu, b_glu, w_out):
    depth = w_ada.shape[0]
    assert depth == 1, "context stream update between layers is not implemented"
    return _layer(x, ctx, c, c_ctx, w_ada[0], b_ada[0], pre_norm[0], post_norm[0], w_in[0], q_norm[0],
                  k_norm[0], ssm_lam_re[0], ssm_lam_im[0], ssm_log_dt[0], ssm_b_re[0], ssm_b_im[0],
                  ssm_c_re[0], ssm_c_im[0], ssm_d[0], w_glu[0], b_glu[0], w_out[0])
```

```python
import functools
import math

import jax
import jax.numpy as jnp
from jax import lax
from jax.experimental import pallas as pl
from jax.experimental.pallas import tpu as pltpu

F32 = jnp.float32
BF16 = jnp.bfloat16

D_MODEL = 1024
HEAD_DIM = 64
N_Q_HEADS = 8
N_KV_HEADS = 2
ATTN_WIDTH = N_Q_HEADS * HEAD_DIM
KV_WIDTH = N_KV_HEADS * HEAD_DIM
SSM_WIDTH = 512
SSM_GROUP = 16
N_SSM_GROUPS = SSM_WIDTH // SSM_GROUP
SSM_STATE = 64
GRID_W = 64
ROPE_THETA = 10000.0
ROPE_FREQS = 16
NORM_EPS = 1e-6
ATTN_SCALE = HEAD_DIM ** -0.5
LOG2_E = math.log2(math.e)
Q_END = ATTN_WIDTH
K_END = Q_END + KV_WIDTH
V_END = K_END + KV_WIDTH
GA_END = V_END + ATTN_WIDTH
U_END = GA_END + SSM_WIDTH
IN_WIDTH = U_END + SSM_WIDTH

CHUNK = 16
CHUNK_LANES = CHUNK * SSM_GROUP
LANES = 128
GROUPS_PER_VREG = LANES // SSM_GROUP
STATE_LANES = 2 * SSM_STATE
SUBLANES = 8
POWER_ROW0 = 16
V_ROWS = HEAD_DIM + 16
VMEM_LIMIT = 48 * 1024 * 1024
NEG_BIG = -1e30


def _cparams(*sem):
    return pltpu.CompilerParams(dimension_semantics=sem, vmem_limit_bytes=VMEM_LIMIT)


def _row_spec(tm, width):
    return pl.BlockSpec((1, tm, width), lambda b, i: (b, i, 0))


def _const_spec(shape):
    return pl.BlockSpec(shape, lambda b, i: (0,) * len(shape))


def _batch_vec_spec(width):
    return pl.BlockSpec((1, 1, width), lambda b, i: (b, 0, 0))


def _adaln_kernel(c_ref, w_ref, b_ref, o_ref):
    c = c_ref[...]
    s = c * jax.nn.sigmoid(c)
    o_ref[...] = jnp.dot(s, w_ref[...], preferred_element_type=F32,
                         precision=lax.Precision.HIGHEST) + b_ref[...]


def _adaln(cvecs, w_ada, b_ada):
    rows, d = cvecs.shape
    n = w_ada.shape[1]
    tn = 512
    return pl.pallas_call(
        _adaln_kernel,
        grid=(n // tn,),
        in_specs=[pl.BlockSpec((rows, d), lambda j: (0, 0)),
                  pl.BlockSpec((d, tn), lambda j: (0, j)),
                  pl.BlockSpec((1, tn), lambda j: (0, j))],
        out_specs=pl.BlockSpec((rows, tn), lambda j: (0, j)),
        out_shape=jax.ShapeDtypeStruct((rows, n), F32),
        compiler_params=_cparams("arbitrary"),
        name="adaln",
    )(cvecs, w_ada, b_ada.reshape(1, n))


def _lane_group(rows):
    return lax.broadcasted_iota(jnp.int32, (rows, LANES), 1) // SSM_GROUP


def _to_chunk_lanes(u_scr, u_o):
    rt = u_o.shape[2]
    grp = _lane_group(rt)
    for col in range(SSM_WIDTH // LANES):
        rolled = []
        for step in range(CHUNK):
            s = u_scr[col, pl.ds(step, rt, stride=CHUNK), :]
            rolled.append([s if k == 0 else pltpu.roll(s, k * SSM_GROUP, 1)
                           for k in range(GROUPS_PER_VREG)])
        for g_lo in range(GROUPS_PER_VREG):
            for half in range(CHUNK // GROUPS_PER_VREG):
                out = None
                for s8 in range(GROUPS_PER_VREG):
                    piece = rolled[half * GROUPS_PER_VREG + s8][(s8 - g_lo) % GROUPS_PER_VREG]
                    out = piece if out is None else jnp.where(grp == s8, piece, out)
                u_o[0, col * GROUPS_PER_VREG + g_lo, :, half * LANES:(half + 1) * LANES] = out.astype(BF16)


def _from_chunk_lanes(y_ref, y_scr, r0, rt):
    grp = _lane_group(rt)
    for col in range(SSM_WIDTH // LANES):
        rolled = {}
        for g_lo in range(GROUPS_PER_VREG):
            for half in range(CHUNK // GROUPS_PER_VREG):
                s = y_ref[0, col * GROUPS_PER_VREG + g_lo, r0:r0 + rt,
                          half * LANES:(half + 1) * LANES].astype(F32)
                rolled[g_lo, half] = [s if k == 0 else pltpu.roll(s, k * SSM_GROUP, 1)
                                      for k in range(GROUPS_PER_VREG)]
        for step in range(CHUNK):
            half, s8 = divmod(step, GROUPS_PER_VREG)
            out = None
            for g_lo in range(GROUPS_PER_VREG):
                piece = rolled[g_lo, half][(g_lo - s8) % GROUPS_PER_VREG]
                out = piece if out is None else jnp.where(grp == g_lo, piece, out)
            y_scr[col, pl.ds(r0 * CHUNK + step, rt, stride=CHUNK), :] = out


def _head_mean_sq(z, bd):
    return jnp.dot((z * z).astype(BF16), bd, preferred_element_type=F32)


def _swap16(x):
    w = x.shape[1]
    lane = lax.broadcasted_iota(jnp.int32, x.shape, 1)
    return jnp.where((lane & 16) == 0, pltpu.roll(x, w - 16, 1), pltpu.roll(x, 16, 1))


def _rope(x, cos, sin_signed):
    cols = []
    for c in range(x.shape[1] // LANES):
        xc = x[:, c * LANES:(c + 1) * LANES]
        cols.append(xc * cos + _swap16(xc) * sin_signed)
    return cols[0] if len(cols) == 1 else jnp.concatenate(cols, axis=1)


def _silu(z):
    return z * jax.nn.sigmoid(z)


def _inproj_kernel(x_ref, c_ref, shl_ref, scl_ref, shc_ref, scc_ref, pg_ref, w_ref, qg_ref, kg_ref,
                   cos_ref, sin_ref, bd_ref, qt_o, k_o, vt_o, ga_o, u_o, gs_o, v_scr, u_scr, *, n_ctx_tiles):
    is_ctx = pl.program_id(1) < n_ctx_tiles
    x = jnp.where(is_ctx, c_ref[0], x_ref[0])
    shift = jnp.where(is_ctx, shc_ref[0], shl_ref[0])
    scale = jnp.where(is_ctx, scc_ref[0], scl_ref[0])
    ms = jnp.mean(x * x, axis=-1, keepdims=True)
    xn = x * lax.rsqrt(ms + NORM_EPS) * pg_ref[...]
    h = (xn * (1.0 + scale) + shift).astype(BF16)

    def proj(a, b):
        return jnp.dot(h, w_ref[:, a:b], preferred_element_type=F32)

    u = proj(GA_END, U_END)
    for col in range(SSM_WIDTH // LANES):
        u_scr[col] = u[:, col * LANES:(col + 1) * LANES]
    _to_chunk_lanes(u_scr, u_o)

    cos = cos_ref[...]
    sin = sin_ref[...]
    zq = proj(0, Q_END)
    qn = zq * lax.rsqrt(_head_mean_sq(zq, bd_ref[...]) + NORM_EPS) * qg_ref[...]
    q = _rope(qn, cos, sin) * (ATTN_SCALE * LOG2_E)
    group = N_Q_HEADS // N_KV_HEADS
    none = jnp.zeros((HEAD_DIM, q.shape[0]), BF16)
    for c in range(ATTN_WIDTH // LANES):
        qt = q[:, c * LANES:(c + 1) * LANES].T.astype(BF16)
        for par in range(2):
            head = 2 * c + par
            kv = head // group
            qt_o[0, head, kv * HEAD_DIM:(kv + 1) * HEAD_DIM, :] = qt[par * HEAD_DIM:(par + 1) * HEAD_DIM]
            qt_o[0, head, (1 - kv) * HEAD_DIM:(2 - kv) * HEAD_DIM, :] = none
    zk = proj(Q_END, K_END)
    kn = zk * lax.rsqrt(_head_mean_sq(zk, bd_ref[:KV_WIDTH, :KV_WIDTH]) + NORM_EPS) * kg_ref[...]
    k_o[0] = _rope(kn, cos, sin).astype(BF16)

    v_scr[...] = proj(K_END, V_END)
    vt = v_scr[...].T.astype(BF16)
    tm = vt.shape[1]
    ones_row = lax.broadcasted_iota(jnp.int32, (V_ROWS - HEAD_DIM, tm), 0) == 0
    for j in range(N_KV_HEADS):
        vt_o[0, j, :HEAD_DIM, :] = vt[j * HEAD_DIM:(j + 1) * HEAD_DIM]
        vt_o[0, j, HEAD_DIM:, :] = jnp.where(ones_row, 1.0, 0.0).astype(BF16)

    ga_o[0] = _silu(proj(V_END, GA_END)).astype(BF16)
    gs_o[0] = _silu(proj(U_END, IN_WIDTH)).astype(BF16)


def _inproj(x, ctx, shift_l, scale_l, shift_c, scale_c, pre_g, w_bf, qg, kg, cos, sin, bd, tm):
    bsz, n_lat, d = x.shape
    n_ctx = ctx.shape[1]
    assert n_ctx % tm == 0 and n_lat % tm == 0 and tm % (CHUNK * 16) == 0
    nct = n_ctx // tm
    n_tot = n_ctx + n_lat
    lat_rows = lambda w: pl.BlockSpec((1, tm, w), lambda b, i: (b, jnp.maximum(i - nct, 0), 0))
    all_rows = lambda w: pl.BlockSpec((1, tm, w), lambda b, i: (b, i, 0))
    ctx_vec = pl.BlockSpec((1, 1, d), lambda b, i: (0, 0, 0))
    table = pl.BlockSpec((tm, LANES), lambda b, i: (i, 0))
    return pl.pallas_call(
        functools.partial(_inproj_kernel, n_ctx_tiles=nct),
        grid=(bsz, n_tot // tm),
        in_specs=[lat_rows(d),
                  pl.BlockSpec((1, tm, d), lambda b, i: (b, jnp.minimum(i, nct - 1), 0)),
                  _batch_vec_spec(d), _batch_vec_spec(d), ctx_vec, ctx_vec, _const_spec((1, d)),
                  _const_spec((d, IN_WIDTH)), _const_spec((1, ATTN_WIDTH)), _const_spec((1, KV_WIDTH)),
                  table, table, _const_spec((ATTN_WIDTH, ATTN_WIDTH))],
        out_specs=[pl.BlockSpec((1, N_Q_HEADS, LANES, tm), lambda b, i: (b, 0, 0, jnp.maximum(i - nct, 0))),
                   all_rows(KV_WIDTH),
                   pl.BlockSpec((1, N_KV_HEADS, V_ROWS, tm), lambda b, i: (b, 0, 0, i)),
                   lat_rows(ATTN_WIDTH),
                   pl.BlockSpec((1, N_SSM_GROUPS, tm // CHUNK, CHUNK_LANES), lambda b, i: (b, 0, i, 0)),
                   lat_rows(SSM_WIDTH)],
        out_shape=[jax.ShapeDtypeStruct((bsz, N_Q_HEADS, LANES, n_lat), BF16),
                   jax.ShapeDtypeStruct((bsz, n_tot, KV_WIDTH), BF16),
                   jax.ShapeDtypeStruct((bsz, N_KV_HEADS, V_ROWS, n_tot), BF16),
                   jax.ShapeDtypeStruct((bsz, n_lat, ATTN_WIDTH), BF16),
                   jax.ShapeDtypeStruct((bsz, N_SSM_GROUPS, n_tot // CHUNK, CHUNK_LANES), BF16),
                   jax.ShapeDtypeStruct((bsz, n_lat, SSM_WIDTH), BF16)],
        scratch_shapes=[pltpu.VMEM((tm, KV_WIDTH), F32), pltpu.VMEM((SSM_WIDTH // LANES, tm, LANES), F32)],
        compiler_params=_cparams("parallel", "arbitrary"),
        name="inproj",
    )(x, ctx, shift_l, scale_l, shift_c, scale_c, pre_g, w_bf, qg, kg, cos, sin, bd)


def _attn_kernel(qt_ref, k_ref, vt_ref, o_ref, *, tk, sub, lookahead, heads_per_pass):
    tq = qt_ref.shape[3]
    n_keys = k_ref.shape[1]
    group = N_Q_HEADS // N_KV_HEADS

    outs = []
    for h0 in range(0, N_Q_HEADS, heads_per_pass):
        heads = list(range(h0, h0 + heads_per_pass))
        q_wide = [qt_ref[0, h] for h in heads]

        def body(t, carry, heads=heads, q_wide=q_wide):
            tasks = [(j, i) for j in range(tk // sub) for i in range(len(heads))]
            state = list(carry)
            scores = {}

            def keys_at(j):
                return pl.ds(pl.multiple_of(t * tk + j * sub, sub), sub)

            def issue(n):
                j, i = tasks[n]
                scores[n] = jnp.dot(k_ref[0, keys_at(j), :], q_wide[i],
                                    preferred_element_type=F32)

            def consume(n):
                j, i = tasks[n]
                s = scores.pop(n)
                m_old, acc = state[i]
                vt = vt_ref[0, heads[i] // group, :, keys_at(j)]
                m_new = jnp.maximum(m_old, jnp.max(s, axis=0, keepdims=True))
                p = jnp.exp2((s - m_new).astype(BF16))
                acc = jnp.exp2(m_old - m_new) * acc + jnp.dot(vt, p, preferred_element_type=F32)
                state[i] = (m_new, acc)

            for n in range(len(tasks) + lookahead):
                if n < len(tasks):
                    issue(n)
                if n >= lookahead:
                    consume(n - lookahead)
            return tuple(state)

        init = tuple((jnp.full((1, tq), NEG_BIG, F32), jnp.zeros((V_ROWS, tq), F32)) for _ in heads)
        for _, acc in lax.fori_loop(0, n_keys // tk, body, init):
            outs.append(acc[:HEAD_DIM] / acc[HEAD_DIM:HEAD_DIM + 1])

    for c in range(N_Q_HEADS // 2):
        pair = jnp.concatenate([outs[2 * c], outs[2 * c + 1]], axis=0)
        o_ref[0, :, c * LANES:(c + 1) * LANES] = pair.T.astype(BF16)


def _attention(qt, k_all, vt_aug, tq, tk, sub, lookahead, heads_per_pass):
    bsz, _, _, n = qt.shape
    n_keys = k_all.shape[1]
    assert n_keys % tk == 0 and tk % sub == 0 and n % tq == 0
    return pl.pallas_call(
        functools.partial(_attn_kernel, tk=tk, sub=sub, lookahead=lookahead,
                          heads_per_pass=heads_per_pass),
        grid=(bsz, n // tq),
        in_specs=[pl.BlockSpec((1, N_Q_HEADS, LANES, tq), lambda b, i: (b, 0, 0, i)),
                  pl.BlockSpec((1, n_keys, KV_WIDTH), lambda b, i: (b, 0, 0)),
                  pl.BlockSpec((1, N_KV_HEADS, V_ROWS, n_keys), lambda b, i: (b, 0, 0, 0))],
        out_specs=_row_spec(tq, ATTN_WIDTH),
        out_shape=jax.ShapeDtypeStruct((bsz, n, ATTN_WIDTH), BF16),
        compiler_params=_cparams("parallel", "parallel"),
        name="attention",
    )(qt, k_all, vt_aug)


def _scan_steps(n_rows):
    return max(1, math.ceil(math.log2(n_rows)))


def _ssm_prep_kernel(lre_ref, lim_ref, ldt_ref, bt_ref, bts_ref, c_ref, cs_ref, d_ref,
                     win_o, m_o, cout_o, mult_o, *, n_steps):
    lane = lax.broadcasted_iota(jnp.int32, (1, STATE_LANES), 1)
    low = lane < SSM_STATE
    sign_lo = jnp.where(low, -1.0, 1.0)
    k_idx = lax.broadcasted_iota(jnp.int32, (CHUNK, STATE_LANES), 0).astype(F32)
    step_of_row = lax.broadcasted_iota(jnp.int32, (CHUNK_LANES, STATE_LANES), 0) // SSM_GROUP
    nt = (((1,), (1,)), ((), ()))

    def outer(pw, pws, mat, mats):
        full = pw[:, None, :] * mat[None, :, :] + pws[:, None, :] * mats[None, :, :]
        return full.reshape(CHUNK * mat.shape[0], STATE_LANES)

    state_in, state_out, mults, tables = {}, {}, {}, {}
    for gi in range(2):
        taps_lo, taps_hi = None, None
        for d in range(2):
            lre = lre_ref[gi, d:d + 1, :]
            lim = lim_ref[gi, d:d + 1, :]
            dt = jnp.exp(ldt_ref[gi, d:d + 1, :])

            def power(expo, lre=lre, lim=lim, dt=dt):
                mag = jnp.exp(expo * dt * lre)
                ang = expo * dt * lim
                return mag * jnp.cos(ang), mag * jnp.sin(ang) * sign_lo

            a_re, a_ims = power(jnp.ones((1, STATE_LANES), F32))
            a_im = a_ims * sign_lo
            nr, ni = a_re - 1.0, a_im
            den = lre * lre + lim * lim
            cr = (nr * lre + ni * lim) / den
            ci = (ni * lre - nr * lim) / den
            cis = ci * sign_lo
            bbar = cr * bt_ref[gi, d] + cis * bts_ref[gi, d]
            bbar_s = cr * bts_ref[gi, d] - cis * bt_ref[gi, d]
            cmat, cmat_s = c_ref[gi, d], cs_ref[gi, d]

            asc, ascs = power(k_idx)
            desc, descs = power(CHUNK - 1.0 - k_idx)
            nxt, nxts = power(k_idx + 1.0)
            conj_c = lambda pw, pws: outer(pw, pws, cmat, cmat_s) * (-sign_lo)
            if d == 0:
                state_in[gi, d] = outer(desc, descs, bbar, bbar_s)
                state_out[gi, d] = conj_c(nxt, nxts)
                lag0 = jnp.where(step_of_row == CHUNK - 1, conj_c(desc, descs), 0.0)
                lags = jnp.where(step_of_row < CHUNK - 1, conj_c(nxt, nxts), 0.0)
                hi = lax.dot_general(bbar, lags, nt, preferred_element_type=F32,
                                     precision=lax.Precision.HIGHEST)
                lo = lax.dot_general(bbar, lag0, nt, preferred_element_type=F32,
                                     precision=lax.Precision.HIGHEST)
                taps_hi = hi
                taps_lo = lo if taps_lo is None else taps_lo + lo
            else:
                state_in[gi, d] = outer(asc, ascs, bbar, bbar_s)
                rev, revs = power(CHUNK - k_idx)
                state_out[gi, d] = conj_c(rev, revs)
                lo = lax.dot_general(bbar, conj_c(desc, descs), nt, preferred_element_type=F32,
                                     precision=lax.Precision.HIGHEST)
                taps_lo = lo if taps_lo is None else taps_lo + lo

            sr, sis = power(jnp.full((1, STATE_LANES), float(CHUNK), F32))
            chain = []
            for k in range(n_steps):
                chain.append((sr, sis))
                si = sis * sign_lo
                sr, sis = sr * sr - si * si, 2.0 * sr * si * sign_lo
            mults[gi, d] = chain
            r_idx = lax.broadcasted_iota(jnp.int32, (SUBLANES, STATE_LANES), 0).astype(F32)
            tables[gi, d] = power(CHUNK * (r_idx + 1.0) if d == 0 else CHUNK * (SUBLANES - r_idx))

        qq = lax.broadcasted_iota(jnp.int32, (SSM_GROUP, CHUNK_LANES), 0)
        ll = lax.broadcasted_iota(jnp.int32, (SSM_GROUP, CHUNK_LANES), 1)
        taps_lo = taps_lo + jnp.where(ll == qq + (CHUNK - 1) * SSM_GROUP, d_ref[gi], 0.0)
        taps = jnp.concatenate([taps_lo, taps_hi], axis=1)
        for j in range(CHUNK):
            off = (CHUNK - 1 - j) * SSM_GROUP
            m_o[gi, j * SSM_GROUP:(j + 1) * SSM_GROUP, :] = taps[:, off:off + CHUNK_LANES].astype(BF16)

    def swap(x):
        return pltpu.roll(x, SSM_STATE, 1)

    mult_o[...] = jnp.zeros(mult_o.shape, F32)
    wide_low = lax.broadcasted_iota(jnp.int32, (CHUNK_LANES, STATE_LANES), 1) < SSM_STATE
    zeros_c = jnp.zeros((SSM_STATE, CHUNK_LANES), F32)
    for d in range(2):
        s0, s1 = state_in[0, d], state_in[1, d]
        re_blk = jnp.concatenate([jnp.where(wide_low, s0, 0.0), jnp.where(wide_low, 0.0, swap(s1))], axis=0)
        im_blk = jnp.concatenate([jnp.where(wide_low, swap(s0), 0.0), jnp.where(wide_low, 0.0, s1)], axis=0)
        win_o[0, :, (2 * d) * LANES:(2 * d + 1) * LANES] = re_blk.astype(BF16)
        win_o[0, :, (2 * d + 1) * LANES:(2 * d + 2) * LANES] = im_blk.astype(BF16)
        t0, t1 = state_out[0, d].T, state_out[1, d].T
        for part in range(2):
            rows = slice(part * SSM_STATE, (part + 1) * SSM_STATE)
            blk = jnp.concatenate([jnp.concatenate([t0[rows], zeros_c], axis=1),
                                   jnp.concatenate([zeros_c, t1[rows]], axis=1)], axis=0)
            cout_o[0, (2 * d + part) * LANES:(2 * d + part + 1) * LANES, :] = blk.astype(BF16)
        for k in range(n_steps):
            (sr0, sis0), (sr1, sis1) = mults[0, d][k], mults[1, d][k]
            mult_o[0, k:k + 1, (2 * d) * LANES:(2 * d + 1) * LANES] = jnp.where(low, sr0, sr1)
            mult_o[0, k:k + 1, (2 * d + 1) * LANES:(2 * d + 2) * LANES] = jnp.where(low, -sis0, sis1)
        (tr0, tis0), (tr1, tis1) = tables[0, d], tables[1, d]
        mult_o[0, POWER_ROW0:POWER_ROW0 + SUBLANES, (2 * d) * LANES:(2 * d + 1) * LANES] = jnp.where(low, tr0, tr1)
        mult_o[0, POWER_ROW0:POWER_ROW0 + SUBLANES, (2 * d + 1) * LANES:(2 * d + 2) * LANES] = jnp.where(
            low, -tis0, tis1)


def _ssm_prep(lam_re, lam_im, log_dt, b_re, b_im, c_re, c_im, d_skip, n_steps):
    g = N_SSM_GROUPS
    dup = lambda a: jnp.concatenate([a, a], axis=-1)
    lre = dup(jnp.swapaxes(lam_re, 0, 1))
    lim = dup(jnp.swapaxes(lam_im, 0, 1))
    ldt = jnp.broadcast_to(jnp.swapaxes(log_dt, 0, 1)[..., None], (g, 2, STATE_LANES))
    btr = jnp.transpose(b_re, (1, 0, 3, 2))
    bti = jnp.transpose(b_im, (1, 0, 3, 2))
    bt, bts = jnp.concatenate([btr, bti], -1), jnp.concatenate([bti, btr], -1)
    cr, ci = jnp.swapaxes(c_re, 0, 1), jnp.swapaxes(c_im, 0, 1)
    cm, cms = jnp.concatenate([cr, ci], -1), jnp.concatenate([ci, cr], -1)
    dsk = jnp.pad(d_skip, ((0, 0), (CHUNK_LANES - SSM_GROUP, 0))).reshape(g, 1, CHUNK_LANES)
    n_pad = POWER_ROW0 + SUBLANES
    vec = pl.BlockSpec((2, 2, STATE_LANES), lambda i: (i, 0, 0))
    mat = pl.BlockSpec((2, 2, SSM_GROUP, STATE_LANES), lambda i: (i, 0, 0, 0))
    pair_sq = pl.BlockSpec((1, 2 * CHUNK_LANES, 4 * LANES), lambda i: (i, 0, 0))
    return pl.pallas_call(
        functools.partial(_ssm_prep_kernel, n_steps=n_steps),
        grid=(g // 2,),
        in_specs=[vec, vec, vec, mat, mat, mat, mat,
                  pl.BlockSpec((2, 1, CHUNK_LANES), lambda i: (i, 0, 0))],
        out_specs=[pair_sq, pl.BlockSpec((2, CHUNK_LANES, CHUNK_LANES), lambda i: (i, 0, 0)), pair_sq,
                   pl.BlockSpec((1, n_pad, 4 * LANES), lambda i: (i, 0, 0))],
        out_shape=[jax.ShapeDtypeStruct((g // 2, 2 * CHUNK_LANES, 4 * LANES), BF16),
                   jax.ShapeDtypeStruct((g, CHUNK_LANES, CHUNK_LANES), BF16),
                   jax.ShapeDtypeStruct((g // 2, 4 * LANES, 2 * CHUNK_LANES), BF16),
                   jax.ShapeDtypeStruct((g // 2, n_pad, 4 * LANES), F32)],
        compiler_params=_cparams("parallel"),
        name="ssm_prep",
    )(lre, lim, ldt, bt, bts, cm, cms, dsk)


def _complex_step(re, im, sr, si, ar, ai):
    return re + ar * sr - ai * si, im + ar * si + ai * sr


def _ssm_kernel(u_ref, win_ref, m_ref, cout_ref, mult_ref, y_o, loc_scr, car_scr, *, n_ctx_rows, n_steps):
    u0, u1 = u_ref[0, 0], u_ref[0, 1]
    rows = u0.shape[0]
    n_lat_rows = rows - n_ctx_rows
    n_blk = rows // SUBLANES
    n_blk_pad = -(-n_blk // SUBLANES) * SUBLANES
    x = jnp.dot(jnp.concatenate([u0, u1], axis=1), win_ref[0], preferred_element_type=F32)
    to_rev = lambda a: jnp.concatenate([a[n_ctx_rows:], a[:n_ctx_rows]], axis=0)
    state = [[x[:, :LANES], x[:, LANES:2 * LANES]],
             [to_rev(x[:, 2 * LANES:3 * LANES]), to_rev(x[:, 3 * LANES:])]]
    in_blk = lax.broadcasted_iota(jnp.int32, (rows, LANES), 0) & (SUBLANES - 1)
    blk = lax.broadcasted_iota(jnp.int32, (n_blk_pad, LANES), 0)

    def mult(k0, k1, d):
        return (mult_ref[0, k0:k1, 2 * d * LANES:(2 * d + 1) * LANES],
                mult_ref[0, k0:k1, (2 * d + 1) * LANES:(2 * d + 2) * LANES])

    def shifted(a, s, idx, n, down):
        if down:
            return jnp.where(idx >= s, pltpu.roll(a, s, 0), 0.0)
        return jnp.where(idx < n - s, pltpu.roll(a, a.shape[0] - s, 0), 0.0)

    tile = lambda t: jnp.broadcast_to(t[None], (n_blk, SUBLANES, LANES)).reshape(rows, LANES)
    sub_row = lax.broadcasted_iota(jnp.int32, (SUBLANES, LANES), 0)
    local_steps = SUBLANES.bit_length() - 1
    for d, down in ((0, True), (1, False)):
        re, im = state[d]
        for k in range(local_steps):
            s = 1 << k
            keep = sub_row >= s if down else sub_row < SUBLANES - s
            ar, ai = (tile(jnp.where(keep, m, 0.0)) for m in mult(k, k + 1, d))
            shift = s if down else rows - s
            re, im = _complex_step(re, im, pltpu.roll(re, shift, 0), pltpu.roll(im, shift, 0), ar, ai)
        loc_scr[2 * d], loc_scr[2 * d + 1] = re, im
        state[d] = [re, im]

        end_row = SUBLANES - 1 if down else 0
        pad = jnp.zeros((n_blk_pad - n_blk, LANES), F32)
        ere = jnp.concatenate([loc_scr[2 * d, pl.ds(end_row, n_blk, stride=SUBLANES), :], pad], axis=0)
        eim = jnp.concatenate([loc_scr[2 * d + 1, pl.ds(end_row, n_blk, stride=SUBLANES), :], pad], axis=0)
        for k in range(local_steps, n_steps):
            ar, ai = mult(k, k + 1, d)
            s = 1 << (k - local_steps)
            ere, eim = _complex_step(ere, eim, shifted(ere, s, blk, n_blk_pad, down),
                                     shifted(eim, s, blk, n_blk_pad, down), ar, ai)
        for c, ends in enumerate((ere, eim)):
            entering = shifted(ends, 1, blk, n_blk_pad, down)[:n_blk]
            for r in range(SUBLANES):
                car_scr[2 * d + c, pl.ds(r, n_blk, stride=SUBLANES), :] = entering

    for d in range(2):
        tre, tim = mult(POWER_ROW0, POWER_ROW0 + SUBLANES, d)
        re, im = state[d]
        state[d] = list(_complex_step(re, im, car_scr[2 * d], car_scr[2 * d + 1], tile(tre), tile(tim)))

    prev = lambda a: pltpu.roll(a, 1, 0)[n_ctx_rows:]
    nxt = lambda a: pltpu.roll(a, rows - 1, 0)[:n_lat_rows]
    st = jnp.concatenate([prev(state[0][0]), prev(state[0][1]), nxt(state[1][0]), nxt(state[1][1])],
                         axis=1).astype(BF16)
    y = jnp.dot(st, cout_ref[0], preferred_element_type=F32)
    y_o[0, 0] = (y[:, :CHUNK_LANES]
                 + jnp.dot(u0[n_ctx_rows:], m_ref[0], preferred_element_type=F32)).astype(BF16)
    y_o[0, 1] = (y[:, CHUNK_LANES:]
                 + jnp.dot(u1[n_ctx_rows:], m_ref[1], preferred_element_type=F32)).astype(BF16)


def _ssm(u_cf, win_pair, m, cout_pair, mult, n_ctx_rows, n_steps):
    bsz, g, rows, _ = u_cf.shape
    assert rows % SUBLANES == 0 and n_ctx_rows % 16 == 0
    pair_spec = lambda a: pl.BlockSpec((1,) + a.shape[1:], lambda b, i: (i, 0, 0))
    return pl.pallas_call(
        functools.partial(_ssm_kernel, n_ctx_rows=n_ctx_rows, n_steps=n_steps),
        grid=(bsz, g // 2),
        in_specs=[pl.BlockSpec((1, 2, rows, CHUNK_LANES), lambda b, i: (b, i, 0, 0)),
                  pair_spec(win_pair),
                  pl.BlockSpec((2, CHUNK_LANES, CHUNK_LANES), lambda b, i: (i, 0, 0)),
                  pair_spec(cout_pair), pair_spec(mult)],
        out_specs=pl.BlockSpec((1, 2, rows - n_ctx_rows, CHUNK_LANES), lambda b, i: (b, i, 0, 0)),
        out_shape=jax.ShapeDtypeStruct((bsz, g, rows - n_ctx_rows, CHUNK_LANES), BF16),
        scratch_shapes=[pltpu.VMEM((4, rows, LANES), F32), pltpu.VMEM((4, rows, LANES), F32)],
        compiler_params=_cparams("parallel", "parallel"),
        name="ssm",
    )(u_cf, win_pair, m, cout_pair, mult)


def _final_kernel(x_ref, at_ref, ga_ref, y_ref, gs_ref, gate_ref, wglu_ref, bglu_ref, wout_ref,
                  pg_ref, o_ref, y_scr):
    _from_chunk_lanes(y_ref, y_scr, 0, y_ref.shape[2])
    y = jax.nn.gelu(jnp.concatenate([y_scr[col] for col in range(SSM_WIDTH // LANES)], axis=1))
    t = jnp.dot(y.astype(BF16), wglu_ref[...], preferred_element_type=F32) + bglu_ref[...]
    s = (y * jax.nn.sigmoid(t) * gs_ref[0].astype(F32)).astype(BF16)
    a = at_ref[0] * ga_ref[0]
    out = jnp.dot(jnp.concatenate([a, s], axis=1), wout_ref[...], preferred_element_type=F32)
    ms = jnp.mean(out * out, axis=-1, keepdims=True)
    o_ref[0] = x_ref[0] + out * lax.rsqrt(ms + NORM_EPS) * (gate_ref[0] * pg_ref[...])


def _final(x, attn, ga, y_cf, gs, gate, wglu_bf, b_glu, wout_bf, post_g, tm):
    bsz, n, d = x.shape
    assert n % tm == 0 and tm % (CHUNK * 16) == 0
    return pl.pallas_call(
        _final_kernel,
        grid=(bsz, n // tm),
        in_specs=[_row_spec(tm, d), _row_spec(tm, ATTN_WIDTH), _row_spec(tm, ATTN_WIDTH),
                  pl.BlockSpec((1, N_SSM_GROUPS, tm // CHUNK, CHUNK_LANES), lambda b, i: (b, 0, i, 0)),
                  _row_spec(tm, SSM_WIDTH), _batch_vec_spec(d),
                  _const_spec((SSM_WIDTH, SSM_WIDTH)), _const_spec((1, SSM_WIDTH)),
                  _const_spec((d, d)), _const_spec((1, d))],
        out_specs=_row_spec(tm, d),
        out_shape=jax.ShapeDtypeStruct((bsz, n, d), F32),
        scratch_shapes=[pltpu.VMEM((SSM_WIDTH // LANES, tm, LANES), F32)],
        compiler_params=_cparams("parallel", "parallel"),
        name="final",
    )(x, attn, ga, y_cf, gs, gate, wglu_bf, b_glu, wout_bf, post_g)


def _rope_tables(n_ctx, n_lat):
    t = jnp.arange(n_lat, dtype=jnp.int32)
    row_pos = (t // GRID_W).astype(F32)
    col_pos = (t % GRID_W).astype(F32)
    inv_freq = ROPE_THETA ** (-jnp.arange(ROPE_FREQS, dtype=F32) / ROPE_FREQS)
    ang_r = row_pos[:, None] * inv_freq
    ang_c = col_pos[:, None] * inv_freq
    cos = jnp.concatenate([jnp.cos(ang_r)] * 2 + [jnp.cos(ang_c)] * 2, axis=1)
    sin = jnp.concatenate([-jnp.sin(ang_r), jnp.sin(ang_r), -jnp.sin(ang_c), jnp.sin(ang_c)], axis=1)
    cos = jnp.concatenate([jnp.ones((n_ctx, HEAD_DIM), F32), cos], axis=0)
    sin = jnp.concatenate([jnp.zeros((n_ctx, HEAD_DIM), F32), sin], axis=0)
    return jnp.tile(cos, (1, LANES // HEAD_DIM)), jnp.tile(sin, (1, LANES // HEAD_DIM))


def _layer(x, ctx, c, c_ctx, w_ada, b_ada, pre_g, post_g, w_in, q_g, k_g, lam_re, lam_im, log_dt,
           b_re, b_im, c_re, c_im, d_skip, w_glu, b_glu, w_out):
    bsz, n_lat, d = x.shape
    n_ctx = ctx.shape[1]
    assert n_lat % GRID_W == 0

    cvecs = jnp.zeros((8, d), F32).at[:bsz].set(c).at[bsz].set(c_ctx)
    mod = _adaln(cvecs, w_ada, b_ada)
    shift, scale, gate = (mod[:, i * d:(i + 1) * d] for i in range(3))
    lat = lambda a: a[:bsz].reshape(bsz, 1, d)
    cvec = lambda a: a[bsz].reshape(1, 1, d)

    head = jnp.arange(ATTN_WIDTH) // HEAD_DIM
    bd = (head[:, None] == head[None, :]).astype(BF16) * (1.0 / HEAD_DIM)
    qg = jnp.tile(q_g, N_Q_HEADS).reshape(1, ATTN_WIDTH)
    kg = jnp.tile(k_g, N_KV_HEADS).reshape(1, KV_WIDTH)
    cos, sin = _rope_tables(n_ctx, n_lat)

    qt, k_all, vt_aug, ga, u_cf, gs = _inproj(x, ctx, lat(shift), lat(scale), cvec(shift), cvec(scale),
                                              pre_g.reshape(1, d), w_in.astype(BF16), qg, kg, cos, sin,
                                              bd, tm=256)
    attn = _attention(qt, k_all, vt_aug, tq=256, tk=2816, sub=256, lookahead=5, heads_per_pass=8)

    rows = (n_ctx + n_lat) // CHUNK
    n_steps = _scan_steps(rows)
    win_pair, m_op, cout_pair, mult = _ssm_prep(lam_re, lam_im, log_dt, b_re, b_im, c_re, c_im, d_skip,
                                                n_steps)
    y_cf = _ssm(u_cf, win_pair, m_op, cout_pair, mult, n_ctx // CHUNK, n_steps)

    return _final(x, attn, ga, y_cf, gs, lat(gate), w_glu.astype(BF16), b_glu.reshape(1, SSM_WIDTH),
                  w_out.astype(BF16), post_g.reshape(1, d), tm=512)


def kernel(x, c, ctx, c_ctx, w_ada, b_ada, pre_norm, post_norm, w_in, q_norm, k_norm, ssm_lam_re,
           ssm_lam_im, ssm_log_dt, ssm_b_re, ssm_b_im, ssm_c_re, ssm_c_im, ssm_d, w_glu, b_glu, w_out):
    depth = w_ada.shape[0]
    assert depth == 1, "context stream update between layers is not implemented"
    return _layer(x, ctx, c, c_ctx, w_ada[0], b_ada[0], pre_norm[0], post_norm[0], w_in[0], q_norm[0],
                  k_norm[0], ssm_lam_re[0], ssm_lam_im[0], ssm_log_dt[0], ssm_b_re[0], ssm_b_im[0],
                  ssm_c_re[0], ssm_c_im[0], ssm_d[0], w_glu[0], b_glu[0], w_out[0])
```

```python
import functools
import math

import jax
import jax.numpy as jnp
from jax import lax
from jax.experimental import pallas as pl
from jax.experimental.pallas import tpu as pltpu

F32 = jnp.float32
BF16 = jnp.bfloat16

D_MODEL = 1024
HEAD_DIM = 64
N_Q_HEADS = 8
N_KV_HEADS = 2
ATTN_WIDTH = N_Q_HEADS * HEAD_DIM
KV_WIDTH = N_KV_HEADS * HEAD_DIM
SSM_WIDTH = 512
SSM_GROUP = 16
N_SSM_GROUPS = SSM_WIDTH // SSM_GROUP
SSM_STATE = 64
GRID_W = 64
ROPE_THETA = 10000.0
ROPE_FREQS = 16
NORM_EPS = 1e-6
ATTN_SCALE = HEAD_DIM ** -0.5
LOG2_E = math.log2(math.e)
Q_END = ATTN_WIDTH
K_END = Q_END + KV_WIDTH
V_END = K_END + KV_WIDTH
GA_END = V_END + ATTN_WIDTH
U_END = GA_END + SSM_WIDTH
IN_WIDTH = U_END + SSM_WIDTH

CHUNK = 16
CHUNK_LANES = CHUNK * SSM_GROUP
LANES = 128
GROUPS_PER_VREG = LANES // SSM_GROUP
STATE_LANES = 2 * SSM_STATE
SUBLANES = 8
POWER_ROW0 = 16
V_ROWS = HEAD_DIM + 16
VMEM_LIMIT = 48 * 1024 * 1024
NEG_BIG = -1e30


def _cparams(*sem):
    return pltpu.CompilerParams(dimension_semantics=sem, vmem_limit_bytes=VMEM_LIMIT)


def _row_spec(tm, width):
    return pl.BlockSpec((1, tm, width), lambda b, i: (b, i, 0))


def _const_spec(shape):
    return pl.BlockSpec(shape, lambda b, i: (0,) * len(shape))


def _batch_vec_spec(width):
    return pl.BlockSpec((1, 1, width), lambda b, i: (b, 0, 0))


def _adaln_kernel(c_ref, w_ref, b_ref, o_ref):
    c = c_ref[...]
    s = c * jax.nn.sigmoid(c)
    o_ref[...] = jnp.dot(s, w_ref[...], preferred_element_type=F32,
                         precision=lax.Precision.HIGHEST) + b_ref[...]


def _adaln(cvecs, w_ada, b_ada):
    rows, d = cvecs.shape
    n = w_ada.shape[1]
    tn = 512
    return pl.pallas_call(
        _adaln_kernel,
        grid=(n // tn,),
        in_specs=[pl.BlockSpec((rows, d), lambda j: (0, 0)),
                  pl.BlockSpec((d, tn), lambda j: (0, j)),
                  pl.BlockSpec((1, tn), lambda j: (0, j))],
        out_specs=pl.BlockSpec((rows, tn), lambda j: (0, j)),
        out_shape=jax.ShapeDtypeStruct((rows, n), F32),
        compiler_params=_cparams("arbitrary"),
        name="adaln",
    )(cvecs, w_ada, b_ada.reshape(1, n))


def _lane_group(rows):
    return lax.broadcasted_iota(jnp.int32, (rows, LANES), 1) // SSM_GROUP


def _to_chunk_lanes(u_scr, u_o):
    rt = u_o.shape[2]
    grp = _lane_group(rt)
    for col in range(SSM_WIDTH // LANES):
        rolled = []
        for step in range(CHUNK):
            s = u_scr[col, pl.ds(step, rt, stride=CHUNK), :]
            rolled.append([s if k == 0 else pltpu.roll(s, k * SSM_GROUP, 1)
                           for k in range(GROUPS_PER_VREG)])
        for g_lo in range(GROUPS_PER_VREG):
            for half in range(CHUNK // GROUPS_PER_VREG):
                out = None
                for s8 in range(GROUPS_PER_VREG):
                    piece = rolled[half * GROUPS_PER_VREG + s8][(s8 - g_lo) % GROUPS_PER_VREG]
                    out = piece if out is None else jnp.where(grp == s8, piece, out)
                u_o[0, col * GROUPS_PER_VREG + g_lo, :, half * LANES:(half + 1) * LANES] = out.astype(BF16)


def _from_chunk_lanes(y_ref, y_scr, r0, rt):
    grp = _lane_group(rt)
    for col in range(SSM_WIDTH // LANES):
        rolled = {}
        for g_lo in range(GROUPS_PER_VREG):
            for half in range(CHUNK // GROUPS_PER_VREG):
                s = y_ref[0, col * GROUPS_PER_VREG + g_lo, r0:r0 + rt,
                          half * LANES:(half + 1) * LANES].astype(F32)
                rolled[g_lo, half] = [s if k == 0 else pltpu.roll(s, k * SSM_GROUP, 1)
                                      for k in range(GROUPS_PER_VREG)]
        for step in range(CHUNK):
            half, s8 = divmod(step, GROUPS_PER_VREG)
            out = None
            for g_lo in range(GROUPS_PER_VREG):
                piece = rolled[g_lo, half][(g_lo - s8) % GROUPS_PER_VREG]
                out = piece if out is None else jnp.where(grp == g_lo, piece, out)
            y_scr[col, pl.ds(r0 * CHUNK + step, rt, stride=CHUNK), :] = out


def _head_mean_sq(z, bd):
    return jnp.dot((z * z).astype(BF16), bd, preferred_element_type=F32)


def _swap16(x):
    w = x.shape[1]
    lane = lax.broadcasted_iota(jnp.int32, x.shape, 1)
    return jnp.where((lane & 16) == 0, pltpu.roll(x, w - 16, 1), pltpu.roll(x, 16, 1))


def _rope(x, cos, sin_signed):
    cols = []
    for c in range(x.shape[1] // LANES):
        xc = x[:, c * LANES:(c + 1) * LANES]
        cols.append(xc * cos + _swap16(xc) * sin_signed)
    return cols[0] if len(cols) == 1 else jnp.concatenate(cols, axis=1)


def _silu(z):
    return z * jax.nn.sigmoid(z)


def _inproj_kernel(x_ref, c_ref, shl_ref, scl_ref, shc_ref, scc_ref, pg_ref, w_ref, qg_ref, kg_ref,
                   cos_ref, sin_ref, bd_ref, qt_o, k_o, vt_o, ga_o, u_o, gs_o, v_scr, u_scr, *, n_ctx_tiles):
    is_ctx = pl.program_id(1) < n_ctx_tiles
    x = jnp.where(is_ctx, c_ref[0], x_ref[0])
    shift = jnp.where(is_ctx, shc_ref[0], shl_ref[0])
    scale = jnp.where(is_ctx, scc_ref[0], scl_ref[0])
    ms = jnp.mean(x * x, axis=-1, keepdims=True)
    xn = x * lax.rsqrt(ms + NORM_EPS) * pg_ref[...]
    h = (xn * (1.0 + scale) + shift).astype(BF16)

    def proj(a, b):
        return jnp.dot(h, w_ref[:, a:b], preferred_element_type=F32)

    u = proj(GA_END, U_END)
    for col in range(SSM_WIDTH // LANES):
        u_scr[col] = u[:, col * LANES:(col + 1) * LANES]
    _to_chunk_lanes(u_scr, u_o)

    cos = cos_ref[...]
    sin = sin_ref[...]
    zq = proj(0, Q_END)
    qn = zq * lax.rsqrt(_head_mean_sq(zq, bd_ref[...]) + NORM_EPS) * qg_ref[...]
    q = _rope(qn, cos, sin) * (ATTN_SCALE * LOG2_E)
    group = N_Q_HEADS // N_KV_HEADS
    none = jnp.zeros((HEAD_DIM, q.shape[0]), BF16)
    for c in range(ATTN_WIDTH // LANES):
        qt = q[:, c * LANES:(c + 1) * LANES].T.astype(BF16)
        for par in range(2):
            head = 2 * c + par
            kv = head // group
            qt_o[0, head, kv * HEAD_DIM:(kv + 1) * HEAD_DIM, :] = qt[par * HEAD_DIM:(par + 1) * HEAD_DIM]
            qt_o[0, head, (1 - kv) * HEAD_DIM:(2 - kv) * HEAD_DIM, :] = none
    zk = proj(Q_END, K_END)
    kn = zk * lax.rsqrt(_head_mean_sq(zk, bd_ref[:KV_WIDTH, :KV_WIDTH]) + NORM_EPS) * kg_ref[...]
    k_o[0] = _rope(kn, cos, sin).astype(BF16)

    v_scr[...] = proj(K_END, V_END)
    vt = v_scr[...].T.astype(BF16)
    tm = vt.shape[1]
    ones_row = lax.broadcasted_iota(jnp.int32, (V_ROWS - HEAD_DIM, tm), 0) == 0
    for j in range(N_KV_HEADS):
        vt_o[0, j, :HEAD_DIM, :] = vt[j * HEAD_DIM:(j + 1) * HEAD_DIM]
        vt_o[0, j, HEAD_DIM:, :] = jnp.where(ones_row, 1.0, 0.0).astype(BF16)

    ga_o[0] = _silu(proj(V_END, GA_END)).astype(BF16)
    gs_o[0] = _silu(proj(U_END, IN_WIDTH)).astype(BF16)


def _inproj(x, ctx, shift_l, scale_l, shift_c, scale_c, pre_g, w_bf, qg, kg, cos, sin, bd, tm):
    bsz, n_lat, d = x.shape
    n_ctx = ctx.shape[1]
    assert n_ctx % tm == 0 and n_lat % tm == 0 and tm % (CHUNK * 16) == 0
    nct = n_ctx // tm
    n_tot = n_ctx + n_lat
    lat_rows = lambda w: pl.BlockSpec((1, tm, w), lambda b, i: (b, jnp.maximum(i - nct, 0), 0))
    all_rows = lambda w: pl.BlockSpec((1, tm, w), lambda b, i: (b, i, 0))
    ctx_vec = pl.BlockSpec((1, 1, d), lambda b, i: (0, 0, 0))
    table = pl.BlockSpec((tm, LANES), lambda b, i: (i, 0))
    return pl.pallas_call(
        functools.partial(_inproj_kernel, n_ctx_tiles=nct),
        grid=(bsz, n_tot // tm),
        in_specs=[lat_rows(d),
                  pl.BlockSpec((1, tm, d), lambda b, i: (b, jnp.minimum(i, nct - 1), 0)),
                  _batch_vec_spec(d), _batch_vec_spec(d), ctx_vec, ctx_vec, _const_spec((1, d)),
                  _const_spec((d, IN_WIDTH)), _const_spec((1, ATTN_WIDTH)), _const_spec((1, KV_WIDTH)),
                  table, table, _const_spec((ATTN_WIDTH, ATTN_WIDTH))],
        out_specs=[pl.BlockSpec((1, N_Q_HEADS, LANES, tm), lambda b, i: (b, 0, 0, jnp.maximum(i - nct, 0))),
                   all_rows(KV_WIDTH),
                   pl.BlockSpec((1, N_KV_HEADS, V_ROWS, tm), lambda b, i: (b, 0, 0, i)),
                   lat_rows(ATTN_WIDTH),
                   pl.BlockSpec((1, N_SSM_GROUPS, tm // CHUNK, CHUNK_LANES), lambda b, i: (b, 0, i, 0)),
                   lat_rows(SSM_WIDTH)],
        out_shape=[jax.ShapeDtypeStruct((bsz, N_Q_HEADS, LANES, n_lat), BF16),
                   jax.ShapeDtypeStruct((bsz, n_tot, KV_WIDTH), BF16),
                   jax.ShapeDtypeStruct((bsz, N_KV_HEADS, V_ROWS, n_tot), BF16),
                   jax.ShapeDtypeStruct((bsz, n_lat, ATTN_WIDTH), BF16),
                   jax.ShapeDtypeStruct((bsz, N_SSM_GROUPS, n_tot // CHUNK, CHUNK_LANES), BF16),
                   jax.ShapeDtypeStruct((bsz, n_lat, SSM_WIDTH), BF16)],
        scratch_shapes=[pltpu.VMEM((tm, KV_WIDTH), F32), pltpu.VMEM((SSM_WIDTH // LANES, tm, LANES), F32)],
        compiler_params=_cparams("parallel", "arbitrary"),
        name="inproj",
    )(x, ctx, shift_l, scale_l, shift_c, scale_c, pre_g, w_bf, qg, kg, cos, sin, bd)


def _attn_kernel(qt_ref, k_ref, vt_ref, o_ref, *, tk, sub, lookahead, heads_per_pass):
    tq = qt_ref.shape[3]
    n_keys = k_ref.shape[1]
    group = N_Q_HEADS // N_KV_HEADS

    outs = []
    for h0 in range(0, N_Q_HEADS, heads_per_pass):
        heads = list(range(h0, h0 + heads_per_pass))
        q_wide = [qt_ref[0, h] for h in heads]

        def body(t, carry, heads=heads, q_wide=q_wide):
            tasks = [(j, i) for j in range(tk // sub) for i in range(len(heads))]
            state = list(carry)
            scores = {}

            def keys_at(j):
                return pl.ds(pl.multiple_of(t * tk + j * sub, sub), sub)

            def issue(n):
                j, i = tasks[n]
                scores[n] = jnp.dot(k_ref[0, keys_at(j), :], q_wide[i],
                                    preferred_element_type=F32)

            def consume(n):
                j, i = tasks[n]
                s = scores.pop(n)
                m_old, acc = state[i]
                vt = vt_ref[0, heads[i] // group, :, keys_at(j)]
                m_new = jnp.maximum(m_old, jnp.max(s, axis=0, keepdims=True))
                p = jnp.exp2((s - m_new).astype(BF16))
                acc = jnp.exp2(m_old - m_new) * acc + jnp.dot(vt, p, preferred_element_type=F32)
                state[i] = (m_new, acc)

            for n in range(len(tasks) + lookahead):
                if n < len(tasks):
                    issue(n)
                if n >= lookahead:
                    consume(n - lookahead)
            return tuple(state)

        init = tuple((jnp.full((1, tq), NEG_BIG, F32), jnp.zeros((V_ROWS, tq), F32)) for _ in heads)
        for _, acc in lax.fori_loop(0, n_keys // tk, body, init):
            outs.append(acc[:HEAD_DIM] / acc[HEAD_DIM:HEAD_DIM + 1])

    for c in range(N_Q_HEADS // 2):
        pair = jnp.concatenate([outs[2 * c], outs[2 * c + 1]], axis=0)
        o_ref[0, :, c * LANES:(c + 1) * LANES] = pair.T.astype(BF16)


def _attention(qt, k_all, vt_aug, tq, tk, sub, lookahead, heads_per_pass):
    bsz, _, _, n = qt.shape
    n_keys = k_all.shape[1]
    assert n_keys % tk == 0 and tk % sub == 0 and n % tq == 0
    return pl.pallas_call(
        functools.partial(_attn_kernel, tk=tk, sub=sub, lookahead=lookahead,
                          heads_per_pass=heads_per_pass),
        grid=(bsz, n // tq),
        in_specs=[pl.BlockSpec((1, N_Q_HEADS, LANES, tq), lambda b, i: (b, 0, 0, i)),
                  pl.BlockSpec((1, n_keys, KV_WIDTH), lambda b, i: (b, 0, 0)),
                  pl.BlockSpec((1, N_KV_HEADS, V_ROWS, n_keys), lambda b, i: (b, 0, 0, 0))],
        out_specs=_row_spec(tq, ATTN_WIDTH),
        out_shape=jax.ShapeDtypeStruct((bsz, n, ATTN_WIDTH), BF16),
        compiler_params=_cparams("parallel", "parallel"),
        name="attention",
    )(qt, k_all, vt_aug)


def _scan_steps(n_rows):
    return max(1, math.ceil(math.log2(n_rows)))


def _ssm_prep_kernel(lre_ref, lim_ref, ldt_ref, bt_ref, bts_ref, c_ref, cs_ref, d_ref,
                     win_o, m_o, cout_o, mult_o, *, n_steps):
    lane = lax.broadcasted_iota(jnp.int32, (1, STATE_LANES), 1)
    low = lane < SSM_STATE
    sign_lo = jnp.where(low, -1.0, 1.0)
    k_idx = lax.broadcasted_iota(jnp.int32, (CHUNK, STATE_LANES), 0).astype(F32)
    step_of_row = lax.broadcasted_iota(jnp.int32, (CHUNK_LANES, STATE_LANES), 0) // SSM_GROUP
    nt = (((1,), (1,)), ((), ()))

    def outer(pw, pws, mat, mats):
        full = pw[:, None, :] * mat[None, :, :] + pws[:, None, :] * mats[None, :, :]
        return full.reshape(CHUNK * mat.shape[0], STATE_LANES)

    state_in, state_out, mults, tables = {}, {}, {}, {}
    for gi in range(2):
        taps_lo, taps_hi = None, None
        for d in range(2):
            lre = lre_ref[gi, d:d + 1, :]
            lim = lim_ref[gi, d:d + 1, :]
            dt = jnp.exp(ldt_ref[gi, d:d + 1, :])

            def power(expo, lre=lre, lim=lim, dt=dt):
                mag = jnp.exp(expo * dt * lre)
                ang = expo * dt * lim
                return mag * jnp.cos(ang), mag * jnp.sin(ang) * sign_lo

            a_re, a_ims = power(jnp.ones((1, STATE_LANES), F32))
            a_im = a_ims * sign_lo
            nr, ni = a_re - 1.0, a_im
            den = lre * lre + lim * lim
            cr = (nr * lre + ni * lim) / den
            ci = (ni * lre - nr * lim) / den
            cis = ci * sign_lo
            bbar = cr * bt_ref[gi, d] + cis * bts_ref[gi, d]
            bbar_s = cr * bts_ref[gi, d] - cis * bt_ref[gi, d]
            cmat, cmat_s = c_ref[gi, d], cs_ref[gi, d]

            asc, ascs = power(k_idx)
            desc, descs = power(CHUNK - 1.0 - k_idx)
            nxt, nxts = power(k_idx + 1.0)
            conj_c = lambda pw, pws: outer(pw, pws, cmat, cmat_s) * (-sign_lo)
            if d == 0:
                state_in[gi, d] = outer(desc, descs, bbar, bbar_s)
                state_out[gi, d] = conj_c(nxt, nxts)
                lag0 = jnp.where(step_of_row == CHUNK - 1, conj_c(desc, descs), 0.0)
                lags = jnp.where(step_of_row < CHUNK - 1, conj_c(nxt, nxts), 0.0)
                hi = lax.dot_general(bbar, lags, nt, preferred_element_type=F32,
                                     precision=lax.Precision.HIGHEST)
                lo = lax.dot_general(bbar, lag0, nt, preferred_element_type=F32,
                                     precision=lax.Precision.HIGHEST)
                taps_hi = hi
                taps_lo = lo if taps_lo is None else taps_lo + lo
            else:
                state_in[gi, d] = outer(asc, ascs, bbar, bbar_s)
                rev, revs = power(CHUNK - k_idx)
                state_out[gi, d] = conj_c(rev, revs)
                lo = lax.dot_general(bbar, conj_c(desc, descs), nt, preferred_element_type=F32,
                                     precision=lax.Precision.HIGHEST)
                taps_lo = lo if taps_lo is None else taps_lo + lo

            sr, sis = power(jnp.full((1, STATE_LANES), float(CHUNK), F32))
            chain = []
            for k in range(n_steps):
                chain.append((sr, sis))
                si = sis * sign_lo
                sr, sis = sr * sr - si * si, 2.0 * sr * si * sign_lo
            mults[gi, d] = chain
            r_idx = lax.broadcasted_iota(jnp.int32, (SUBLANES, STATE_LANES), 0).astype(F32)
            tables[gi, d] = power(CHUNK * (r_idx + 1.0) if d == 0 else CHUNK * (SUBLANES - r_idx))

        qq = lax.broadcasted_iota(jnp.int32, (SSM_GROUP, CHUNK_LANES), 0)
        ll = lax.broadcasted_iota(jnp.int32, (SSM_GROUP, CHUNK_LANES), 1)
        taps_lo = taps_lo + jnp.where(ll == qq + (CHUNK - 1) * SSM_GROUP, d_ref[gi], 0.0)
        taps = jnp.concatenate([taps_lo, taps_hi], axis=1)
        for j in range(CHUNK):
            off = (CHUNK - 1 - j) * SSM_GROUP
            m_o[gi, j * SSM_GROUP:(j + 1) * SSM_GROUP, :] = taps[:, off:off + CHUNK_LANES].astype(BF16)

    def swap(x):
        return pltpu.roll(x, SSM_STATE, 1)

    mult_o[...] = jnp.zeros(mult_o.shape, F32)
    wide_low = lax.broadcasted_iota(jnp.int32, (CHUNK_LANES, STATE_LANES), 1) < SSM_STATE
    zeros_c = jnp.zeros((SSM_STATE, CHUNK_LANES), F32)
    for d in range(2):
        s0, s1 = state_in[0, d], state_in[1, d]
        re_blk = jnp.concatenate([jnp.where(wide_low, s0, 0.0), jnp.where(wide_low, 0.0, swap(s1))], axis=0)
        im_blk = jnp.concatenate([jnp.where(wide_low, swap(s0), 0.0), jnp.where(wide_low, 0.0, s1)], axis=0)
        win_o[0, :, (2 * d) * LANES:(2 * d + 1) * LANES] = re_blk.astype(BF16)
        win_o[0, :, (2 * d + 1) * LANES:(2 * d + 2) * LANES] = im_blk.astype(BF16)
        t0, t1 = state_out[0, d].T, state_out[1, d].T
        for part in range(2):
            rows = slice(part * SSM_STATE, (part + 1) * SSM_STATE)
            blk = jnp.concatenate([jnp.concatenate([t0[rows], zeros_c], axis=1),
                                   jnp.concatenate([zeros_c, t1[rows]], axis=1)], axis=0)
            cout_o[0, (2 * d + part) * LANES:(2 * d + part + 1) * LANES, :] = blk.astype(BF16)
        for k in range(n_steps):
            (sr0, sis0), (sr1, sis1) = mults[0, d][k], mults[1, d][k]
            mult_o[0, k:k + 1, (2 * d) * LANES:(2 * d + 1) * LANES] = jnp.where(low, sr0, sr1)
            mult_o[0, k:k + 1, (2 * d + 1) * LANES:(2 * d + 2) * LANES] = jnp.where(low, -sis0, sis1)
        (tr0, tis0), (tr1, tis1) = tables[0, d], tables[1, d]
        mult_o[0, POWER_ROW0:POWER_ROW0 + SUBLANES, (2 * d) * LANES:(2 * d + 1) * LANES] = jnp.where(low, tr0, tr1)
        mult_o[0, POWER_ROW0:POWER_ROW0 + SUBLANES, (2 * d + 1) * LANES:(2 * d + 2) * LANES] = jnp.where(
            low, -tis0, tis1)


def _ssm_prep(lam_re, lam_im, log_dt, b_re, b_im, c_re, c_im, d_skip, n_steps):
    g = N_SSM_GROUPS
    dup = lambda a: jnp.concatenate([a, a], axis=-1)
    lre = dup(jnp.swapaxes(lam_re, 0, 1))
    lim = dup(jnp.swapaxes(lam_im, 0, 1))
    ldt = jnp.broadcast_to(jnp.swapaxes(log_dt, 0, 1)[..., None], (g, 2, STATE_LANES))
    btr = jnp.transpose(b_re, (1, 0, 3, 2))
    bti = jnp.transpose(b_im, (1, 0, 3, 2))
    bt, bts = jnp.concatenate([btr, bti], -1), jnp.concatenate([bti, btr], -1)
    cr, ci = jnp.swapaxes(c_re, 0, 1), jnp.swapaxes(c_im, 0, 1)
    cm, cms = jnp.concatenate([cr, ci], -1), jnp.concatenate([ci, cr], -1)
    dsk = jnp.pad(d_skip, ((0, 0), (CHUNK_LANES - SSM_GROUP, 0))).reshape(g, 1, CHUNK_LANES)
    n_pad = POWER_ROW0 + SUBLANES
    vec = pl.BlockSpec((2, 2, STATE_LANES), lambda i: (i, 0, 0))
    mat = pl.BlockSpec((2, 2, SSM_GROUP, STATE_LANES), lambda i: (i, 0, 0, 0))
    pair_sq = pl.BlockSpec((1, 2 * CHUNK_LANES, 4 * LANES), lambda i: (i, 0, 0))
    return pl.pallas_call(
        functools.partial(_ssm_prep_kernel, n_steps=n_steps),
        grid=(g // 2,),
        in_specs=[vec, vec, vec, mat, mat, mat, mat,
                  pl.BlockSpec((2, 1, CHUNK_LANES), lambda i: (i, 0, 0))],
        out_specs=[pair_sq, pl.BlockSpec((2, CHUNK_LANES, CHUNK_LANES), lambda i: (i, 0, 0)), pair_sq,
                   pl.BlockSpec((1, n_pad, 4 * LANES), lambda i: (i, 0, 0))],
        out_shape=[jax.ShapeDtypeStruct((g // 2, 2 * CHUNK_LANES, 4 * LANES), BF16),
                   jax.ShapeDtypeStruct((g, CHUNK_LANES, CHUNK_LANES), BF16),
                   jax.ShapeDtypeStruct((g // 2, 4 * LANES, 2 * CHUNK_LANES), BF16),
                   jax.ShapeDtypeStruct((g // 2, n_pad, 4 * LANES), F32)],
        compiler_params=_cparams("parallel"),
        name="ssm_prep",
    )(lre, lim, ldt, bt, bts, cm, cms, dsk)


def _complex_step(re, im, sr, si, ar, ai):
    return re + ar * sr - ai * si, im + ar * si + ai * sr


def _ssm_kernel(u_ref, win_ref, m_ref, cout_ref, mult_ref, y_o, loc_scr, car_scr, *, n_ctx_rows, n_steps):
    u0, u1 = u_ref[0, 0], u_ref[0, 1]
    rows = u0.shape[0]
    n_lat_rows = rows - n_ctx_rows
    n_blk = rows // SUBLANES
    n_blk_pad = -(-n_blk // SUBLANES) * SUBLANES
    x = jnp.dot(jnp.concatenate([u0, u1], axis=1), win_ref[0], preferred_element_type=F32)
    to_rev = lambda a: jnp.concatenate([a[n_ctx_rows:], a[:n_ctx_rows]], axis=0)
    state = [[x[:, :LANES], x[:, LANES:2 * LANES]],
             [to_rev(x[:, 2 * LANES:3 * LANES]), to_rev(x[:, 3 * LANES:])]]
    in_blk = lax.broadcasted_iota(jnp.int32, (rows, LANES), 0) & (SUBLANES - 1)
    blk = lax.broadcasted_iota(jnp.int32, (n_blk_pad, LANES), 0)

    def mult(k0, k1, d):
        return (mult_ref[0, k0:k1, 2 * d * LANES:(2 * d + 1) * LANES],
                mult_ref[0, k0:k1, (2 * d + 1) * LANES:(2 * d + 2) * LANES])

    def shifted(a, s, idx, n, down):
        if down:
            return jnp.where(idx >= s, pltpu.roll(a, s, 0), 0.0)
        return jnp.where(idx < n - s, pltpu.roll(a, a.shape[0] - s, 0), 0.0)

    tile = lambda t: jnp.broadcast_to(t[None], (n_blk, SUBLANES, LANES)).reshape(rows, LANES)
    sub_row = lax.broadcasted_iota(jnp.int32, (SUBLANES, LANES), 0)
    local_steps = SUBLANES.bit_length() - 1
    for d, down in ((0, True), (1, False)):
        re, im = state[d]
        for k in range(local_steps):
            s = 1 << k
            keep = sub_row >= s if down else sub_row < SUBLANES - s
            ar, ai = (tile(jnp.where(keep, m, 0.0)) for m in mult(k, k + 1, d))
            shift = s if down else rows - s
            re, im = _complex_step(re, im, pltpu.roll(re, shift, 0), pltpu.roll(im, shift, 0), ar, ai)
        loc_scr[2 * d], loc_scr[2 * d + 1] = re, im
        state[d] = [re, im]

        end_row = SUBLANES - 1 if down else 0
        pad = jnp.zeros((n_blk_pad - n_blk, LANES), F32)
        ere = jnp.concatenate([loc_scr[2 * d, pl.ds(end_row, n_blk, stride=SUBLANES), :], pad], axis=0)
        eim = jnp.concatenate([loc_scr[2 * d + 1, pl.ds(end_row, n_blk, stride=SUBLANES), :], pad], axis=0)
        for k in range(local_steps, n_steps):
            ar, ai = mult(k, k + 1, d)
            s = 1 << (k - local_steps)
            ere, eim = _complex_step(ere, eim, shifted(ere, s, blk, n_blk_pad, down),
                                     shifted(eim, s, blk, n_blk_pad, down), ar, ai)
        for c, ends in enumerate((ere, eim)):
            entering = shifted(ends, 1, blk, n_blk_pad, down)[:n_blk]
            for r in range(SUBLANES):
                car_scr[2 * d + c, pl.ds(r, n_blk, stride=SUBLANES), :] = entering

    for d in range(2):
        tre, tim = mult(POWER_ROW0, POWER_ROW0 + SUBLANES, d)
        re, im = state[d]
        state[d] = list(_complex_step(re, im, car_scr[2 * d], car_scr[2 * d + 1], tile(tre), tile(tim)))

    prev = lambda a: pltpu.roll(a, 1, 0)[n_ctx_rows:]
    nxt = lambda a: pltpu.roll(a, rows - 1, 0)[:n_lat_rows]
    st = jnp.concatenate([prev(state[0][0]), prev(state[0][1]), nxt(state[1][0]), nxt(state[1][1])],
                         axis=1).astype(BF16)
    y = jnp.dot(st, cout_ref[0], preferred_element_type=F32)
    y_o[0, 0] = (y[:, :CHUNK_LANES]
                 + jnp.dot(u0[n_ctx_rows:], m_ref[0], preferred_element_type=F32)).astype(BF16)
    y_o[0, 1] = (y[:, CHUNK_LANES:]
                 + jnp.dot(u1[n_ctx_rows:], m_ref[1], preferred_element_type=F32)).astype(BF16)


def _ssm(u_cf, win_pair, m, cout_pair, mult, n_ctx_rows, n_steps):
    bsz, g, rows, _ = u_cf.shape
    assert rows % SUBLANES == 0 and n_ctx_rows % 16 == 0
    pair_spec = lambda a: pl.BlockSpec((1,) + a.shape[1:], lambda b, i: (i, 0, 0))
    return pl.pallas_call(
        functools.partial(_ssm_kernel, n_ctx_rows=n_ctx_rows, n_steps=n_steps),
        grid=(bsz, g // 2),
        in_specs=[pl.BlockSpec((1, 2, rows, CHUNK_LANES), lambda b, i: (b, i, 0, 0)),
                  pair_spec(win_pair),
                  pl.BlockSpec((2, CHUNK_LANES, CHUNK_LANES), lambda b, i: (i, 0, 0)),
                  pair_spec(cout_pair), pair_spec(mult)],
        out_specs=pl.BlockSpec((1, 2, rows - n_ctx_rows, CHUNK_LANES), lambda b, i: (b, i, 0, 0)),
        out_shape=jax.ShapeDtypeStruct((bsz, g, rows - n_ctx_rows, CHUNK_LANES), BF16),
        scratch_shapes=[pltpu.VMEM((4, rows, LANES), F32), pltpu.VMEM((4, rows, LANES), F32)],
        compiler_params=_cparams("parallel", "parallel"),
        name="ssm",
    )(u_cf, win_pair, m, cout_pair, mult)


def _final_kernel(x_ref, at_ref, ga_ref, y_ref, gs_ref, gate_ref, wglu_ref, bglu_ref, wout_ref,
                  pg_ref, o_ref, y_scr):
    _from_chunk_lanes(y_ref, y_scr, 0, y_ref.shape[2])
    y = jax.nn.gelu(jnp.concatenate([y_scr[col] for col in range(SSM_WIDTH // LANES)], axis=1))
    t = jnp.dot(y.astype(BF16), wglu_ref[...], preferred_element_type=F32) + bglu_ref[...]
    s = (y * jax.nn.sigmoid(t) * gs_ref[0].astype(F32)).astype(BF16)
    a = at_ref[0] * ga_ref[0]
    out = jnp.dot(jnp.concatenate([a, s], axis=1), wout_ref[...], preferred_element_type=F32)
    ms = jnp.mean(out * out, axis=-1, keepdims=True)
    o_ref[0] = x_ref[0] + out * lax.rsqrt(ms + NORM_EPS) * (gate_ref[0] * pg_ref[...])


def _final(x, attn, ga, y_cf, gs, gate, wglu_bf, b_glu, wout_bf, post_g, tm):
    bsz, n, d = x.shape
    assert n % tm == 0 and tm % (CHUNK * 16) == 0
    return pl.pallas_call(
        _final_kernel,
        grid=(bsz, n // tm),
        in_specs=[_row_spec(tm, d), _row_spec(tm, ATTN_WIDTH), _row_spec(tm, ATTN_WIDTH),
                  pl.BlockSpec((1, N_SSM_GROUPS, tm // CHUNK, CHUNK_LANES), lambda b, i: (b, 0, i, 0)),
                  _row_spec(tm, SSM_WIDTH), _batch_vec_spec(d),
                  _const_spec((SSM_WIDTH, SSM_WIDTH)), _const_spec((1, SSM_WIDTH)),
                  _const_spec((d, d)), _const_spec((1, d))],
        out_specs=_row_spec(tm, d),
        out_shape=jax.ShapeDtypeStruct((bsz, n, d), F32),
        scratch_shapes=[pltpu.VMEM((SSM_WIDTH // LANES, tm, LANES), F32)],
        compiler_params=_cparams("parallel", "parallel"),
        name="final",
    )(x, attn, ga, y_cf, gs, gate, wglu_bf, b_glu, wout_bf, post_g)


def _rope_tables(n_ctx, n_lat):
    t = jnp.arange(n_lat, dtype=jnp.int32)
    row_pos = (t // GRID_W).astype(F32)
    col_pos = (t % GRID_W).astype(F32)
    inv_freq = ROPE_THETA ** (-jnp.arange(ROPE_FREQS, dtype=F32) / ROPE_FREQS)
    ang_r = row_pos[:, None] * inv_freq
    ang_c = col_pos[:, None] * inv_freq
    cos = jnp.concatenate([jnp.cos(ang_r)] * 2 + [jnp.cos(ang_c)] * 2, axis=1)
    sin = jnp.concatenate([-jnp.sin(ang_r), jnp.sin(ang_r), -jnp.sin(ang_c), jnp.sin(ang_c)], axis=1)
    cos = jnp.concatenate([jnp.ones((n_ctx, HEAD_DIM), F32), cos], axis=0)
    sin = jnp.concatenate([jnp.zeros((n_ctx, HEAD_DIM), F32), sin], axis=0)
    return jnp.tile(cos, (1, LANES // HEAD_DIM)), jnp.tile(sin, (1, LANES // HEAD_DIM))


def _layer(x, ctx, c, c_ctx, w_ada, b_ada, pre_g, post_g, w_in, q_g, k_g, lam_re, lam_im, log_dt,
           b_re, b_im, c_re, c_im, d_skip, w_glu, b_glu, w_out):
    bsz, n_lat, d = x.shape
    n_ctx = ctx.shape[1]
    assert n_lat % GRID_W == 0

    cvecs = jnp.zeros((8, d), F32).at[:bsz].set(c).at[bsz].set(c_ctx)
    mod = _adaln(cvecs, w_ada, b_ada)
    shift, scale, gate = (mod[:, i * d:(i + 1) * d] for i in range(3))
    lat = lambda a: a[:bsz].reshape(bsz, 1, d)
    cvec = lambda a: a[bsz].reshape(1, 1, d)

    head = jnp.arange(ATTN_WIDTH) // HEAD_DIM
    bd = (head[:, None] == head[None, :]).astype(BF16) * (1.0 / HEAD_DIM)
    qg = jnp.tile(q_g, N_Q_HEADS).reshape(1, ATTN_WIDTH)
    kg = jnp.tile(k_g, N_KV_HEADS).reshape(1, KV_WIDTH)
    cos, sin = _rope_tables(n_ctx, n_lat)

    qt, k_all, vt_aug, ga, u_cf, gs = _inproj(x, ctx, lat(shift), lat(scale), cvec(shift), cvec(scale),
                                              pre_g.reshape(1, d), w_in.astype(BF16), qg, kg, cos, sin,
                                              bd, tm=256)
    attn = _attention(qt, k_all, vt_aug, tq=256, tk=8448, sub=256, lookahead=5, heads_per_pass=8)

    rows = (n_ctx + n_lat) // CHUNK
    n_steps = _scan_steps(rows)
    win_pair, m_op, cout_pair, mult = _ssm_prep(lam_re, lam_im, log_dt, b_re, b_im, c_re, c_im, d_skip,
                                                n_steps)
    y_cf = _ssm(u_cf, win_pair, m_op, cout_pair, mult, n_ctx // CHUNK, n_steps)

    return _final(x, attn, ga, y_cf, gs, lat(gate), w_glu.astype(BF16), b_glu.reshape(1, SSM_WIDTH),
                  w_out.astype(BF16), post_g.reshape(1, d), tm=512)


def kernel(x, c, ctx, c_ctx, w_ada, b_ada, pre_norm, post_norm, w_in, q_norm, k_norm, ssm_lam_re,
           ssm_lam_im, ssm_log_dt, ssm_b_re, ssm_b_im, ssm_c_re, ssm_c_im, ssm_d, w_glu, b_glu, w_out):
    depth = w_ada.shape[0]
    assert depth == 1, "context stream update between layers is not implemented"
    return _layer(x, ctx, c, c_ctx, w_ada[0], b_ada[0], pre_norm[0], post_norm[0], w_in[0], q_norm[0],
                  k_norm[0], ssm_lam_re[0], ssm_lam_im[0], ssm_log_dt[0], ssm_b_re[0], ssm_b_im[0],
                  ssm_c_re[0], ssm_c_im[0], ssm_d[0], w_glu[0], b_glu[0], w_out[0])
```

```python
import functools
import math

import jax
import jax.numpy as jnp
from jax import lax
from jax.experimental import pallas as pl
from jax.experimental.pallas import tpu as pltpu

F32 = jnp.float32
BF16 = jnp.bfloat16

D_MODEL = 1024
HEAD_DIM = 64
N_Q_HEADS = 8
N_KV_HEADS = 2
ATTN_WIDTH = N_Q_HEADS * HEAD_DIM
KV_WIDTH = N_KV_HEADS * HEAD_DIM
SSM_WIDTH = 512
SSM_GROUP = 16
N_SSM_GROUPS = SSM_WIDTH // SSM_GROUP
SSM_STATE = 64
GRID_W = 64
ROPE_THETA = 10000.0
ROPE_FREQS = 16
NORM_EPS = 1e-6
ATTN_SCALE = HEAD_DIM ** -0.5
LOG2_E = math.log2(math.e)
Q_END = ATTN_WIDTH
K_END = Q_END + KV_WIDTH
V_END = K_END + KV_WIDTH
GA_END = V_END + ATTN_WIDTH
U_END = GA_END + SSM_WIDTH
IN_WIDTH = U_END + SSM_WIDTH

CHUNK = 16
CHUNK_LANES = CHUNK * SSM_GROUP
LANES = 128
GROUPS_PER_VREG = LANES // SSM_GROUP
STATE_LANES = 2 * SSM_STATE
SUBLANES = 8
POWER_ROW0 = 16
V_ROWS = HEAD_DIM + 16
VMEM_LIMIT = 48 * 1024 * 1024
NEG_BIG = -1e30


def _cparams(*sem):
    return pltpu.CompilerParams(dimension_semantics=sem, vmem_limit_bytes=VMEM_LIMIT)


def _row_spec(tm, width):
    return pl.BlockSpec((1, tm, width), lambda b, i: (b, i, 0))


def _const_spec(shape):
    return pl.BlockSpec(shape, lambda b, i: (0,) * len(shape))


def _batch_vec_spec(width):
    return pl.BlockSpec((1, 1, width), lambda b, i: (b, 0, 0))


def _adaln_kernel(c_ref, w_ref, b_ref, o_ref):
    c = c_ref[...]
    s = c * jax.nn.sigmoid(c)
    o_ref[...] = jnp.dot(s, w_ref[...], preferred_element_type=F32,
                         precision=lax.Precision.HIGHEST) + b_ref[...]


def _adaln(cvecs, w_ada, b_ada):
    rows, d = cvecs.shape
    n = w_ada.shape[1]
    tn = 512
    return pl.pallas_call(
        _adaln_kernel,
        grid=(n // tn,),
        in_specs=[pl.BlockSpec((rows, d), lambda j: (0, 0)),
                  pl.BlockSpec((d, tn), lambda j: (0, j)),
                  pl.BlockSpec((1, tn), lambda j: (0, j))],
        out_specs=pl.BlockSpec((rows, tn), lambda j: (0, j)),
        out_shape=jax.ShapeDtypeStruct((rows, n), F32),
        compiler_params=_cparams("arbitrary"),
        name="adaln",
    )(cvecs, w_ada, b_ada.reshape(1, n))


def _lane_group(rows):
    return lax.broadcasted_iota(jnp.int32, (rows, LANES), 1) // SSM_GROUP


def _to_chunk_lanes(u_scr, u_o):
    rt = u_o.shape[2]
    grp = _lane_group(rt)
    for col in range(SSM_WIDTH // LANES):
        rolled = []
        for step in range(CHUNK):
            s = u_scr[col, pl.ds(step, rt, stride=CHUNK), :]
            rolled.append([s if k == 0 else pltpu.roll(s, k * SSM_GROUP, 1)
                           for k in range(GROUPS_PER_VREG)])
        for g_lo in range(GROUPS_PER_VREG):
            for half in range(CHUNK // GROUPS_PER_VREG):
                out = None
                for s8 in range(GROUPS_PER_VREG):
                    piece = rolled[half * GROUPS_PER_VREG + s8][(s8 - g_lo) % GROUPS_PER_VREG]
                    out = piece if out is None else jnp.where(grp == s8, piece, out)
                u_o[0, col * GROUPS_PER_VREG + g_lo, :, half * LANES:(half + 1) * LANES] = out.astype(BF16)


def _from_chunk_lanes(y_ref, y_scr, r0, rt):
    grp = _lane_group(rt)
    for col in range(SSM_WIDTH // LANES):
        rolled = {}
        for g_lo in range(GROUPS_PER_VREG):
            for half in range(CHUNK // GROUPS_PER_VREG):
                s = y_ref[0, col * GROUPS_PER_VREG + g_lo, r0:r0 + rt,
                          half * LANES:(half + 1) * LANES].astype(F32)
                rolled[g_lo, half] = [s if k == 0 else pltpu.roll(s, k * SSM_GROUP, 1)
                                      for k in range(GROUPS_PER_VREG)]
        for step in range(CHUNK):
            half, s8 = divmod(step, GROUPS_PER_VREG)
            out = None
            for g_lo in range(GROUPS_PER_VREG):
                piece = rolled[g_lo, half][(g_lo - s8) % GROUPS_PER_VREG]
                out = piece if out is None else jnp.where(grp == g_lo, piece, out)
            y_scr[col, pl.ds(r0 * CHUNK + step, rt, stride=CHUNK), :] = out


def _head_mean_sq(z, bd):
    return jnp.dot((z * z).astype(BF16), bd, preferred_element_type=F32)


def _swap16(x):
    w = x.shape[1]
    lane = lax.broadcasted_iota(jnp.int32, x.shape, 1)
    return jnp.where((lane & 16) == 0, pltpu.roll(x, w - 16, 1), pltpu.roll(x, 16, 1))


def _rope(x, cos, sin_signed):
    cols = []
    for c in range(x.shape[1] // LANES):
        xc = x[:, c * LANES:(c + 1) * LANES]
        cols.append(xc * cos + _swap16(xc) * sin_signed)
    return cols[0] if len(cols) == 1 else jnp.concatenate(cols, axis=1)


def _silu(z):
    return z * jax.nn.sigmoid(z)


def _inproj_kernel(x_ref, c_ref, shl_ref, scl_ref, shc_ref, scc_ref, pg_ref, w_ref, qg_ref, kg_ref,
                   cos_ref, sin_ref, bd_ref, qt_o, k_o, vt_o, ga_o, u_o, gs_o, v_scr, u_scr, *, n_ctx_tiles):
    is_ctx = pl.program_id(1) < n_ctx_tiles
    x = jnp.where(is_ctx, c_ref[0], x_ref[0])
    shift = jnp.where(is_ctx, shc_ref[0], shl_ref[0])
    scale = jnp.where(is_ctx, scc_ref[0], scl_ref[0])
    ms = jnp.mean(x * x, axis=-1, keepdims=True)
    xn = x * lax.rsqrt(ms + NORM_EPS) * pg_ref[...]
    h = (xn * (1.0 + scale) + shift).astype(BF16)

    def proj(a, b):
        return jnp.dot(h, w_ref[:, a:b], preferred_element_type=F32)

    u = proj(GA_END, U_END)
    for col in range(SSM_WIDTH // LANES):
        u_scr[col] = u[:, col * LANES:(col + 1) * LANES]
    _to_chunk_lanes(u_scr, u_o)

    cos = cos_ref[...]
    sin = sin_ref[...]
    zq = proj(0, Q_END)
    zk = proj(Q_END, K_END)
    gs_o[0] = _silu(proj(U_END, IN_WIDTH)).astype(BF16)
    qn = zq * lax.rsqrt(_head_mean_sq(zq, bd_ref[...]) + NORM_EPS) * qg_ref[...]
    kn = zk * lax.rsqrt(_head_mean_sq(zk, bd_ref[:KV_WIDTH, :KV_WIDTH]) + NORM_EPS) * kg_ref[...]
    ga_o[0] = _silu(proj(V_END, GA_END)).astype(BF16)
    v_scr[...] = proj(K_END, V_END)

    k_o[0] = _rope(kn, cos, sin).astype(BF16)
    q = _rope(qn, cos, sin) * (ATTN_SCALE * LOG2_E)
    group = N_Q_HEADS // N_KV_HEADS
    none = jnp.zeros((HEAD_DIM, q.shape[0]), BF16)
    for c in range(ATTN_WIDTH // LANES):
        qt = q[:, c * LANES:(c + 1) * LANES].T.astype(BF16)
        for par in range(2):
            head = 2 * c + par
            kv = head // group
            qt_o[0, head, kv * HEAD_DIM:(kv + 1) * HEAD_DIM, :] = qt[par * HEAD_DIM:(par + 1) * HEAD_DIM]
            qt_o[0, head, (1 - kv) * HEAD_DIM:(2 - kv) * HEAD_DIM, :] = none

    vt = v_scr[...].T.astype(BF16)
    tm = vt.shape[1]
    ones_row = lax.broadcasted_iota(jnp.int32, (V_ROWS - HEAD_DIM, tm), 0) == 0
    for j in range(N_KV_HEADS):
        vt_o[0, j, :HEAD_DIM, :] = vt[j * HEAD_DIM:(j + 1) * HEAD_DIM]
        vt_o[0, j, HEAD_DIM:, :] = jnp.where(ones_row, 1.0, 0.0).astype(BF16)


def _inproj(x, ctx, shift_l, scale_l, shift_c, scale_c, pre_g, w_bf, qg, kg, cos, sin, bd, tm):
    bsz, n_lat, d = x.shape
    n_ctx = ctx.shape[1]
    assert n_ctx % tm == 0 and n_lat % tm == 0 and tm % (CHUNK * 16) == 0
    nct = n_ctx // tm
    n_tot = n_ctx + n_lat
    lat_rows = lambda w: pl.BlockSpec((1, tm, w), lambda b, i: (b, jnp.maximum(i - nct, 0), 0))
    all_rows = lambda w: pl.BlockSpec((1, tm, w), lambda b, i: (b, i, 0))
    ctx_vec = pl.BlockSpec((1, 1, d), lambda b, i: (0, 0, 0))
    table = pl.BlockSpec((tm, LANES), lambda b, i: (i, 0))
    return pl.pallas_call(
        functools.partial(_inproj_kernel, n_ctx_tiles=nct),
        grid=(bsz, n_tot // tm),
        in_specs=[lat_rows(d),
                  pl.BlockSpec((1, tm, d), lambda b, i: (b, jnp.minimum(i, nct - 1), 0)),
                  _batch_vec_spec(d), _batch_vec_spec(d), ctx_vec, ctx_vec, _const_spec((1, d)),
                  _const_spec((d, IN_WIDTH)), _const_spec((1, ATTN_WIDTH)), _const_spec((1, KV_WIDTH)),
                  table, table, _const_spec((ATTN_WIDTH, ATTN_WIDTH))],
        out_specs=[pl.BlockSpec((1, N_Q_HEADS, LANES, tm), lambda b, i: (b, 0, 0, jnp.maximum(i - nct, 0))),
                   all_rows(KV_WIDTH),
                   pl.BlockSpec((1, N_KV_HEADS, V_ROWS, tm), lambda b, i: (b, 0, 0, i)),
                   lat_rows(ATTN_WIDTH),
                   pl.BlockSpec((1, N_SSM_GROUPS, tm // CHUNK, CHUNK_LANES), lambda b, i: (b, 0, i, 0)),
                   lat_rows(SSM_WIDTH)],
        out_shape=[jax.ShapeDtypeStruct((bsz, N_Q_HEADS, LANES, n_lat), BF16),
                   jax.ShapeDtypeStruct((bsz, n_tot, KV_WIDTH), BF16),
                   jax.ShapeDtypeStruct((bsz, N_KV_HEADS, V_ROWS, n_tot), BF16),
                   jax.ShapeDtypeStruct((bsz, n_lat, ATTN_WIDTH), BF16),
                   jax.ShapeDtypeStruct((bsz, N_SSM_GROUPS, n_tot // CHUNK, CHUNK_LANES), BF16),
                   jax.ShapeDtypeStruct((bsz, n_lat, SSM_WIDTH), BF16)],
        scratch_shapes=[pltpu.VMEM((tm, KV_WIDTH), F32), pltpu.VMEM((SSM_WIDTH // LANES, tm, LANES), F32)],
        compiler_params=_cparams("parallel", "arbitrary"),
        name="inproj",
    )(x, ctx, shift_l, scale_l, shift_c, scale_c, pre_g, w_bf, qg, kg, cos, sin, bd)


def _attn_kernel(qt_ref, k_ref, vt_ref, o_ref, *, tk, sub, lookahead, heads_per_pass):
    tq = qt_ref.shape[3]
    n_keys = k_ref.shape[1]
    group = N_Q_HEADS // N_KV_HEADS

    outs = []
    for h0 in range(0, N_Q_HEADS, heads_per_pass):
        heads = list(range(h0, h0 + heads_per_pass))
        q_wide = [qt_ref[0, h] for h in heads]

        def body(t, carry, heads=heads, q_wide=q_wide):
            tasks = [(j, i) for j in range(tk // sub) for i in range(len(heads))]
            state = list(carry)
            scores = {}

            def keys_at(j):
                return pl.ds(pl.multiple_of(t * tk + j * sub, sub), sub)

            def issue(n):
                j, i = tasks[n]
                scores[n] = jnp.dot(k_ref[0, keys_at(j), :], q_wide[i],
                                    preferred_element_type=F32)

            def consume(n):
                j, i = tasks[n]
                s = scores.pop(n)
                m_old, acc = state[i]
                vt = vt_ref[0, heads[i] // group, :, keys_at(j)]
                m_new = jnp.maximum(m_old, jnp.max(s, axis=0, keepdims=True))
                p = jnp.exp2((s - m_new).astype(BF16))
                acc = jnp.exp2(m_old - m_new) * acc + jnp.dot(vt, p, preferred_element_type=F32)
                state[i] = (m_new, acc)

            for n in range(len(tasks) + lookahead):
                if n < len(tasks):
                    issue(n)
                if n >= lookahead:
                    consume(n - lookahead)
            return tuple(state)

        init = tuple((jnp.full((1, tq), NEG_BIG, F32), jnp.zeros((V_ROWS, tq), F32)) for _ in heads)
        for _, acc in lax.fori_loop(0, n_keys // tk, body, init):
            outs.append(acc[:HEAD_DIM] / acc[HEAD_DIM:HEAD_DIM + 1])

    for c in range(N_Q_HEADS // 2):
        pair = jnp.concatenate([outs[2 * c], outs[2 * c + 1]], axis=0)
        o_ref[0, :, c * LANES:(c + 1) * LANES] = pair.T.astype(BF16)


def _attention(qt, k_all, vt_aug, tq, tk, sub, lookahead, heads_per_pass):
    bsz, _, _, n = qt.shape
    n_keys = k_all.shape[1]
    assert n_keys % tk == 0 and tk % sub == 0 and n % tq == 0
    return pl.pallas_call(
        functools.partial(_attn_kernel, tk=tk, sub=sub, lookahead=lookahead,
                          heads_per_pass=heads_per_pass),
        grid=(bsz, n // tq),
        in_specs=[pl.BlockSpec((1, N_Q_HEADS, LANES, tq), lambda b, i: (b, 0, 0, i)),
                  pl.BlockSpec((1, n_keys, KV_WIDTH), lambda b, i: (b, 0, 0)),
                  pl.BlockSpec((1, N_KV_HEADS, V_ROWS, n_keys), lambda b, i: (b, 0, 0, 0))],
        out_specs=_row_spec(tq, ATTN_WIDTH),
        out_shape=jax.ShapeDtypeStruct((bsz, n, ATTN_WIDTH), BF16),
        compiler_params=_cparams("parallel", "parallel"),
        name="attention",
    )(qt, k_all, vt_aug)


def _scan_steps(n_rows):
    return max(1, math.ceil(math.log2(n_rows)))


def _ssm_prep_kernel(lre_ref, lim_ref, ldt_ref, bt_ref, bts_ref, c_ref, cs_ref, d_ref,
                     win_o, m_o, cout_o, mult_o, *, n_steps):
    lane = lax.broadcasted_iota(jnp.int32, (1, STATE_LANES), 1)
    low = lane < SSM_STATE
    sign_lo = jnp.where(low, -1.0, 1.0)
    k_idx = lax.broadcasted_iota(jnp.int32, (CHUNK, STATE_LANES), 0).astype(F32)
    step_of_row = lax.broadcasted_iota(jnp.int32, (CHUNK_LANES, STATE_LANES), 0) // SSM_GROUP
    nt = (((1,), (1,)), ((), ()))

    def outer(pw, pws, mat, mats):
        full = pw[:, None, :] * mat[None, :, :] + pws[:, None, :] * mats[None, :, :]
        return full.reshape(CHUNK * mat.shape[0], STATE_LANES)

    state_in, state_out, mults, tables = {}, {}, {}, {}
    for gi in range(2):
        taps_lo, taps_hi = None, None
        for d in range(2):
            lre = lre_ref[gi, d:d + 1, :]
            lim = lim_ref[gi, d:d + 1, :]
            dt = jnp.exp(ldt_ref[gi, d:d + 1, :])

            def power(expo, lre=lre, lim=lim, dt=dt):
                mag = jnp.exp(expo * dt * lre)
                ang = expo * dt * lim
                return mag * jnp.cos(ang), mag * jnp.sin(ang) * sign_lo

            a_re, a_ims = power(jnp.ones((1, STATE_LANES), F32))
            a_im = a_ims * sign_lo
            nr, ni = a_re - 1.0, a_im
            den = lre * lre + lim * lim
            cr = (nr * lre + ni * lim) / den
            ci = (ni * lre - nr * lim) / den
            cis = ci * sign_lo
            bbar = cr * bt_ref[gi, d] + cis * bts_ref[gi, d]
            bbar_s = cr * bts_ref[gi, d] - cis * bt_ref[gi, d]
            cmat, cmat_s = c_ref[gi, d], cs_ref[gi, d]

            asc, ascs = power(k_idx)
            desc, descs = power(CHUNK - 1.0 - k_idx)
            nxt, nxts = power(k_idx + 1.0)
            conj_c = lambda pw, pws: outer(pw, pws, cmat, cmat_s) * (-sign_lo)
            if d == 0:
                state_in[gi, d] = outer(desc, descs, bbar, bbar_s)
                state_out[gi, d] = conj_c(nxt, nxts)
                lag0 = jnp.where(step_of_row == CHUNK - 1, conj_c(desc, descs), 0.0)
                lags = jnp.where(step_of_row < CHUNK - 1, conj_c(nxt, nxts), 0.0)
                hi = lax.dot_general(bbar, lags, nt, preferred_element_type=F32,
                                     precision=lax.Precision.HIGHEST)
                lo = lax.dot_general(bbar, lag0, nt, preferred_element_type=F32,
                                     precision=lax.Precision.HIGHEST)
                taps_hi = hi
                taps_lo = lo if taps_lo is None else taps_lo + lo
            else:
                state_in[gi, d] = outer(asc, ascs, bbar, bbar_s)
                rev, revs = power(CHUNK - k_idx)
                state_out[gi, d] = conj_c(rev, revs)
                lo = lax.dot_general(bbar, conj_c(desc, descs), nt, preferred_element_type=F32,
                                     precision=lax.Precision.HIGHEST)
                taps_lo = lo if taps_lo is None else taps_lo + lo

            sr, sis = power(jnp.full((1, STATE_LANES), float(CHUNK), F32))
            chain = []
            for k in range(n_steps):
                chain.append((sr, sis))
                si = sis * sign_lo
                sr, sis = sr * sr - si * si, 2.0 * sr * si * sign_lo
            mults[gi, d] = chain
            r_idx = lax.broadcasted_iota(jnp.int32, (SUBLANES, STATE_LANES), 0).astype(F32)
            tables[gi, d] = power(CHUNK * (r_idx + 1.0) if d == 0 else CHUNK * (SUBLANES - r_idx))

        qq = lax.broadcasted_iota(jnp.int32, (SSM_GROUP, CHUNK_LANES), 0)
        ll = lax.broadcasted_iota(jnp.int32, (SSM_GROUP, CHUNK_LANES), 1)
        taps_lo = taps_lo + jnp.where(ll == qq + (CHUNK - 1) * SSM_GROUP, d_ref[gi], 0.0)
        taps = jnp.concatenate([taps_lo, taps_hi], axis=1)
        for j in range(CHUNK):
            off = (CHUNK - 1 - j) * SSM_GROUP
            m_o[gi, j * SSM_GROUP:(j + 1) * SSM_GROUP, :] = taps[:, off:off + CHUNK_LANES].astype(BF16)

    def swap(x):
        return pltpu.roll(x, SSM_STATE, 1)

    mult_o[...] = jnp.zeros(mult_o.shape, F32)
    wide_low = lax.broadcasted_iota(jnp.int32, (CHUNK_LANES, STATE_LANES), 1) < SSM_STATE
    zeros_c = jnp.zeros((SSM_STATE, CHUNK_LANES), F32)
    for d in range(2):
        s0, s1 = state_in[0, d], state_in[1, d]
        re_blk = jnp.concatenate([jnp.where(wide_low, s0, 0.0), jnp.where(wide_low, 0.0, swap(s1))], axis=0)
        im_blk = jnp.concatenate([jnp.where(wide_low, swap(s0), 0.0), jnp.where(wide_low, 0.0, s1)], axis=0)
        win_o[0, :, (2 * d) * LANES:(2 * d + 1) * LANES] = re_blk.astype(BF16)
        win_o[0, :, (2 * d + 1) * LANES:(2 * d + 2) * LANES] = im_blk.astype(BF16)
        t0, t1 = state_out[0, d].T, state_out[1, d].T
        for part in range(2):
            rows = slice(part * SSM_STATE, (part + 1) * SSM_STATE)
            blk = jnp.concatenate([jnp.concatenate([t0[rows], zeros_c], axis=1),
                                   jnp.concatenate([zeros_c, t1[rows]], axis=1)], axis=0)
            cout_o[0, (2 * d + part) * LANES:(2 * d + part + 1) * LANES, :] = blk.astype(BF16)
        for k in range(n_steps):
            (sr0, sis0), (sr1, sis1) = mults[0, d][k], mults[1, d][k]
            mult_o[0, k:k + 1, (2 * d) * LANES:(2 * d + 1) * LANES] = jnp.where(low, sr0, sr1)
            mult_o[0, k:k + 1, (2 * d + 1) * LANES:(2 * d + 2) * LANES] = jnp.where(low, -sis0, sis1)
        (tr0, tis0), (tr1, tis1) = tables[0, d], tables[1, d]
        mult_o[0, POWER_ROW0:POWER_ROW0 + SUBLANES, (2 * d) * LANES:(2 * d + 1) * LANES] = jnp.where(low, tr0, tr1)
        mult_o[0, POWER_ROW0:POWER_ROW0 + SUBLANES, (2 * d + 1) * LANES:(2 * d + 2) * LANES] = jnp.where(
            low, -tis0, tis1)


def _ssm_prep(lam_re, lam_im, log_dt, b_re, b_im, c_re, c_im, d_skip, n_steps):
    g = N_SSM_GROUPS
    dup = lambda a: jnp.concatenate([a, a], axis=-1)
    lre = dup(jnp.swapaxes(lam_re, 0, 1))
    lim = dup(jnp.swapaxes(lam_im, 0, 1))
    ldt = jnp.broadcast_to(jnp.swapaxes(log_dt, 0, 1)[..., None], (g, 2, STATE_LANES))
    btr = jnp.transpose(b_re, (1, 0, 3, 2))
    bti = jnp.transpose(b_im, (1, 0, 3, 2))
    bt, bts = jnp.concatenate([btr, bti], -1), jnp.concatenate([bti, btr], -1)
    cr, ci = jnp.swapaxes(c_re, 0, 1), jnp.swapaxes(c_im, 0, 1)
    cm, cms = jnp.concatenate([cr, ci], -1), jnp.concatenate([ci, cr], -1)
    dsk = jnp.pad(d_skip, ((0, 0), (CHUNK_LANES - SSM_GROUP, 0))).reshape(g, 1, CHUNK_LANES)
    n_pad = POWER_ROW0 + SUBLANES
    vec = pl.BlockSpec((2, 2, STATE_LANES), lambda i: (i, 0, 0))
    mat = pl.BlockSpec((2, 2, SSM_GROUP, STATE_LANES), lambda i: (i, 0, 0, 0))
    pair_sq = pl.BlockSpec((1, 2 * CHUNK_LANES, 4 * LANES), lambda i: (i, 0, 0))
    return pl.pallas_call(
        functools.partial(_ssm_prep_kernel, n_steps=n_steps),
        grid=(g // 2,),
        in_specs=[vec, vec, vec, mat, mat, mat, mat,
                  pl.BlockSpec((2, 1, CHUNK_LANES), lambda i: (i, 0, 0))],
        out_specs=[pair_sq, pl.BlockSpec((2, CHUNK_LANES, CHUNK_LANES), lambda i: (i, 0, 0)), pair_sq,
                   pl.BlockSpec((1, n_pad, 4 * LANES), lambda i: (i, 0, 0))],
        out_shape=[jax.ShapeDtypeStruct((g // 2, 2 * CHUNK_LANES, 4 * LANES), BF16),
                   jax.ShapeDtypeStruct((g, CHUNK_LANES, CHUNK_LANES), BF16),
                   jax.ShapeDtypeStruct((g // 2, 4 * LANES, 2 * CHUNK_LANES), BF16),
                   jax.ShapeDtypeStruct((g // 2, n_pad, 4 * LANES), F32)],
        compiler_params=_cparams("parallel"),
        name="ssm_prep",
    )(lre, lim, ldt, bt, bts, cm, cms, dsk)


def _complex_step(re, im, sr, si, ar, ai):
    return re + ar * sr - ai * si, im + ar * si + ai * sr


def _ssm_kernel(u_ref, win_ref, m_ref, cout_ref, mult_ref, y_o, loc_scr, car_scr, *, n_ctx_rows, n_steps):
    u0, u1 = u_ref[0, 0], u_ref[0, 1]
    rows = u0.shape[0]
    n_lat_rows = rows - n_ctx_rows
    n_blk = rows // SUBLANES
    n_blk_pad = -(-n_blk // SUBLANES) * SUBLANES
    x = jnp.dot(jnp.concatenate([u0, u1], axis=1), win_ref[0], preferred_element_type=F32)
    to_rev = lambda a: jnp.concatenate([a[n_ctx_rows:], a[:n_ctx_rows]], axis=0)
    state = [[x[:, :LANES], x[:, LANES:2 * LANES]],
             [to_rev(x[:, 2 * LANES:3 * LANES]), to_rev(x[:, 3 * LANES:])]]
    in_blk = lax.broadcasted_iota(jnp.int32, (rows, LANES), 0) & (SUBLANES - 1)
    blk = lax.broadcasted_iota(jnp.int32, (n_blk_pad, LANES), 0)

    def mult(k0, k1, d):
        return (mult_ref[0, k0:k1, 2 * d * LANES:(2 * d + 1) * LANES],
                mult_ref[0, k0:k1, (2 * d + 1) * LANES:(2 * d + 2) * LANES])

    def shifted(a, s, idx, n, down):
        if down:
            return jnp.where(idx >= s, pltpu.roll(a, s, 0), 0.0)
        return jnp.where(idx < n - s, pltpu.roll(a, a.shape[0] - s, 0), 0.0)

    tile = lambda t: jnp.broadcast_to(t[None], (n_blk, SUBLANES, LANES)).reshape(rows, LANES)
    sub_row = lax.broadcasted_iota(jnp.int32, (SUBLANES, LANES), 0)
    local_steps = SUBLANES.bit_length() - 1
    for d, down in ((0, True), (1, False)):
        re, im = state[d]
        for k in range(local_steps):
            s = 1 << k
            keep = sub_row >= s if down else sub_row < SUBLANES - s
            ar, ai = (tile(jnp.where(keep, m, 0.0)) for m in mult(k, k + 1, d))
            shift = s if down else rows - s
            re, im = _complex_step(re, im, pltpu.roll(re, shift, 0), pltpu.roll(im, shift, 0), ar, ai)
        loc_scr[2 * d], loc_scr[2 * d + 1] = re, im
        state[d] = [re, im]

        end_row = SUBLANES - 1 if down else 0
        pad = jnp.zeros((n_blk_pad - n_blk, LANES), F32)
        ere = jnp.concatenate([loc_scr[2 * d, pl.ds(end_row, n_blk, stride=SUBLANES), :], pad], axis=0)
        eim = jnp.concatenate([loc_scr[2 * d + 1, pl.ds(end_row, n_blk, stride=SUBLANES), :], pad], axis=0)
        for k in range(local_steps, n_steps):
            ar, ai = mult(k, k + 1, d)
            s = 1 << (k - local_steps)
            ere, eim = _complex_step(ere, eim, shifted(ere, s, blk, n_blk_pad, down),
                                     shifted(eim, s, blk, n_blk_pad, down), ar, ai)
        for c, ends in enumerate((ere, eim)):
            entering = shifted(ends, 1, blk, n_blk_pad, down)[:n_blk]
            for r in range(SUBLANES):
                car_scr[2 * d + c, pl.ds(r, n_blk, stride=SUBLANES), :] = entering

    for d in range(2):
        tre, tim = mult(POWER_ROW0, POWER_ROW0 + SUBLANES, d)
        re, im = state[d]
        state[d] = list(_complex_step(re, im, car_scr[2 * d], car_scr[2 * d + 1], tile(tre), tile(tim)))

    prev = lambda a: pltpu.roll(a, 1, 0)[n_ctx_rows:]
    nxt = lambda a: pltpu.roll(a, rows - 1, 0)[:n_lat_rows]
    st = jnp.concatenate([prev(state[0][0]), prev(state[0][1]), nxt(state[1][0]), nxt(state[1][1])],
                         axis=1).astype(BF16)
    y = jnp.dot(st, cout_ref[0], preferred_element_type=F32)
    y_o[0, 0] = (y[:, :CHUNK_LANES]
                 + jnp.dot(u0[n_ctx_rows:], m_ref[0], preferred_element_type=F32)).astype(BF16)
    y_o[0, 1] = (y[:, CHUNK_LANES:]
                 + jnp.dot(u1[n_ctx_rows:], m_ref[1], preferred_element_type=F32)).astype(BF16)


def _ssm(u_cf, win_pair, m, cout_pair, mult, n_ctx_rows, n_steps):
    bsz, g, rows, _ = u_cf.shape
    assert rows % SUBLANES == 0 and n_ctx_rows % 16 == 0
    pair_spec = lambda a: pl.BlockSpec((1,) + a.shape[1:], lambda b, i: (i, 0, 0))
    return pl.pallas_call(
        functools.partial(_ssm_kernel, n_ctx_rows=n_ctx_rows, n_steps=n_steps),
        grid=(bsz, g // 2),
        in_specs=[pl.BlockSpec((1, 2, rows, CHUNK_LANES), lambda b, i: (b, i, 0, 0)),
                  pair_spec(win_pair),
                  pl.BlockSpec((2, CHUNK_LANES, CHUNK_LANES), lambda b, i: (i, 0, 0)),
                  pair_spec(cout_pair), pair_spec(mult)],
        out_specs=pl.BlockSpec((1, 2, rows - n_ctx_rows, CHUNK_LANES), lambda b, i: (b, i, 0, 0)),
        out_shape=jax.ShapeDtypeStruct((bsz, g, rows - n_ctx_rows, CHUNK_LANES), BF16),
        scratch_shapes=[pltpu.VMEM((4, rows, LANES), F32), pltpu.VMEM((4, rows, LANES), F32)],
        compiler_params=_cparams("parallel", "parallel"),
        name="ssm",
    )(u_cf, win_pair, m, cout_pair, mult)


def _final_kernel(x_ref, at_ref, ga_ref, y_ref, gs_ref, gate_ref, wglu_ref, bglu_ref, wout_ref,
                  pg_ref, o_ref, y_scr):
    _from_chunk_lanes(y_ref, y_scr, 0, y_ref.shape[2])
    y = jax.nn.gelu(jnp.concatenate([y_scr[col] for col in range(SSM_WIDTH // LANES)], axis=1))
    t = jnp.dot(y.astype(BF16), wglu_ref[...], preferred_element_type=F32) + bglu_ref[...]
    s = (y * jax.nn.sigmoid(t) * gs_ref[0].astype(F32)).astype(BF16)
    a = at_ref[0] * ga_ref[0]
    out = jnp.dot(jnp.concatenate([a, s], axis=1), wout_ref[...], preferred_element_type=F32)
    ms = jnp.mean(out * out, axis=-1, keepdims=True)
    o_ref[0] = x_ref[0] + out * lax.rsqrt(ms + NORM_EPS) * (gate_ref[0] * pg_ref[...])


def _final(x, attn, ga, y_cf, gs, gate, wglu_bf, b_glu, wout_bf, post_g, tm):
    bsz, n, d = x.shape
    assert n % tm == 0 and tm % (CHUNK * 16) == 0
    return pl.pallas_call(
        _final_kernel,
        grid=(bsz, n // tm),
        in_specs=[_row_spec(tm, d), _row_spec(tm, ATTN_WIDTH), _row_spec(tm, ATTN_WIDTH),
                  pl.BlockSpec((1, N_SSM_GROUPS, tm // CHUNK, CHUNK_LANES), lambda b, i: (b, 0, i, 0)),
                  _row_spec(tm, SSM_WIDTH), _batch_vec_spec(d),
                  _const_spec((SSM_WIDTH, SSM_WIDTH)), _const_spec((1, SSM_WIDTH)),
                  _const_spec((d, d)), _const_spec((1, d))],
        out_specs=_row_spec(tm, d),
        out_shape=jax.ShapeDtypeStruct((bsz, n, d), F32),
        scratch_shapes=[pltpu.VMEM((SSM_WIDTH // LANES, tm, LANES), F32)],
        compiler_params=_cparams("parallel", "parallel"),
        name="final",
    )(x, attn, ga, y_cf, gs, gate, wglu_bf, b_glu, wout_bf, post_g)


def _rope_tables(n_ctx, n_lat):
    t = jnp.arange(n_lat, dtype=jnp.int32)
    row_pos = (t // GRID_W).astype(F32)
    col_pos = (t % GRID_W).astype(F32)
    inv_freq = ROPE_THETA ** (-jnp.arange(ROPE_FREQS, dtype=F32) / ROPE_FREQS)
    ang_r = row_pos[:, None] * inv_freq
    ang_c = col_pos[:, None] * inv_freq
    cos = jnp.concatenate([jnp.cos(ang_r)] * 2 + [jnp.cos(ang_c)] * 2, axis=1)
    sin = jnp.concatenate([-jnp.sin(ang_r), jnp.sin(ang_r), -jnp.sin(ang_c), jnp.sin(ang_c)], axis=1)
    cos = jnp.concatenate([jnp.ones((n_ctx, HEAD_DIM), F32), cos], axis=0)
    sin = jnp.concatenate([jnp.zeros((n_ctx, HEAD_DIM), F32), sin], axis=0)
    return jnp.tile(cos, (1, LANES // HEAD_DIM)), jnp.tile(sin, (1, LANES // HEAD_DIM))


def _layer(x, ctx, c, c_ctx, w_ada, b_ada, pre_g, post_g, w_in, q_g, k_g, lam_re, lam_im, log_dt,
           b_re, b_im, c_re, c_im, d_skip, w_glu, b_glu, w_out):
    bsz, n_lat, d = x.shape
    n_ctx = ctx.shape[1]
    assert n_lat % GRID_W == 0

    cvecs = jnp.zeros((8, d), F32).at[:bsz].set(c).at[bsz].set(c_ctx)
    mod = _adaln(cvecs, w_ada, b_ada)
    shift, scale, gate = (mod[:, i * d:(i + 1) * d] for i in range(3))
    lat = lambda a: a[:bsz].reshape(bsz, 1, d)
    cvec = lambda a: a[bsz].reshape(1, 1, d)

    head = jnp.arange(ATTN_WIDTH) // HEAD_DIM
    bd = (head[:, None] == head[None, :]).astype(BF16) * (1.0 / HEAD_DIM)
    qg = jnp.tile(q_g, N_Q_HEADS).reshape(1, ATTN_WIDTH)
    kg = jnp.tile(k_g, N_KV_HEADS).reshape(1, KV_WIDTH)
    cos, sin = _rope_tables(n_ctx, n_lat)

    qt, k_all, vt_aug, ga, u_cf, gs = _inproj(x, ctx, lat(shift), lat(scale), cvec(shift), cvec(scale),
                                              pre_g.reshape(1, d), w_in.astype(BF16), qg, kg, cos, sin,
                                              bd, tm=256)
    attn = _attention(qt, k_all, vt_aug, tq=256, tk=8448, sub=256, lookahead=5, heads_per_pass=8)

    rows = (n_ctx + n_lat) // CHUNK
    n_steps = _scan_steps(rows)
    win_pair, m_op, cout_pair, mult = _ssm_prep(lam_re, lam_im, log_dt, b_re, b_im, c_re, c_im, d_skip,
                                                n_steps)
    y_cf = _ssm(u_cf, win_pair, m_op, cout_pair, mult, n_ctx // CHUNK, n_steps)

    return _final(x, attn, ga, y_cf, gs, lat(gate), w_glu.astype(BF16), b_glu.reshape(1, SSM_WIDTH),
                  w_out.astype(BF16), post_g.reshape(1, d), tm=512)


def kernel(x, c, ctx, c_ctx, w_ada, b_ada, pre_norm, post_norm, w_in, q_norm, k_norm, ssm_lam_re,
           ssm_lam_im, ssm_log_dt, ssm_b_re, ssm_b_im, ssm_c_re, ssm_c_im, ssm_d, w_glu, b_glu, w_out):
    depth = w_ada.shape[0]
    assert depth == 1, "context stream update between layers is not implemented"
    return _layer(x, ctx, c, c_ctx, w_ada[0], b_ada[0], pre_norm[0], post_norm[0], w_in[0], q_norm[0],
                  k_norm[0], ssm_lam_re[0], ssm_lam_im[0], ssm_log_dt[0], ssm_b_re[0], ssm_b_im[0],
                  ssm_c_re[0], ssm_c_im[0], ssm_d[0], w_glu[0], b_glu[0], w_out[0])
```

```python
import functools
import math

import jax
import jax.numpy as jnp
from jax import lax
from jax.experimental import pallas as pl
from jax.experimental.pallas import tpu as pltpu

F32 = jnp.float32
BF16 = jnp.bfloat16

D_MODEL = 1024
HEAD_DIM = 64
N_Q_HEADS = 8
N_KV_HEADS = 2
ATTN_WIDTH = N_Q_HEADS * HEAD_DIM
KV_WIDTH = N_KV_HEADS * HEAD_DIM
SSM_WIDTH = 512
SSM_GROUP = 16
N_SSM_GROUPS = SSM_WIDTH // SSM_GROUP
SSM_STATE = 64
GRID_W = 64
ROPE_THETA = 10000.0
ROPE_FREQS = 16
NORM_EPS = 1e-6
ATTN_SCALE = HEAD_DIM ** -0.5
LOG2_E = math.log2(math.e)
Q_END = ATTN_WIDTH
K_END = Q_END + KV_WIDTH
V_END = K_END + KV_WIDTH
GA_END = V_END + ATTN_WIDTH
U_END = GA_END + SSM_WIDTH
IN_WIDTH = U_END + SSM_WIDTH

CHUNK = 16
CHUNK_LANES = CHUNK * SSM_GROUP
LANES = 128
GROUPS_PER_VREG = LANES // SSM_GROUP
STATE_LANES = 2 * SSM_STATE
SUBLANES = 8
POWER_ROW0 = 16
MAX_POWER_BITS = 8
FINAL_SUBTILES = 4
V_ROWS = HEAD_DIM + 16
VMEM_LIMIT = 48 * 1024 * 1024
NEG_BIG = -1e30


def _cparams(*sem):
    return pltpu.CompilerParams(dimension_semantics=sem, vmem_limit_bytes=VMEM_LIMIT)


def _row_spec(tm, width):
    return pl.BlockSpec((1, tm, width), lambda b, i: (b, i, 0))


def _const_spec(shape):
    return pl.BlockSpec(shape, lambda b, i: (0,) * len(shape))


def _batch_vec_spec(width):
    return pl.BlockSpec((1, 1, width), lambda b, i: (b, 0, 0))


def _adaln_kernel(c_ref, w_ref, b_ref, o_ref):
    c = c_ref[...]
    s = c * jax.nn.sigmoid(c)
    o_ref[...] = jnp.dot(s, w_ref[...], preferred_element_type=F32,
                         precision=lax.Precision.HIGHEST) + b_ref[...]


def _adaln(cvecs, w_ada, b_ada):
    rows, d = cvecs.shape
    n = w_ada.shape[1]
    tn = 512
    return pl.pallas_call(
        _adaln_kernel,
        grid=(n // tn,),
        in_specs=[pl.BlockSpec((rows, d), lambda j: (0, 0)),
                  pl.BlockSpec((d, tn), lambda j: (0, j)),
                  pl.BlockSpec((1, tn), lambda j: (0, j))],
        out_specs=pl.BlockSpec((rows, tn), lambda j: (0, j)),
        out_shape=jax.ShapeDtypeStruct((rows, n), F32),
        compiler_params=_cparams("arbitrary"),
        name="adaln",
    )(cvecs, w_ada, b_ada.reshape(1, n))


def _lane_group(rows):
    return lax.broadcasted_iota(jnp.int32, (rows, LANES), 1) // SSM_GROUP


def _to_chunk_lanes(u_scr, u_o):
    rt = u_o.shape[2]
    grp = _lane_group(rt)
    for col in range(SSM_WIDTH // LANES):
        rolled = []
        for step in range(CHUNK):
            s = u_scr[col, pl.ds(step, rt, stride=CHUNK), :]
            rolled.append([s if k == 0 else pltpu.roll(s, k * SSM_GROUP, 1)
                           for k in range(GROUPS_PER_VREG)])
        for g_lo in range(GROUPS_PER_VREG):
            for half in range(CHUNK // GROUPS_PER_VREG):
                out = None
                for s8 in range(GROUPS_PER_VREG):
                    piece = rolled[half * GROUPS_PER_VREG + s8][(s8 - g_lo) % GROUPS_PER_VREG]
                    out = piece if out is None else jnp.where(grp == s8, piece, out)
                u_o[0, col * GROUPS_PER_VREG + g_lo, :, half * LANES:(half + 1) * LANES] = out.astype(BF16)


def _from_chunk_lanes(y_ref, y_scr, r0, rt):
    grp = _lane_group(rt)
    for col in range(SSM_WIDTH // LANES):
        rolled = {}
        for g_lo in range(GROUPS_PER_VREG):
            for half in range(CHUNK // GROUPS_PER_VREG):
                s = y_ref[0, col * GROUPS_PER_VREG + g_lo, r0:r0 + rt,
                          half * LANES:(half + 1) * LANES].astype(F32)
                rolled[g_lo, half] = [s if k == 0 else pltpu.roll(s, k * SSM_GROUP, 1)
                                      for k in range(GROUPS_PER_VREG)]
        for step in range(CHUNK):
            half, s8 = divmod(step, GROUPS_PER_VREG)
            out = None
            for g_lo in range(GROUPS_PER_VREG):
                piece = rolled[g_lo, half][(g_lo - s8) % GROUPS_PER_VREG]
                out = piece if out is None else jnp.where(grp == g_lo, piece, out)
            y_scr[col, pl.ds(r0 * CHUNK + step, rt, stride=CHUNK), :] = out


def _head_mean_sq(z, bd):
    return jnp.dot((z * z).astype(BF16), bd, preferred_element_type=F32)


def _swap16(x):
    w = x.shape[1]
    lane = lax.broadcasted_iota(jnp.int32, x.shape, 1)
    return jnp.where((lane & 16) == 0, pltpu.roll(x, w - 16, 1), pltpu.roll(x, 16, 1))


def _rope(x, cos, sin_signed):
    cols = []
    for c in range(x.shape[1] // LANES):
        xc = x[:, c * LANES:(c + 1) * LANES]
        cols.append(xc * cos + _swap16(xc) * sin_signed)
    return cols[0] if len(cols) == 1 else jnp.concatenate(cols, axis=1)


def _silu(z):
    return z * jax.nn.sigmoid(z)


def _inproj_kernel(x_ref, c_ref, shl_ref, scl_ref, shc_ref, scc_ref, pg_ref, w_ref, qg_ref, kg_ref,
                   cos_ref, sin_ref, bd_ref, qt_o, k_o, vt_o, ga_o, u_o, gs_o, v_scr, u_scr, *, n_ctx_tiles):
    is_ctx = pl.program_id(1) < n_ctx_tiles
    x = jnp.where(is_ctx, c_ref[0], x_ref[0])
    shift = jnp.where(is_ctx, shc_ref[0], shl_ref[0])
    scale = jnp.where(is_ctx, scc_ref[0], scl_ref[0])
    ms = jnp.mean(x * x, axis=-1, keepdims=True)
    xn = x * lax.rsqrt(ms + NORM_EPS) * pg_ref[...]
    h = (xn * (1.0 + scale) + shift).astype(BF16)

    def proj(a, b):
        return jnp.dot(h, w_ref[:, a:b], preferred_element_type=F32)

    u = proj(GA_END, U_END)
    for col in range(SSM_WIDTH // LANES):
        u_scr[col] = u[:, col * LANES:(col + 1) * LANES]
    _to_chunk_lanes(u_scr, u_o)

    cos = cos_ref[...]
    sin = sin_ref[...]
    zq = proj(0, Q_END)
    zk = proj(Q_END, K_END)
    gs_o[0] = _silu(proj(U_END, IN_WIDTH)).astype(BF16)
    qn = zq * lax.rsqrt(_head_mean_sq(zq, bd_ref[...]) + NORM_EPS) * qg_ref[...]
    kn = zk * lax.rsqrt(_head_mean_sq(zk, bd_ref[:KV_WIDTH, :KV_WIDTH]) + NORM_EPS) * kg_ref[...]
    ga_o[0] = _silu(proj(V_END, GA_END)).astype(BF16)
    v_scr[...] = proj(K_END, V_END)

    k_o[0] = _rope(kn, cos, sin).astype(BF16)
    q = _rope(qn, cos, sin) * (ATTN_SCALE * LOG2_E)
    group = N_Q_HEADS // N_KV_HEADS
    none = jnp.zeros((HEAD_DIM, q.shape[0]), BF16)
    for c in range(ATTN_WIDTH // LANES):
        qt = q[:, c * LANES:(c + 1) * LANES].T.astype(BF16)
        for par in range(2):
            head = 2 * c + par
            kv = head // group
            qt_o[0, head, kv * HEAD_DIM:(kv + 1) * HEAD_DIM, :] = qt[par * HEAD_DIM:(par + 1) * HEAD_DIM]
            qt_o[0, head, (1 - kv) * HEAD_DIM:(2 - kv) * HEAD_DIM, :] = none

    vt = v_scr[...].T.astype(BF16)
    tm = vt.shape[1]
    ones_row = lax.broadcasted_iota(jnp.int32, (V_ROWS - HEAD_DIM, tm), 0) == 0
    for j in range(N_KV_HEADS):
        vt_o[0, j, :HEAD_DIM, :] = vt[j * HEAD_DIM:(j + 1) * HEAD_DIM]
        vt_o[0, j, HEAD_DIM:, :] = jnp.where(ones_row, 1.0, 0.0).astype(BF16)


def _inproj(x, ctx, shift_l, scale_l, shift_c, scale_c, pre_g, w_bf, qg, kg, cos, sin, bd, tm):
    bsz, n_lat, d = x.shape
    n_ctx = ctx.shape[1]
    assert n_ctx % tm == 0 and n_lat % tm == 0 and tm % (CHUNK * 16) == 0
    nct = n_ctx // tm
    n_tot = n_ctx + n_lat
    lat_rows = lambda w: pl.BlockSpec((1, tm, w), lambda b, i: (b, jnp.maximum(i - nct, 0), 0))
    all_rows = lambda w: pl.BlockSpec((1, tm, w), lambda b, i: (b, i, 0))
    ctx_vec = pl.BlockSpec((1, 1, d), lambda b, i: (0, 0, 0))
    table = pl.BlockSpec((tm, LANES), lambda b, i: (i, 0))
    return pl.pallas_call(
        functools.partial(_inproj_kernel, n_ctx_tiles=nct),
        grid=(bsz, n_tot // tm),
        in_specs=[lat_rows(d),
                  pl.BlockSpec((1, tm, d), lambda b, i: (b, jnp.minimum(i, nct - 1), 0)),
                  _batch_vec_spec(d), _batch_vec_spec(d), ctx_vec, ctx_vec, _const_spec((1, d)),
                  _const_spec((d, IN_WIDTH)), _const_spec((1, ATTN_WIDTH)), _const_spec((1, KV_WIDTH)),
                  table, table, _const_spec((ATTN_WIDTH, ATTN_WIDTH))],
        out_specs=[pl.BlockSpec((1, N_Q_HEADS, LANES, tm), lambda b, i: (b, 0, 0, jnp.maximum(i - nct, 0))),
                   all_rows(KV_WIDTH),
                   pl.BlockSpec((1, N_KV_HEADS, V_ROWS, tm), lambda b, i: (b, 0, 0, i)),
                   lat_rows(ATTN_WIDTH),
                   pl.BlockSpec((1, N_SSM_GROUPS, tm // CHUNK, CHUNK_LANES), lambda b, i: (b, 0, i, 0)),
                   lat_rows(SSM_WIDTH)],
        out_shape=[jax.ShapeDtypeStruct((bsz, N_Q_HEADS, LANES, n_lat), BF16),
                   jax.ShapeDtypeStruct((bsz, n_tot, KV_WIDTH), BF16),
                   jax.ShapeDtypeStruct((bsz, N_KV_HEADS, V_ROWS, n_tot), BF16),
                   jax.ShapeDtypeStruct((bsz, n_lat, ATTN_WIDTH), BF16),
                   jax.ShapeDtypeStruct((bsz, N_SSM_GROUPS, n_tot // CHUNK, CHUNK_LANES), BF16),
                   jax.ShapeDtypeStruct((bsz, n_lat, SSM_WIDTH), BF16)],
        scratch_shapes=[pltpu.VMEM((tm, KV_WIDTH), F32), pltpu.VMEM((SSM_WIDTH // LANES, tm, LANES), F32)],
        compiler_params=_cparams("parallel", "arbitrary"),
        name="inproj",
    )(x, ctx, shift_l, scale_l, shift_c, scale_c, pre_g, w_bf, qg, kg, cos, sin, bd)


def _attn_kernel(qt_ref, k_ref, vt_ref, o_ref, *, tk, sub, lookahead, heads_per_pass):
    tq = qt_ref.shape[3]
    n_keys = k_ref.shape[1]
    group = N_Q_HEADS // N_KV_HEADS

    outs = []
    for h0 in range(0, N_Q_HEADS, heads_per_pass):
        heads = list(range(h0, h0 + heads_per_pass))
        q_wide = [qt_ref[0, h] for h in heads]

        def body(t, carry, heads=heads, q_wide=q_wide):
            tasks = [(j, i) for j in range(tk // sub) for i in range(len(heads))]
            state = list(carry)
            scores = {}

            def keys_at(j):
                return pl.ds(pl.multiple_of(t * tk + j * sub, sub), sub)

            def issue(n):
                j, i = tasks[n]
                scores[n] = jnp.dot(k_ref[0, keys_at(j), :], q_wide[i],
                                    preferred_element_type=F32)

            def consume(n):
                j, i = tasks[n]
                s = scores.pop(n)
                m_old, acc = state[i]
                vt = vt_ref[0, heads[i] // group, :, keys_at(j)]
                m_new = jnp.maximum(m_old, jnp.max(s, axis=0, keepdims=True))
                p = jnp.exp2((s - m_new).astype(BF16))
                acc = jnp.exp2(m_old - m_new) * acc + jnp.dot(vt, p, preferred_element_type=F32)
                state[i] = (m_new, acc)

            for n in range(len(tasks) + lookahead):
                if n < len(tasks):
                    issue(n)
                if n >= lookahead:
                    consume(n - lookahead)
            return tuple(state)

        init = tuple((jnp.full((1, tq), NEG_BIG, F32), jnp.zeros((V_ROWS, tq), F32)) for _ in heads)
        for _, acc in lax.fori_loop(0, n_keys // tk, body, init):
            outs.append(acc[:HEAD_DIM] / acc[HEAD_DIM:HEAD_DIM + 1])

    for c in range(N_Q_HEADS // 2):
        pair = jnp.concatenate([outs[2 * c], outs[2 * c + 1]], axis=0)
        o_ref[0, :, c * LANES:(c + 1) * LANES] = pair.T.astype(BF16)


def _attention(qt, k_all, vt_aug, tq, tk, sub, lookahead, heads_per_pass):
    bsz, _, _, n = qt.shape
    n_keys = k_all.shape[1]
    assert n_keys % tk == 0 and tk % sub == 0 and n % tq == 0
    return pl.pallas_call(
        functools.partial(_attn_kernel, tk=tk, sub=sub, lookahead=lookahead,
                          heads_per_pass=heads_per_pass),
        grid=(bsz, n // tq),
        in_specs=[pl.BlockSpec((1, N_Q_HEADS, LANES, tq), lambda b, i: (b, 0, 0, i)),
                  pl.BlockSpec((1, n_keys, KV_WIDTH), lambda b, i: (b, 0, 0)),
                  pl.BlockSpec((1, N_KV_HEADS, V_ROWS, n_keys), lambda b, i: (b, 0, 0, 0))],
        out_specs=_row_spec(tq, ATTN_WIDTH),
        out_shape=jax.ShapeDtypeStruct((bsz, n, ATTN_WIDTH), BF16),
        compiler_params=_cparams("parallel", "parallel"),
        name="attention",
    )(qt, k_all, vt_aug)


def _scan_steps(n_rows):
    return max(1, math.ceil(math.log2(n_rows)))


def _ssm_prep_kernel(lre_ref, lim_ref, ldt_ref, bt_ref, bts_ref, c_ref, cs_ref, d_ref,
                     win_o, m_o, cout_o, mult_o, *, n_steps):
    lane = lax.broadcasted_iota(jnp.int32, (1, STATE_LANES), 1)
    low = lane < SSM_STATE
    sign_lo = jnp.where(low, -1.0, 1.0)
    k_idx = lax.broadcasted_iota(jnp.int32, (CHUNK, STATE_LANES), 0)
    step_of_row = lax.broadcasted_iota(jnp.int32, (CHUNK_LANES, STATE_LANES), 0) // SSM_GROUP
    nt = (((1,), (1,)), ((), ()))

    def outer(pw, pws, mat, mats):
        full = pw[:, None, :] * mat[None, :, :] + pws[:, None, :] * mats[None, :, :]
        return full.reshape(CHUNK * mat.shape[0], STATE_LANES)

    state_in, state_out, mults, tables = {}, {}, {}, {}
    for gi in range(2):
        taps_lo, taps_hi = None, None
        for d in range(2):
            lre = lre_ref[gi, d:d + 1, :]
            lim = lim_ref[gi, d:d + 1, :]
            dt = jnp.exp(ldt_ref[gi, d:d + 1, :])

            mag = jnp.exp(dt * lre)
            a_re, a_im = mag * jnp.cos(dt * lim), mag * jnp.sin(dt * lim)
            squares = [(a_re, a_im)]
            for _ in range(MAX_POWER_BITS - 1):
                pr, pi = squares[-1]
                squares.append((pr * pr - pi * pi, 2.0 * pr * pi))

            def power(expo, squares=squares):
                re = jnp.ones(expo.shape, F32)
                im = jnp.zeros(expo.shape, F32)
                for b, (pr, pi) in enumerate(squares):
                    take = ((expo >> b) & 1) == 1
                    re, im = jnp.where(take, re * pr - im * pi, re), jnp.where(take, re * pi + im * pr, im)
                return re, im * sign_lo

            nr, ni = a_re - 1.0, a_im
            den = lre * lre + lim * lim
            cr = (nr * lre + ni * lim) / den
            ci = (ni * lre - nr * lim) / den
            cis = ci * sign_lo
            bbar = cr * bt_ref[gi, d] + cis * bts_ref[gi, d]
            bbar_s = cr * bts_ref[gi, d] - cis * bt_ref[gi, d]
            cmat, cmat_s = c_ref[gi, d], cs_ref[gi, d]

            asc, ascs = power(k_idx)
            desc, descs = power(CHUNK - 1 - k_idx)
            nxt, nxts = power(k_idx + 1)
            conj_c = lambda pw, pws: outer(pw, pws, cmat, cmat_s) * (-sign_lo)
            if d == 0:
                state_in[gi, d] = outer(desc, descs, bbar, bbar_s)
                state_out[gi, d] = conj_c(nxt, nxts)
                lag0 = jnp.where(step_of_row == CHUNK - 1, conj_c(desc, descs), 0.0)
                lags = jnp.where(step_of_row < CHUNK - 1, conj_c(nxt, nxts), 0.0)
                hi = lax.dot_general(bbar, lags, nt, preferred_element_type=F32,
                                     precision=lax.Precision.HIGHEST)
                lo = lax.dot_general(bbar, lag0, nt, preferred_element_type=F32,
                                     precision=lax.Precision.HIGHEST)
                taps_hi = hi
                taps_lo = lo if taps_lo is None else taps_lo + lo
            else:
                state_in[gi, d] = outer(asc, ascs, bbar, bbar_s)
                rev, revs = power(CHUNK - k_idx)
                state_out[gi, d] = conj_c(rev, revs)
                lo = lax.dot_general(bbar, conj_c(desc, descs), nt, preferred_element_type=F32,
                                     precision=lax.Precision.HIGHEST)
                taps_lo = lo if taps_lo is None else taps_lo + lo

            sr, sis = power(jnp.full((1, STATE_LANES), CHUNK, jnp.int32))
            chain = []
            for k in range(n_steps):
                chain.append((sr, sis))
                si = sis * sign_lo
                sr, sis = sr * sr - si * si, 2.0 * sr * si * sign_lo
            mults[gi, d] = chain
            r_idx = lax.broadcasted_iota(jnp.int32, (SUBLANES, STATE_LANES), 0)
            tables[gi, d] = power(CHUNK * (r_idx + 1) if d == 0 else CHUNK * (SUBLANES - r_idx))

        qq = lax.broadcasted_iota(jnp.int32, (SSM_GROUP, CHUNK_LANES), 0)
        ll = lax.broadcasted_iota(jnp.int32, (SSM_GROUP, CHUNK_LANES), 1)
        taps_lo = taps_lo + jnp.where(ll == qq + (CHUNK - 1) * SSM_GROUP, d_ref[gi], 0.0)
        taps = jnp.concatenate([taps_lo, taps_hi], axis=1)
        for j in range(CHUNK):
            off = (CHUNK - 1 - j) * SSM_GROUP
            m_o[gi, j * SSM_GROUP:(j + 1) * SSM_GROUP, :] = taps[:, off:off + CHUNK_LANES].astype(BF16)

    def swap(x):
        return pltpu.roll(x, SSM_STATE, 1)

    mult_o[...] = jnp.zeros(mult_o.shape, F32)
    wide_low = lax.broadcasted_iota(jnp.int32, (CHUNK_LANES, STATE_LANES), 1) < SSM_STATE
    zeros_c = jnp.zeros((SSM_STATE, CHUNK_LANES), F32)
    for d in range(2):
        s0, s1 = state_in[0, d], state_in[1, d]
        re_blk = jnp.concatenate([jnp.where(wide_low, s0, 0.0), jnp.where(wide_low, 0.0, swap(s1))], axis=0)
        im_blk = jnp.concatenate([jnp.where(wide_low, swap(s0), 0.0), jnp.where(wide_low, 0.0, s1)], axis=0)
        win_o[0, :, (2 * d) * LANES:(2 * d + 1) * LANES] = re_blk.astype(BF16)
        win_o[0, :, (2 * d + 1) * LANES:(2 * d + 2) * LANES] = im_blk.astype(BF16)
        t0, t1 = state_out[0, d].T, state_out[1, d].T
        for part in range(2):
            rows = slice(part * SSM_STATE, (part + 1) * SSM_STATE)
            blk = jnp.concatenate([jnp.concatenate([t0[rows], zeros_c], axis=1),
                                   jnp.concatenate([zeros_c, t1[rows]], axis=1)], axis=0)
            cout_o[0, (2 * d + part) * LANES:(2 * d + part + 1) * LANES, :] = blk.astype(BF16)
        for k in range(n_steps):
            (sr0, sis0), (sr1, sis1) = mults[0, d][k], mults[1, d][k]
            mult_o[0, k:k + 1, (2 * d) * LANES:(2 * d + 1) * LANES] = jnp.where(low, sr0, sr1)
            mult_o[0, k:k + 1, (2 * d + 1) * LANES:(2 * d + 2) * LANES] = jnp.where(low, -sis0, sis1)
        (tr0, tis0), (tr1, tis1) = tables[0, d], tables[1, d]
        mult_o[0, POWER_ROW0:POWER_ROW0 + SUBLANES, (2 * d) * LANES:(2 * d + 1) * LANES] = jnp.where(low, tr0, tr1)
        mult_o[0, POWER_ROW0:POWER_ROW0 + SUBLANES, (2 * d + 1) * LANES:(2 * d + 2) * LANES] = jnp.where(
            low, -tis0, tis1)


def _ssm_prep(lam_re, lam_im, log_dt, b_re, b_im, c_re, c_im, d_skip, n_steps):
    g = N_SSM_GROUPS
    dup = lambda a: jnp.concatenate([a, a], axis=-1)
    lre = dup(jnp.swapaxes(lam_re, 0, 1))
    lim = dup(jnp.swapaxes(lam_im, 0, 1))
    ldt = jnp.broadcast_to(jnp.swapaxes(log_dt, 0, 1)[..., None], (g, 2, STATE_LANES))
    btr = jnp.transpose(b_re, (1, 0, 3, 2))
    bti = jnp.transpose(b_im, (1, 0, 3, 2))
    bt, bts = jnp.concatenate([btr, bti], -1), jnp.concatenate([bti, btr], -1)
    cr, ci = jnp.swapaxes(c_re, 0, 1), jnp.swapaxes(c_im, 0, 1)
    cm, cms = jnp.concatenate([cr, ci], -1), jnp.concatenate([ci, cr], -1)
    dsk = jnp.pad(d_skip, ((0, 0), (CHUNK_LANES - SSM_GROUP, 0))).reshape(g, 1, CHUNK_LANES)
    n_pad = POWER_ROW0 + SUBLANES
    vec = pl.BlockSpec((2, 2, STATE_LANES), lambda i: (i, 0, 0))
    mat = pl.BlockSpec((2, 2, SSM_GROUP, STATE_LANES), lambda i: (i, 0, 0, 0))
    pair_sq = pl.BlockSpec((1, 2 * CHUNK_LANES, 4 * LANES), lambda i: (i, 0, 0))
    return pl.pallas_call(
        functools.partial(_ssm_prep_kernel, n_steps=n_steps),
        grid=(g // 2,),
        in_specs=[vec, vec, vec, mat, mat, mat, mat,
                  pl.BlockSpec((2, 1, CHUNK_LANES), lambda i: (i, 0, 0))],
        out_specs=[pair_sq, pl.BlockSpec((2, CHUNK_LANES, CHUNK_LANES), lambda i: (i, 0, 0)), pair_sq,
                   pl.BlockSpec((1, n_pad, 4 * LANES), lambda i: (i, 0, 0))],
        out_shape=[jax.ShapeDtypeStruct((g // 2, 2 * CHUNK_LANES, 4 * LANES), BF16),
                   jax.ShapeDtypeStruct((g, CHUNK_LANES, CHUNK_LANES), BF16),
                   jax.ShapeDtypeStruct((g // 2, 4 * LANES, 2 * CHUNK_LANES), BF16),
                   jax.ShapeDtypeStruct((g // 2, n_pad, 4 * LANES), F32)],
        compiler_params=_cparams("parallel"),
        name="ssm_prep",
    )(lre, lim, ldt, bt, bts, cm, cms, dsk)


def _complex_step(re, im, sr, si, ar, ai):
    return re + ar * sr - ai * si, im + ar * si + ai * sr


def _ssm_kernel(u_ref, win_ref, m_ref, cout_ref, mult_ref, y_o, loc_scr, car_scr, *, n_ctx_rows, n_steps):
    u0, u1 = u_ref[0, 0], u_ref[0, 1]
    rows = u0.shape[0]
    n_lat_rows = rows - n_ctx_rows
    n_blk = rows // SUBLANES
    n_blk_pad = -(-n_blk // SUBLANES) * SUBLANES
    x = jnp.dot(jnp.concatenate([u0, u1], axis=1), win_ref[0], preferred_element_type=F32)
    to_rev = lambda a: jnp.concatenate([a[n_ctx_rows:], a[:n_ctx_rows]], axis=0)
    state = [[x[:, :LANES], x[:, LANES:2 * LANES]],
             [to_rev(x[:, 2 * LANES:3 * LANES]), to_rev(x[:, 3 * LANES:])]]
    in_blk = lax.broadcasted_iota(jnp.int32, (rows, LANES), 0) & (SUBLANES - 1)
    blk = lax.broadcasted_iota(jnp.int32, (n_blk_pad, LANES), 0)

    def mult(k0, k1, d):
        return (mult_ref[0, k0:k1, 2 * d * LANES:(2 * d + 1) * LANES],
                mult_ref[0, k0:k1, (2 * d + 1) * LANES:(2 * d + 2) * LANES])

    def shifted(a, s, idx, n, down):
        if down:
            return jnp.where(idx >= s, pltpu.roll(a, s, 0), 0.0)
        return jnp.where(idx < n - s, pltpu.roll(a, a.shape[0] - s, 0), 0.0)

    tile = lambda t: jnp.broadcast_to(t[None], (n_blk, SUBLANES, LANES)).reshape(rows, LANES)
    sub_row = lax.broadcasted_iota(jnp.int32, (SUBLANES, LANES), 0)
    local_steps = SUBLANES.bit_length() - 1
    for d, down in ((0, True), (1, False)):
        re, im = state[d]
        for k in range(local_steps):
            s = 1 << k
            keep = sub_row >= s if down else sub_row < SUBLANES - s
            ar, ai = (tile(jnp.where(keep, m, 0.0)) for m in mult(k, k + 1, d))
            shift = s if down else rows - s
            re, im = _complex_step(re, im, pltpu.roll(re, shift, 0), pltpu.roll(im, shift, 0), ar, ai)
        loc_scr[2 * d], loc_scr[2 * d + 1] = re, im
        state[d] = [re, im]

        end_row = SUBLANES - 1 if down else 0
        pad = jnp.zeros((n_blk_pad - n_blk, LANES), F32)
        ere = jnp.concatenate([loc_scr[2 * d, pl.ds(end_row, n_blk, stride=SUBLANES), :], pad], axis=0)
        eim = jnp.concatenate([loc_scr[2 * d + 1, pl.ds(end_row, n_blk, stride=SUBLANES), :], pad], axis=0)
        for k in range(local_steps, n_steps):
            ar, ai = mult(k, k + 1, d)
            s = 1 << (k - local_steps)
            ere, eim = _complex_step(ere, eim, shifted(ere, s, blk, n_blk_pad, down),
                                     shifted(eim, s, blk, n_blk_pad, down), ar, ai)
        for c, ends in enumerate((ere, eim)):
            entering = shifted(ends, 1, blk, n_blk_pad, down)[:n_blk]
            for r in range(SUBLANES):
                car_scr[2 * d + c, pl.ds(r, n_blk, stride=SUBLANES), :] = entering

    for d in range(2):
        tre, tim = mult(POWER_ROW0, POWER_ROW0 + SUBLANES, d)
        re, im = state[d]
        state[d] = list(_complex_step(re, im, car_scr[2 * d], car_scr[2 * d + 1], tile(tre), tile(tim)))

    prev = lambda a: pltpu.roll(a, 1, 0)[n_ctx_rows:]
    nxt = lambda a: pltpu.roll(a, rows - 1, 0)[:n_lat_rows]
    st = jnp.concatenate([prev(state[0][0]), prev(state[0][1]), nxt(state[1][0]), nxt(state[1][1])],
                         axis=1).astype(BF16)
    y = jnp.dot(st, cout_ref[0], preferred_element_type=F32)
    y_o[0, 0] = (y[:, :CHUNK_LANES]
                 + jnp.dot(u0[n_ctx_rows:], m_ref[0], preferred_element_type=F32)).astype(BF16)
    y_o[0, 1] = (y[:, CHUNK_LANES:]
                 + jnp.dot(u1[n_ctx_rows:], m_ref[1], preferred_element_type=F32)).astype(BF16)


def _ssm(u_cf, win_pair, m, cout_pair, mult, n_ctx_rows, n_steps):
    bsz, g, rows, _ = u_cf.shape
    assert rows % SUBLANES == 0 and n_ctx_rows % 16 == 0
    pair_spec = lambda a: pl.BlockSpec((1,) + a.shape[1:], lambda b, i: (i, 0, 0))
    return pl.pallas_call(
        functools.partial(_ssm_kernel, n_ctx_rows=n_ctx_rows, n_steps=n_steps),
        grid=(bsz, g // 2),
        in_specs=[pl.BlockSpec((1, 2, rows, CHUNK_LANES), lambda b, i: (b, i, 0, 0)),
                  pair_spec(win_pair),
                  pl.BlockSpec((2, CHUNK_LANES, CHUNK_LANES), lambda b, i: (i, 0, 0)),
                  pair_spec(cout_pair), pair_spec(mult)],
        out_specs=pl.BlockSpec((1, 2, rows - n_ctx_rows, CHUNK_LANES), lambda b, i: (b, i, 0, 0)),
        out_shape=jax.ShapeDtypeStruct((bsz, g, rows - n_ctx_rows, CHUNK_LANES), BF16),
        scratch_shapes=[pltpu.VMEM((4, rows, LANES), F32), pltpu.VMEM((4, rows, LANES), F32)],
        compiler_params=_cparams("parallel", "parallel"),
        name="ssm",
    )(u_cf, win_pair, m, cout_pair, mult)


def _final_kernel(x_ref, at_ref, ga_ref, y_ref, gs_ref, gate_ref, wglu_ref, bglu_ref, wout_ref,
                  pg_ref, o_ref, y_scr):
    tm = x_ref.shape[1]
    sub = tm // FINAL_SUBTILES
    gain = gate_ref[0] * pg_ref[...]
    def finish(h, out):
        rows = slice(h * sub, (h + 1) * sub)
        ms = jnp.mean(out * out, axis=-1, keepdims=True)
        o_ref[0, rows] = x_ref[0, rows] + out * lax.rsqrt(ms + NORM_EPS) * gain

    pending = None
    for h in range(FINAL_SUBTILES):
        rows = slice(h * sub, (h + 1) * sub)
        _from_chunk_lanes(y_ref, y_scr, h * sub // CHUNK, sub // CHUNK)
        y = jax.nn.gelu(jnp.concatenate([y_scr[col, rows] for col in range(SSM_WIDTH // LANES)], axis=1))
        t = jnp.dot(y.astype(BF16), wglu_ref[...], preferred_element_type=F32) + bglu_ref[...]
        s = (y * jax.nn.sigmoid(t) * gs_ref[0, rows].astype(F32)).astype(BF16)
        a = at_ref[0, rows] * ga_ref[0, rows]
        out = jnp.dot(jnp.concatenate([a, s], axis=1), wout_ref[...], preferred_element_type=F32)
        if pending is not None:
            finish(*pending)
        pending = (h, out)
    finish(*pending)


def _final(x, attn, ga, y_cf, gs, gate, wglu_bf, b_glu, wout_bf, post_g, tm):
    bsz, n, d = x.shape
    assert n % tm == 0 and tm % (CHUNK * 16) == 0
    return pl.pallas_call(
        _final_kernel,
        grid=(bsz, n // tm),
        in_specs=[_row_spec(tm, d), _row_spec(tm, ATTN_WIDTH), _row_spec(tm, ATTN_WIDTH),
                  pl.BlockSpec((1, N_SSM_GROUPS, tm // CHUNK, CHUNK_LANES), lambda b, i: (b, 0, i, 0)),
                  _row_spec(tm, SSM_WIDTH), _batch_vec_spec(d),
                  _const_spec((SSM_WIDTH, SSM_WIDTH)), _const_spec((1, SSM_WIDTH)),
                  _const_spec((d, d)), _const_spec((1, d))],
        out_specs=_row_spec(tm, d),
        out_shape=jax.ShapeDtypeStruct((bsz, n, d), F32),
        scratch_shapes=[pltpu.VMEM((SSM_WIDTH // LANES, tm, LANES), F32)],
        compiler_params=_cparams("parallel", "parallel"),
        name="final",
    )(x, attn, ga, y_cf, gs, gate, wglu_bf, b_glu, wout_bf, post_g)


def _rope_tables(n_ctx, n_lat):
    t = jnp.arange(n_lat, dtype=jnp.int32)
    row_pos = (t // GRID_W).astype(F32)
    col_pos = (t % GRID_W).astype(F32)
    inv_freq = ROPE_THETA ** (-jnp.arange(ROPE_FREQS, dtype=F32) / ROPE_FREQS)
    ang_r = row_pos[:, None] * inv_freq
    ang_c = col_pos[:, None] * inv_freq
    cos = jnp.concatenate([jnp.cos(ang_r)] * 2 + [jnp.cos(ang_c)] * 2, axis=1)
    sin = jnp.concatenate([-jnp.sin(ang_r), jnp.sin(ang_r), -jnp.sin(ang_c), jnp.sin(ang_c)], axis=1)
    cos = jnp.concatenate([jnp.ones((n_ctx, HEAD_DIM), F32), cos], axis=0)
    sin = jnp.concatenate([jnp.zeros((n_ctx, HEAD_DIM), F32), sin], axis=0)
    return jnp.tile(cos, (1, LANES // HEAD_DIM)), jnp.tile(sin, (1, LANES // HEAD_DIM))


def _layer(x, ctx, c, c_ctx, w_ada, b_ada, pre_g, post_g, w_in, q_g, k_g, lam_re, lam_im, log_dt,
           b_re, b_im, c_re, c_im, d_skip, w_glu, b_glu, w_out):
    bsz, n_lat, d = x.shape
    n_ctx = ctx.shape[1]
    assert n_lat % GRID_W == 0

    cvecs = jnp.zeros((8, d), F32).at[:bsz].set(c).at[bsz].set(c_ctx)
    mod = _adaln(cvecs, w_ada, b_ada)
    shift, scale, gate = (mod[:, i * d:(i + 1) * d] for i in range(3))
    lat = lambda a: a[:bsz].reshape(bsz, 1, d)
    cvec = lambda a: a[bsz].reshape(1, 1, d)

    head = jnp.arange(ATTN_WIDTH) // HEAD_DIM
    bd = (head[:, None] == head[None, :]).astype(BF16) * (1.0 / HEAD_DIM)
    qg = jnp.tile(q_g, N_Q_HEADS).reshape(1, ATTN_WIDTH)
    kg = jnp.tile(k_g, N_KV_HEADS).reshape(1, KV_WIDTH)
    cos, sin = _rope_tables(n_ctx, n_lat)

    qt, k_all, vt_aug, ga, u_cf, gs = _inproj(x, ctx, lat(shift), lat(scale), cvec(shift), cvec(scale),
                                              pre_g.reshape(1, d), w_in.astype(BF16), qg, kg, cos, sin,
                                              bd, tm=256)
    attn = _attention(qt, k_all, vt_aug, tq=256, tk=8448, sub=256, lookahead=5, heads_per_pass=8)

    rows = (n_ctx + n_lat) // CHUNK
    n_steps = _scan_steps(rows)
    win_pair, m_op, cout_pair, mult = _ssm_prep(lam_re, lam_im, log_dt, b_re, b_im, c_re, c_im, d_skip,
                                                n_steps)
    y_cf = _ssm(u_cf, win_pair, m_op, cout_pair, mult, n_ctx // CHUNK, n_steps)

    return _final(x, attn, ga, y_cf, gs, lat(gate), w_glu.astype(BF16), b_glu.reshape(1, SSM_WIDTH),
                  w_out.astype(BF16), post_g.reshape(1, d), tm=1024)


def kernel(x, c, ctx, c_ctx, w_ada, b_ada, pre_norm, post_norm, w_in, q_norm, k_norm, ssm_lam_re,
           ssm_lam_im, ssm_log_dt, ssm_b_re, ssm_b_im, ssm_c_re, ssm_c_im, ssm_d, w_glu, b_glu, w_out):
    depth = w_ada.shape[0]
    assert depth == 1, "context stream update between layers is not implemented"
    return _layer(x, ctx, c, c_ctx, w_ada[0], b_ada[0], pre_norm[0], post_norm[0], w_in[0], q_norm[0],
                  k_norm[0], ssm_lam_re[0], ssm_lam_im[0], ssm_log_dt[0], ssm_b_re[0], ssm_b_im[0],
                  ssm_c_re[0], ssm_c_im[0], ssm_d[0], w_glu[0], b_glu[0], w_out[0])
```

```python
import functools
import math

import jax
import jax.numpy as jnp
from jax import lax
from jax.experimental import pallas as pl
from jax.experimental.pallas import tpu as pltpu

F32 = jnp.float32
BF16 = jnp.bfloat16

D_MODEL = 1024
HEAD_DIM = 64
N_Q_HEADS = 8
N_KV_HEADS = 2
ATTN_WIDTH = N_Q_HEADS * HEAD_DIM
KV_WIDTH = N_KV_HEADS * HEAD_DIM
SSM_WIDTH = 512
SSM_GROUP = 16
N_SSM_GROUPS = SSM_WIDTH // SSM_GROUP
SSM_STATE = 64
GRID_W = 64
ROPE_THETA = 10000.0
ROPE_FREQS = 16
NORM_EPS = 1e-6
ATTN_SCALE = HEAD_DIM ** -0.5
LOG2_E = math.log2(math.e)
Q_END = ATTN_WIDTH
K_END = Q_END + KV_WIDTH
V_END = K_END + KV_WIDTH
GA_END = V_END + ATTN_WIDTH
U_END = GA_END + SSM_WIDTH
IN_WIDTH = U_END + SSM_WIDTH

CHUNK = 16
CHUNK_LANES = CHUNK * SSM_GROUP
LANES = 128
GROUPS_PER_VREG = LANES // SSM_GROUP
STATE_LANES = 2 * SSM_STATE
SUBLANES = 8
POWER_ROW0 = 16
MAX_POWER_BITS = 8
SSM_PAIRS_PER_STEP = 4
FINAL_SUBTILES = 4
V_ROWS = HEAD_DIM + 16
VMEM_LIMIT = 48 * 1024 * 1024
NEG_BIG = -1e30


def _cparams(*sem):
    return pltpu.CompilerParams(dimension_semantics=sem, vmem_limit_bytes=VMEM_LIMIT)


def _row_spec(tm, width):
    return pl.BlockSpec((1, tm, width), lambda b, i: (b, i, 0))


def _const_spec(shape):
    return pl.BlockSpec(shape, lambda b, i: (0,) * len(shape))


def _batch_vec_spec(width):
    return pl.BlockSpec((1, 1, width), lambda b, i: (b, 0, 0))


def _adaln_kernel(c_ref, w_ref, b_ref, o_ref):
    c = c_ref[...]
    s = c * jax.nn.sigmoid(c)
    o_ref[...] = jnp.dot(s, w_ref[...], preferred_element_type=F32,
                         precision=lax.Precision.HIGHEST) + b_ref[...]


def _adaln(cvecs, w_ada, b_ada):
    rows, d = cvecs.shape
    n = w_ada.shape[1]
    tn = 512
    return pl.pallas_call(
        _adaln_kernel,
        grid=(n // tn,),
        in_specs=[pl.BlockSpec((rows, d), lambda j: (0, 0)),
                  pl.BlockSpec((d, tn), lambda j: (0, j)),
                  pl.BlockSpec((1, tn), lambda j: (0, j))],
        out_specs=pl.BlockSpec((rows, tn), lambda j: (0, j)),
        out_shape=jax.ShapeDtypeStruct((rows, n), F32),
        compiler_params=_cparams("arbitrary"),
        name="adaln",
    )(cvecs, w_ada, b_ada.reshape(1, n))


def _lane_group(rows):
    return lax.broadcasted_iota(jnp.int32, (rows, LANES), 1) // SSM_GROUP


def _to_chunk_lanes(u_scr, u_o):
    rt = u_o.shape[2]
    grp = _lane_group(rt)
    for col in range(SSM_WIDTH // LANES):
        rolled = []
        for step in range(CHUNK):
            s = u_scr[col, pl.ds(step, rt, stride=CHUNK), :]
            rolled.append([s if k == 0 else pltpu.roll(s, k * SSM_GROUP, 1)
                           for k in range(GROUPS_PER_VREG)])
        for g_lo in range(GROUPS_PER_VREG):
            for half in range(CHUNK // GROUPS_PER_VREG):
                out = None
                for s8 in range(GROUPS_PER_VREG):
                    piece = rolled[half * GROUPS_PER_VREG + s8][(s8 - g_lo) % GROUPS_PER_VREG]
                    out = piece if out is None else jnp.where(grp == s8, piece, out)
                u_o[0, col * GROUPS_PER_VREG + g_lo, :, half * LANES:(half + 1) * LANES] = out.astype(BF16)


def _from_chunk_lanes(y_ref, y_scr, r0, rt):
    grp = _lane_group(rt)
    for col in range(SSM_WIDTH // LANES):
        rolled = {}
        for g_lo in range(GROUPS_PER_VREG):
            for half in range(CHUNK // GROUPS_PER_VREG):
                s = y_ref[0, col * GROUPS_PER_VREG + g_lo, r0:r0 + rt,
                          half * LANES:(half + 1) * LANES].astype(F32)
                rolled[g_lo, half] = [s if k == 0 else pltpu.roll(s, k * SSM_GROUP, 1)
                                      for k in range(GROUPS_PER_VREG)]
        for step in range(CHUNK):
            half, s8 = divmod(step, GROUPS_PER_VREG)
            out = None
            for g_lo in range(GROUPS_PER_VREG):
                piece = rolled[g_lo, half][(g_lo - s8) % GROUPS_PER_VREG]
                out = piece if out is None else jnp.where(grp == g_lo, piece, out)
            y_scr[col, pl.ds(r0 * CHUNK + step, rt, stride=CHUNK), :] = out


def _head_mean_sq(z, bd):
    return jnp.dot((z * z).astype(BF16), bd, preferred_element_type=F32)


def _swap16(x):
    w = x.shape[1]
    lane = lax.broadcasted_iota(jnp.int32, x.shape, 1)
    return jnp.where((lane & 16) == 0, pltpu.roll(x, w - 16, 1), pltpu.roll(x, 16, 1))


def _rope(x, cos, sin_signed):
    cols = []
    for c in range(x.shape[1] // LANES):
        xc = x[:, c * LANES:(c + 1) * LANES]
        cols.append(xc * cos + _swap16(xc) * sin_signed)
    return cols[0] if len(cols) == 1 else jnp.concatenate(cols, axis=1)


def _silu(z):
    return z * jax.nn.sigmoid(z)


def _inproj_kernel(x_ref, c_ref, shl_ref, scl_ref, shc_ref, scc_ref, pg_ref, w_ref, qg_ref, kg_ref,
                   cos_ref, sin_ref, bd_ref, qt_o, k_o, vt_o, ga_o, u_o, gs_o, v_scr, u_scr, *, n_ctx_tiles):
    is_ctx = pl.program_id(1) < n_ctx_tiles
    x = jnp.where(is_ctx, c_ref[0], x_ref[0])
    shift = jnp.where(is_ctx, shc_ref[0], shl_ref[0])
    scale = jnp.where(is_ctx, scc_ref[0], scl_ref[0])
    ms = jnp.mean(x * x, axis=-1, keepdims=True)
    xn = x * lax.rsqrt(ms + NORM_EPS) * pg_ref[...]
    h = (xn * (1.0 + scale) + shift).astype(BF16)

    def proj(a, b):
        return jnp.dot(h, w_ref[:, a:b], preferred_element_type=F32)

    u = proj(GA_END, U_END)
    for col in range(SSM_WIDTH // LANES):
        u_scr[col] = u[:, col * LANES:(col + 1) * LANES]
    _to_chunk_lanes(u_scr, u_o)

    cos = cos_ref[...]
    sin = sin_ref[...]
    zq = proj(0, Q_END)
    zk = proj(Q_END, K_END)
    gs_o[0] = _silu(proj(U_END, IN_WIDTH)).astype(BF16)
    qn = zq * lax.rsqrt(_head_mean_sq(zq, bd_ref[...]) + NORM_EPS) * qg_ref[...]
    kn = zk * lax.rsqrt(_head_mean_sq(zk, bd_ref[:KV_WIDTH, :KV_WIDTH]) + NORM_EPS) * kg_ref[...]
    ga_o[0] = _silu(proj(V_END, GA_END)).astype(BF16)
    v_scr[...] = proj(K_END, V_END)

    k_o[0] = _rope(kn, cos, sin).astype(BF16)
    q = _rope(qn, cos, sin) * (ATTN_SCALE * LOG2_E)
    group = N_Q_HEADS // N_KV_HEADS
    none = jnp.zeros((HEAD_DIM, q.shape[0]), BF16)
    for c in range(ATTN_WIDTH // LANES):
        qt = q[:, c * LANES:(c + 1) * LANES].T.astype(BF16)
        for par in range(2):
            head = 2 * c + par
            kv = head // group
            qt_o[0, head, kv * HEAD_DIM:(kv + 1) * HEAD_DIM, :] = qt[par * HEAD_DIM:(par + 1) * HEAD_DIM]
            qt_o[0, head, (1 - kv) * HEAD_DIM:(2 - kv) * HEAD_DIM, :] = none

    vt = v_scr[...].T.astype(BF16)
    tm = vt.shape[1]
    ones_row = lax.broadcasted_iota(jnp.int32, (V_ROWS - HEAD_DIM, tm), 0) == 0
    for j in range(N_KV_HEADS):
        vt_o[0, j, :HEAD_DIM, :] = vt[j * HEAD_DIM:(j + 1) * HEAD_DIM]
        vt_o[0, j, HEAD_DIM:, :] = jnp.where(ones_row, 1.0, 0.0).astype(BF16)


def _inproj(x, ctx, shift_l, scale_l, shift_c, scale_c, pre_g, w_bf, qg, kg, cos, sin, bd, tm):
    bsz, n_lat, d = x.shape
    n_ctx = ctx.shape[1]
    assert n_ctx % tm == 0 and n_lat % tm == 0 and tm % (CHUNK * 16) == 0
    nct = n_ctx // tm
    n_tot = n_ctx + n_lat
    lat_rows = lambda w: pl.BlockSpec((1, tm, w), lambda b, i: (b, jnp.maximum(i - nct, 0), 0))
    all_rows = lambda w: pl.BlockSpec((1, tm, w), lambda b, i: (b, i, 0))
    ctx_vec = pl.BlockSpec((1, 1, d), lambda b, i: (0, 0, 0))
    table = pl.BlockSpec((tm, LANES), lambda b, i: (i, 0))
    return pl.pallas_call(
        functools.partial(_inproj_kernel, n_ctx_tiles=nct),
        grid=(bsz, n_tot // tm),
        in_specs=[lat_rows(d),
                  pl.BlockSpec((1, tm, d), lambda b, i: (b, jnp.minimum(i, nct - 1), 0)),
                  _batch_vec_spec(d), _batch_vec_spec(d), ctx_vec, ctx_vec, _const_spec((1, d)),
                  _const_spec((d, IN_WIDTH)), _const_spec((1, ATTN_WIDTH)), _const_spec((1, KV_WIDTH)),
                  table, table, _const_spec((ATTN_WIDTH, ATTN_WIDTH))],
        out_specs=[pl.BlockSpec((1, N_Q_HEADS, LANES, tm), lambda b, i: (b, 0, 0, jnp.maximum(i - nct, 0))),
                   all_rows(KV_WIDTH),
                   pl.BlockSpec((1, N_KV_HEADS, V_ROWS, tm), lambda b, i: (b, 0, 0, i)),
                   lat_rows(ATTN_WIDTH),
                   pl.BlockSpec((1, N_SSM_GROUPS, tm // CHUNK, CHUNK_LANES), lambda b, i: (b, 0, i, 0)),
                   lat_rows(SSM_WIDTH)],
        out_shape=[jax.ShapeDtypeStruct((bsz, N_Q_HEADS, LANES, n_lat), BF16),
                   jax.ShapeDtypeStruct((bsz, n_tot, KV_WIDTH), BF16),
                   jax.ShapeDtypeStruct((bsz, N_KV_HEADS, V_ROWS, n_tot), BF16),
                   jax.ShapeDtypeStruct((bsz, n_lat, ATTN_WIDTH), BF16),
                   jax.ShapeDtypeStruct((bsz, N_SSM_GROUPS, n_tot // CHUNK, CHUNK_LANES), BF16),
                   jax.ShapeDtypeStruct((bsz, n_lat, SSM_WIDTH), BF16)],
        scratch_shapes=[pltpu.VMEM((tm, KV_WIDTH), F32), pltpu.VMEM((SSM_WIDTH // LANES, tm, LANES), F32)],
        compiler_params=_cparams("parallel", "arbitrary"),
        name="inproj",
    )(x, ctx, shift_l, scale_l, shift_c, scale_c, pre_g, w_bf, qg, kg, cos, sin, bd)


def _attn_kernel(qt_ref, k_ref, vt_ref, o_ref, *, tk, sub, lookahead, heads_per_pass):
    tq = qt_ref.shape[3]
    n_keys = k_ref.shape[1]
    group = N_Q_HEADS // N_KV_HEADS

    outs = []
    for h0 in range(0, N_Q_HEADS, heads_per_pass):
        heads = list(range(h0, h0 + heads_per_pass))
        q_wide = [qt_ref[0, h] for h in heads]

        def body(t, carry, heads=heads, q_wide=q_wide):
            tasks = [(j, i) for j in range(tk // sub) for i in range(len(heads))]
            state = list(carry)
            scores = {}

            def keys_at(j):
                return pl.ds(pl.multiple_of(t * tk + j * sub, sub), sub)

            def issue(n):
                j, i = tasks[n]
                scores[n] = jnp.dot(k_ref[0, keys_at(j), :], q_wide[i],
                                    preferred_element_type=F32)

            def consume(n):
                j, i = tasks[n]
                s = scores.pop(n)
                m_old, acc = state[i]
                vt = vt_ref[0, heads[i] // group, :, keys_at(j)]
                m_new = jnp.maximum(m_old, jnp.max(s, axis=0, keepdims=True))
                p = jnp.exp2((s - m_new).astype(BF16))
                acc = jnp.exp2(m_old - m_new) * acc + jnp.dot(vt, p, preferred_element_type=F32)
                state[i] = (m_new, acc)

            for n in range(len(tasks) + lookahead):
                if n < len(tasks):
                    issue(n)
                if n >= lookahead:
                    consume(n - lookahead)
            return tuple(state)

        init = tuple((jnp.full((1, tq), NEG_BIG, F32), jnp.zeros((V_ROWS, tq), F32)) for _ in heads)
        for _, acc in lax.fori_loop(0, n_keys // tk, body, init):
            outs.append(acc[:HEAD_DIM] / acc[HEAD_DIM:HEAD_DIM + 1])

    for c in range(N_Q_HEADS // 2):
        pair = jnp.concatenate([outs[2 * c], outs[2 * c + 1]], axis=0)
        o_ref[0, :, c * LANES:(c + 1) * LANES] = pair.T.astype(BF16)


def _attention(qt, k_all, vt_aug, tq, tk, sub, lookahead, heads_per_pass):
    bsz, _, _, n = qt.shape
    n_keys = k_all.shape[1]
    assert n_keys % tk == 0 and tk % sub == 0 and n % tq == 0
    return pl.pallas_call(
        functools.partial(_attn_kernel, tk=tk, sub=sub, lookahead=lookahead,
                          heads_per_pass=heads_per_pass),
        grid=(bsz, n // tq),
        in_specs=[pl.BlockSpec((1, N_Q_HEADS, LANES, tq), lambda b, i: (b, 0, 0, i)),
                  pl.BlockSpec((1, n_keys, KV_WIDTH), lambda b, i: (b, 0, 0)),
                  pl.BlockSpec((1, N_KV_HEADS, V_ROWS, n_keys), lambda b, i: (b, 0, 0, 0))],
        out_specs=_row_spec(tq, ATTN_WIDTH),
        out_shape=jax.ShapeDtypeStruct((bsz, n, ATTN_WIDTH), BF16),
        compiler_params=_cparams("parallel", "parallel"),
        name="attention",
    )(qt, k_all, vt_aug)


def _scan_steps(n_rows):
    return max(1, math.ceil(math.log2(n_rows)))


def _ssm_prep_kernel(lre_ref, lim_ref, ldt_ref, bt_ref, bts_ref, c_ref, cs_ref, d_ref,
                     win_o, m_o, cout_o, mult_o, *, n_steps):
    lane = lax.broadcasted_iota(jnp.int32, (1, STATE_LANES), 1)
    low = lane < SSM_STATE
    sign_lo = jnp.where(low, -1.0, 1.0)
    k_idx = lax.broadcasted_iota(jnp.int32, (CHUNK, STATE_LANES), 0)
    step_of_row = lax.broadcasted_iota(jnp.int32, (CHUNK_LANES, STATE_LANES), 0) // SSM_GROUP
    nt = (((1,), (1,)), ((), ()))

    def outer(pw, pws, mat, mats):
        full = pw[:, None, :] * mat[None, :, :] + pws[:, None, :] * mats[None, :, :]
        return full.reshape(CHUNK * mat.shape[0], STATE_LANES)

    state_in, state_out, mults, tables = {}, {}, {}, {}
    for gi in range(2):
        taps_lo, taps_hi = None, None
        for d in range(2):
            lre = lre_ref[gi, d:d + 1, :]
            lim = lim_ref[gi, d:d + 1, :]
            dt = jnp.exp(ldt_ref[gi, d:d + 1, :])

            mag = jnp.exp(dt * lre)
            a_re, a_im = mag * jnp.cos(dt * lim), mag * jnp.sin(dt * lim)
            squares = [(a_re, a_im)]
            for _ in range(MAX_POWER_BITS - 1):
                pr, pi = squares[-1]
                squares.append((pr * pr - pi * pi, 2.0 * pr * pi))

            def power(expo, squares=squares):
                re = jnp.ones(expo.shape, F32)
                im = jnp.zeros(expo.shape, F32)
                for b, (pr, pi) in enumerate(squares):
                    take = ((expo >> b) & 1) == 1
                    re, im = jnp.where(take, re * pr - im * pi, re), jnp.where(take, re * pi + im * pr, im)
                return re, im * sign_lo

            nr, ni = a_re - 1.0, a_im
            den = lre * lre + lim * lim
            cr = (nr * lre + ni * lim) / den
            ci = (ni * lre - nr * lim) / den
            cis = ci * sign_lo
            bbar = cr * bt_ref[gi, d] + cis * bts_ref[gi, d]
            bbar_s = cr * bts_ref[gi, d] - cis * bt_ref[gi, d]
            cmat, cmat_s = c_ref[gi, d], cs_ref[gi, d]

            asc, ascs = power(k_idx)
            desc, descs = power(CHUNK - 1 - k_idx)
            nxt, nxts = power(k_idx + 1)
            conj_c = lambda pw, pws: outer(pw, pws, cmat, cmat_s) * (-sign_lo)
            if d == 0:
                state_in[gi, d] = outer(desc, descs, bbar, bbar_s)
                state_out[gi, d] = conj_c(nxt, nxts)
                lag0 = jnp.where(step_of_row == CHUNK - 1, conj_c(desc, descs), 0.0)
                lags = jnp.where(step_of_row < CHUNK - 1, conj_c(nxt, nxts), 0.0)
                hi = lax.dot_general(bbar, lags, nt, preferred_element_type=F32,
                                     precision=lax.Precision.HIGHEST)
                lo = lax.dot_general(bbar, lag0, nt, preferred_element_type=F32,
                                     precision=lax.Precision.HIGHEST)
                taps_hi = hi
                taps_lo = lo if taps_lo is None else taps_lo + lo
            else:
                state_in[gi, d] = outer(asc, ascs, bbar, bbar_s)
                rev, revs = power(CHUNK - k_idx)
                state_out[gi, d] = conj_c(rev, revs)
                lo = lax.dot_general(bbar, conj_c(desc, descs), nt, preferred_element_type=F32,
                                     precision=lax.Precision.HIGHEST)
                taps_lo = lo if taps_lo is None else taps_lo + lo

            sr, sis = power(jnp.full((1, STATE_LANES), CHUNK, jnp.int32))
            chain = []
            for k in range(n_steps):
                chain.append((sr, sis))
                si = sis * sign_lo
                sr, sis = sr * sr - si * si, 2.0 * sr * si * sign_lo
            mults[gi, d] = chain
            r_idx = lax.broadcasted_iota(jnp.int32, (SUBLANES, STATE_LANES), 0)
            tables[gi, d] = power(CHUNK * (r_idx + 1) if d == 0 else CHUNK * (SUBLANES - r_idx))

        qq = lax.broadcasted_iota(jnp.int32, (SSM_GROUP, CHUNK_LANES), 0)
        ll = lax.broadcasted_iota(jnp.int32, (SSM_GROUP, CHUNK_LANES), 1)
        taps_lo = taps_lo + jnp.where(ll == qq + (CHUNK - 1) * SSM_GROUP, d_ref[gi], 0.0)
        taps = jnp.concatenate([taps_lo, taps_hi], axis=1)
        for j in range(CHUNK):
            off = (CHUNK - 1 - j) * SSM_GROUP
            m_o[gi, j * SSM_GROUP:(j + 1) * SSM_GROUP, :] = taps[:, off:off + CHUNK_LANES].astype(BF16)

    def swap(x):
        return pltpu.roll(x, SSM_STATE, 1)

    mult_o[...] = jnp.zeros(mult_o.shape, F32)
    wide_low = lax.broadcasted_iota(jnp.int32, (CHUNK_LANES, STATE_LANES), 1) < SSM_STATE
    zeros_c = jnp.zeros((SSM_STATE, CHUNK_LANES), F32)
    for d in range(2):
        s0, s1 = state_in[0, d], state_in[1, d]
        re_blk = jnp.concatenate([jnp.where(wide_low, s0, 0.0), jnp.where(wide_low, 0.0, swap(s1))], axis=0)
        im_blk = jnp.concatenate([jnp.where(wide_low, swap(s0), 0.0), jnp.where(wide_low, 0.0, s1)], axis=0)
        win_o[0, :, (2 * d) * LANES:(2 * d + 1) * LANES] = re_blk.astype(BF16)
        win_o[0, :, (2 * d + 1) * LANES:(2 * d + 2) * LANES] = im_blk.astype(BF16)
        t0, t1 = state_out[0, d].T, state_out[1, d].T
        for part in range(2):
            rows = slice(part * SSM_STATE, (part + 1) * SSM_STATE)
            blk = jnp.concatenate([jnp.concatenate([t0[rows], zeros_c], axis=1),
                                   jnp.concatenate([zeros_c, t1[rows]], axis=1)], axis=0)
            cout_o[0, (2 * d + part) * LANES:(2 * d + part + 1) * LANES, :] = blk.astype(BF16)
        for k in range(n_steps):
            (sr0, sis0), (sr1, sis1) = mults[0, d][k], mults[1, d][k]
            mult_o[0, k:k + 1, (2 * d) * LANES:(2 * d + 1) * LANES] = jnp.where(low, sr0, sr1)
            mult_o[0, k:k + 1, (2 * d + 1) * LANES:(2 * d + 2) * LANES] = jnp.where(low, -sis0, sis1)
        (tr0, tis0), (tr1, tis1) = tables[0, d], tables[1, d]
        mult_o[0, POWER_ROW0:POWER_ROW0 + SUBLANES, (2 * d) * LANES:(2 * d + 1) * LANES] = jnp.where(low, tr0, tr1)
        mult_o[0, POWER_ROW0:POWER_ROW0 + SUBLANES, (2 * d + 1) * LANES:(2 * d + 2) * LANES] = jnp.where(
            low, -tis0, tis1)


def _ssm_prep(lam_re, lam_im, log_dt, b_re, b_im, c_re, c_im, d_skip, n_steps):
    g = N_SSM_GROUPS
    dup = lambda a: jnp.concatenate([a, a], axis=-1)
    lre = dup(jnp.swapaxes(lam_re, 0, 1))
    lim = dup(jnp.swapaxes(lam_im, 0, 1))
    ldt = jnp.broadcast_to(jnp.swapaxes(log_dt, 0, 1)[..., None], (g, 2, STATE_LANES))
    btr = jnp.transpose(b_re, (1, 0, 3, 2))
    bti = jnp.transpose(b_im, (1, 0, 3, 2))
    bt, bts = jnp.concatenate([btr, bti], -1), jnp.concatenate([bti, btr], -1)
    cr, ci = jnp.swapaxes(c_re, 0, 1), jnp.swapaxes(c_im, 0, 1)
    cm, cms = jnp.concatenate([cr, ci], -1), jnp.concatenate([ci, cr], -1)
    dsk = jnp.pad(d_skip, ((0, 0), (CHUNK_LANES - SSM_GROUP, 0))).reshape(g, 1, CHUNK_LANES)
    n_pad = POWER_ROW0 + SUBLANES
    vec = pl.BlockSpec((2, 2, STATE_LANES), lambda i: (i, 0, 0))
    mat = pl.BlockSpec((2, 2, SSM_GROUP, STATE_LANES), lambda i: (i, 0, 0, 0))
    pair_sq = pl.BlockSpec((1, 2 * CHUNK_LANES, 4 * LANES), lambda i: (i, 0, 0))
    return pl.pallas_call(
        functools.partial(_ssm_prep_kernel, n_steps=n_steps),
        grid=(g // 2,),
        in_specs=[vec, vec, vec, mat, mat, mat, mat,
                  pl.BlockSpec((2, 1, CHUNK_LANES), lambda i: (i, 0, 0))],
        out_specs=[pair_sq, pl.BlockSpec((2, CHUNK_LANES, CHUNK_LANES), lambda i: (i, 0, 0)), pair_sq,
                   pl.BlockSpec((1, n_pad, 4 * LANES), lambda i: (i, 0, 0))],
        out_shape=[jax.ShapeDtypeStruct((g // 2, 2 * CHUNK_LANES, 4 * LANES), BF16),
                   jax.ShapeDtypeStruct((g, CHUNK_LANES, CHUNK_LANES), BF16),
                   jax.ShapeDtypeStruct((g // 2, 4 * LANES, 2 * CHUNK_LANES), BF16),
                   jax.ShapeDtypeStruct((g // 2, n_pad, 4 * LANES), F32)],
        compiler_params=_cparams("parallel"),
        name="ssm_prep",
    )(lre, lim, ldt, bt, bts, cm, cms, dsk)


def _complex_step(re, im, sr, si, ar, ai):
    return re + ar * sr - ai * si, im + ar * si + ai * sr


def _ssm_kernel(u_ref, win_ref, m_ref, cout_ref, mult_ref, y_o, loc_scr, car_scr, *, n_ctx_rows, n_steps):
    n_pairs = win_ref.shape[0]
    rows = u_ref.shape[2]
    n_lat_rows = rows - n_ctx_rows
    n_blk = rows // SUBLANES
    n_blk_pad = -(-n_blk // SUBLANES) * SUBLANES
    in_blk_row = lax.broadcasted_iota(jnp.int32, (SUBLANES, LANES), 0)
    blk = lax.broadcasted_iota(jnp.int32, (n_blk_pad, LANES), 0)
    tile = lambda t: jnp.broadcast_to(t[None], (n_blk, SUBLANES, LANES)).reshape(rows, LANES)
    local_steps = SUBLANES.bit_length() - 1
    to_rev = lambda a: jnp.concatenate([a[n_ctx_rows:], a[:n_ctx_rows]], axis=0)

    def shifted(a, s, idx, n, down):
        if down:
            return jnp.where(idx >= s, pltpu.roll(a, s, 0), 0.0)
        return jnp.where(idx < n - s, pltpu.roll(a, a.shape[0] - s, 0), 0.0)

    def scan(pp, x):
        def mult(k0, k1, d):
            return (mult_ref[pp, k0:k1, 2 * d * LANES:(2 * d + 1) * LANES],
                    mult_ref[pp, k0:k1, (2 * d + 1) * LANES:(2 * d + 2) * LANES])

        state = [[x[:, :LANES], x[:, LANES:2 * LANES]],
                 [to_rev(x[:, 2 * LANES:3 * LANES]), to_rev(x[:, 3 * LANES:])]]
        for d, down in ((0, True), (1, False)):
            re, im = state[d]
            for k in range(local_steps):
                s = 1 << k
                keep = in_blk_row >= s if down else in_blk_row < SUBLANES - s
                ar, ai = (tile(jnp.where(keep, m, 0.0)) for m in mult(k, k + 1, d))
                shift = s if down else rows - s
                re, im = _complex_step(re, im, pltpu.roll(re, shift, 0), pltpu.roll(im, shift, 0), ar, ai)
            loc_scr[pp, 2 * d], loc_scr[pp, 2 * d + 1] = re, im
            state[d] = [re, im]

            end_row = SUBLANES - 1 if down else 0
            pad = jnp.zeros((n_blk_pad - n_blk, LANES), F32)
            ere = jnp.concatenate([loc_scr[pp, 2 * d, pl.ds(end_row, n_blk, stride=SUBLANES), :], pad], axis=0)
            eim = jnp.concatenate([loc_scr[pp, 2 * d + 1, pl.ds(end_row, n_blk, stride=SUBLANES), :], pad],
                                  axis=0)
            for k in range(local_steps, n_steps):
                ar, ai = mult(k, k + 1, d)
                s = 1 << (k - local_steps)
                ere, eim = _complex_step(ere, eim, shifted(ere, s, blk, n_blk_pad, down),
                                         shifted(eim, s, blk, n_blk_pad, down), ar, ai)
            for c, ends in enumerate((ere, eim)):
                entering = shifted(ends, 1, blk, n_blk_pad, down)[:n_blk]
                for r in range(SUBLANES):
                    car_scr[pp, 2 * d + c, pl.ds(r, n_blk, stride=SUBLANES), :] = entering

        for d in range(2):
            tre, tim = mult(POWER_ROW0, POWER_ROW0 + SUBLANES, d)
            re, im = state[d]
            state[d] = list(_complex_step(re, im, car_scr[pp, 2 * d], car_scr[pp, 2 * d + 1],
                                          tile(tre), tile(tim)))
        return state

    xs = [jnp.dot(jnp.concatenate([u_ref[0, 2 * pp], u_ref[0, 2 * pp + 1]], axis=1), win_ref[pp],
                  preferred_element_type=F32) for pp in range(n_pairs)]
    prev = lambda a: pltpu.roll(a, 1, 0)[n_ctx_rows:]
    nxt = lambda a: pltpu.roll(a, rows - 1, 0)[:n_lat_rows]
    for pp in range(n_pairs):
        state = scan(pp, xs[pp])
        st = jnp.concatenate([prev(state[0][0]), prev(state[0][1]), nxt(state[1][0]), nxt(state[1][1])],
                             axis=1).astype(BF16)
        y = jnp.dot(st, cout_ref[pp], preferred_element_type=F32)
        for gi in range(2):
            g = 2 * pp + gi
            y_o[0, g] = (y[:, gi * CHUNK_LANES:(gi + 1) * CHUNK_LANES]
                         + jnp.dot(u_ref[0, g, n_ctx_rows:, :], m_ref[g],
                                   preferred_element_type=F32)).astype(BF16)


def _ssm(u_cf, win_pair, m, cout_pair, mult, n_ctx_rows, n_steps):
    bsz, g, rows, _ = u_cf.shape
    pairs = SSM_PAIRS_PER_STEP
    assert rows % SUBLANES == 0 and n_ctx_rows % 16 == 0 and g % (2 * pairs) == 0
    pair_spec = lambda a: pl.BlockSpec((pairs,) + a.shape[1:], lambda b, i: (i, 0, 0))
    return pl.pallas_call(
        functools.partial(_ssm_kernel, n_ctx_rows=n_ctx_rows, n_steps=n_steps),
        grid=(bsz, g // (2 * pairs)),
        in_specs=[pl.BlockSpec((1, 2 * pairs, rows, CHUNK_LANES), lambda b, i: (b, i, 0, 0)),
                  pair_spec(win_pair),
                  pl.BlockSpec((2 * pairs, CHUNK_LANES, CHUNK_LANES), lambda b, i: (i, 0, 0)),
                  pair_spec(cout_pair), pair_spec(mult)],
        out_specs=pl.BlockSpec((1, 2 * pairs, rows - n_ctx_rows, CHUNK_LANES), lambda b, i: (b, i, 0, 0)),
        out_shape=jax.ShapeDtypeStruct((bsz, g, rows - n_ctx_rows, CHUNK_LANES), BF16),
        scratch_shapes=[pltpu.VMEM((pairs, 4, rows, LANES), F32), pltpu.VMEM((pairs, 4, rows, LANES), F32)],
        compiler_params=_cparams("parallel", "parallel"),
        name="ssm",
    )(u_cf, win_pair, m, cout_pair, mult)


def _final_kernel(x_ref, at_ref, ga_ref, y_ref, gs_ref, gate_ref, wglu_ref, bglu_ref, wout_ref,
                  pg_ref, o_ref, y_scr):
    tm = x_ref.shape[1]
    sub = tm // FINAL_SUBTILES
    gain = gate_ref[0] * pg_ref[...]
    def finish(h, out):
        rows = slice(h * sub, (h + 1) * sub)
        ms = jnp.mean(out * out, axis=-1, keepdims=True)
        o_ref[0, rows] = x_ref[0, rows] + out * lax.rsqrt(ms + NORM_EPS) * gain

    pending = None
    for h in range(FINAL_SUBTILES):
        rows = slice(h * sub, (h + 1) * sub)
        _from_chunk_lanes(y_ref, y_scr, h * sub // CHUNK, sub // CHUNK)
        y = jax.nn.gelu(jnp.concatenate([y_scr[col, rows] for col in range(SSM_WIDTH // LANES)], axis=1))
        t = jnp.dot(y.astype(BF16), wglu_ref[...], preferred_element_type=F32) + bglu_ref[...]
        s = (y * jax.nn.sigmoid(t) * gs_ref[0, rows].astype(F32)).astype(BF16)
        a = at_ref[0, rows] * ga_ref[0, rows]
        out = jnp.dot(jnp.concatenate([a, s], axis=1), wout_ref[...], preferred_element_type=F32)
        if pending is not None:
            finish(*pending)
        pending = (h, out)
    finish(*pending)


def _final(x, attn, ga, y_cf, gs, gate, wglu_bf, b_glu, wout_bf, post_g, tm):
    bsz, n, d = x.shape
    assert n % tm == 0 and tm % (CHUNK * 16) == 0
    return pl.pallas_call(
        _final_kernel,
        grid=(bsz, n // tm),
        in_specs=[_row_spec(tm, d), _row_spec(tm, ATTN_WIDTH), _row_spec(tm, ATTN_WIDTH),
                  pl.BlockSpec((1, N_SSM_GROUPS, tm // CHUNK, CHUNK_LANES), lambda b, i: (b, 0, i, 0)),
                  _row_spec(tm, SSM_WIDTH), _batch_vec_spec(d),
                  _const_spec((SSM_WIDTH, SSM_WIDTH)), _const_spec((1, SSM_WIDTH)),
                  _const_spec((d, d)), _const_spec((1, d))],
        out_specs=_row_spec(tm, d),
        out_shape=jax.ShapeDtypeStruct((bsz, n, d), F32),
        scratch_shapes=[pltpu.VMEM((SSM_WIDTH // LANES, tm, LANES), F32)],
        compiler_params=_cparams("parallel", "parallel"),
        name="final",
    )(x, attn, ga, y_cf, gs, gate, wglu_bf, b_glu, wout_bf, post_g)


def _rope_tables(n_ctx, n_lat):
    t = jnp.arange(n_lat, dtype=jnp.int32)
    row_pos = (t // GRID_W).astype(F32)
    col_pos = (t % GRID_W).astype(F32)
    inv_freq = ROPE_THETA ** (-jnp.arange(ROPE_FREQS, dtype=F32) / ROPE_FREQS)
    ang_r = row_pos[:, None] * inv_freq
    ang_c = col_pos[:, None] * inv_freq
    cos = jnp.concatenate([jnp.cos(ang_r)] * 2 + [jnp.cos(ang_c)] * 2, axis=1)
    sin = jnp.concatenate([-jnp.sin(ang_r), jnp.sin(ang_r), -jnp.sin(ang_c), jnp.sin(ang_c)], axis=1)
    cos = jnp.concatenate([jnp.ones((n_ctx, HEAD_DIM), F32), cos], axis=0)
    sin = jnp.concatenate([jnp.zeros((n_ctx, HEAD_DIM), F32), sin], axis=0)
    return jnp.tile(cos, (1, LANES // HEAD_DIM)), jnp.tile(sin, (1, LANES // HEAD_DIM))


def _layer(x, ctx, c, c_ctx, w_ada, b_ada, pre_g, post_g, w_in, q_g, k_g, lam_re, lam_im, log_dt,
           b_re, b_im, c_re, c_im, d_skip, w_glu, b_glu, w_out):
    bsz, n_lat, d = x.shape
    n_ctx = ctx.shape[1]
    assert n_lat % GRID_W == 0

    cvecs = jnp.zeros((8, d), F32).at[:bsz].set(c).at[bsz].set(c_ctx)
    mod = _adaln(cvecs, w_ada, b_ada)
    shift, scale, gate = (mod[:, i * d:(i + 1) * d] for i in range(3))
    lat = lambda a: a[:bsz].reshape(bsz, 1, d)
    cvec = lambda a: a[bsz].reshape(1, 1, d)

    head = jnp.arange(ATTN_WIDTH) // HEAD_DIM
    bd = (head[:, None] == head[None, :]).astype(BF16) * (1.0 / HEAD_DIM)
    qg = jnp.tile(q_g, N_Q_HEADS).reshape(1, ATTN_WIDTH)
    kg = jnp.tile(k_g, N_KV_HEADS).reshape(1, KV_WIDTH)
    cos, sin = _rope_tables(n_ctx, n_lat)

    qt, k_all, vt_aug, ga, u_cf, gs = _inproj(x, ctx, lat(shift), lat(scale), cvec(shift), cvec(scale),
                                              pre_g.reshape(1, d), w_in.astype(BF16), qg, kg, cos, sin,
                                              bd, tm=256)
    attn = _attention(qt, k_all, vt_aug, tq=256, tk=8448, sub=256, lookahead=5, heads_per_pass=8)

    rows = (n_ctx + n_lat) // CHUNK
    n_steps = _scan_steps(rows)
    win_pair, m_op, cout_pair, mult = _ssm_prep(lam_re, lam_im, log_dt, b_re, b_im, c_re, c_im, d_skip,
                                                n_steps)
    y_cf = _ssm(u_cf, win_pair, m_op, cout_pair, mult, n_ctx // CHUNK, n_steps)

    return _final(x, attn, ga, y_cf, gs, lat(gate), w_glu.astype(BF16), b_glu.reshape(1, SSM_WIDTH),
                  w_out.astype(BF16), post_g.reshape(1, d), tm=1024)


def kernel(x, c, ctx, c_ctx, w_ada, b_ada, pre_norm, post_norm, w_in, q_norm, k_norm, ssm_lam_re,
           ssm_lam_im, ssm_log_dt, ssm_b_re, ssm_b_im, ssm_c_re, ssm_c_im, ssm_d, w_glu, b_glu, w_out):
    depth = w_ada.shape[0]
    assert depth == 1, "context stream update between layers is not implemented"
    return _layer(x, ctx, c, c_ctx, w_ada[0], b_ada[0], pre_norm[0], post_norm[0], w_in[0], q_norm[0],
                  k_norm[0], ssm_lam_re[0], ssm_lam_im[0], ssm_log_dt[0], ssm_b_re[0], ssm_b_im[0],
                  ssm_c_re[0], ssm_c_im[0], ssm_d[0], w_glu[0], b_glu[0], w_out[0])
```

```python
import functools
import math

import jax
import jax.numpy as jnp
from jax import lax
from jax.experimental import pallas as pl
from jax.experimental.pallas import tpu as pltpu

F32 = jnp.float32
BF16 = jnp.bfloat16

D_MODEL = 1024
HEAD_DIM = 64
N_Q_HEADS = 8
N_KV_HEADS = 2
ATTN_WIDTH = N_Q_HEADS * HEAD_DIM
KV_WIDTH = N_KV_HEADS * HEAD_DIM
SSM_WIDTH = 512
SSM_GROUP = 16
N_SSM_GROUPS = SSM_WIDTH // SSM_GROUP
SSM_STATE = 64
GRID_W = 64
ROPE_THETA = 10000.0
ROPE_FREQS = 16
NORM_EPS = 1e-6
ATTN_SCALE = HEAD_DIM ** -0.5
LOG2_E = math.log2(math.e)
Q_END = ATTN_WIDTH
K_END = Q_END + KV_WIDTH
V_END = K_END + KV_WIDTH
GA_END = V_END + ATTN_WIDTH
U_END = GA_END + SSM_WIDTH
IN_WIDTH = U_END + SSM_WIDTH

CHUNK = 16
CHUNK_LANES = CHUNK * SSM_GROUP
LANES = 128
GROUPS_PER_VREG = LANES // SSM_GROUP
STATE_LANES = 2 * SSM_STATE
SUBLANES = 8
POWER_ROW0 = 16
MAX_POWER_BITS = 8
SSM_PAIRS_PER_STEP = 2
FINAL_SUBTILES = 4
V_ROWS = HEAD_DIM + 16
VMEM_LIMIT = 48 * 1024 * 1024
NEG_BIG = -1e30


def _cparams(*sem):
    return pltpu.CompilerParams(dimension_semantics=sem, vmem_limit_bytes=VMEM_LIMIT)


def _row_spec(tm, width):
    return pl.BlockSpec((1, tm, width), lambda b, i: (b, i, 0))


def _const_spec(shape):
    return pl.BlockSpec(shape, lambda b, i: (0,) * len(shape))


def _batch_vec_spec(width):
    return pl.BlockSpec((1, 1, width), lambda b, i: (b, 0, 0))


def _adaln_kernel(c_ref, w_ref, b_ref, o_ref):
    c = c_ref[...]
    s = c * jax.nn.sigmoid(c)
    o_ref[...] = jnp.dot(s, w_ref[...], preferred_element_type=F32,
                         precision=lax.Precision.HIGHEST) + b_ref[...]


def _adaln(cvecs, w_ada, b_ada):
    rows, d = cvecs.shape
    n = w_ada.shape[1]
    tn = 512
    return pl.pallas_call(
        _adaln_kernel,
        grid=(n // tn,),
        in_specs=[pl.BlockSpec((rows, d), lambda j: (0, 0)),
                  pl.BlockSpec((d, tn), lambda j: (0, j)),
                  pl.BlockSpec((1, tn), lambda j: (0, j))],
        out_specs=pl.BlockSpec((rows, tn), lambda j: (0, j)),
        out_shape=jax.ShapeDtypeStruct((rows, n), F32),
        compiler_params=_cparams("arbitrary"),
        name="adaln",
    )(cvecs, w_ada, b_ada.reshape(1, n))


def _lane_group(rows):
    return lax.broadcasted_iota(jnp.int32, (rows, LANES), 1) // SSM_GROUP


def _to_chunk_lanes(u_scr, u_o):
    rt = u_o.shape[2]
    grp = _lane_group(rt)
    for col in range(SSM_WIDTH // LANES):
        rolled = []
        for step in range(CHUNK):
            s = u_scr[col, pl.ds(step, rt, stride=CHUNK), :]
            rolled.append([s if k == 0 else pltpu.roll(s, k * SSM_GROUP, 1)
                           for k in range(GROUPS_PER_VREG)])
        for g_lo in range(GROUPS_PER_VREG):
            for half in range(CHUNK // GROUPS_PER_VREG):
                out = None
                for s8 in range(GROUPS_PER_VREG):
                    piece = rolled[half * GROUPS_PER_VREG + s8][(s8 - g_lo) % GROUPS_PER_VREG]
                    out = piece if out is None else jnp.where(grp == s8, piece, out)
                u_o[0, col * GROUPS_PER_VREG + g_lo, :, half * LANES:(half + 1) * LANES] = out.astype(BF16)


def _from_chunk_lanes(y_ref, y_scr, r0, rt):
    grp = _lane_group(rt)
    for col in range(SSM_WIDTH // LANES):
        rolled = {}
        for g_lo in range(GROUPS_PER_VREG):
            for half in range(CHUNK // GROUPS_PER_VREG):
                s = y_ref[0, col * GROUPS_PER_VREG + g_lo, r0:r0 + rt,
                          half * LANES:(half + 1) * LANES].astype(F32)
                rolled[g_lo, half] = [s if k == 0 else pltpu.roll(s, k * SSM_GROUP, 1)
                                      for k in range(GROUPS_PER_VREG)]
        for step in range(CHUNK):
            half, s8 = divmod(step, GROUPS_PER_VREG)
            out = None
            for g_lo in range(GROUPS_PER_VREG):
                piece = rolled[g_lo, half][(g_lo - s8) % GROUPS_PER_VREG]
                out = piece if out is None else jnp.where(grp == g_lo, piece, out)
            y_scr[col, pl.ds(r0 * CHUNK + step, rt, stride=CHUNK), :] = out


def _head_mean_sq(z, bd):
    return jnp.dot((z * z).astype(BF16), bd, preferred_element_type=F32)


def _swap16(x):
    w = x.shape[1]
    lane = lax.broadcasted_iota(jnp.int32, x.shape, 1)
    return jnp.where((lane & 16) == 0, pltpu.roll(x, w - 16, 1), pltpu.roll(x, 16, 1))


def _rope(x, cos, sin_signed):
    cols = []
    for c in range(x.shape[1] // LANES):
        xc = x[:, c * LANES:(c + 1) * LANES]
        cols.append(xc * cos + _swap16(xc) * sin_signed)
    return cols[0] if len(cols) == 1 else jnp.concatenate(cols, axis=1)


def _silu(z):
    return z * jax.nn.sigmoid(z)


def _inproj_kernel(x_ref, c_ref, shl_ref, scl_ref, shc_ref, scc_ref, pg_ref, w_ref, qg_ref, kg_ref,
                   cos_ref, sin_ref, bd_ref, qt_o, k_o, vt_o, ga_o, u_o, gs_o, v_scr, u_scr, *, n_ctx_tiles):
    is_ctx = pl.program_id(1) < n_ctx_tiles
    x = jnp.where(is_ctx, c_ref[0], x_ref[0])
    shift = jnp.where(is_ctx, shc_ref[0], shl_ref[0])
    scale = jnp.where(is_ctx, scc_ref[0], scl_ref[0])
    ms = jnp.mean(x * x, axis=-1, keepdims=True)
    xn = x * lax.rsqrt(ms + NORM_EPS) * pg_ref[...]
    h = (xn * (1.0 + scale) + shift).astype(BF16)

    def proj(a, b):
        return jnp.dot(h, w_ref[:, a:b], preferred_element_type=F32)

    u = proj(GA_END, U_END)
    for col in range(SSM_WIDTH // LANES):
        u_scr[col] = u[:, col * LANES:(col + 1) * LANES]
    _to_chunk_lanes(u_scr, u_o)

    cos = cos_ref[...]
    sin = sin_ref[...]
    zq = proj(0, Q_END)
    zk = proj(Q_END, K_END)
    gs_o[0] = _silu(proj(U_END, IN_WIDTH)).astype(BF16)
    qn = zq * lax.rsqrt(_head_mean_sq(zq, bd_ref[...]) + NORM_EPS) * qg_ref[...]
    kn = zk * lax.rsqrt(_head_mean_sq(zk, bd_ref[:KV_WIDTH, :KV_WIDTH]) + NORM_EPS) * kg_ref[...]
    ga_o[0] = _silu(proj(V_END, GA_END)).astype(BF16)
    v_scr[...] = proj(K_END, V_END)

    k_o[0] = _rope(kn, cos, sin).astype(BF16)
    q = _rope(qn, cos, sin) * (ATTN_SCALE * LOG2_E)
    group = N_Q_HEADS // N_KV_HEADS
    none = jnp.zeros((HEAD_DIM, q.shape[0]), BF16)
    for c in range(ATTN_WIDTH // LANES):
        qt = q[:, c * LANES:(c + 1) * LANES].T.astype(BF16)
        for par in range(2):
            head = 2 * c + par
            kv = head // group
            qt_o[0, head, kv * HEAD_DIM:(kv + 1) * HEAD_DIM, :] = qt[par * HEAD_DIM:(par + 1) * HEAD_DIM]
            qt_o[0, head, (1 - kv) * HEAD_DIM:(2 - kv) * HEAD_DIM, :] = none

    vt = v_scr[...].T.astype(BF16)
    tm = vt.shape[1]
    ones_row = lax.broadcasted_iota(jnp.int32, (V_ROWS - HEAD_DIM, tm), 0) == 0
    for j in range(N_KV_HEADS):
        vt_o[0, j, :HEAD_DIM, :] = vt[j * HEAD_DIM:(j + 1) * HEAD_DIM]
        vt_o[0, j, HEAD_DIM:, :] = jnp.where(ones_row, 1.0, 0.0).astype(BF16)


def _inproj(x, ctx, shift_l, scale_l, shift_c, scale_c, pre_g, w_bf, qg, kg, cos, sin, bd, tm):
    bsz, n_lat, d = x.shape
    n_ctx = ctx.shape[1]
    assert n_ctx % tm == 0 and n_lat % tm == 0 and tm % (CHUNK * 16) == 0
    nct = n_ctx // tm
    n_tot = n_ctx + n_lat
    lat_rows = lambda w: pl.BlockSpec((1, tm, w), lambda b, i: (b, jnp.maximum(i - nct, 0), 0))
    all_rows = lambda w: pl.BlockSpec((1, tm, w), lambda b, i: (b, i, 0))
    ctx_vec = pl.BlockSpec((1, 1, d), lambda b, i: (0, 0, 0))
    table = pl.BlockSpec((tm, LANES), lambda b, i: (i, 0))
    return pl.pallas_call(
        functools.partial(_inproj_kernel, n_ctx_tiles=nct),
        grid=(bsz, n_tot // tm),
        in_specs=[lat_rows(d),
                  pl.BlockSpec((1, tm, d), lambda b, i: (b, jnp.minimum(i, nct - 1), 0)),
                  _batch_vec_spec(d), _batch_vec_spec(d), ctx_vec, ctx_vec, _const_spec((1, d)),
                  _const_spec((d, IN_WIDTH)), _const_spec((1, ATTN_WIDTH)), _const_spec((1, KV_WIDTH)),
                  table, table, _const_spec((ATTN_WIDTH, ATTN_WIDTH))],
        out_specs=[pl.BlockSpec((1, N_Q_HEADS, LANES, tm), lambda b, i: (b, 0, 0, jnp.maximum(i - nct, 0))),
                   all_rows(KV_WIDTH),
                   pl.BlockSpec((1, N_KV_HEADS, V_ROWS, tm), lambda b, i: (b, 0, 0, i)),
                   lat_rows(ATTN_WIDTH),
                   pl.BlockSpec((1, N_SSM_GROUPS, tm // CHUNK, CHUNK_LANES), lambda b, i: (b, 0, i, 0)),
                   lat_rows(SSM_WIDTH)],
        out_shape=[jax.ShapeDtypeStruct((bsz, N_Q_HEADS, LANES, n_lat), BF16),
                   jax.ShapeDtypeStruct((bsz, n_tot, KV_WIDTH), BF16),
                   jax.ShapeDtypeStruct((bsz, N_KV_HEADS, V_ROWS, n_tot), BF16),
                   jax.ShapeDtypeStruct((bsz, n_lat, ATTN_WIDTH), BF16),
                   jax.ShapeDtypeStruct((bsz, N_SSM_GROUPS, n_tot // CHUNK, CHUNK_LANES), BF16),
                   jax.ShapeDtypeStruct((bsz, n_lat, SSM_WIDTH), BF16)],
        scratch_shapes=[pltpu.VMEM((tm, KV_WIDTH), F32), pltpu.VMEM((SSM_WIDTH // LANES, tm, LANES), F32)],
        compiler_params=_cparams("parallel", "arbitrary"),
        name="inproj",
    )(x, ctx, shift_l, scale_l, shift_c, scale_c, pre_g, w_bf, qg, kg, cos, sin, bd)


def _attn_kernel(qt_ref, k_ref, vt_ref, o_ref, *, tk, sub, lookahead, heads_per_pass):
    tq = qt_ref.shape[3]
    n_keys = k_ref.shape[1]
    group = N_Q_HEADS // N_KV_HEADS

    outs = []
    for h0 in range(0, N_Q_HEADS, heads_per_pass):
        heads = list(range(h0, h0 + heads_per_pass))
        q_wide = [qt_ref[0, h] for h in heads]

        def body(t, carry, heads=heads, q_wide=q_wide):
            tasks = [(j, i) for j in range(tk // sub) for i in range(len(heads))]
            state = list(carry)
            scores = {}

            def keys_at(j):
                return pl.ds(pl.multiple_of(t * tk + j * sub, sub), sub)

            def issue(n):
                j, i = tasks[n]
                scores[n] = jnp.dot(k_ref[0, keys_at(j), :], q_wide[i],
                                    preferred_element_type=F32)

            def consume(n):
                j, i = tasks[n]
                s = scores.pop(n)
                m_old, acc = state[i]
                vt = vt_ref[0, heads[i] // group, :, keys_at(j)]
                m_new = jnp.maximum(m_old, jnp.max(s, axis=0, keepdims=True))
                p = jnp.exp2((s - m_new).astype(BF16))
                acc = jnp.exp2(m_old - m_new) * acc + jnp.dot(vt, p, preferred_element_type=F32)
                state[i] = (m_new, acc)

            for n in range(len(tasks) + lookahead):
                if n < len(tasks):
                    issue(n)
                if n >= lookahead:
                    consume(n - lookahead)
            return tuple(state)

        init = tuple((jnp.full((1, tq), NEG_BIG, F32), jnp.zeros((V_ROWS, tq), F32)) for _ in heads)
        for _, acc in lax.fori_loop(0, n_keys // tk, body, init):
            outs.append(acc[:HEAD_DIM] / acc[HEAD_DIM:HEAD_DIM + 1])

    for c in range(N_Q_HEADS // 2):
        pair = jnp.concatenate([outs[2 * c], outs[2 * c + 1]], axis=0)
        o_ref[0, :, c * LANES:(c + 1) * LANES] = pair.T.astype(BF16)


def _attention(qt, k_all, vt_aug, tq, tk, sub, lookahead, heads_per_pass):
    bsz, _, _, n = qt.shape
    n_keys = k_all.shape[1]
    assert n_keys % tk == 0 and tk % sub == 0 and n % tq == 0
    return pl.pallas_call(
        functools.partial(_attn_kernel, tk=tk, sub=sub, lookahead=lookahead,
                          heads_per_pass=heads_per_pass),
        grid=(bsz, n // tq),
        in_specs=[pl.BlockSpec((1, N_Q_HEADS, LANES, tq), lambda b, i: (b, 0, 0, i)),
                  pl.BlockSpec((1, n_keys, KV_WIDTH), lambda b, i: (b, 0, 0)),
                  pl.BlockSpec((1, N_KV_HEADS, V_ROWS, n_keys), lambda b, i: (b, 0, 0, 0))],
        out_specs=_row_spec(tq, ATTN_WIDTH),
        out_shape=jax.ShapeDtypeStruct((bsz, n, ATTN_WIDTH), BF16),
        compiler_params=_cparams("parallel", "parallel"),
        name="attention",
    )(qt, k_all, vt_aug)


def _scan_steps(n_rows):
    return max(1, math.ceil(math.log2(n_rows)))


def _ssm_prep_kernel(lre_ref, lim_ref, ldt_ref, bt_ref, bts_ref, c_ref, cs_ref, d_ref,
                     win_o, m_o, cout_o, mult_o, *, n_steps):
    lane = lax.broadcasted_iota(jnp.int32, (1, STATE_LANES), 1)
    low = lane < SSM_STATE
    sign_lo = jnp.where(low, -1.0, 1.0)
    k_idx = lax.broadcasted_iota(jnp.int32, (CHUNK, STATE_LANES), 0)
    step_of_row = lax.broadcasted_iota(jnp.int32, (CHUNK_LANES, STATE_LANES), 0) // SSM_GROUP
    nt = (((1,), (1,)), ((), ()))

    def outer(pw, pws, mat, mats):
        full = pw[:, None, :] * mat[None, :, :] + pws[:, None, :] * mats[None, :, :]
        return full.reshape(CHUNK * mat.shape[0], STATE_LANES)

    state_in, state_out, mults, tables = {}, {}, {}, {}
    for gi in range(2):
        taps_lo, taps_hi = None, None
        for d in range(2):
            lre = lre_ref[gi, d:d + 1, :]
            lim = lim_ref[gi, d:d + 1, :]
            dt = jnp.exp(ldt_ref[gi, d:d + 1, :])

            mag = jnp.exp(dt * lre)
            a_re, a_im = mag * jnp.cos(dt * lim), mag * jnp.sin(dt * lim)
            squares = [(a_re, a_im)]
            for _ in range(MAX_POWER_BITS - 1):
                pr, pi = squares[-1]
                squares.append((pr * pr - pi * pi, 2.0 * pr * pi))

            def power(expo, squares=squares):
                re = jnp.ones(expo.shape, F32)
                im = jnp.zeros(expo.shape, F32)
                for b, (pr, pi) in enumerate(squares):
                    take = ((expo >> b) & 1) == 1
                    re, im = jnp.where(take, re * pr - im * pi, re), jnp.where(take, re * pi + im * pr, im)
                return re, im * sign_lo

            nr, ni = a_re - 1.0, a_im
            den = lre * lre + lim * lim
            cr = (nr * lre + ni * lim) / den
            ci = (ni * lre - nr * lim) / den
            cis = ci * sign_lo
            bbar = cr * bt_ref[gi, d] + cis * bts_ref[gi, d]
            bbar_s = cr * bts_ref[gi, d] - cis * bt_ref[gi, d]
            cmat, cmat_s = c_ref[gi, d], cs_ref[gi, d]

            asc, ascs = power(k_idx)
            desc, descs = power(CHUNK - 1 - k_idx)
            nxt, nxts = power(k_idx + 1)
            conj_c = lambda pw, pws: outer(pw, pws, cmat, cmat_s) * (-sign_lo)
            if d == 0:
                state_in[gi, d] = outer(desc, descs, bbar, bbar_s)
                state_out[gi, d] = conj_c(nxt, nxts)
                lag0 = jnp.where(step_of_row == CHUNK - 1, conj_c(desc, descs), 0.0)
                lags = jnp.where(step_of_row < CHUNK - 1, conj_c(nxt, nxts), 0.0)
                hi = lax.dot_general(bbar, lags, nt, preferred_element_type=F32,
                                     precision=lax.Precision.HIGHEST)
                lo = lax.dot_general(bbar, lag0, nt, preferred_element_type=F32,
                                     precision=lax.Precision.HIGHEST)
                taps_hi = hi
                taps_lo = lo if taps_lo is None else taps_lo + lo
            else:
                state_in[gi, d] = outer(asc, ascs, bbar, bbar_s)
                rev, revs = power(CHUNK - k_idx)
                state_out[gi, d] = conj_c(rev, revs)
                lo = lax.dot_general(bbar, conj_c(desc, descs), nt, preferred_element_type=F32,
                                     precision=lax.Precision.HIGHEST)
                taps_lo = lo if taps_lo is None else taps_lo + lo

            sr, sis = power(jnp.full((1, STATE_LANES), CHUNK, jnp.int32))
            chain = []
            for k in range(n_steps):
                chain.append((sr, sis))
                si = sis * sign_lo
                sr, sis = sr * sr - si * si, 2.0 * sr * si * sign_lo
            mults[gi, d] = chain
            r_idx = lax.broadcasted_iota(jnp.int32, (SUBLANES, STATE_LANES), 0)
            tables[gi, d] = power(CHUNK * (r_idx + 1) if d == 0 else CHUNK * (SUBLANES - r_idx))

        qq = lax.broadcasted_iota(jnp.int32, (SSM_GROUP, CHUNK_LANES), 0)
        ll = lax.broadcasted_iota(jnp.int32, (SSM_GROUP, CHUNK_LANES), 1)
        taps_lo = taps_lo + jnp.where(ll == qq + (CHUNK - 1) * SSM_GROUP, d_ref[gi], 0.0)
        taps = jnp.concatenate([taps_lo, taps_hi], axis=1)
        for j in range(CHUNK):
            off = (CHUNK - 1 - j) * SSM_GROUP
            m_o[gi, j * SSM_GROUP:(j + 1) * SSM_GROUP, :] = taps[:, off:off + CHUNK_LANES].astype(BF16)

    def swap(x):
        return pltpu.roll(x, SSM_STATE, 1)

    mult_o[...] = jnp.zeros(mult_o.shape, F32)
    wide_low = lax.broadcasted_iota(jnp.int32, (CHUNK_LANES, STATE_LANES), 1) < SSM_STATE
    zeros_c = jnp.zeros((SSM_STATE, CHUNK_LANES), F32)
    for d in range(2):
        s0, s1 = state_in[0, d], state_in[1, d]
        re_blk = jnp.concatenate([jnp.where(wide_low, s0, 0.0), jnp.where(wide_low, 0.0, swap(s1))], axis=0)
        im_blk = jnp.concatenate([jnp.where(wide_low, swap(s0), 0.0), jnp.where(wide_low, 0.0, s1)], axis=0)
        win_o[0, :, (2 * d) * LANES:(2 * d + 1) * LANES] = re_blk.astype(BF16)
        win_o[0, :, (2 * d + 1) * LANES:(2 * d + 2) * LANES] = im_blk.astype(BF16)
        t0, t1 = state_out[0, d].T, state_out[1, d].T
        for part in range(2):
            rows = slice(part * SSM_STATE, (part + 1) * SSM_STATE)
            blk = jnp.concatenate([jnp.concatenate([t0[rows], zeros_c], axis=1),
                                   jnp.concatenate([zeros_c, t1[rows]], axis=1)], axis=0)
            cout_o[0, (2 * d + part) * LANES:(2 * d + part + 1) * LANES, :] = blk.astype(BF16)
        for k in range(n_steps):
            (sr0, sis0), (sr1, sis1) = mults[0, d][k], mults[1, d][k]
            mult_o[0, k:k + 1, (2 * d) * LANES:(2 * d + 1) * LANES] = jnp.where(low, sr0, sr1)
            mult_o[0, k:k + 1, (2 * d + 1) * LANES:(2 * d + 2) * LANES] = jnp.where(low, -sis0, sis1)
        (tr0, tis0), (tr1, tis1) = tables[0, d], tables[1, d]
        mult_o[0, POWER_ROW0:POWER_ROW0 + SUBLANES, (2 * d) * LANES:(2 * d + 1) * LANES] = jnp.where(low, tr0, tr1)
        mult_o[0, POWER_ROW0:POWER_ROW0 + SUBLANES, (2 * d + 1) * LANES:(2 * d + 2) * LANES] = jnp.where(
            low, -tis0, tis1)


def _ssm_prep(lam_re, lam_im, log_dt, b_re, b_im, c_re, c_im, d_skip, n_steps):
    g = N_SSM_GROUPS
    dup = lambda a: jnp.concatenate([a, a], axis=-1)
    lre = dup(jnp.swapaxes(lam_re, 0, 1))
    lim = dup(jnp.swapaxes(lam_im, 0, 1))
    ldt = jnp.broadcast_to(jnp.swapaxes(log_dt, 0, 1)[..., None], (g, 2, STATE_LANES))
    btr = jnp.transpose(b_re, (1, 0, 3, 2))
    bti = jnp.transpose(b_im, (1, 0, 3, 2))
    bt, bts = jnp.concatenate([btr, bti], -1), jnp.concatenate([bti, btr], -1)
    cr, ci = jnp.swapaxes(c_re, 0, 1), jnp.swapaxes(c_im, 0, 1)
    cm, cms = jnp.concatenate([cr, ci], -1), jnp.concatenate([ci, cr], -1)
    dsk = jnp.pad(d_skip, ((0, 0), (CHUNK_LANES - SSM_GROUP, 0))).reshape(g, 1, CHUNK_LANES)
    n_pad = POWER_ROW0 + SUBLANES
    vec = pl.BlockSpec((2, 2, STATE_LANES), lambda i: (i, 0, 0))
    mat = pl.BlockSpec((2, 2, SSM_GROUP, STATE_LANES), lambda i: (i, 0, 0, 0))
    pair_sq = pl.BlockSpec((1, 2 * CHUNK_LANES, 4 * LANES), lambda i: (i, 0, 0))
    return pl.pallas_call(
        functools.partial(_ssm_prep_kernel, n_steps=n_steps),
        grid=(g // 2,),
        in_specs=[vec, vec, vec, mat, mat, mat, mat,
                  pl.BlockSpec((2, 1, CHUNK_LANES), lambda i: (i, 0, 0))],
        out_specs=[pair_sq, pl.BlockSpec((2, CHUNK_LANES, CHUNK_LANES), lambda i: (i, 0, 0)), pair_sq,
                   pl.BlockSpec((1, n_pad, 4 * LANES), lambda i: (i, 0, 0))],
        out_shape=[jax.ShapeDtypeStruct((g // 2, 2 * CHUNK_LANES, 4 * LANES), BF16),
                   jax.ShapeDtypeStruct((g, CHUNK_LANES, CHUNK_LANES), BF16),
                   jax.ShapeDtypeStruct((g // 2, 4 * LANES, 2 * CHUNK_LANES), BF16),
                   jax.ShapeDtypeStruct((g // 2, n_pad, 4 * LANES), F32)],
        compiler_params=_cparams("parallel"),
        name="ssm_prep",
    )(lre, lim, ldt, bt, bts, cm, cms, dsk)


def _complex_step(re, im, sr, si, ar, ai):
    return re + ar * sr - ai * si, im + ar * si + ai * sr


def _ssm_kernel(u_ref, win_ref, m_ref, cout_ref, mult_ref, y_o, loc_scr, car_scr, *, n_ctx_rows, n_steps):
    n_pairs = win_ref.shape[0]
    rows = u_ref.shape[2]
    n_lat_rows = rows - n_ctx_rows
    n_blk = rows // SUBLANES
    n_blk_pad = -(-n_blk // SUBLANES) * SUBLANES
    in_blk_row = lax.broadcasted_iota(jnp.int32, (SUBLANES, LANES), 0)
    blk = lax.broadcasted_iota(jnp.int32, (n_blk_pad, LANES), 0)
    tile = lambda t: jnp.broadcast_to(t[None], (n_blk, SUBLANES, LANES)).reshape(rows, LANES)
    local_steps = SUBLANES.bit_length() - 1
    to_rev = lambda a: jnp.concatenate([a[n_ctx_rows:], a[:n_ctx_rows]], axis=0)

    def shifted(a, s, idx, n, down):
        if down:
            return jnp.where(idx >= s, pltpu.roll(a, s, 0), 0.0)
        return jnp.where(idx < n - s, pltpu.roll(a, a.shape[0] - s, 0), 0.0)

    def scan(pp, x):
        def mult(k0, k1, d):
            return (mult_ref[pp, k0:k1, 2 * d * LANES:(2 * d + 1) * LANES],
                    mult_ref[pp, k0:k1, (2 * d + 1) * LANES:(2 * d + 2) * LANES])

        state = [[x[:, :LANES], x[:, LANES:2 * LANES]],
                 [to_rev(x[:, 2 * LANES:3 * LANES]), to_rev(x[:, 3 * LANES:])]]
        for d, down in ((0, True), (1, False)):
            re, im = state[d]
            for k in range(local_steps):
                s = 1 << k
                keep = in_blk_row >= s if down else in_blk_row < SUBLANES - s
                ar, ai = (tile(jnp.where(keep, m, 0.0)) for m in mult(k, k + 1, d))
                shift = s if down else rows - s
                re, im = _complex_step(re, im, pltpu.roll(re, shift, 0), pltpu.roll(im, shift, 0), ar, ai)
            loc_scr[pp, 2 * d], loc_scr[pp, 2 * d + 1] = re, im
            state[d] = [re, im]

            end_row = SUBLANES - 1 if down else 0
            pad = jnp.zeros((n_blk_pad - n_blk, LANES), F32)
            ere = jnp.concatenate([loc_scr[pp, 2 * d, pl.ds(end_row, n_blk, stride=SUBLANES), :], pad], axis=0)
            eim = jnp.concatenate([loc_scr[pp, 2 * d + 1, pl.ds(end_row, n_blk, stride=SUBLANES), :], pad],
                                  axis=0)
            for k in range(local_steps, n_steps):
                ar, ai = mult(k, k + 1, d)
                s = 1 << (k - local_steps)
                ere, eim = _complex_step(ere, eim, shifted(ere, s, blk, n_blk_pad, down),
                                         shifted(eim, s, blk, n_blk_pad, down), ar, ai)
            for c, ends in enumerate((ere, eim)):
                entering = shifted(ends, 1, blk, n_blk_pad, down)[:n_blk]
                for r in range(SUBLANES):
                    car_scr[pp, 2 * d + c, pl.ds(r, n_blk, stride=SUBLANES), :] = entering

        for d in range(2):
            tre, tim = mult(POWER_ROW0, POWER_ROW0 + SUBLANES, d)
            re, im = state[d]
            state[d] = list(_complex_step(re, im, car_scr[pp, 2 * d], car_scr[pp, 2 * d + 1],
                                          tile(tre), tile(tim)))
        return state

    xs = [jnp.dot(jnp.concatenate([u_ref[0, 2 * pp], u_ref[0, 2 * pp + 1]], axis=1), win_ref[pp],
                  preferred_element_type=F32) for pp in range(n_pairs)]
    prev = lambda a: pltpu.roll(a, 1, 0)[n_ctx_rows:]
    nxt = lambda a: pltpu.roll(a, rows - 1, 0)[:n_lat_rows]
    for pp in range(n_pairs):
        state = scan(pp, xs[pp])
        st = jnp.concatenate([prev(state[0][0]), prev(state[0][1]), nxt(state[1][0]), nxt(state[1][1])],
                             axis=1).astype(BF16)
        y = jnp.dot(st, cout_ref[pp], preferred_element_type=F32)
        for gi in range(2):
            g = 2 * pp + gi
            y_o[0, g] = (y[:, gi * CHUNK_LANES:(gi + 1) * CHUNK_LANES]
                         + jnp.dot(u_ref[0, g, n_ctx_rows:, :], m_ref[g],
                                   preferred_element_type=F32)).astype(BF16)


def _ssm(u_cf, win_pair, m, cout_pair, mult, n_ctx_rows, n_steps):
    bsz, g, rows, _ = u_cf.shape
    pairs = SSM_PAIRS_PER_STEP
    assert rows % SUBLANES == 0 and n_ctx_rows % 16 == 0 and g % (2 * pairs) == 0
    pair_spec = lambda a: pl.BlockSpec((pairs,) + a.shape[1:], lambda b, i: (i, 0, 0))
    return pl.pallas_call(
        functools.partial(_ssm_kernel, n_ctx_rows=n_ctx_rows, n_steps=n_steps),
        grid=(bsz, g // (2 * pairs)),
        in_specs=[pl.BlockSpec((1, 2 * pairs, rows, CHUNK_LANES), lambda b, i: (b, i, 0, 0)),
                  pair_spec(win_pair),
                  pl.BlockSpec((2 * pairs, CHUNK_LANES, CHUNK_LANES), lambda b, i: (i, 0, 0)),
                  pair_spec(cout_pair), pair_spec(mult)],
        out_specs=pl.BlockSpec((1, 2 * pairs, rows - n_ctx_rows, CHUNK_LANES), lambda b, i: (b, i, 0, 0)),
        out_shape=jax.ShapeDtypeStruct((bsz, g, rows - n_ctx_rows, CHUNK_LANES), BF16),
        scratch_shapes=[pltpu.VMEM((pairs, 4, rows, LANES), F32), pltpu.VMEM((pairs, 4, rows, LANES), F32)],
        compiler_params=_cparams("parallel", "parallel"),
        name="ssm",
    )(u_cf, win_pair, m, cout_pair, mult)


def _final_kernel(x_ref, at_ref, ga_ref, y_ref, gs_ref, gate_ref, wglu_ref, bglu_ref, wout_ref,
                  pg_ref, o_ref, y_scr):
    tm = x_ref.shape[1]
    sub = tm // FINAL_SUBTILES
    gain = gate_ref[0] * pg_ref[...]
    def finish(h, out):
        rows = slice(h * sub, (h + 1) * sub)
        ms = jnp.mean(out * out, axis=-1, keepdims=True)
        o_ref[0, rows] = x_ref[0, rows] + out * lax.rsqrt(ms + NORM_EPS) * gain

    pending = None
    for h in range(FINAL_SUBTILES):
        rows = slice(h * sub, (h + 1) * sub)
        _from_chunk_lanes(y_ref, y_scr, h * sub // CHUNK, sub // CHUNK)
        y = jax.nn.gelu(jnp.concatenate([y_scr[col, rows] for col in range(SSM_WIDTH // LANES)], axis=1))
        t = jnp.dot(y.astype(BF16), wglu_ref[...], preferred_element_type=F32) + bglu_ref[...]
        s = (y * jax.nn.sigmoid(t) * gs_ref[0, rows].astype(F32)).astype(BF16)
        a = at_ref[0, rows] * ga_ref[0, rows]
        out = jnp.dot(jnp.concatenate([a, s], axis=1), wout_ref[...], preferred_element_type=F32)
        if pending is not None:
            finish(*pending)
        pending = (h, out)
    finish(*pending)


def _final(x, attn, ga, y_cf, gs, gate, wglu_bf, b_glu, wout_bf, post_g, tm):
    bsz, n, d = x.shape
    assert n % tm == 0 and tm % (CHUNK * 16) == 0
    return pl.pallas_call(
        _final_kernel,
        grid=(bsz, n // tm),
        in_specs=[_row_spec(tm, d), _row_spec(tm, ATTN_WIDTH), _row_spec(tm, ATTN_WIDTH),
                  pl.BlockSpec((1, N_SSM_GROUPS, tm // CHUNK, CHUNK_LANES), lambda b, i: (b, 0, i, 0)),
                  _row_spec(tm, SSM_WIDTH), _batch_vec_spec(d),
                  _const_spec((SSM_WIDTH, SSM_WIDTH)), _const_spec((1, SSM_WIDTH)),
                  _const_spec((d, d)), _const_spec((1, d))],
        out_specs=_row_spec(tm, d),
        out_shape=jax.ShapeDtypeStruct((bsz, n, d), F32),
        scratch_shapes=[pltpu.VMEM((SSM_WIDTH // LANES, tm, LANES), F32)],
        compiler_params=_cparams("parallel", "parallel"),
        name="final",
    )(x, attn, ga, y_cf, gs, gate, wglu_bf, b_glu, wout_bf, post_g)


def _rope_tables(n_ctx, n_lat):
    t = jnp.arange(n_lat, dtype=jnp.int32)
    row_pos = (t // GRID_W).astype(F32)
    col_pos = (t % GRID_W).astype(F32)
    inv_freq = ROPE_THETA ** (-jnp.arange(ROPE_FREQS, dtype=F32) / ROPE_FREQS)
    ang_r = row_pos[:, None] * inv_freq
    ang_c = col_pos[:, None] * inv_freq
    cos = jnp.concatenate([jnp.cos(ang_r)] * 2 + [jnp.cos(ang_c)] * 2, axis=1)
    sin = jnp.concatenate([-jnp.sin(ang_r), jnp.sin(ang_r), -jnp.sin(ang_c), jnp.sin(ang_c)], axis=1)
    cos = jnp.concatenate([jnp.ones((n_ctx, HEAD_DIM), F32), cos], axis=0)
    sin = jnp.concatenate([jnp.zeros((n_ctx, HEAD_DIM), F32), sin], axis=0)
    return jnp.tile(cos, (1, LANES // HEAD_DIM)), jnp.tile(sin, (1, LANES // HEAD_DIM))


def _layer(x, ctx, c, c_ctx, w_ada, b_ada, pre_g, post_g, w_in, q_g, k_g, lam_re, lam_im, log_dt,
           b_re, b_im, c_re, c_im, d_skip, w_glu, b_glu, w_out):
    bsz, n_lat, d = x.shape
    n_ctx = ctx.shape[1]
    assert n_lat % GRID_W == 0

    cvecs = jnp.zeros((8, d), F32).at[:bsz].set(c).at[bsz].set(c_ctx)
    mod = _adaln(cvecs, w_ada, b_ada)
    shift, scale, gate = (mod[:, i * d:(i + 1) * d] for i in range(3))
    lat = lambda a: a[:bsz].reshape(bsz, 1, d)
    cvec = lambda a: a[bsz].reshape(1, 1, d)

    head = jnp.arange(ATTN_WIDTH) // HEAD_DIM
    bd = (head[:, None] == head[None, :]).astype(BF16) * (1.0 / HEAD_DIM)
    qg = jnp.tile(q_g, N_Q_HEADS).reshape(1, ATTN_WIDTH)
    kg = jnp.tile(k_g, N_KV_HEADS).reshape(1, KV_WIDTH)
    cos, sin = _rope_tables(n_ctx, n_lat)

    qt, k_all, vt_aug, ga, u_cf, gs = _inproj(x, ctx, lat(shift), lat(scale), cvec(shift), cvec(scale),
                                              pre_g.reshape(1, d), w_in.astype(BF16), qg, kg, cos, sin,
                                              bd, tm=256)
    attn = _attention(qt, k_all, vt_aug, tq=256, tk=8448, sub=256, lookahead=5, heads_per_pass=8)

    rows = (n_ctx + n_lat) // CHUNK
    n_steps = _scan_steps(rows)
    win_pair, m_op, cout_pair, mult = _ssm_prep(lam_re, lam_im, log_dt, b_re, b_im, c_re, c_im, d_skip,
                                                n_steps)
    y_cf = _ssm(u_cf, win_pair, m_op, cout_pair, mult, n_ctx // CHUNK, n_steps)

    return _final(x, attn, ga, y_cf, gs, lat(gate), w_glu.astype(BF16), b_glu.reshape(1, SSM_WIDTH),
                  w_out.astype(BF16), post_g.reshape(1, d), tm=1024)


def kernel(x, c, ctx, c_ctx, w_ada, b_ada, pre_norm, post_norm, w_in, q_norm, k_norm, ssm_lam_re,
           ssm_lam_im, ssm_log_dt, ssm_b_re, ssm_b_im, ssm_c_re, ssm_c_im, ssm_d, w_glu, b_glu, w_out):
    depth = w_ada.shape[0]
    assert depth == 1, "context stream update between layers is not implemented"
    return _layer(x, ctx, c, c_ctx, w_ada[0], b_ada[0], pre_norm[0], post_norm[0], w_in[0], q_norm[0],
                  k_norm[0], ssm_lam_re[0], ssm_lam_im[0], ssm_log_dt[0], ssm_b_re[0], ssm_b_im[0],
                  ssm_c_re[0], ssm_c_im[0], ssm_d[0], w_glu[0], b_glu[0], w_out[0])
```

```python
import functools
import math

import jax
import jax.numpy as jnp
from jax import lax
from jax.experimental import pallas as pl
from jax.experimental.pallas import tpu as pltpu

F32 = jnp.float32
BF16 = jnp.bfloat16

D_MODEL = 1024
HEAD_DIM = 64
N_Q_HEADS = 8
N_KV_HEADS = 2
ATTN_WIDTH = N_Q_HEADS * HEAD_DIM
KV_WIDTH = N_KV_HEADS * HEAD_DIM
SSM_WIDTH = 512
SSM_GROUP = 16
N_SSM_GROUPS = SSM_WIDTH // SSM_GROUP
SSM_STATE = 64
GRID_W = 64
ROPE_THETA = 10000.0
ROPE_FREQS = 16
NORM_EPS = 1e-6
ATTN_SCALE = HEAD_DIM ** -0.5
LOG2_E = math.log2(math.e)
Q_END = ATTN_WIDTH
K_END = Q_END + KV_WIDTH
V_END = K_END + KV_WIDTH
GA_END = V_END + ATTN_WIDTH
U_END = GA_END + SSM_WIDTH
IN_WIDTH = U_END + SSM_WIDTH

CHUNK = 16
CHUNK_LANES = CHUNK * SSM_GROUP
LANES = 128
GROUPS_PER_VREG = LANES // SSM_GROUP
STATE_LANES = 2 * SSM_STATE
SUBLANES = 8
POWER_ROW0 = 16
MAX_POWER_BITS = 8
SSM_PAIRS_PER_STEP = 2
FINAL_SUBTILES = 4
V_ROWS = HEAD_DIM + 16
VMEM_LIMIT = 48 * 1024 * 1024
NEG_BIG = -1e30


def _cparams(*sem):
    return pltpu.CompilerParams(dimension_semantics=sem, vmem_limit_bytes=VMEM_LIMIT)


def _row_spec(tm, width):
    return pl.BlockSpec((1, tm, width), lambda b, i: (b, i, 0))


def _const_spec(shape):
    return pl.BlockSpec(shape, lambda b, i: (0,) * len(shape))


def _batch_vec_spec(width):
    return pl.BlockSpec((1, 1, width), lambda b, i: (b, 0, 0))


def _adaln_kernel(c_ref, w_ref, b_ref, o_ref):
    c = c_ref[...]
    s = c * jax.nn.sigmoid(c)
    o_ref[...] = jnp.dot(s, w_ref[...], preferred_element_type=F32,
                         precision=lax.Precision.HIGHEST) + b_ref[...]


def _adaln(cvecs, w_ada, b_ada):
    rows, d = cvecs.shape
    n = w_ada.shape[1]
    tn = 1024
    return pl.pallas_call(
        _adaln_kernel,
        grid=(n // tn,),
        in_specs=[pl.BlockSpec((rows, d), lambda j: (0, 0)),
                  pl.BlockSpec((d, tn), lambda j: (0, j)),
                  pl.BlockSpec((1, tn), lambda j: (0, j))],
        out_specs=pl.BlockSpec((rows, tn), lambda j: (0, j)),
        out_shape=jax.ShapeDtypeStruct((rows, n), F32),
        compiler_params=_cparams("arbitrary"),
        name="adaln",
    )(cvecs, w_ada, b_ada.reshape(1, n))


def _lane_group(rows):
    return lax.broadcasted_iota(jnp.int32, (rows, LANES), 1) // SSM_GROUP


def _to_chunk_lanes(u_scr, u_o):
    rt = u_o.shape[2]
    grp = _lane_group(rt)
    for col in range(SSM_WIDTH // LANES):
        rolled = []
        for step in range(CHUNK):
            s = u_scr[col, pl.ds(step, rt, stride=CHUNK), :]
            rolled.append([s if k == 0 else pltpu.roll(s, k * SSM_GROUP, 1)
                           for k in range(GROUPS_PER_VREG)])
        for g_lo in range(GROUPS_PER_VREG):
            for half in range(CHUNK // GROUPS_PER_VREG):
                out = None
                for s8 in range(GROUPS_PER_VREG):
                    piece = rolled[half * GROUPS_PER_VREG + s8][(s8 - g_lo) % GROUPS_PER_VREG]
                    out = piece if out is None else jnp.where(grp == s8, piece, out)
                u_o[0, col * GROUPS_PER_VREG + g_lo, :, half * LANES:(half + 1) * LANES] = out.astype(BF16)


def _from_chunk_lanes(y_ref, y_scr, r0, rt):
    grp = _lane_group(rt)
    for col in range(SSM_WIDTH // LANES):
        rolled = {}
        for g_lo in range(GROUPS_PER_VREG):
            for half in range(CHUNK // GROUPS_PER_VREG):
                s = y_ref[0, col * GROUPS_PER_VREG + g_lo, r0:r0 + rt,
                          half * LANES:(half + 1) * LANES].astype(F32)
                rolled[g_lo, half] = [s if k == 0 else pltpu.roll(s, k * SSM_GROUP, 1)
                                      for k in range(GROUPS_PER_VREG)]
        for step in range(CHUNK):
            half, s8 = divmod(step, GROUPS_PER_VREG)
            out = None
            for g_lo in range(GROUPS_PER_VREG):
                piece = rolled[g_lo, half][(g_lo - s8) % GROUPS_PER_VREG]
                out = piece if out is None else jnp.where(grp == g_lo, piece, out)
            y_scr[col, pl.ds(r0 * CHUNK + step, rt, stride=CHUNK), :] = out


def _head_mean_sq(z, bd):
    return jnp.dot((z * z).astype(BF16), bd, preferred_element_type=F32)


def _swap16(x):
    w = x.shape[1]
    lane = lax.broadcasted_iota(jnp.int32, x.shape, 1)
    return jnp.where((lane & 16) == 0, pltpu.roll(x, w - 16, 1), pltpu.roll(x, 16, 1))


def _rope(x, cos, sin_signed):
    cols = []
    for c in range(x.shape[1] // LANES):
        xc = x[:, c * LANES:(c + 1) * LANES]
        cols.append(xc * cos + _swap16(xc) * sin_signed)
    return cols[0] if len(cols) == 1 else jnp.concatenate(cols, axis=1)


def _silu(z):
    return z * jax.nn.sigmoid(z)


def _inproj_kernel(x_ref, c_ref, shl_ref, scl_ref, shc_ref, scc_ref, pg_ref, w_ref, qg_ref, kg_ref,
                   cos_ref, sin_ref, bd_ref, qt_o, k_o, vt_o, ga_o, u_o, gs_o, v_scr, u_scr, *, n_ctx_tiles):
    is_ctx = pl.program_id(1) < n_ctx_tiles
    x = jnp.where(is_ctx, c_ref[0], x_ref[0])
    shift = jnp.where(is_ctx, shc_ref[0], shl_ref[0])
    scale = jnp.where(is_ctx, scc_ref[0], scl_ref[0])
    ms = jnp.mean(x * x, axis=-1, keepdims=True)
    xn = x * lax.rsqrt(ms + NORM_EPS) * pg_ref[...]
    h = (xn * (1.0 + scale) + shift).astype(BF16)

    def proj(a, b):
        return jnp.dot(h, w_ref[:, a:b], preferred_element_type=F32)

    u = proj(GA_END, U_END)
    for col in range(SSM_WIDTH // LANES):
        u_scr[col] = u[:, col * LANES:(col + 1) * LANES]
    _to_chunk_lanes(u_scr, u_o)

    cos = cos_ref[...]
    sin = sin_ref[...]
    zq = proj(0, Q_END)
    zk = proj(Q_END, K_END)
    gs_o[0] = _silu(proj(U_END, IN_WIDTH)).astype(BF16)
    qn = zq * lax.rsqrt(_head_mean_sq(zq, bd_ref[...]) + NORM_EPS) * qg_ref[...]
    kn = zk * lax.rsqrt(_head_mean_sq(zk, bd_ref[:KV_WIDTH, :KV_WIDTH]) + NORM_EPS) * kg_ref[...]
    ga_o[0] = _silu(proj(V_END, GA_END)).astype(BF16)
    v_scr[...] = proj(K_END, V_END)

    k_o[0] = _rope(kn, cos, sin).astype(BF16)
    q = _rope(qn, cos, sin) * (ATTN_SCALE * LOG2_E)
    group = N_Q_HEADS // N_KV_HEADS
    none = jnp.zeros((HEAD_DIM, q.shape[0]), BF16)
    for c in range(ATTN_WIDTH // LANES):
        qt = q[:, c * LANES:(c + 1) * LANES].T.astype(BF16)
        for par in range(2):
            head = 2 * c + par
            kv = head // group
            qt_o[0, head, kv * HEAD_DIM:(kv + 1) * HEAD_DIM, :] = qt[par * HEAD_DIM:(par + 1) * HEAD_DIM]
            qt_o[0, head, (1 - kv) * HEAD_DIM:(2 - kv) * HEAD_DIM, :] = none

    vt = v_scr[...].T.astype(BF16)
    tm = vt.shape[1]
    ones_row = lax.broadcasted_iota(jnp.int32, (V_ROWS - HEAD_DIM, tm), 0) == 0
    for j in range(N_KV_HEADS):
        vt_o[0, j, :HEAD_DIM, :] = vt[j * HEAD_DIM:(j + 1) * HEAD_DIM]
        vt_o[0, j, HEAD_DIM:, :] = jnp.where(ones_row, 1.0, 0.0).astype(BF16)


def _inproj(x, ctx, shift_l, scale_l, shift_c, scale_c, pre_g, w_bf, qg, kg, cos, sin, bd, tm):
    bsz, n_lat, d = x.shape
    n_ctx = ctx.shape[1]
    assert n_ctx % tm == 0 and n_lat % tm == 0 and tm % (CHUNK * 16) == 0
    nct = n_ctx // tm
    n_tot = n_ctx + n_lat
    lat_rows = lambda w: pl.BlockSpec((1, tm, w), lambda b, i: (b, jnp.maximum(i - nct, 0), 0))
    all_rows = lambda w: pl.BlockSpec((1, tm, w), lambda b, i: (b, i, 0))
    ctx_vec = pl.BlockSpec((1, 1, d), lambda b, i: (0, 0, 0))
    table = pl.BlockSpec((tm, LANES), lambda b, i: (i, 0))
    return pl.pallas_call(
        functools.partial(_inproj_kernel, n_ctx_tiles=nct),
        grid=(bsz, n_tot // tm),
        in_specs=[lat_rows(d),
                  pl.BlockSpec((1, tm, d), lambda b, i: (b, jnp.minimum(i, nct - 1), 0)),
                  _batch_vec_spec(d), _batch_vec_spec(d), ctx_vec, ctx_vec, _const_spec((1, d)),
                  _const_spec((d, IN_WIDTH)), _const_spec((1, ATTN_WIDTH)), _const_spec((1, KV_WIDTH)),
                  table, table, _const_spec((ATTN_WIDTH, ATTN_WIDTH))],
        out_specs=[pl.BlockSpec((1, N_Q_HEADS, LANES, tm), lambda b, i: (b, 0, 0, jnp.maximum(i - nct, 0))),
                   all_rows(KV_WIDTH),
                   pl.BlockSpec((1, N_KV_HEADS, V_ROWS, tm), lambda b, i: (b, 0, 0, i)),
                   lat_rows(ATTN_WIDTH),
                   pl.BlockSpec((1, N_SSM_GROUPS, tm // CHUNK, CHUNK_LANES), lambda b, i: (b, 0, i, 0)),
                   lat_rows(SSM_WIDTH)],
        out_shape=[jax.ShapeDtypeStruct((bsz, N_Q_HEADS, LANES, n_lat), BF16),
                   jax.ShapeDtypeStruct((bsz, n_tot, KV_WIDTH), BF16),
                   jax.ShapeDtypeStruct((bsz, N_KV_HEADS, V_ROWS, n_tot), BF16),
                   jax.ShapeDtypeStruct((bsz, n_lat, ATTN_WIDTH), BF16),
                   jax.ShapeDtypeStruct((bsz, N_SSM_GROUPS, n_tot // CHUNK, CHUNK_LANES), BF16),
                   jax.ShapeDtypeStruct((bsz, n_lat, SSM_WIDTH), BF16)],
        scratch_shapes=[pltpu.VMEM((tm, KV_WIDTH), F32), pltpu.VMEM((SSM_WIDTH // LANES, tm, LANES), F32)],
        compiler_params=_cparams("parallel", "arbitrary"),
        name="inproj",
    )(x, ctx, shift_l, scale_l, shift_c, scale_c, pre_g, w_bf, qg, kg, cos, sin, bd)


def _attn_kernel(qt_ref, k_ref, vt_ref, o_ref, *, tk, sub, lookahead, heads_per_pass):
    tq = qt_ref.shape[3]
    n_keys = k_ref.shape[1]
    group = N_Q_HEADS // N_KV_HEADS

    outs = []
    for h0 in range(0, N_Q_HEADS, heads_per_pass):
        heads = list(range(h0, h0 + heads_per_pass))
        q_wide = [qt_ref[0, h] for h in heads]

        def body(t, carry, heads=heads, q_wide=q_wide):
            tasks = [(j, i) for j in range(tk // sub) for i in range(len(heads))]
            state = list(carry)
            scores = {}

            def keys_at(j):
                return pl.ds(pl.multiple_of(t * tk + j * sub, sub), sub)

            def issue(n):
                j, i = tasks[n]
                scores[n] = jnp.dot(k_ref[0, keys_at(j), :], q_wide[i],
                                    preferred_element_type=F32)

            def consume(n):
                j, i = tasks[n]
                s = scores.pop(n)
                m_old, acc = state[i]
                vt = vt_ref[0, heads[i] // group, :, keys_at(j)]
                m_new = jnp.maximum(m_old, jnp.max(s, axis=0, keepdims=True))
                p = jnp.exp2((s - m_new).astype(BF16))
                acc = jnp.exp2(m_old - m_new) * acc + jnp.dot(vt, p, preferred_element_type=F32)
                state[i] = (m_new, acc)

            for n in range(len(tasks) + lookahead):
                if n < len(tasks):
                    issue(n)
                if n >= lookahead:
                    consume(n - lookahead)
            return tuple(state)

        init = tuple((jnp.full((1, tq), NEG_BIG, F32), jnp.zeros((V_ROWS, tq), F32)) for _ in heads)
        for _, acc in lax.fori_loop(0, n_keys // tk, body, init):
            outs.append(acc[:HEAD_DIM] / acc[HEAD_DIM:HEAD_DIM + 1])

    for c in range(N_Q_HEADS // 2):
        pair = jnp.concatenate([outs[2 * c], outs[2 * c + 1]], axis=0)
        o_ref[0, :, c * LANES:(c + 1) * LANES] = pair.T.astype(BF16)


def _attention(qt, k_all, vt_aug, tq, tk, sub, lookahead, heads_per_pass):
    bsz, _, _, n = qt.shape
    n_keys = k_all.shape[1]
    assert n_keys % tk == 0 and tk % sub == 0 and n % tq == 0
    return pl.pallas_call(
        functools.partial(_attn_kernel, tk=tk, sub=sub, lookahead=lookahead,
                          heads_per_pass=heads_per_pass),
        grid=(bsz, n // tq),
        in_specs=[pl.BlockSpec((1, N_Q_HEADS, LANES, tq), lambda b, i: (b, 0, 0, i)),
                  pl.BlockSpec((1, n_keys, KV_WIDTH), lambda b, i: (b, 0, 0)),
                  pl.BlockSpec((1, N_KV_HEADS, V_ROWS, n_keys), lambda b, i: (b, 0, 0, 0))],
        out_specs=_row_spec(tq, ATTN_WIDTH),
        out_shape=jax.ShapeDtypeStruct((bsz, n, ATTN_WIDTH), BF16),
        compiler_params=_cparams("parallel", "parallel"),
        name="attention",
    )(qt, k_all, vt_aug)


def _scan_steps(n_rows):
    return max(1, math.ceil(math.log2(n_rows)))


def _ssm_prep_kernel(lre_ref, lim_ref, ldt_ref, bt_ref, bts_ref, c_ref, cs_ref, d_ref,
                     win_o, m_o, cout_o, mult_o, *, n_steps):
    lane = lax.broadcasted_iota(jnp.int32, (1, STATE_LANES), 1)
    low = lane < SSM_STATE
    sign_lo = jnp.where(low, -1.0, 1.0)
    k_idx = lax.broadcasted_iota(jnp.int32, (CHUNK, STATE_LANES), 0)
    step_of_row = lax.broadcasted_iota(jnp.int32, (CHUNK_LANES, STATE_LANES), 0) // SSM_GROUP
    nt = (((1,), (1,)), ((), ()))

    def outer(pw, pws, mat, mats):
        full = pw[:, None, :] * mat[None, :, :] + pws[:, None, :] * mats[None, :, :]
        return full.reshape(CHUNK * mat.shape[0], STATE_LANES)

    state_in, state_out, mults, tables = {}, {}, {}, {}
    for gi in range(2):
        taps_lo, taps_hi = None, None
        for d in range(2):
            lre = lre_ref[gi, d:d + 1, :]
            lim = lim_ref[gi, d:d + 1, :]
            dt = jnp.exp(ldt_ref[gi, d:d + 1, :])

            mag = jnp.exp(dt * lre)
            a_re, a_im = mag * jnp.cos(dt * lim), mag * jnp.sin(dt * lim)
            squares = [(a_re, a_im)]
            for _ in range(MAX_POWER_BITS - 1):
                pr, pi = squares[-1]
                squares.append((pr * pr - pi * pi, 2.0 * pr * pi))

            def power(expo, squares=squares):
                re = jnp.ones(expo.shape, F32)
                im = jnp.zeros(expo.shape, F32)
                for b, (pr, pi) in enumerate(squares):
                    take = ((expo >> b) & 1) == 1
                    re, im = jnp.where(take, re * pr - im * pi, re), jnp.where(take, re * pi + im * pr, im)
                return re, im * sign_lo

            nr, ni = a_re - 1.0, a_im
            den = lre * lre + lim * lim
            cr = (nr * lre + ni * lim) / den
            ci = (ni * lre - nr * lim) / den
            cis = ci * sign_lo
            bbar = cr * bt_ref[gi, d] + cis * bts_ref[gi, d]
            bbar_s = cr * bts_ref[gi, d] - cis * bt_ref[gi, d]
            cmat, cmat_s = c_ref[gi, d], cs_ref[gi, d]

            asc, ascs = power(k_idx)
            desc, descs = power(CHUNK - 1 - k_idx)
            nxt, nxts = power(k_idx + 1)
            conj_c = lambda pw, pws: outer(pw, pws, cmat, cmat_s) * (-sign_lo)
            if d == 0:
                state_in[gi, d] = outer(desc, descs, bbar, bbar_s)
                state_out[gi, d] = conj_c(nxt, nxts)
                lag0 = jnp.where(step_of_row == CHUNK - 1, conj_c(desc, descs), 0.0)
                lags = jnp.where(step_of_row < CHUNK - 1, conj_c(nxt, nxts), 0.0)
                hi = lax.dot_general(bbar, lags, nt, preferred_element_type=F32,
                                     precision=lax.Precision.HIGHEST)
                lo = lax.dot_general(bbar, lag0, nt, preferred_element_type=F32,
                                     precision=lax.Precision.HIGHEST)
                taps_hi = hi
                taps_lo = lo if taps_lo is None else taps_lo + lo
            else:
                state_in[gi, d] = outer(asc, ascs, bbar, bbar_s)
                rev, revs = power(CHUNK - k_idx)
                state_out[gi, d] = conj_c(rev, revs)
                lo = lax.dot_general(bbar, conj_c(desc, descs), nt, preferred_element_type=F32,
                                     precision=lax.Precision.HIGHEST)
                taps_lo = lo if taps_lo is None else taps_lo + lo

            sr, sis = power(jnp.full((1, STATE_LANES), CHUNK, jnp.int32))
            chain = []
            for k in range(n_steps):
                chain.append((sr, sis))
                si = sis * sign_lo
                sr, sis = sr * sr - si * si, 2.0 * sr * si * sign_lo
            mults[gi, d] = chain
            r_idx = lax.broadcasted_iota(jnp.int32, (SUBLANES, STATE_LANES), 0)
            tables[gi, d] = power(CHUNK * (r_idx + 1) if d == 0 else CHUNK * (SUBLANES - r_idx))

        qq = lax.broadcasted_iota(jnp.int32, (SSM_GROUP, CHUNK_LANES), 0)
        ll = lax.broadcasted_iota(jnp.int32, (SSM_GROUP, CHUNK_LANES), 1)
        taps_lo = taps_lo + jnp.where(ll == qq + (CHUNK - 1) * SSM_GROUP, d_ref[gi], 0.0)
        taps = jnp.concatenate([taps_lo, taps_hi], axis=1)
        for j in range(CHUNK):
            off = (CHUNK - 1 - j) * SSM_GROUP
            m_o[gi, j * SSM_GROUP:(j + 1) * SSM_GROUP, :] = taps[:, off:off + CHUNK_LANES].astype(BF16)

    def swap(x):
        return pltpu.roll(x, SSM_STATE, 1)

    mult_o[...] = jnp.zeros(mult_o.shape, F32)
    wide_low = lax.broadcasted_iota(jnp.int32, (CHUNK_LANES, STATE_LANES), 1) < SSM_STATE
    zeros_c = jnp.zeros((SSM_STATE, CHUNK_LANES), F32)
    for d in range(2):
        s0, s1 = state_in[0, d], state_in[1, d]
        re_blk = jnp.concatenate([jnp.where(wide_low, s0, 0.0), jnp.where(wide_low, 0.0, swap(s1))], axis=0)
        im_blk = jnp.concatenate([jnp.where(wide_low, swap(s0), 0.0), jnp.where(wide_low, 0.0, s1)], axis=0)
        win_o[0, :, (2 * d) * LANES:(2 * d + 1) * LANES] = re_blk.astype(BF16)
        win_o[0, :, (2 * d + 1) * LANES:(2 * d + 2) * LANES] = im_blk.astype(BF16)
        t0, t1 = state_out[0, d].T, state_out[1, d].T
        for part in range(2):
            rows = slice(part * SSM_STATE, (part + 1) * SSM_STATE)
            blk = jnp.concatenate([jnp.concatenate([t0[rows], zeros_c], axis=1),
                                   jnp.concatenate([zeros_c, t1[rows]], axis=1)], axis=0)
            cout_o[0, (2 * d + part) * LANES:(2 * d + part + 1) * LANES, :] = blk.astype(BF16)
        for k in range(n_steps):
            (sr0, sis0), (sr1, sis1) = mults[0, d][k], mults[1, d][k]
            mult_o[0, k:k + 1, (2 * d) * LANES:(2 * d + 1) * LANES] = jnp.where(low, sr0, sr1)
            mult_o[0, k:k + 1, (2 * d + 1) * LANES:(2 * d + 2) * LANES] = jnp.where(low, -sis0, sis1)
        (tr0, tis0), (tr1, tis1) = tables[0, d], tables[1, d]
        mult_o[0, POWER_ROW0:POWER_ROW0 + SUBLANES, (2 * d) * LANES:(2 * d + 1) * LANES] = jnp.where(low, tr0, tr1)
        mult_o[0, POWER_ROW0:POWER_ROW0 + SUBLANES, (2 * d + 1) * LANES:(2 * d + 2) * LANES] = jnp.where(
            low, -tis0, tis1)


def _ssm_prep(lam_re, lam_im, log_dt, b_re, b_im, c_re, c_im, d_skip, n_steps):
    g = N_SSM_GROUPS
    dup = lambda a: jnp.concatenate([a, a], axis=-1)
    lre = dup(jnp.swapaxes(lam_re, 0, 1))
    lim = dup(jnp.swapaxes(lam_im, 0, 1))
    ldt = jnp.broadcast_to(jnp.swapaxes(log_dt, 0, 1)[..., None], (g, 2, STATE_LANES))
    btr = jnp.transpose(b_re, (1, 0, 3, 2))
    bti = jnp.transpose(b_im, (1, 0, 3, 2))
    bt, bts = jnp.concatenate([btr, bti], -1), jnp.concatenate([bti, btr], -1)
    cr, ci = jnp.swapaxes(c_re, 0, 1), jnp.swapaxes(c_im, 0, 1)
    cm, cms = jnp.concatenate([cr, ci], -1), jnp.concatenate([ci, cr], -1)
    dsk = jnp.pad(d_skip, ((0, 0), (CHUNK_LANES - SSM_GROUP, 0))).reshape(g, 1, CHUNK_LANES)
    n_pad = POWER_ROW0 + SUBLANES
    vec = pl.BlockSpec((2, 2, STATE_LANES), lambda i: (i, 0, 0))
    mat = pl.BlockSpec((2, 2, SSM_GROUP, STATE_LANES), lambda i: (i, 0, 0, 0))
    pair_sq = pl.BlockSpec((1, 2 * CHUNK_LANES, 4 * LANES), lambda i: (i, 0, 0))
    return pl.pallas_call(
        functools.partial(_ssm_prep_kernel, n_steps=n_steps),
        grid=(g // 2,),
        in_specs=[vec, vec, vec, mat, mat, mat, mat,
                  pl.BlockSpec((2, 1, CHUNK_LANES), lambda i: (i, 0, 0))],
        out_specs=[pair_sq, pl.BlockSpec((2, CHUNK_LANES, CHUNK_LANES), lambda i: (i, 0, 0)), pair_sq,
                   pl.BlockSpec((1, n_pad, 4 * LANES), lambda i: (i, 0, 0))],
        out_shape=[jax.ShapeDtypeStruct((g // 2, 2 * CHUNK_LANES, 4 * LANES), BF16),
                   jax.ShapeDtypeStruct((g, CHUNK_LANES, CHUNK_LANES), BF16),
                   jax.ShapeDtypeStruct((g // 2, 4 * LANES, 2 * CHUNK_LANES), BF16),
                   jax.ShapeDtypeStruct((g // 2, n_pad, 4 * LANES), F32)],
        compiler_params=_cparams("parallel"),
        name="ssm_prep",
    )(lre, lim, ldt, bt, bts, cm, cms, dsk)


def _complex_step(re, im, sr, si, ar, ai):
    return re + ar * sr - ai * si, im + ar * si + ai * sr


def _ssm_kernel(u_ref, win_ref, m_ref, cout_ref, mult_ref, y_o, loc_scr, car_scr, *, n_ctx_rows, n_steps):
    n_pairs = win_ref.shape[0]
    rows = u_ref.shape[2]
    n_lat_rows = rows - n_ctx_rows
    n_blk = rows // SUBLANES
    n_blk_pad = -(-n_blk // SUBLANES) * SUBLANES
    in_blk_row = lax.broadcasted_iota(jnp.int32, (SUBLANES, LANES), 0)
    blk = lax.broadcasted_iota(jnp.int32, (n_blk_pad, LANES), 0)
    tile = lambda t: jnp.broadcast_to(t[None], (n_blk, SUBLANES, LANES)).reshape(rows, LANES)
    local_steps = SUBLANES.bit_length() - 1
    to_rev = lambda a: jnp.concatenate([a[n_ctx_rows:], a[:n_ctx_rows]], axis=0)

    def shifted(a, s, idx, n, down):
        if down:
            return jnp.where(idx >= s, pltpu.roll(a, s, 0), 0.0)
        return jnp.where(idx < n - s, pltpu.roll(a, a.shape[0] - s, 0), 0.0)

    def scan(pp, x):
        def mult(k0, k1, d):
            return (mult_ref[pp, k0:k1, 2 * d * LANES:(2 * d + 1) * LANES],
                    mult_ref[pp, k0:k1, (2 * d + 1) * LANES:(2 * d + 2) * LANES])

        state = [[x[:, :LANES], x[:, LANES:2 * LANES]],
                 [to_rev(x[:, 2 * LANES:3 * LANES]), to_rev(x[:, 3 * LANES:])]]
        for d, down in ((0, True), (1, False)):
            re, im = state[d]
            for k in range(local_steps):
                s = 1 << k
                keep = in_blk_row >= s if down else in_blk_row < SUBLANES - s
                ar, ai = (tile(jnp.where(keep, m, 0.0)) for m in mult(k, k + 1, d))
                shift = s if down else rows - s
                re, im = _complex_step(re, im, pltpu.roll(re, shift, 0), pltpu.roll(im, shift, 0), ar, ai)
            loc_scr[pp, 2 * d], loc_scr[pp, 2 * d + 1] = re, im
            state[d] = [re, im]

            end_row = SUBLANES - 1 if down else 0
            pad = jnp.zeros((n_blk_pad - n_blk, LANES), F32)
            ere = jnp.concatenate([loc_scr[pp, 2 * d, pl.ds(end_row, n_blk, stride=SUBLANES), :], pad], axis=0)
            eim = jnp.concatenate([loc_scr[pp, 2 * d + 1, pl.ds(end_row, n_blk, stride=SUBLANES), :], pad],
                                  axis=0)
            for k in range(local_steps, n_steps):
                ar, ai = mult(k, k + 1, d)
                s = 1 << (k - local_steps)
                ere, eim = _complex_step(ere, eim, shifted(ere, s, blk, n_blk_pad, down),
                                         shifted(eim, s, blk, n_blk_pad, down), ar, ai)
            for c, ends in enumerate((ere, eim)):
                entering = shifted(ends, 1, blk, n_blk_pad, down)[:n_blk]
                for r in range(SUBLANES):
                    car_scr[pp, 2 * d + c, pl.ds(r, n_blk, stride=SUBLANES), :] = entering

        for d in range(2):
            tre, tim = mult(POWER_ROW0, POWER_ROW0 + SUBLANES, d)
            re, im = state[d]
            state[d] = list(_complex_step(re, im, car_scr[pp, 2 * d], car_scr[pp, 2 * d + 1],
                                          tile(tre), tile(tim)))
        return state

    xs = [jnp.dot(jnp.concatenate([u_ref[0, 2 * pp], u_ref[0, 2 * pp + 1]], axis=1), win_ref[pp],
                  preferred_element_type=F32) for pp in range(n_pairs)]
    prev = lambda a: pltpu.roll(a, 1, 0)[n_ctx_rows:]
    nxt = lambda a: pltpu.roll(a, rows - 1, 0)[:n_lat_rows]
    for pp in range(n_pairs):
        state = scan(pp, xs[pp])
        st = jnp.concatenate([prev(state[0][0]), prev(state[0][1]), nxt(state[1][0]), nxt(state[1][1])],
                             axis=1).astype(BF16)
        y = jnp.dot(st, cout_ref[pp], preferred_element_type=F32)
        for gi in range(2):
            g = 2 * pp + gi
            y_o[0, g] = (y[:, gi * CHUNK_LANES:(gi + 1) * CHUNK_LANES]
                         + jnp.dot(u_ref[0, g, n_ctx_rows:, :], m_ref[g],
                                   preferred_element_type=F32)).astype(BF16)


def _ssm(u_cf, win_pair, m, cout_pair, mult, n_ctx_rows, n_steps):
    bsz, g, rows, _ = u_cf.shape
    pairs = SSM_PAIRS_PER_STEP
    assert rows % SUBLANES == 0 and n_ctx_rows % 16 == 0 and g % (2 * pairs) == 0
    pair_spec = lambda a: pl.BlockSpec((pairs,) + a.shape[1:], lambda b, i: (i, 0, 0))
    return pl.pallas_call(
        functools.partial(_ssm_kernel, n_ctx_rows=n_ctx_rows, n_steps=n_steps),
        grid=(bsz, g // (2 * pairs)),
        in_specs=[pl.BlockSpec((1, 2 * pairs, rows, CHUNK_LANES), lambda b, i: (b, i, 0, 0)),
                  pair_spec(win_pair),
                  pl.BlockSpec((2 * pairs, CHUNK_LANES, CHUNK_LANES), lambda b, i: (i, 0, 0)),
                  pair_spec(cout_pair), pair_spec(mult)],
        out_specs=pl.BlockSpec((1, 2 * pairs, rows - n_ctx_rows, CHUNK_LANES), lambda b, i: (b, i, 0, 0)),
        out_shape=jax.ShapeDtypeStruct((bsz, g, rows - n_ctx_rows, CHUNK_LANES), BF16),
        scratch_shapes=[pltpu.VMEM((pairs, 4, rows, LANES), F32), pltpu.VMEM((pairs, 4, rows, LANES), F32)],
        compiler_params=_cparams("parallel", "parallel"),
        name="ssm",
    )(u_cf, win_pair, m, cout_pair, mult)


def _final_kernel(x_ref, at_ref, ga_ref, y_ref, gs_ref, gate_ref, wglu_ref, bglu_ref, wout_ref,
                  pg_ref, o_ref, y_scr):
    tm = x_ref.shape[1]
    sub = tm // FINAL_SUBTILES
    gain = gate_ref[0] * pg_ref[...]
    def finish(h, out):
        rows = slice(h * sub, (h + 1) * sub)
        ms = jnp.mean(out * out, axis=-1, keepdims=True)
        o_ref[0, rows] = x_ref[0, rows] + out * lax.rsqrt(ms + NORM_EPS) * gain

    pending = None
    for h in range(FINAL_SUBTILES):
        rows = slice(h * sub, (h + 1) * sub)
        _from_chunk_lanes(y_ref, y_scr, h * sub // CHUNK, sub // CHUNK)
        y = jax.nn.gelu(jnp.concatenate([y_scr[col, rows] for col in range(SSM_WIDTH // LANES)], axis=1))
        t = jnp.dot(y.astype(BF16), wglu_ref[...], preferred_element_type=F32) + bglu_ref[...]
        s = (y * jax.nn.sigmoid(t) * gs_ref[0, rows].astype(F32)).astype(BF16)
        a = at_ref[0, rows] * ga_ref[0, rows]
        out = jnp.dot(jnp.concatenate([a, s], axis=1), wout_ref[...], preferred_element_type=F32)
        if pending is not None:
            finish(*pending)
        pending = (h, out)
    finish(*pending)


def _final(x, attn, ga, y_cf, gs, gate, wglu_bf, b_glu, wout_bf, post_g, tm):
    bsz, n, d = x.shape
    assert n % tm == 0 and tm % (CHUNK * 16) == 0
    return pl.pallas_call(
        _final_kernel,
        grid=(bsz, n // tm),
        in_specs=[_row_spec(tm, d), _row_spec(tm, ATTN_WIDTH), _row_spec(tm, ATTN_WIDTH),
                  pl.BlockSpec((1, N_SSM_GROUPS, tm // CHUNK, CHUNK_LANES), lambda b, i: (b, 0, i, 0)),
                  _row_spec(tm, SSM_WIDTH), _batch_vec_spec(d),
                  _const_spec((SSM_WIDTH, SSM_WIDTH)), _const_spec((1, SSM_WIDTH)),
                  _const_spec((d, d)), _const_spec((1, d))],
        out_specs=_row_spec(tm, d),
        out_shape=jax.ShapeDtypeStruct((bsz, n, d), F32),
        scratch_shapes=[pltpu.VMEM((SSM_WIDTH // LANES, tm, LANES), F32)],
        compiler_params=_cparams("parallel", "parallel"),
        name="final",
    )(x, attn, ga, y_cf, gs, gate, wglu_bf, b_glu, wout_bf, post_g)


def _rope_tables(n_ctx, n_lat):
    t = jnp.arange(n_lat, dtype=jnp.int32)
    row_pos = (t // GRID_W).astype(F32)
    col_pos = (t % GRID_W).astype(F32)
    inv_freq = ROPE_THETA ** (-jnp.arange(ROPE_FREQS, dtype=F32) / ROPE_FREQS)
    ang_r = row_pos[:, None] * inv_freq
    ang_c = col_pos[:, None] * inv_freq
    cos = jnp.concatenate([jnp.cos(ang_r)] * 2 + [jnp.cos(ang_c)] * 2, axis=1)
    sin = jnp.concatenate([-jnp.sin(ang_r), jnp.sin(ang_r), -jnp.sin(ang_c), jnp.sin(ang_c)], axis=1)
    cos = jnp.concatenate([jnp.ones((n_ctx, HEAD_DIM), F32), cos], axis=0)
    sin = jnp.concatenate([jnp.zeros((n_ctx, HEAD_DIM), F32), sin], axis=0)
    return jnp.tile(cos, (1, LANES // HEAD_DIM)), jnp.tile(sin, (1, LANES // HEAD_DIM))


def _layer(x, ctx, c, c_ctx, w_ada, b_ada, pre_g, post_g, w_in, q_g, k_g, lam_re, lam_im, log_dt,
           b_re, b_im, c_re, c_im, d_skip, w_glu, b_glu, w_out):
    bsz, n_lat, d = x.shape
    n_ctx = ctx.shape[1]
    assert n_lat % GRID_W == 0

    cvecs = jnp.zeros((8, d), F32).at[:bsz].set(c).at[bsz].set(c_ctx)
    mod = _adaln(cvecs, w_ada, b_ada)
    shift, scale, gate = (mod[:, i * d:(i + 1) * d] for i in range(3))
    lat = lambda a: a[:bsz].reshape(bsz, 1, d)
    cvec = lambda a: a[bsz].reshape(1, 1, d)

    head = jnp.arange(ATTN_WIDTH) // HEAD_DIM
    bd = (head[:, None] == head[None, :]).astype(BF16) * (1.0 / HEAD_DIM)
    qg = jnp.tile(q_g, N_Q_HEADS).reshape(1, ATTN_WIDTH)
    kg = jnp.tile(k_g, N_KV_HEADS).reshape(1, KV_WIDTH)
    cos, sin = _rope_tables(n_ctx, n_lat)

    qt, k_all, vt_aug, ga, u_cf, gs = _inproj(x, ctx, lat(shift), lat(scale), cvec(shift), cvec(scale),
                                              pre_g.reshape(1, d), w_in.astype(BF16), qg, kg, cos, sin,
                                              bd, tm=256)
    attn = _attention(qt, k_all, vt_aug, tq=256, tk=8448, sub=256, lookahead=5, heads_per_pass=8)

    rows = (n_ctx + n_lat) // CHUNK
    n_steps = _scan_steps(rows)
    win_pair, m_op, cout_pair, mult = _ssm_prep(lam_re, lam_im, log_dt, b_re, b_im, c_re, c_im, d_skip,
                                                n_steps)
    y_cf = _ssm(u_cf, win_pair, m_op, cout_pair, mult, n_ctx // CHUNK, n_steps)

    return _final(x, attn, ga, y_cf, gs, lat(gate), w_glu.astype(BF16), b_glu.reshape(1, SSM_WIDTH),
                  w_out.astype(BF16), post_g.reshape(1, d), tm=1024)


def kernel(x, c, ctx, c_ctx, w_ada, b_ada, pre_norm, post_norm, w_in, q_norm, k_norm, ssm_lam_re,
           ssm_lam_im, ssm_log_dt, ssm_b_re, ssm_b_im, ssm_c_re, ssm_c_im, ssm_d, w_glu, b_glu, w_out):
    depth = w_ada.shape[0]
    assert depth == 1, "context stream update between layers is not implemented"
    return _layer(x, ctx, c, c_ctx, w_ada[0], b_ada[0], pre_norm[0], post_norm[0], w_in[0], q_norm[0],
                  k_norm[0], ssm_lam_re[0], ssm_lam_im[0], ssm_log_dt[0], ssm_b_re[0], ssm_b_im[0],
                  ssm_c_re[0], ssm_c_im[0], ssm_d[0], w_glu[0], b_glu[0], w_out[0])
```

```python
import functools
import math

import jax
import jax.numpy as jnp
from jax import lax
from jax.experimental import pallas as pl
from jax.experimental.pallas import tpu as pltpu

F32 = jnp.float32
BF16 = jnp.bfloat16

D_MODEL = 1024
HEAD_DIM = 64
N_Q_HEADS = 8
N_KV_HEADS = 2
ATTN_WIDTH = N_Q_HEADS * HEAD_DIM
KV_WIDTH = N_KV_HEADS * HEAD_DIM
SSM_WIDTH = 512
SSM_GROUP = 16
N_SSM_GROUPS = SSM_WIDTH // SSM_GROUP
SSM_STATE = 64
GRID_W = 64
ROPE_THETA = 10000.0
ROPE_FREQS = 16
NORM_EPS = 1e-6
ATTN_SCALE = HEAD_DIM ** -0.5
LOG2_E = math.log2(math.e)
Q_END = ATTN_WIDTH
K_END = Q_END + KV_WIDTH
V_END = K_END + KV_WIDTH
GA_END = V_END + ATTN_WIDTH
U_END = GA_END + SSM_WIDTH
IN_WIDTH = U_END + SSM_WIDTH

CHUNK = 16
CHUNK_LANES = CHUNK * SSM_GROUP
LANES = 128
GROUPS_PER_VREG = LANES // SSM_GROUP
STATE_LANES = 2 * SSM_STATE
SUBLANES = 8
POWER_ROW0 = 16
MAX_POWER_BITS = 8
SSM_PAIRS_PER_STEP = 2
FINAL_SUBTILES = 4
V_ROWS = HEAD_DIM + 16
VMEM_LIMIT = 48 * 1024 * 1024
NEG_BIG = -1e30


def _cparams(*sem):
    return pltpu.CompilerParams(dimension_semantics=sem, vmem_limit_bytes=VMEM_LIMIT)


def _row_spec(tm, width):
    return pl.BlockSpec((1, tm, width), lambda b, i: (b, i, 0))


def _const_spec(shape):
    return pl.BlockSpec(shape, lambda b, i: (0,) * len(shape))


def _batch_vec_spec(width):
    return pl.BlockSpec((1, 1, width), lambda b, i: (b, 0, 0))


def _adaln_kernel(c_ref, w_ref, b_ref, o_ref):
    c = c_ref[...]
    s = c * jax.nn.sigmoid(c)
    o_ref[...] = jnp.dot(s, w_ref[...], preferred_element_type=F32,
                         precision=lax.Precision.HIGHEST) + b_ref[...]


def _adaln(cvecs, w_ada, b_ada):
    rows, d = cvecs.shape
    n = w_ada.shape[1]
    tn = 1024
    return pl.pallas_call(
        _adaln_kernel,
        grid=(n // tn,),
        in_specs=[pl.BlockSpec((rows, d), lambda j: (0, 0)),
                  pl.BlockSpec((d, tn), lambda j: (0, j)),
                  pl.BlockSpec((1, tn), lambda j: (0, j))],
        out_specs=pl.BlockSpec((rows, tn), lambda j: (0, j)),
        out_shape=jax.ShapeDtypeStruct((rows, n), F32),
        compiler_params=_cparams("arbitrary"),
        name="adaln",
    )(cvecs, w_ada, b_ada.reshape(1, n))


def _lane_group(rows):
    return lax.broadcasted_iota(jnp.int32, (rows, LANES), 1) // SSM_GROUP


def _to_chunk_lanes(u_scr, u_o):
    rt = u_o.shape[2]
    grp = _lane_group(rt)
    for col in range(SSM_WIDTH // LANES):
        rolled = []
        for step in range(CHUNK):
            s = u_scr[col, pl.ds(step, rt, stride=CHUNK), :]
            rolled.append([s if k == 0 else pltpu.roll(s, k * SSM_GROUP, 1)
                           for k in range(GROUPS_PER_VREG)])
        for g_lo in range(GROUPS_PER_VREG):
            for half in range(CHUNK // GROUPS_PER_VREG):
                out = None
                for s8 in range(GROUPS_PER_VREG):
                    piece = rolled[half * GROUPS_PER_VREG + s8][(s8 - g_lo) % GROUPS_PER_VREG]
                    out = piece if out is None else jnp.where(grp == s8, piece, out)
                u_o[0, col * GROUPS_PER_VREG + g_lo, :, half * LANES:(half + 1) * LANES] = out.astype(BF16)


def _from_chunk_lanes(y_ref, y_scr, r0, rt):
    grp = _lane_group(rt)
    for col in range(SSM_WIDTH // LANES):
        rolled = {}
        for g_lo in range(GROUPS_PER_VREG):
            for half in range(CHUNK // GROUPS_PER_VREG):
                s = y_ref[0, col * GROUPS_PER_VREG + g_lo, r0:r0 + rt,
                          half * LANES:(half + 1) * LANES].astype(F32)
                rolled[g_lo, half] = [s if k == 0 else pltpu.roll(s, k * SSM_GROUP, 1)
                                      for k in range(GROUPS_PER_VREG)]
        for step in range(CHUNK):
            half, s8 = divmod(step, GROUPS_PER_VREG)
            out = None
            for g_lo in range(GROUPS_PER_VREG):
                piece = rolled[g_lo, half][(g_lo - s8) % GROUPS_PER_VREG]
                out = piece if out is None else jnp.where(grp == g_lo, piece, out)
            y_scr[col, pl.ds(r0 * CHUNK + step, rt, stride=CHUNK), :] = out


def _head_mean_sq(z, bd):
    return jnp.dot((z * z).astype(BF16), bd, preferred_element_type=F32)


def _swap16(x):
    w = x.shape[1]
    lane = lax.broadcasted_iota(jnp.int32, x.shape, 1)
    return jnp.where((lane & 16) == 0, pltpu.roll(x, w - 16, 1), pltpu.roll(x, 16, 1))


def _rope(x, cos, sin_signed):
    cols = []
    for c in range(x.shape[1] // LANES):
        xc = x[:, c * LANES:(c + 1) * LANES]
        cols.append(xc * cos + _swap16(xc) * sin_signed)
    return cols[0] if len(cols) == 1 else jnp.concatenate(cols, axis=1)


def _silu(z):
    return z * jax.nn.sigmoid(z)


def _inproj_kernel(x_ref, c_ref, shl_ref, scl_ref, shc_ref, scc_ref, pg_ref, w_ref, qg_ref, kg_ref,
                   cos_ref, sin_ref, bd_ref, qt_o, k_o, vt_o, ga_o, u_o, gs_o, v_scr, u_scr, *, n_ctx_tiles):
    is_ctx = pl.program_id(1) < n_ctx_tiles
    x = jnp.where(is_ctx, c_ref[0], x_ref[0])
    shift = jnp.where(is_ctx, shc_ref[0], shl_ref[0])
    scale = jnp.where(is_ctx, scc_ref[0], scl_ref[0])
    ms = jnp.mean(x * x, axis=-1, keepdims=True)
    xn = x * lax.rsqrt(ms + NORM_EPS) * pg_ref[...]
    h = (xn * (1.0 + scale) + shift).astype(BF16)

    def proj(a, b):
        return jnp.dot(h, w_ref[:, a:b], preferred_element_type=F32)

    u = proj(GA_END, U_END)
    for col in range(SSM_WIDTH // LANES):
        u_scr[col] = u[:, col * LANES:(col + 1) * LANES]
    _to_chunk_lanes(u_scr, u_o)

    cos = cos_ref[...]
    sin = sin_ref[...]
    zq = proj(0, Q_END)
    zk = proj(Q_END, K_END)
    gs_o[0] = _silu(proj(U_END, IN_WIDTH)).astype(BF16)
    qn = zq * lax.rsqrt(_head_mean_sq(zq, bd_ref[...]) + NORM_EPS) * qg_ref[...]
    kn = zk * lax.rsqrt(_head_mean_sq(zk, bd_ref[:KV_WIDTH, :KV_WIDTH]) + NORM_EPS) * kg_ref[...]
    ga_o[0] = _silu(proj(V_END, GA_END)).astype(BF16)
    v_scr[...] = proj(K_END, V_END)

    k_o[0] = _rope(kn, cos, sin).astype(BF16)
    q = _rope(qn, cos, sin) * (ATTN_SCALE * LOG2_E)
    group = N_Q_HEADS // N_KV_HEADS
    none = jnp.zeros((HEAD_DIM, q.shape[0]), BF16)
    for c in range(ATTN_WIDTH // LANES):
        qt = q[:, c * LANES:(c + 1) * LANES].T.astype(BF16)
        for par in range(2):
            head = 2 * c + par
            kv = head // group
            qt_o[0, head, kv * HEAD_DIM:(kv + 1) * HEAD_DIM, :] = qt[par * HEAD_DIM:(par + 1) * HEAD_DIM]
            qt_o[0, head, (1 - kv) * HEAD_DIM:(2 - kv) * HEAD_DIM, :] = none

    vt = v_scr[...].T.astype(BF16)
    tm = vt.shape[1]
    ones_row = lax.broadcasted_iota(jnp.int32, (V_ROWS - HEAD_DIM, tm), 0) == 0
    for j in range(N_KV_HEADS):
        vt_o[0, j, :HEAD_DIM, :] = vt[j * HEAD_DIM:(j + 1) * HEAD_DIM]
        vt_o[0, j, HEAD_DIM:, :] = jnp.where(ones_row, 1.0, 0.0).astype(BF16)


def _inproj(x, ctx, shift_l, scale_l, shift_c, scale_c, pre_g, w_bf, qg, kg, cos, sin, bd, tm):
    bsz, n_lat, d = x.shape
    n_ctx = ctx.shape[1]
    assert n_ctx % tm == 0 and n_lat % tm == 0 and tm % (CHUNK * 16) == 0
    nct = n_ctx // tm
    n_tot = n_ctx + n_lat
    lat_rows = lambda w: pl.BlockSpec((1, tm, w), lambda b, i: (b, jnp.maximum(i - nct, 0), 0))
    all_rows = lambda w: pl.BlockSpec((1, tm, w), lambda b, i: (b, i, 0))
    ctx_vec = pl.BlockSpec((1, 1, d), lambda b, i: (0, 0, 0))
    table = pl.BlockSpec((tm, LANES), lambda b, i: (i, 0))
    return pl.pallas_call(
        functools.partial(_inproj_kernel, n_ctx_tiles=nct),
        grid=(bsz, n_tot // tm),
        in_specs=[lat_rows(d),
                  pl.BlockSpec((1, tm, d), lambda b, i: (b, jnp.minimum(i, nct - 1), 0)),
                  _batch_vec_spec(d), _batch_vec_spec(d), ctx_vec, ctx_vec, _const_spec((1, d)),
                  _const_spec((d, IN_WIDTH)), _const_spec((1, ATTN_WIDTH)), _const_spec((1, KV_WIDTH)),
                  table, table, _const_spec((ATTN_WIDTH, ATTN_WIDTH))],
        out_specs=[pl.BlockSpec((1, N_Q_HEADS, LANES, tm), lambda b, i: (b, 0, 0, jnp.maximum(i - nct, 0))),
                   all_rows(KV_WIDTH),
                   pl.BlockSpec((1, N_KV_HEADS, V_ROWS, tm), lambda b, i: (b, 0, 0, i)),
                   lat_rows(ATTN_WIDTH),
                   pl.BlockSpec((1, N_SSM_GROUPS, tm // CHUNK, CHUNK_LANES), lambda b, i: (b, 0, i, 0)),
                   lat_rows(SSM_WIDTH)],
        out_shape=[jax.ShapeDtypeStruct((bsz, N_Q_HEADS, LANES, n_lat), BF16),
                   jax.ShapeDtypeStruct((bsz, n_tot, KV_WIDTH), BF16),
                   jax.ShapeDtypeStruct((bsz, N_KV_HEADS, V_ROWS, n_tot), BF16),
                   jax.ShapeDtypeStruct((bsz, n_lat, ATTN_WIDTH), BF16),
                   jax.ShapeDtypeStruct((bsz, N_SSM_GROUPS, n_tot // CHUNK, CHUNK_LANES), BF16),
                   jax.ShapeDtypeStruct((bsz, n_lat, SSM_WIDTH), BF16)],
        scratch_shapes=[pltpu.VMEM((tm, KV_WIDTH), F32), pltpu.VMEM((SSM_WIDTH // LANES, tm, LANES), F32)],
        compiler_params=_cparams("parallel", "arbitrary"),
        name="inproj",
    )(x, ctx, shift_l, scale_l, shift_c, scale_c, pre_g, w_bf, qg, kg, cos, sin, bd)


def _attn_kernel(qt_ref, k_ref, vt_ref, o_ref, *, tk, sub, lookahead, heads_per_pass):
    tq = qt_ref.shape[3]
    n_keys = k_ref.shape[1]
    group = N_Q_HEADS // N_KV_HEADS

    outs = []
    for h0 in range(0, N_Q_HEADS, heads_per_pass):
        heads = list(range(h0, h0 + heads_per_pass))
        q_wide = [qt_ref[0, h] for h in heads]

        def body(t, carry, heads=heads, q_wide=q_wide):
            tasks = [(j, i) for j in range(tk // sub) for i in range(len(heads))]
            state = list(carry)
            scores = {}

            def keys_at(j):
                return pl.ds(pl.multiple_of(t * tk + j * sub, sub), sub)

            def issue(n):
                j, i = tasks[n]
                scores[n] = jnp.dot(k_ref[0, keys_at(j), :], q_wide[i],
                                    preferred_element_type=F32)

            def consume(n):
                j, i = tasks[n]
                s = scores.pop(n)
                m_old, acc = state[i]
                vt = vt_ref[0, heads[i] // group, :, keys_at(j)]
                m_new = jnp.maximum(m_old, jnp.max(s, axis=0, keepdims=True))
                p = jnp.exp2((s - m_new).astype(BF16))
                acc = jnp.exp2(m_old - m_new) * acc + jnp.dot(vt, p, preferred_element_type=F32)
                state[i] = (m_new, acc)

            for n in range(len(tasks) + lookahead):
                if n < len(tasks):
                    issue(n)
                if n >= lookahead:
                    consume(n - lookahead)
            return tuple(state)

        init = tuple((jnp.full((1, tq), NEG_BIG, F32), jnp.zeros((V_ROWS, tq), F32)) for _ in heads)
        for _, acc in lax.fori_loop(0, n_keys // tk, body, init):
            outs.append(acc[:HEAD_DIM] / acc[HEAD_DIM:HEAD_DIM + 1])

    for c in range(N_Q_HEADS // 2):
        pair = jnp.concatenate([outs[2 * c], outs[2 * c + 1]], axis=0)
        o_ref[0, :, c * LANES:(c + 1) * LANES] = pair.T.astype(BF16)


def _attention(qt, k_all, vt_aug, tq, tk, sub, lookahead, heads_per_pass):
    bsz, _, _, n = qt.shape
    n_keys = k_all.shape[1]
    assert n_keys % tk == 0 and tk % sub == 0 and n % tq == 0
    return pl.pallas_call(
        functools.partial(_attn_kernel, tk=tk, sub=sub, lookahead=lookahead,
                          heads_per_pass=heads_per_pass),
        grid=(bsz, n // tq),
        in_specs=[pl.BlockSpec((1, N_Q_HEADS, LANES, tq), lambda b, i: (b, 0, 0, i)),
                  pl.BlockSpec((1, n_keys, KV_WIDTH), lambda b, i: (b, 0, 0)),
                  pl.BlockSpec((1, N_KV_HEADS, V_ROWS, n_keys), lambda b, i: (b, 0, 0, 0))],
        out_specs=_row_spec(tq, ATTN_WIDTH),
        out_shape=jax.ShapeDtypeStruct((bsz, n, ATTN_WIDTH), BF16),
        compiler_params=_cparams("parallel", "parallel"),
        name="attention",
    )(qt, k_all, vt_aug)


def _scan_steps(n_rows):
    return max(1, math.ceil(math.log2(n_rows)))


def _ssm_prep_kernel(lre_ref, lim_ref, ldt_ref, bt_ref, bts_ref, c_ref, cs_ref, d_ref,
                     win_o, m_o, cout_o, mult_o, *, n_steps):
    lane = lax.broadcasted_iota(jnp.int32, (1, STATE_LANES), 1)
    low = lane < SSM_STATE
    sign_lo = jnp.where(low, -1.0, 1.0)
    k_idx = lax.broadcasted_iota(jnp.int32, (CHUNK, STATE_LANES), 0)
    step_of_row = lax.broadcasted_iota(jnp.int32, (CHUNK_LANES, STATE_LANES), 0) // SSM_GROUP
    nt = (((1,), (1,)), ((), ()))

    def outer(pw, pws, mat, mats):
        full = pw[:, None, :] * mat[None, :, :] + pws[:, None, :] * mats[None, :, :]
        return full.reshape(CHUNK * mat.shape[0], STATE_LANES)

    state_in, state_out, mults, tables = {}, {}, {}, {}
    for gi in range(2):
        taps_lo, taps_hi = None, None
        for d in range(2):
            lre = lre_ref[gi, d:d + 1, :]
            lim = lim_ref[gi, d:d + 1, :]
            dt = jnp.exp(ldt_ref[gi, d:d + 1, :])

            mag = jnp.exp(dt * lre)
            a_re, a_im = mag * jnp.cos(dt * lim), mag * jnp.sin(dt * lim)
            squares = [(a_re, a_im)]
            for _ in range(MAX_POWER_BITS - 1):
                pr, pi = squares[-1]
                squares.append((pr * pr - pi * pi, 2.0 * pr * pi))

            def power(expo, squares=squares):
                re = jnp.ones(expo.shape, F32)
                im = jnp.zeros(expo.shape, F32)
                for b, (pr, pi) in enumerate(squares):
                    take = ((expo >> b) & 1) == 1
                    re, im = jnp.where(take, re * pr - im * pi, re), jnp.where(take, re * pi + im * pr, im)
                return re, im * sign_lo

            nr, ni = a_re - 1.0, a_im
            den = lre * lre + lim * lim
            cr = (nr * lre + ni * lim) / den
            ci = (ni * lre - nr * lim) / den
            cis = ci * sign_lo
            bbar = cr * bt_ref[gi, d] + cis * bts_ref[gi, d]
            bbar_s = cr * bts_ref[gi, d] - cis * bt_ref[gi, d]
            cmat, cmat_s = c_ref[gi, d], cs_ref[gi, d]

            asc, ascs = power(k_idx)
            desc, descs = power(CHUNK - 1 - k_idx)
            nxt, nxts = power(k_idx + 1)
            conj_c = lambda pw, pws: outer(pw, pws, cmat, cmat_s) * (-sign_lo)
            if d == 0:
                state_in[gi, d] = outer(desc, descs, bbar, bbar_s)
                state_out[gi, d] = conj_c(nxt, nxts)
                lag0 = jnp.where(step_of_row == CHUNK - 1, conj_c(desc, descs), 0.0)
                lags = jnp.where(step_of_row < CHUNK - 1, conj_c(nxt, nxts), 0.0)
                hi = lax.dot_general(bbar, lags, nt, preferred_element_type=F32,
                                     precision=lax.Precision.HIGHEST)
                lo = lax.dot_general(bbar, lag0, nt, preferred_element_type=F32,
                                     precision=lax.Precision.HIGHEST)
                taps_hi = hi
                taps_lo = lo if taps_lo is None else taps_lo + lo
            else:
                state_in[gi, d] = outer(asc, ascs, bbar, bbar_s)
                rev, revs = power(CHUNK - k_idx)
                state_out[gi, d] = conj_c(rev, revs)
                lo = lax.dot_general(bbar, conj_c(desc, descs), nt, preferred_element_type=F32,
                                     precision=lax.Precision.HIGHEST)
                taps_lo = lo if taps_lo is None else taps_lo + lo

            sr, sis = power(jnp.full((1, STATE_LANES), CHUNK, jnp.int32))
            chain = []
            for k in range(n_steps):
                chain.append((sr, sis))
                si = sis * sign_lo
                sr, sis = sr * sr - si * si, 2.0 * sr * si * sign_lo
            mults[gi, d] = chain
            r_idx = lax.broadcasted_iota(jnp.int32, (SUBLANES, STATE_LANES), 0)
            tables[gi, d] = power(CHUNK * (r_idx + 1) if d == 0 else CHUNK * (SUBLANES - r_idx))

        qq = lax.broadcasted_iota(jnp.int32, (SSM_GROUP, CHUNK_LANES), 0)
        ll = lax.broadcasted_iota(jnp.int32, (SSM_GROUP, CHUNK_LANES), 1)
        taps_lo = taps_lo + jnp.where(ll == qq + (CHUNK - 1) * SSM_GROUP, d_ref[gi], 0.0)
        taps = jnp.concatenate([taps_lo, taps_hi], axis=1)
        for j in range(CHUNK):
            off = (CHUNK - 1 - j) * SSM_GROUP
            m_o[gi, j * SSM_GROUP:(j + 1) * SSM_GROUP, :] = taps[:, off:off + CHUNK_LANES].astype(BF16)

    def swap(x):
        return pltpu.roll(x, SSM_STATE, 1)

    mult_o[...] = jnp.zeros(mult_o.shape, F32)
    wide_low = lax.broadcasted_iota(jnp.int32, (CHUNK_LANES, STATE_LANES), 1) < SSM_STATE
    zeros_c = jnp.zeros((SSM_STATE, CHUNK_LANES), F32)
    for d in range(2):
        s0, s1 = state_in[0, d], state_in[1, d]
        re_blk = jnp.concatenate([jnp.where(wide_low, s0, 0.0), jnp.where(wide_low, 0.0, swap(s1))], axis=0)
        im_blk = jnp.concatenate([jnp.where(wide_low, swap(s0), 0.0), jnp.where(wide_low, 0.0, s1)], axis=0)
        win_o[0, :, (2 * d) * LANES:(2 * d + 1) * LANES] = re_blk.astype(BF16)
        win_o[0, :, (2 * d + 1) * LANES:(2 * d + 2) * LANES] = im_blk.astype(BF16)
        t0, t1 = state_out[0, d].T, state_out[1, d].T
        for part in range(2):
            rows = slice(part * SSM_STATE, (part + 1) * SSM_STATE)
            blk = jnp.concatenate([jnp.concatenate([t0[rows], zeros_c], axis=1),
                                   jnp.concatenate([zeros_c, t1[rows]], axis=1)], axis=0)
            cout_o[0, (2 * d + part) * LANES:(2 * d + part + 1) * LANES, :] = blk.astype(BF16)
        for k in range(n_steps):
            (sr0, sis0), (sr1, sis1) = mults[0, d][k], mults[1, d][k]
            mult_o[0, k:k + 1, (2 * d) * LANES:(2 * d + 1) * LANES] = jnp.where(low, sr0, sr1)
            mult_o[0, k:k + 1, (2 * d + 1) * LANES:(2 * d + 2) * LANES] = jnp.where(low, -sis0, sis1)
        (tr0, tis0), (tr1, tis1) = tables[0, d], tables[1, d]
        mult_o[0, POWER_ROW0:POWER_ROW0 + SUBLANES, (2 * d) * LANES:(2 * d + 1) * LANES] = jnp.where(low, tr0, tr1)
        mult_o[0, POWER_ROW0:POWER_ROW0 + SUBLANES, (2 * d + 1) * LANES:(2 * d + 2) * LANES] = jnp.where(
            low, -tis0, tis1)


def _ssm_prep(lam_re, lam_im, log_dt, b_re, b_im, c_re, c_im, d_skip, n_steps):
    g = N_SSM_GROUPS
    dup = lambda a: jnp.concatenate([a, a], axis=-1)
    lre = dup(jnp.swapaxes(lam_re, 0, 1))
    lim = dup(jnp.swapaxes(lam_im, 0, 1))
    ldt = jnp.broadcast_to(jnp.swapaxes(log_dt, 0, 1)[..., None], (g, 2, STATE_LANES))
    b_both = jnp.transpose(jnp.stack([b_re, b_im]), (2, 1, 4, 0, 3))
    bt = b_both.reshape(g, 2, SSM_GROUP, STATE_LANES)
    bts = b_both[:, :, :, ::-1].reshape(g, 2, SSM_GROUP, STATE_LANES)
    cr, ci = jnp.swapaxes(c_re, 0, 1), jnp.swapaxes(c_im, 0, 1)
    cm, cms = jnp.concatenate([cr, ci], -1), jnp.concatenate([ci, cr], -1)
    dsk = jnp.pad(d_skip, ((0, 0), (CHUNK_LANES - SSM_GROUP, 0))).reshape(g, 1, CHUNK_LANES)
    n_pad = POWER_ROW0 + SUBLANES
    vec = pl.BlockSpec((2, 2, STATE_LANES), lambda i: (i, 0, 0))
    mat = pl.BlockSpec((2, 2, SSM_GROUP, STATE_LANES), lambda i: (i, 0, 0, 0))
    pair_sq = pl.BlockSpec((1, 2 * CHUNK_LANES, 4 * LANES), lambda i: (i, 0, 0))
    return pl.pallas_call(
        functools.partial(_ssm_prep_kernel, n_steps=n_steps),
        grid=(g // 2,),
        in_specs=[vec, vec, vec, mat, mat, mat, mat,
                  pl.BlockSpec((2, 1, CHUNK_LANES), lambda i: (i, 0, 0))],
        out_specs=[pair_sq, pl.BlockSpec((2, CHUNK_LANES, CHUNK_LANES), lambda i: (i, 0, 0)), pair_sq,
                   pl.BlockSpec((1, n_pad, 4 * LANES), lambda i: (i, 0, 0))],
        out_shape=[jax.ShapeDtypeStruct((g // 2, 2 * CHUNK_LANES, 4 * LANES), BF16),
                   jax.ShapeDtypeStruct((g, CHUNK_LANES, CHUNK_LANES), BF16),
                   jax.ShapeDtypeStruct((g // 2, 4 * LANES, 2 * CHUNK_LANES), BF16),
                   jax.ShapeDtypeStruct((g // 2, n_pad, 4 * LANES), F32)],
        compiler_params=_cparams("parallel"),
        name="ssm_prep",
    )(lre, lim, ldt, bt, bts, cm, cms, dsk)


def _complex_step(re, im, sr, si, ar, ai):
    return re + ar * sr - ai * si, im + ar * si + ai * sr


def _ssm_kernel(u_ref, win_ref, m_ref, cout_ref, mult_ref, y_o, loc_scr, car_scr, *, n_ctx_rows, n_steps):
    n_pairs = win_ref.shape[0]
    rows = u_ref.shape[2]
    n_lat_rows = rows - n_ctx_rows
    n_blk = rows // SUBLANES
    n_blk_pad = -(-n_blk // SUBLANES) * SUBLANES
    in_blk_row = lax.broadcasted_iota(jnp.int32, (SUBLANES, LANES), 0)
    blk = lax.broadcasted_iota(jnp.int32, (n_blk_pad, LANES), 0)
    tile = lambda t: jnp.broadcast_to(t[None], (n_blk, SUBLANES, LANES)).reshape(rows, LANES)
    local_steps = SUBLANES.bit_length() - 1
    to_rev = lambda a: jnp.concatenate([a[n_ctx_rows:], a[:n_ctx_rows]], axis=0)

    def shifted(a, s, idx, n, down):
        if down:
            return jnp.where(idx >= s, pltpu.roll(a, s, 0), 0.0)
        return jnp.where(idx < n - s, pltpu.roll(a, a.shape[0] - s, 0), 0.0)

    def scan(pp, x):
        def mult(k0, k1, d):
            return (mult_ref[pp, k0:k1, 2 * d * LANES:(2 * d + 1) * LANES],
                    mult_ref[pp, k0:k1, (2 * d + 1) * LANES:(2 * d + 2) * LANES])

        state = [[x[:, :LANES], x[:, LANES:2 * LANES]],
                 [to_rev(x[:, 2 * LANES:3 * LANES]), to_rev(x[:, 3 * LANES:])]]
        for d, down in ((0, True), (1, False)):
            re, im = state[d]
            for k in range(local_steps):
                s = 1 << k
                keep = in_blk_row >= s if down else in_blk_row < SUBLANES - s
                ar, ai = (tile(jnp.where(keep, m, 0.0)) for m in mult(k, k + 1, d))
                shift = s if down else rows - s
                re, im = _complex_step(re, im, pltpu.roll(re, shift, 0), pltpu.roll(im, shift, 0), ar, ai)
            loc_scr[pp, 2 * d], loc_scr[pp, 2 * d + 1] = re, im
            state[d] = [re, im]

            end_row = SUBLANES - 1 if down else 0
            pad = jnp.zeros((n_blk_pad - n_blk, LANES), F32)
            ere = jnp.concatenate([loc_scr[pp, 2 * d, pl.ds(end_row, n_blk, stride=SUBLANES), :], pad], axis=0)
            eim = jnp.concatenate([loc_scr[pp, 2 * d + 1, pl.ds(end_row, n_blk, stride=SUBLANES), :], pad],
                                  axis=0)
            for k in range(local_steps, n_steps):
                ar, ai = mult(k, k + 1, d)
                s = 1 << (k - local_steps)
                ere, eim = _complex_step(ere, eim, shifted(ere, s, blk, n_blk_pad, down),
                                         shifted(eim, s, blk, n_blk_pad, down), ar, ai)
            for c, ends in enumerate((ere, eim)):
                entering = shifted(ends, 1, blk, n_blk_pad, down)[:n_blk]
                for r in range(SUBLANES):
                    car_scr[pp, 2 * d + c, pl.ds(r, n_blk, stride=SUBLANES), :] = entering

        for d in range(2):
            tre, tim = mult(POWER_ROW0, POWER_ROW0 + SUBLANES, d)
            re, im = state[d]
            state[d] = list(_complex_step(re, im, car_scr[pp, 2 * d], car_scr[pp, 2 * d + 1],
                                          tile(tre), tile(tim)))
        return state

    xs = [jnp.dot(jnp.concatenate([u_ref[0, 2 * pp], u_ref[0, 2 * pp + 1]], axis=1), win_ref[pp],
                  preferred_element_type=F32) for pp in range(n_pairs)]
    prev = lambda a: pltpu.roll(a, 1, 0)[n_ctx_rows:]
    nxt = lambda a: pltpu.roll(a, rows - 1, 0)[:n_lat_rows]
    for pp in range(n_pairs):
        state = scan(pp, xs[pp])
        st = jnp.concatenate([prev(state[0][0]), prev(state[0][1]), nxt(state[1][0]), nxt(state[1][1])],
                             axis=1).astype(BF16)
        y = jnp.dot(st, cout_ref[pp], preferred_element_type=F32)
        for gi in range(2):
            g = 2 * pp + gi
            y_o[0, g] = (y[:, gi * CHUNK_LANES:(gi + 1) * CHUNK_LANES]
                         + jnp.dot(u_ref[0, g, n_ctx_rows:, :], m_ref[g],
                                   preferred_element_type=F32)).astype(BF16)


def _ssm(u_cf, win_pair, m, cout_pair, mult, n_ctx_rows, n_steps):
    bsz, g, rows, _ = u_cf.shape
    pairs = SSM_PAIRS_PER_STEP
    assert rows % SUBLANES == 0 and n_ctx_rows % 16 == 0 and g % (2 * pairs) == 0
    pair_spec = lambda a: pl.BlockSpec((pairs,) + a.shape[1:], lambda b, i: (i, 0, 0))
    return pl.pallas_call(
        functools.partial(_ssm_kernel, n_ctx_rows=n_ctx_rows, n_steps=n_steps),
        grid=(bsz, g // (2 * pairs)),
        in_specs=[pl.BlockSpec((1, 2 * pairs, rows, CHUNK_LANES), lambda b, i: (b, i, 0, 0)),
                  pair_spec(win_pair),
                  pl.BlockSpec((2 * pairs, CHUNK_LANES, CHUNK_LANES), lambda b, i: (i, 0, 0)),
                  pair_spec(cout_pair), pair_spec(mult)],
        out_specs=pl.BlockSpec((1, 2 * pairs, rows - n_ctx_rows, CHUNK_LANES), lambda b, i: (b, i, 0, 0)),
        out_shape=jax.ShapeDtypeStruct((bsz, g, rows - n_ctx_rows, CHUNK_LANES), BF16),
        scratch_shapes=[pltpu.VMEM((pairs, 4, rows, LANES), F32), pltpu.VMEM((pairs, 4, rows, LANES), F32)],
        compiler_params=_cparams("parallel", "parallel"),
        name="ssm",
    )(u_cf, win_pair, m, cout_pair, mult)


def _final_kernel(x_ref, at_ref, ga_ref, y_ref, gs_ref, gate_ref, wglu_ref, bglu_ref, wout_ref,
                  pg_ref, o_ref, y_scr):
    tm = x_ref.shape[1]
    sub = tm // FINAL_SUBTILES
    gain = gate_ref[0] * pg_ref[...]
    def finish(h, out):
        rows = slice(h * sub, (h + 1) * sub)
        ms = jnp.mean(out * out, axis=-1, keepdims=True)
        o_ref[0, rows] = x_ref[0, rows] + out * lax.rsqrt(ms + NORM_EPS) * gain

    pending = None
    for h in range(FINAL_SUBTILES):
        rows = slice(h * sub, (h + 1) * sub)
        _from_chunk_lanes(y_ref, y_scr, h * sub // CHUNK, sub // CHUNK)
        y = jax.nn.gelu(jnp.concatenate([y_scr[col, rows] for col in range(SSM_WIDTH // LANES)], axis=1))
        t = jnp.dot(y.astype(BF16), wglu_ref[...], preferred_element_type=F32) + bglu_ref[...]
        s = (y * jax.nn.sigmoid(t) * gs_ref[0, rows].astype(F32)).astype(BF16)
        a = at_ref[0, rows] * ga_ref[0, rows]
        out = jnp.dot(jnp.concatenate([a, s], axis=1), wout_ref[...], preferred_element_type=F32)
        if pending is not None:
            finish(*pending)
        pending = (h, out)
    finish(*pending)


def _final(x, attn, ga, y_cf, gs, gate, wglu_bf, b_glu, wout_bf, post_g, tm):
    bsz, n, d = x.shape
    assert n % tm == 0 and tm % (CHUNK * 16) == 0
    return pl.pallas_call(
        _final_kernel,
        grid=(bsz, n // tm),
        in_specs=[_row_spec(tm, d), _row_spec(tm, ATTN_WIDTH), _row_spec(tm, ATTN_WIDTH),
                  pl.BlockSpec((1, N_SSM_GROUPS, tm // CHUNK, CHUNK_LANES), lambda b, i: (b, 0, i, 0)),
                  _row_spec(tm, SSM_WIDTH), _batch_vec_spec(d),
                  _const_spec((SSM_WIDTH, SSM_WIDTH)), _const_spec((1, SSM_WIDTH)),
                  _const_spec((d, d)), _const_spec((1, d))],
        out_specs=_row_spec(tm, d),
        out_shape=jax.ShapeDtypeStruct((bsz, n, d), F32),
        scratch_shapes=[pltpu.VMEM((SSM_WIDTH // LANES, tm, LANES), F32)],
        compiler_params=_cparams("parallel", "parallel"),
        name="final",
    )(x, attn, ga, y_cf, gs, gate, wglu_bf, b_glu, wout_bf, post_g)


def _rope_tables(n_ctx, n_lat):
    t = jnp.arange(n_lat, dtype=jnp.int32)
    row_pos = (t // GRID_W).astype(F32)
    col_pos = (t % GRID_W).astype(F32)
    inv_freq = ROPE_THETA ** (-jnp.arange(ROPE_FREQS, dtype=F32) / ROPE_FREQS)
    ang_r = row_pos[:, None] * inv_freq
    ang_c = col_pos[:, None] * inv_freq
    cos = jnp.concatenate([jnp.cos(ang_r)] * 2 + [jnp.cos(ang_c)] * 2, axis=1)
    sin = jnp.concatenate([-jnp.sin(ang_r), jnp.sin(ang_r), -jnp.sin(ang_c), jnp.sin(ang_c)], axis=1)
    cos = jnp.concatenate([jnp.ones((n_ctx, HEAD_DIM), F32), cos], axis=0)
    sin = jnp.concatenate([jnp.zeros((n_ctx, HEAD_DIM), F32), sin], axis=0)
    return jnp.tile(cos, (1, LANES // HEAD_DIM)), jnp.tile(sin, (1, LANES // HEAD_DIM))


def _layer(x, ctx, c, c_ctx, w_ada, b_ada, pre_g, post_g, w_in, q_g, k_g, lam_re, lam_im, log_dt,
           b_re, b_im, c_re, c_im, d_skip, w_glu, b_glu, w_out):
    bsz, n_lat, d = x.shape
    n_ctx = ctx.shape[1]
    assert n_lat % GRID_W == 0

    assert bsz < SUBLANES
    cvecs = jnp.concatenate([c, c_ctx[None], jnp.zeros((SUBLANES - bsz - 1, d), F32)], axis=0)
    mod = _adaln(cvecs, w_ada, b_ada)
    shift, scale, gate = (mod[:, i * d:(i + 1) * d] for i in range(3))
    lat = lambda a: a[:bsz].reshape(bsz, 1, d)
    cvec = lambda a: a[bsz].reshape(1, 1, d)

    head = jnp.arange(ATTN_WIDTH) // HEAD_DIM
    bd = (head[:, None] == head[None, :]).astype(BF16) * (1.0 / HEAD_DIM)
    qg = jnp.tile(q_g, N_Q_HEADS).reshape(1, ATTN_WIDTH)
    kg = jnp.tile(k_g, N_KV_HEADS).reshape(1, KV_WIDTH)
    cos, sin = _rope_tables(n_ctx, n_lat)

    qt, k_all, vt_aug, ga, u_cf, gs = _inproj(x, ctx, lat(shift), lat(scale), cvec(shift), cvec(scale),
                                              pre_g.reshape(1, d), w_in.astype(BF16), qg, kg, cos, sin,
                                              bd, tm=256)
    attn = _attention(qt, k_all, vt_aug, tq=256, tk=8448, sub=256, lookahead=5, heads_per_pass=8)

    rows = (n_ctx + n_lat) // CHUNK
    n_steps = _scan_steps(rows)
    win_pair, m_op, cout_pair, mult = _ssm_prep(lam_re, lam_im, log_dt, b_re, b_im, c_re, c_im, d_skip,
                                                n_steps)
    y_cf = _ssm(u_cf, win_pair, m_op, cout_pair, mult, n_ctx // CHUNK, n_steps)

    return _final(x, attn, ga, y_cf, gs, lat(gate), w_glu.astype(BF16), b_glu.reshape(1, SSM_WIDTH),
                  w_out.astype(BF16), post_g.reshape(1, d), tm=1024)


def kernel(x, c, ctx, c_ctx, w_ada, b_ada, pre_norm, post_norm, w_in, q_norm, k_norm, ssm_lam_re,
           ssm_lam_im, ssm_log_dt, ssm_b_re, ssm_b_im, ssm_c_re, ssm_c_im, ssm_d, w_glu, b_glu, w_out):
    depth = w_ada.shape[0]
    assert depth == 1, "context stream update between layers is not implemented"
    return _layer(x, ctx, c, c_ctx, w_ada[0], b_ada[0], pre_norm[0], post_norm[0], w_in[0], q_norm[0],
                  k_norm[0], ssm_lam_re[0], ssm_lam_im[0], ssm_log_dt[0], ssm_b_re[0], ssm_b_im[0],
                  ssm_c_re[0], ssm_c_im[0], ssm_d[0], w_glu[0], b_glu[0], w_out[0])
```

```python
import functools
import math

import jax
import jax.numpy as jnp
from jax import lax
from jax.experimental import pallas as pl
from jax.experimental.pallas import tpu as pltpu

F32 = jnp.float32
BF16 = jnp.bfloat16

D_MODEL = 1024
HEAD_DIM = 64
N_Q_HEADS = 8
N_KV_HEADS = 2
ATTN_WIDTH = N_Q_HEADS * HEAD_DIM
KV_WIDTH = N_KV_HEADS * HEAD_DIM
SSM_WIDTH = 512
SSM_GROUP = 16
N_SSM_GROUPS = SSM_WIDTH // SSM_GROUP
SSM_STATE = 64
GRID_W = 64
ROPE_THETA = 10000.0
ROPE_FREQS = 16
NORM_EPS = 1e-6
ATTN_SCALE = HEAD_DIM ** -0.5
LOG2_E = math.log2(math.e)
Q_END = ATTN_WIDTH
K_END = Q_END + KV_WIDTH
V_END = K_END + KV_WIDTH
GA_END = V_END + ATTN_WIDTH
U_END = GA_END + SSM_WIDTH
IN_WIDTH = U_END + SSM_WIDTH

CHUNK = 16
CHUNK_LANES = CHUNK * SSM_GROUP
LANES = 128
GROUPS_PER_VREG = LANES // SSM_GROUP
STATE_LANES = 2 * SSM_STATE
SUBLANES = 8
POWER_ROW0 = 16
MAX_POWER_BITS = 8
SSM_PAIRS_PER_STEP = 2
FINAL_SUBTILES = 4
V_ROWS = HEAD_DIM + 16
VMEM_LIMIT = 48 * 1024 * 1024
NEG_BIG = -1e30


def _cparams(*sem):
    return pltpu.CompilerParams(dimension_semantics=sem, vmem_limit_bytes=VMEM_LIMIT)


def _row_spec(tm, width):
    return pl.BlockSpec((1, tm, width), lambda b, i: (b, i, 0))


def _const_spec(shape):
    return pl.BlockSpec(shape, lambda b, i: (0,) * len(shape))


def _batch_vec_spec(width):
    return pl.BlockSpec((1, 1, width), lambda b, i: (b, 0, 0))


def _adaln_kernel(c_ref, w_ref, b_ref, o_ref):
    c = c_ref[...]
    s = c * jax.nn.sigmoid(c)
    o_ref[...] = jnp.dot(s, w_ref[...], preferred_element_type=F32,
                         precision=lax.Precision.HIGHEST) + b_ref[...]


def _adaln(cvecs, w_ada, b_ada):
    rows, d = cvecs.shape
    n = w_ada.shape[1]
    tn = 1024
    return pl.pallas_call(
        _adaln_kernel,
        grid=(n // tn,),
        in_specs=[pl.BlockSpec((rows, d), lambda j: (0, 0)),
                  pl.BlockSpec((d, tn), lambda j: (0, j)),
                  pl.BlockSpec((1, tn), lambda j: (0, j))],
        out_specs=pl.BlockSpec((rows, tn), lambda j: (0, j)),
        out_shape=jax.ShapeDtypeStruct((rows, n), F32),
        compiler_params=_cparams("arbitrary"),
        name="adaln",
    )(cvecs, w_ada, b_ada.reshape(1, n))


def _lane_group(rows):
    return lax.broadcasted_iota(jnp.int32, (rows, LANES), 1) // SSM_GROUP


def _to_chunk_lanes(u_scr, u_o):
    rt = u_o.shape[2]
    grp = _lane_group(rt)
    for col in range(SSM_WIDTH // LANES):
        rolled = []
        for step in range(CHUNK):
            s = u_scr[col, pl.ds(step, rt, stride=CHUNK), :]
            rolled.append([s if k == 0 else pltpu.roll(s, k * SSM_GROUP, 1)
                           for k in range(GROUPS_PER_VREG)])
        for g_lo in range(GROUPS_PER_VREG):
            for half in range(CHUNK // GROUPS_PER_VREG):
                out = None
                for s8 in range(GROUPS_PER_VREG):
                    piece = rolled[half * GROUPS_PER_VREG + s8][(s8 - g_lo) % GROUPS_PER_VREG]
                    out = piece if out is None else jnp.where(grp == s8, piece, out)
                u_o[0, col * GROUPS_PER_VREG + g_lo, :, half * LANES:(half + 1) * LANES] = out.astype(BF16)


def _from_chunk_lanes(y_ref, y_scr, r0, rt):
    grp = _lane_group(rt)
    for col in range(SSM_WIDTH // LANES):
        rolled = {}
        for g_lo in range(GROUPS_PER_VREG):
            for half in range(CHUNK // GROUPS_PER_VREG):
                s = y_ref[0, col * GROUPS_PER_VREG + g_lo, r0:r0 + rt,
                          half * LANES:(half + 1) * LANES].astype(F32)
                rolled[g_lo, half] = [s if k == 0 else pltpu.roll(s, k * SSM_GROUP, 1)
                                      for k in range(GROUPS_PER_VREG)]
        for step in range(CHUNK):
            half, s8 = divmod(step, GROUPS_PER_VREG)
            out = None
            for g_lo in range(GROUPS_PER_VREG):
                piece = rolled[g_lo, half][(g_lo - s8) % GROUPS_PER_VREG]
                out = piece if out is None else jnp.where(grp == g_lo, piece, out)
            y_scr[col, pl.ds(r0 * CHUNK + step, rt, stride=CHUNK), :] = out


def _head_mean_sq(z, bd):
    return jnp.dot((z * z).astype(BF16), bd, preferred_element_type=F32)


def _swap16(x):
    w = x.shape[1]
    lane = lax.broadcasted_iota(jnp.int32, x.shape, 1)
    return jnp.where((lane & 16) == 0, pltpu.roll(x, w - 16, 1), pltpu.roll(x, 16, 1))


def _rope(x, cos, sin_signed):
    cols = []
    for c in range(x.shape[1] // LANES):
        xc = x[:, c * LANES:(c + 1) * LANES]
        cols.append(xc * cos + _swap16(xc) * sin_signed)
    return cols[0] if len(cols) == 1 else jnp.concatenate(cols, axis=1)


def _silu(z):
    return z * jax.nn.sigmoid(z)


def _inproj_kernel(x_ref, c_ref, shl_ref, scl_ref, shc_ref, scc_ref, pg_ref, w_ref, qg_ref, kg_ref,
                   cos_ref, sin_ref, bd_ref, qt_o, k_o, vt_o, ga_o, u_o, gs_o, v_scr, u_scr, *, n_ctx_tiles):
    is_ctx = pl.program_id(1) < n_ctx_tiles
    x = jnp.where(is_ctx, c_ref[0], x_ref[0])
    shift = jnp.where(is_ctx, shc_ref[0], shl_ref[0])
    scale = jnp.where(is_ctx, scc_ref[0], scl_ref[0])
    ms = jnp.mean(x * x, axis=-1, keepdims=True)
    xn = x * lax.rsqrt(ms + NORM_EPS) * pg_ref[...]
    h = (xn * (1.0 + scale) + shift).astype(BF16)

    def proj(a, b):
        return jnp.dot(h, w_ref[:, a:b], preferred_element_type=F32)

    u = proj(GA_END, U_END)
    for col in range(SSM_WIDTH // LANES):
        u_scr[col] = u[:, col * LANES:(col + 1) * LANES]
    _to_chunk_lanes(u_scr, u_o)

    cos = cos_ref[...]
    sin = sin_ref[...]
    zq = proj(0, Q_END)
    zk = proj(Q_END, K_END)
    gs_o[0] = _silu(proj(U_END, IN_WIDTH)).astype(BF16)
    qn = zq * lax.rsqrt(_head_mean_sq(zq, bd_ref[...]) + NORM_EPS) * qg_ref[...]
    kn = zk * lax.rsqrt(_head_mean_sq(zk, bd_ref[:KV_WIDTH, :KV_WIDTH]) + NORM_EPS) * kg_ref[...]
    ga_o[0] = _silu(proj(V_END, GA_END)).astype(BF16)
    v_scr[...] = proj(K_END, V_END)

    k_o[0] = _rope(kn, cos, sin).astype(BF16)
    q = _rope(qn, cos, sin) * (ATTN_SCALE * LOG2_E)
    group = N_Q_HEADS // N_KV_HEADS
    none = jnp.zeros((HEAD_DIM, q.shape[0]), BF16)
    for c in range(ATTN_WIDTH // LANES):
        qt = q[:, c * LANES:(c + 1) * LANES].T.astype(BF16)
        for par in range(2):
            head = 2 * c + par
            kv = head // group
            qt_o[0, head, kv * HEAD_DIM:(kv + 1) * HEAD_DIM, :] = qt[par * HEAD_DIM:(par + 1) * HEAD_DIM]
            qt_o[0, head, (1 - kv) * HEAD_DIM:(2 - kv) * HEAD_DIM, :] = none

    vt = v_scr[...].T.astype(BF16)
    tm = vt.shape[1]
    ones_row = lax.broadcasted_iota(jnp.int32, (V_ROWS - HEAD_DIM, tm), 0) == 0
    for j in range(N_KV_HEADS):
        vt_o[0, j, :HEAD_DIM, :] = vt[j * HEAD_DIM:(j + 1) * HEAD_DIM]
        vt_o[0, j, HEAD_DIM:, :] = jnp.where(ones_row, 1.0, 0.0).astype(BF16)


def _inproj(x, ctx, shift_l, scale_l, shift_c, scale_c, pre_g, w_bf, qg, kg, cos, sin, bd, tm):
    bsz, n_lat, d = x.shape
    n_ctx = ctx.shape[1]
    assert n_ctx % tm == 0 and n_lat % tm == 0 and tm % (CHUNK * 16) == 0
    nct = n_ctx // tm
    n_tot = n_ctx + n_lat
    lat_rows = lambda w: pl.BlockSpec((1, tm, w), lambda b, i: (b, jnp.maximum(i - nct, 0), 0))
    all_rows = lambda w: pl.BlockSpec((1, tm, w), lambda b, i: (b, i, 0))
    ctx_vec = pl.BlockSpec((1, 1, d), lambda b, i: (0, 0, 0))
    table = pl.BlockSpec((tm, LANES), lambda b, i: (i, 0))
    return pl.pallas_call(
        functools.partial(_inproj_kernel, n_ctx_tiles=nct),
        grid=(bsz, n_tot // tm),
        in_specs=[lat_rows(d),
                  pl.BlockSpec((1, tm, d), lambda b, i: (b, jnp.minimum(i, nct - 1), 0)),
                  _batch_vec_spec(d), _batch_vec_spec(d), ctx_vec, ctx_vec, _const_spec((1, d)),
                  _const_spec((d, IN_WIDTH)), _const_spec((1, ATTN_WIDTH)), _const_spec((1, KV_WIDTH)),
                  table, table, _const_spec((ATTN_WIDTH, ATTN_WIDTH))],
        out_specs=[pl.BlockSpec((1, N_Q_HEADS, LANES, tm), lambda b, i: (b, 0, 0, jnp.maximum(i - nct, 0))),
                   all_rows(KV_WIDTH),
                   pl.BlockSpec((1, N_KV_HEADS, V_ROWS, tm), lambda b, i: (b, 0, 0, i)),
                   lat_rows(ATTN_WIDTH),
                   pl.BlockSpec((1, N_SSM_GROUPS, tm // CHUNK, CHUNK_LANES), lambda b, i: (b, 0, i, 0)),
                   lat_rows(SSM_WIDTH)],
        out_shape=[jax.ShapeDtypeStruct((bsz, N_Q_HEADS, LANES, n_lat), BF16),
                   jax.ShapeDtypeStruct((bsz, n_tot, KV_WIDTH), BF16),
                   jax.ShapeDtypeStruct((bsz, N_KV_HEADS, V_ROWS, n_tot), BF16),
                   jax.ShapeDtypeStruct((bsz, n_lat, ATTN_WIDTH), BF16),
                   jax.ShapeDtypeStruct((bsz, N_SSM_GROUPS, n_tot // CHUNK, CHUNK_LANES), BF16),
                   jax.ShapeDtypeStruct((bsz, n_lat, SSM_WIDTH), BF16)],
        scratch_shapes=[pltpu.VMEM((tm, KV_WIDTH), F32), pltpu.VMEM((SSM_WIDTH // LANES, tm, LANES), F32)],
        compiler_params=_cparams("parallel", "arbitrary"),
        name="inproj",
    )(x, ctx, shift_l, scale_l, shift_c, scale_c, pre_g, w_bf, qg, kg, cos, sin, bd)


def _attn_kernel(qt_ref, k_ref, vt_ref, o_ref, *, tk, sub, lookahead, heads_per_pass):
    tq = qt_ref.shape[3]
    n_keys = k_ref.shape[1]
    group = N_Q_HEADS // N_KV_HEADS

    outs = []
    for h0 in range(0, N_Q_HEADS, heads_per_pass):
        heads = list(range(h0, h0 + heads_per_pass))
        q_wide = [qt_ref[0, h] for h in heads]

        def body(t, carry, heads=heads, q_wide=q_wide):
            tasks = [(j, i) for j in range(tk // sub) for i in range(len(heads))]
            state = list(carry)
            scores = {}

            def keys_at(j):
                return pl.ds(pl.multiple_of(t * tk + j * sub, sub), sub)

            def issue(n):
                j, i = tasks[n]
                scores[n] = jnp.dot(k_ref[0, keys_at(j), :], q_wide[i],
                                    preferred_element_type=F32)

            def consume(n):
                j, i = tasks[n]
                s = scores.pop(n)
                m_old, acc = state[i]
                vt = vt_ref[0, heads[i] // group, :, keys_at(j)]
                m_new = jnp.maximum(m_old, jnp.max(s, axis=0, keepdims=True))
                p = jnp.exp2((s - m_new).astype(BF16))
                acc = jnp.exp2(m_old - m_new) * acc + jnp.dot(vt, p, preferred_element_type=F32)
                state[i] = (m_new, acc)

            for n in range(len(tasks) + lookahead):
                if n < len(tasks):
                    issue(n)
                if n >= lookahead:
                    consume(n - lookahead)
            return tuple(state)

        init = tuple((jnp.full((1, tq), NEG_BIG, F32), jnp.zeros((V_ROWS, tq), F32)) for _ in heads)
        for _, acc in lax.fori_loop(0, n_keys // tk, body, init):
            outs.append(acc[:HEAD_DIM] / acc[HEAD_DIM:HEAD_DIM + 1])

    for c in range(N_Q_HEADS // 2):
        pair = jnp.concatenate([outs[2 * c], outs[2 * c + 1]], axis=0)
        o_ref[0, :, c * LANES:(c + 1) * LANES] = pair.T.astype(BF16)


def _attention(qt, k_all, vt_aug, tq, tk, sub, lookahead, heads_per_pass):
    bsz, _, _, n = qt.shape
    n_keys = k_all.shape[1]
    assert n_keys % tk == 0 and tk % sub == 0 and n % tq == 0
    return pl.pallas_call(
        functools.partial(_attn_kernel, tk=tk, sub=sub, lookahead=lookahead,
                          heads_per_pass=heads_per_pass),
        grid=(bsz, n // tq),
        in_specs=[pl.BlockSpec((1, N_Q_HEADS, LANES, tq), lambda b, i: (b, 0, 0, i)),
                  pl.BlockSpec((1, n_keys, KV_WIDTH), lambda b, i: (b, 0, 0)),
                  pl.BlockSpec((1, N_KV_HEADS, V_ROWS, n_keys), lambda b, i: (b, 0, 0, 0))],
        out_specs=_row_spec(tq, ATTN_WIDTH),
        out_shape=jax.ShapeDtypeStruct((bsz, n, ATTN_WIDTH), BF16),
        compiler_params=_cparams("parallel", "parallel"),
        name="attention",
    )(qt, k_all, vt_aug)


def _scan_steps(n_rows):
    return max(1, math.ceil(math.log2(n_rows)))


def _ssm_prep_kernel(lre_ref, lim_ref, ldt_ref, bt_ref, bts_ref, c_ref, cs_ref, d_ref,
                     win_o, m_o, cout_o, mult_o, *, n_steps):
    lane = lax.broadcasted_iota(jnp.int32, (1, STATE_LANES), 1)
    low = lane < SSM_STATE
    sign_lo = jnp.where(low, -1.0, 1.0)
    k_idx = lax.broadcasted_iota(jnp.int32, (CHUNK, STATE_LANES), 0)
    step_of_row = lax.broadcasted_iota(jnp.int32, (CHUNK_LANES, STATE_LANES), 0) // SSM_GROUP
    nt = (((1,), (1,)), ((), ()))

    def outer(pw, pws, mat, mats):
        full = pw[:, None, :] * mat[None, :, :] + pws[:, None, :] * mats[None, :, :]
        return full.reshape(CHUNK * mat.shape[0], STATE_LANES)

    state_in, state_out, mults, tables = {}, {}, {}, {}
    for gi in range(2):
        taps_lo, taps_hi = None, None
        for d in range(2):
            lre = lre_ref[gi, d:d + 1, :]
            lim = lim_ref[gi, d:d + 1, :]
            dt = jnp.exp(ldt_ref[gi, d:d + 1, :])

            mag = jnp.exp(dt * lre)
            a_re, a_im = mag * jnp.cos(dt * lim), mag * jnp.sin(dt * lim)
            squares = [(a_re, a_im)]
            for _ in range(MAX_POWER_BITS - 1):
                pr, pi = squares[-1]
                squares.append((pr * pr - pi * pi, 2.0 * pr * pi))

            def power(expo, squares=squares):
                re = jnp.ones(expo.shape, F32)
                im = jnp.zeros(expo.shape, F32)
                for b, (pr, pi) in enumerate(squares):
                    take = ((expo >> b) & 1) == 1
                    re, im = jnp.where(take, re * pr - im * pi, re), jnp.where(take, re * pi + im * pr, im)
                return re, im * sign_lo

            nr, ni = a_re - 1.0, a_im
            den = lre * lre + lim * lim
            cr = (nr * lre + ni * lim) / den
            ci = (ni * lre - nr * lim) / den
            cis = ci * sign_lo
            bbar = cr * bt_ref[gi, d] + cis * bts_ref[gi, d]
            bbar_s = cr * bts_ref[gi, d] - cis * bt_ref[gi, d]
            cmat, cmat_s = c_ref[gi, d], cs_ref[gi, d]

            asc, ascs = power(k_idx)
            desc, descs = power(CHUNK - 1 - k_idx)
            nxt, nxts = power(k_idx + 1)
            conj_c = lambda pw, pws: outer(pw, pws, cmat, cmat_s) * (-sign_lo)
            if d == 0:
                state_in[gi, d] = outer(desc, descs, bbar, bbar_s)
                next_c = conj_c(nxt, nxts)
                state_out[gi, d] = next_c
                lag0 = jnp.where(step_of_row == CHUNK - 1, conj_c(desc, descs), 0.0)
                lags = jnp.where(step_of_row < CHUNK - 1, next_c, 0.0)
                hi = lax.dot_general(bbar, lags, nt, preferred_element_type=F32,
                                     precision=lax.Precision.HIGHEST)
                lo = lax.dot_general(bbar, lag0, nt, preferred_element_type=F32,
                                     precision=lax.Precision.HIGHEST)
                taps_hi = hi
                taps_lo = lo if taps_lo is None else taps_lo + lo
            else:
                state_in[gi, d] = outer(asc, ascs, bbar, bbar_s)
                rev, revs = power(CHUNK - k_idx)
                state_out[gi, d] = conj_c(rev, revs)
                lo = lax.dot_general(bbar, conj_c(desc, descs), nt, preferred_element_type=F32,
                                     precision=lax.Precision.HIGHEST)
                taps_lo = lo if taps_lo is None else taps_lo + lo

            sr, sis = power(jnp.full((1, STATE_LANES), CHUNK, jnp.int32))
            chain = []
            for k in range(n_steps):
                chain.append((sr, sis))
                si = sis * sign_lo
                sr, sis = sr * sr - si * si, 2.0 * sr * si * sign_lo
            mults[gi, d] = chain
            r_idx = lax.broadcasted_iota(jnp.int32, (SUBLANES, STATE_LANES), 0)
            tables[gi, d] = power(CHUNK * (r_idx + 1) if d == 0 else CHUNK * (SUBLANES - r_idx))

        qq = lax.broadcasted_iota(jnp.int32, (SSM_GROUP, CHUNK_LANES), 0)
        ll = lax.broadcasted_iota(jnp.int32, (SSM_GROUP, CHUNK_LANES), 1)
        taps_lo = taps_lo + jnp.where(ll == qq + (CHUNK - 1) * SSM_GROUP, d_ref[gi], 0.0)
        taps = jnp.concatenate([taps_lo, taps_hi], axis=1)
        for j in range(CHUNK):
            off = (CHUNK - 1 - j) * SSM_GROUP
            m_o[gi, j * SSM_GROUP:(j + 1) * SSM_GROUP, :] = taps[:, off:off + CHUNK_LANES].astype(BF16)

    def swap(x):
        return pltpu.roll(x, SSM_STATE, 1)

    mult_o[...] = jnp.zeros(mult_o.shape, F32)
    wide_low = lax.broadcasted_iota(jnp.int32, (CHUNK_LANES, STATE_LANES), 1) < SSM_STATE
    zeros_c = jnp.zeros((SSM_STATE, CHUNK_LANES), F32)
    for d in range(2):
        s0, s1 = state_in[0, d], state_in[1, d]
        re_blk = jnp.concatenate([jnp.where(wide_low, s0, 0.0), jnp.where(wide_low, 0.0, swap(s1))], axis=0)
        im_blk = jnp.concatenate([jnp.where(wide_low, swap(s0), 0.0), jnp.where(wide_low, 0.0, s1)], axis=0)
        win_o[0, :, (2 * d) * LANES:(2 * d + 1) * LANES] = re_blk.astype(BF16)
        win_o[0, :, (2 * d + 1) * LANES:(2 * d + 2) * LANES] = im_blk.astype(BF16)
        t0, t1 = state_out[0, d].T, state_out[1, d].T
        for part in range(2):
            rows = slice(part * SSM_STATE, (part + 1) * SSM_STATE)
            blk = jnp.concatenate([jnp.concatenate([t0[rows], zeros_c], axis=1),
                                   jnp.concatenate([zeros_c, t1[rows]], axis=1)], axis=0)
            cout_o[0, (2 * d + part) * LANES:(2 * d + part + 1) * LANES, :] = blk.astype(BF16)
        for k in range(n_steps):
            (sr0, sis0), (sr1, sis1) = mults[0, d][k], mults[1, d][k]
            mult_o[0, k:k + 1, (2 * d) * LANES:(2 * d + 1) * LANES] = jnp.where(low, sr0, sr1)
            mult_o[0, k:k + 1, (2 * d + 1) * LANES:(2 * d + 2) * LANES] = jnp.where(low, -sis0, sis1)
        (tr0, tis0), (tr1, tis1) = tables[0, d], tables[1, d]
        mult_o[0, POWER_ROW0:POWER_ROW0 + SUBLANES, (2 * d) * LANES:(2 * d + 1) * LANES] = jnp.where(low, tr0, tr1)
        mult_o[0, POWER_ROW0:POWER_ROW0 + SUBLANES, (2 * d + 1) * LANES:(2 * d + 2) * LANES] = jnp.where(
            low, -tis0, tis1)


def _ssm_prep(lam_re, lam_im, log_dt, b_re, b_im, c_re, c_im, d_skip, n_steps):
    g = N_SSM_GROUPS
    dup = lambda a: jnp.concatenate([a, a], axis=-1)
    lre = dup(jnp.swapaxes(lam_re, 0, 1))
    lim = dup(jnp.swapaxes(lam_im, 0, 1))
    ldt = jnp.broadcast_to(jnp.swapaxes(log_dt, 0, 1)[..., None], (g, 2, STATE_LANES))
    btr = jnp.transpose(b_re, (1, 0, 3, 2))
    bti = jnp.transpose(b_im, (1, 0, 3, 2))
    bt, bts = jnp.concatenate([btr, bti], -1), jnp.concatenate([bti, btr], -1)
    cr, ci = jnp.swapaxes(c_re, 0, 1), jnp.swapaxes(c_im, 0, 1)
    cm, cms = jnp.concatenate([cr, ci], -1), jnp.concatenate([ci, cr], -1)
    dsk = jnp.pad(d_skip, ((0, 0), (CHUNK_LANES - SSM_GROUP, 0))).reshape(g, 1, CHUNK_LANES)
    n_pad = POWER_ROW0 + SUBLANES
    vec = pl.BlockSpec((2, 2, STATE_LANES), lambda i: (i, 0, 0))
    mat = pl.BlockSpec((2, 2, SSM_GROUP, STATE_LANES), lambda i: (i, 0, 0, 0))
    pair_sq = pl.BlockSpec((1, 2 * CHUNK_LANES, 4 * LANES), lambda i: (i, 0, 0))
    return pl.pallas_call(
        functools.partial(_ssm_prep_kernel, n_steps=n_steps),
        grid=(g // 2,),
        in_specs=[vec, vec, vec, mat, mat, mat, mat,
                  pl.BlockSpec((2, 1, CHUNK_LANES), lambda i: (i, 0, 0))],
        out_specs=[pair_sq, pl.BlockSpec((2, CHUNK_LANES, CHUNK_LANES), lambda i: (i, 0, 0)), pair_sq,
                   pl.BlockSpec((1, n_pad, 4 * LANES), lambda i: (i, 0, 0))],
        out_shape=[jax.ShapeDtypeStruct((g // 2, 2 * CHUNK_LANES, 4 * LANES), BF16),
                   jax.ShapeDtypeStruct((g, CHUNK_LANES, CHUNK_LANES), BF16),
                   jax.ShapeDtypeStruct((g // 2, 4 * LANES, 2 * CHUNK_LANES), BF16),
                   jax.ShapeDtypeStruct((g // 2, n_pad, 4 * LANES), F32)],
        compiler_params=_cparams("parallel"),
        name="ssm_prep",
    )(lre, lim, ldt, bt, bts, cm, cms, dsk)


def _complex_step(re, im, sr, si, ar, ai):
    return re + ar * sr - ai * si, im + ar * si + ai * sr


def _ssm_kernel(u_ref, win_ref, m_ref, cout_ref, mult_ref, y_o, loc_scr, car_scr, *, n_ctx_rows, n_steps):
    n_pairs = win_ref.shape[0]
    rows = u_ref.shape[2]
    n_lat_rows = rows - n_ctx_rows
    n_blk = rows // SUBLANES
    n_blk_pad = -(-n_blk // SUBLANES) * SUBLANES
    in_blk_row = lax.broadcasted_iota(jnp.int32, (SUBLANES, LANES), 0)
    blk = lax.broadcasted_iota(jnp.int32, (n_blk_pad, LANES), 0)
    tile = lambda t: jnp.broadcast_to(t[None], (n_blk, SUBLANES, LANES)).reshape(rows, LANES)
    local_steps = SUBLANES.bit_length() - 1
    to_rev = lambda a: jnp.concatenate([a[n_ctx_rows:], a[:n_ctx_rows]], axis=0)

    def shifted(a, s, idx, n, down):
        if down:
            return jnp.where(idx >= s, pltpu.roll(a, s, 0), 0.0)
        return jnp.where(idx < n - s, pltpu.roll(a, a.shape[0] - s, 0), 0.0)

    def scan(pp, x):
        def mult(k0, k1, d):
            return (mult_ref[pp, k0:k1, 2 * d * LANES:(2 * d + 1) * LANES],
                    mult_ref[pp, k0:k1, (2 * d + 1) * LANES:(2 * d + 2) * LANES])

        state = [[x[:, :LANES], x[:, LANES:2 * LANES]],
                 [to_rev(x[:, 2 * LANES:3 * LANES]), to_rev(x[:, 3 * LANES:])]]
        for d, down in ((0, True), (1, False)):
            re, im = state[d]
            for k in range(local_steps):
                s = 1 << k
                keep = in_blk_row >= s if down else in_blk_row < SUBLANES - s
                ar, ai = (tile(jnp.where(keep, m, 0.0)) for m in mult(k, k + 1, d))
                shift = s if down else rows - s
                re, im = _complex_step(re, im, pltpu.roll(re, shift, 0), pltpu.roll(im, shift, 0), ar, ai)
            loc_scr[pp, 2 * d], loc_scr[pp, 2 * d + 1] = re, im
            state[d] = [re, im]

            end_row = SUBLANES - 1 if down else 0
            pad = jnp.zeros((n_blk_pad - n_blk, LANES), F32)
            ere = jnp.concatenate([loc_scr[pp, 2 * d, pl.ds(end_row, n_blk, stride=SUBLANES), :], pad], axis=0)
            eim = jnp.concatenate([loc_scr[pp, 2 * d + 1, pl.ds(end_row, n_blk, stride=SUBLANES), :], pad],
                                  axis=0)
            for k in range(local_steps, n_steps):
                ar, ai = mult(k, k + 1, d)
                s = 1 << (k - local_steps)
                ere, eim = _complex_step(ere, eim, shifted(ere, s, blk, n_blk_pad, down),
                                         shifted(eim, s, blk, n_blk_pad, down), ar, ai)
            for c, ends in enumerate((ere, eim)):
                entering = shifted(ends, 1, blk, n_blk_pad, down)[:n_blk]
                for r in range(SUBLANES):
                    car_scr[pp, 2 * d + c, pl.ds(r, n_blk, stride=SUBLANES), :] = entering

        for d in range(2):
            tre, tim = mult(POWER_ROW0, POWER_ROW0 + SUBLANES, d)
            re, im = state[d]
            state[d] = list(_complex_step(re, im, car_scr[pp, 2 * d], car_scr[pp, 2 * d + 1],
                                          tile(tre), tile(tim)))
        return state

    xs = [jnp.dot(jnp.concatenate([u_ref[0, 2 * pp], u_ref[0, 2 * pp + 1]], axis=1), win_ref[pp],
                  preferred_element_type=F32) for pp in range(n_pairs)]
    prev = lambda a: pltpu.roll(a, 1, 0)[n_ctx_rows:]
    nxt = lambda a: pltpu.roll(a, rows - 1, 0)[:n_lat_rows]
    for pp in range(n_pairs):
        state = scan(pp, xs[pp])
        st = jnp.concatenate([prev(state[0][0]), prev(state[0][1]), nxt(state[1][0]), nxt(state[1][1])],
                             axis=1).astype(BF16)
        y = jnp.dot(st, cout_ref[pp], preferred_element_type=F32)
        for gi in range(2):
            g = 2 * pp + gi
            y_o[0, g] = (y[:, gi * CHUNK_LANES:(gi + 1) * CHUNK_LANES]
                         + jnp.dot(u_ref[0, g, n_ctx_rows:, :], m_ref[g],
                                   preferred_element_type=F32)).astype(BF16)


def _ssm(u_cf, win_pair, m, cout_pair, mult, n_ctx_rows, n_steps):
    bsz, g, rows, _ = u_cf.shape
    pairs = SSM_PAIRS_PER_STEP
    assert rows % SUBLANES == 0 and n_ctx_rows % 16 == 0 and g % (2 * pairs) == 0
    pair_spec = lambda a: pl.BlockSpec((pairs,) + a.shape[1:], lambda b, i: (i, 0, 0))
    return pl.pallas_call(
        functools.partial(_ssm_kernel, n_ctx_rows=n_ctx_rows, n_steps=n_steps),
        grid=(bsz, g // (2 * pairs)),
        in_specs=[pl.BlockSpec((1, 2 * pairs, rows, CHUNK_LANES), lambda b, i: (b, i, 0, 0)),
                  pair_spec(win_pair),
                  pl.BlockSpec((2 * pairs, CHUNK_LANES, CHUNK_LANES), lambda b, i: (i, 0, 0)),
                  pair_spec(cout_pair), pair_spec(mult)],
        out_specs=pl.BlockSpec((1, 2 * pairs, rows - n_ctx_rows, CHUNK_LANES), lambda b, i: (b, i, 0, 0)),
        out_shape=jax.ShapeDtypeStruct((bsz, g, rows - n_ctx_rows, CHUNK_LANES), BF16),
        scratch_shapes=[pltpu.VMEM((pairs, 4, rows, LANES), F32), pltpu.VMEM((pairs, 4, rows, LANES), F32)],
        compiler_params=_cparams("parallel", "parallel"),
        name="ssm",
    )(u_cf, win_pair, m, cout_pair, mult)


def _final_kernel(x_ref, at_ref, ga_ref, y_ref, gs_ref, gate_ref, wglu_ref, bglu_ref, wout_ref,
                  pg_ref, o_ref, y_scr):
    tm = x_ref.shape[1]
    sub = tm // FINAL_SUBTILES
    gain = gate_ref[0] * pg_ref[...]
    def finish(h, out):
        rows = slice(h * sub, (h + 1) * sub)
        ms = jnp.mean(out * out, axis=-1, keepdims=True)
        o_ref[0, rows] = x_ref[0, rows] + out * lax.rsqrt(ms + NORM_EPS) * gain

    pending = None
    for h in range(FINAL_SUBTILES):
        rows = slice(h * sub, (h + 1) * sub)
        _from_chunk_lanes(y_ref, y_scr, h * sub // CHUNK, sub // CHUNK)
        y = jax.nn.gelu(jnp.concatenate([y_scr[col, rows] for col in range(SSM_WIDTH // LANES)], axis=1))
        t = jnp.dot(y.astype(BF16), wglu_ref[...], preferred_element_type=F32) + bglu_ref[...]
        s = (y * jax.nn.sigmoid(t) * gs_ref[0, rows].astype(F32)).astype(BF16)
        a = at_ref[0, rows] * ga_ref[0, rows]
        out = jnp.dot(jnp.concatenate([a, s], axis=1), wout_ref[...], preferred_element_type=F32)
        if pending is not None:
            finish(*pending)
        pending = (h, out)
    finish(*pending)


def _final(x, attn, ga, y_cf, gs, gate, wglu_bf, b_glu, wout_bf, post_g, tm):
    bsz, n, d = x.shape
    assert n % tm == 0 and tm % (CHUNK * 16) == 0
    return pl.pallas_call(
        _final_kernel,
        grid=(bsz, n // tm),
        in_specs=[_row_spec(tm, d), _row_spec(tm, ATTN_WIDTH), _row_spec(tm, ATTN_WIDTH),
                  pl.BlockSpec((1, N_SSM_GROUPS, tm // CHUNK, CHUNK_LANES), lambda b, i: (b, 0, i, 0)),
                  _row_spec(tm, SSM_WIDTH), _batch_vec_spec(d),
                  _const_spec((SSM_WIDTH, SSM_WIDTH)), _const_spec((1, SSM_WIDTH)),
                  _const_spec((d, d)), _const_spec((1, d))],
        out_specs=_row_spec(tm, d),
        out_shape=jax.ShapeDtypeStruct((bsz, n, d), F32),
        scratch_shapes=[pltpu.VMEM((SSM_WIDTH // LANES, tm, LANES), F32)],
        compiler_params=_cparams("parallel", "parallel"),
        name="final",
    )(x, attn, ga, y_cf, gs, gate, wglu_bf, b_glu, wout_bf, post_g)


def _rope_tables(n_ctx, n_lat):
    t = jnp.arange(n_lat, dtype=jnp.int32)
    row_pos = (t // GRID_W).astype(F32)
    col_pos = (t % GRID_W).astype(F32)
    inv_freq = ROPE_THETA ** (-jnp.arange(ROPE_FREQS, dtype=F32) / ROPE_FREQS)
    ang_r = row_pos[:, None] * inv_freq
    ang_c = col_pos[:, None] * inv_freq
    cos = jnp.concatenate([jnp.cos(ang_r)] * 2 + [jnp.cos(ang_c)] * 2, axis=1)
    sin = jnp.concatenate([-jnp.sin(ang_r), jnp.sin(ang_r), -jnp.sin(ang_c), jnp.sin(ang_c)], axis=1)
    cos = jnp.concatenate([jnp.ones((n_ctx, HEAD_DIM), F32), cos], axis=0)
    sin = jnp.concatenate([jnp.zeros((n_ctx, HEAD_DIM), F32), sin], axis=0)
    return jnp.tile(cos, (1, LANES // HEAD_DIM)), jnp.tile(sin, (1, LANES // HEAD_DIM))


def _layer(x, ctx, c, c_ctx, w_ada, b_ada, pre_g, post_g, w_in, q_g, k_g, lam_re, lam_im, log_dt,
           b_re, b_im, c_re, c_im, d_skip, w_glu, b_glu, w_out):
    bsz, n_lat, d = x.shape
    n_ctx = ctx.shape[1]
    assert n_lat % GRID_W == 0

    cvecs = jnp.zeros((8, d), F32).at[:bsz].set(c).at[bsz].set(c_ctx)
    mod = _adaln(cvecs, w_ada, b_ada)
    shift, scale, gate = (mod[:, i * d:(i + 1) * d] for i in range(3))
    lat = lambda a: a[:bsz].reshape(bsz, 1, d)
    cvec = lambda a: a[bsz].reshape(1, 1, d)

    head = jnp.arange(ATTN_WIDTH) // HEAD_DIM
    bd = (head[:, None] == head[None, :]).astype(BF16) * (1.0 / HEAD_DIM)
    qg = jnp.tile(q_g, N_Q_HEADS).reshape(1, ATTN_WIDTH)
    kg = jnp.tile(k_g, N_KV_HEADS).reshape(1, KV_WIDTH)
    cos, sin = _rope_tables(n_ctx, n_lat)

    qt, k_all, vt_aug, ga, u_cf, gs = _inproj(x, ctx, lat(shift), lat(scale), cvec(shift), cvec(scale),
                                              pre_g.reshape(1, d), w_in.astype(BF16), qg, kg, cos, sin,
                                              bd, tm=256)
    attn = _attention(qt, k_all, vt_aug, tq=256, tk=8448, sub=256, lookahead=5, heads_per_pass=8)

    rows = (n_ctx + n_lat) // CHUNK
    n_steps = _scan_steps(rows)
    win_pair, m_op, cout_pair, mult = _ssm_prep(lam_re, lam_im, log_dt, b_re, b_im, c_re, c_im, d_skip,
                                                n_steps)
    y_cf = _ssm(u_cf, win_pair, m_op, cout_pair, mult, n_ctx // CHUNK, n_steps)

    return _final(x, attn, ga, y_cf, gs, lat(gate), w_glu.astype(BF16), b_glu.reshape(1, SSM_WIDTH),
                  w_out.astype(BF16), post_g.reshape(1, d), tm=1024)


def kernel(x, c, ctx, c_ctx, w_ada, b_ada, pre_norm, post_norm, w_in, q_norm, k_norm, ssm_lam_re,
           ssm_lam_im, ssm_log_dt, ssm_b_re, ssm_b_im, ssm_c_re, ssm_c_im, ssm_d, w_glu, b_glu, w_out):
    depth = w_ada.shape[0]
    assert depth == 1, "context stream update between layers is not implemented"
    return _layer(x, ctx, c, c_ctx, w_ada[0], b_ada[0], pre_norm[0], post_norm[0], w_in[0], q_norm[0],
                  k_norm[0], ssm_lam_re[0], ssm_lam_im[0], ssm_log_dt[0], ssm_b_re[0], ssm_b_im[0],
                  ssm_c_re[0], ssm_c_im[0], ssm_d[0], w_glu[0], b_glu[0], w_out[0])
```

```python
import functools
import math

import jax
import jax.numpy as jnp
from jax import lax
from jax.experimental import pallas as pl
from jax.experimental.pallas import tpu as pltpu

F32 = jnp.float32
BF16 = jnp.bfloat16

D_MODEL = 1024
HEAD_DIM = 64
N_Q_HEADS = 8
N_KV_HEADS = 2
ATTN_WIDTH = N_Q_HEADS * HEAD_DIM
KV_WIDTH = N_KV_HEADS * HEAD_DIM
SSM_WIDTH = 512
SSM_GROUP = 16
N_SSM_GROUPS = SSM_WIDTH // SSM_GROUP
SSM_STATE = 64
GRID_W = 64
ROPE_THETA = 10000.0
ROPE_FREQS = 16
NORM_EPS = 1e-6
ATTN_SCALE = HEAD_DIM ** -0.5
LOG2_E = math.log2(math.e)
Q_END = ATTN_WIDTH
K_END = Q_END + KV_WIDTH
V_END = K_END + KV_WIDTH
GA_END = V_END + ATTN_WIDTH
U_END = GA_END + SSM_WIDTH
IN_WIDTH = U_END + SSM_WIDTH

CHUNK = 16
CHUNK_LANES = CHUNK * SSM_GROUP
LANES = 128
GROUPS_PER_VREG = LANES // SSM_GROUP
STATE_LANES = 2 * SSM_STATE
SUBLANES = 8
POWER_ROW0 = 16
MAX_POWER_BITS = 8
SSM_PAIRS_PER_STEP = 2
FINAL_SUBTILES = 4
V_ROWS = HEAD_DIM + 16
VMEM_LIMIT = 48 * 1024 * 1024
NEG_BIG = -1e30


def _cparams(*sem):
    return pltpu.CompilerParams(dimension_semantics=sem, vmem_limit_bytes=VMEM_LIMIT)


def _row_spec(tm, width):
    return pl.BlockSpec((1, tm, width), lambda b, i: (b, i, 0))


def _const_spec(shape):
    return pl.BlockSpec(shape, lambda b, i: (0,) * len(shape))


def _batch_vec_spec(width):
    return pl.BlockSpec((1, 1, width), lambda b, i: (b, 0, 0))


def _adaln_kernel(c_ref, w_ref, b_ref, o_ref):
    c = c_ref[...]
    s = c * jax.nn.sigmoid(c)
    o_ref[...] = jnp.dot(s, w_ref[...], preferred_element_type=F32,
                         precision=lax.Precision.HIGHEST) + b_ref[...]


def _adaln(cvecs, w_ada, b_ada):
    rows, d = cvecs.shape
    n = w_ada.shape[1]
    tn = 1024
    return pl.pallas_call(
        _adaln_kernel,
        grid=(n // tn,),
        in_specs=[pl.BlockSpec((rows, d), lambda j: (0, 0)),
                  pl.BlockSpec((d, tn), lambda j: (0, j)),
                  pl.BlockSpec((1, tn), lambda j: (0, j))],
        out_specs=pl.BlockSpec((rows, tn), lambda j: (0, j)),
        out_shape=jax.ShapeDtypeStruct((rows, n), F32),
        compiler_params=_cparams("arbitrary"),
        name="adaln",
    )(cvecs, w_ada, b_ada.reshape(1, n))


def _lane_group(rows):
    return lax.broadcasted_iota(jnp.int32, (rows, LANES), 1) // SSM_GROUP


def _to_chunk_lanes(u_scr, u_o):
    rt = u_o.shape[2]
    grp = _lane_group(rt)
    for col in range(SSM_WIDTH // LANES):
        rolled = []
        for step in range(CHUNK):
            s = u_scr[col, pl.ds(step, rt, stride=CHUNK), :]
            rolled.append([s if k == 0 else pltpu.roll(s, k * SSM_GROUP, 1)
                           for k in range(GROUPS_PER_VREG)])
        for g_lo in range(GROUPS_PER_VREG):
            for half in range(CHUNK // GROUPS_PER_VREG):
                out = None
                for s8 in range(GROUPS_PER_VREG):
                    piece = rolled[half * GROUPS_PER_VREG + s8][(s8 - g_lo) % GROUPS_PER_VREG]
                    out = piece if out is None else jnp.where(grp == s8, piece, out)
                u_o[0, col * GROUPS_PER_VREG + g_lo, :, half * LANES:(half + 1) * LANES] = out.astype(BF16)


def _from_chunk_lanes(y_ref, y_scr, r0, rt):
    grp = _lane_group(rt)
    for col in range(SSM_WIDTH // LANES):
        rolled = {}
        for g_lo in range(GROUPS_PER_VREG):
            for half in range(CHUNK // GROUPS_PER_VREG):
                s = y_ref[0, col * GROUPS_PER_VREG + g_lo, r0:r0 + rt,
                          half * LANES:(half + 1) * LANES].astype(F32)
                rolled[g_lo, half] = [s if k == 0 else pltpu.roll(s, k * SSM_GROUP, 1)
                                      for k in range(GROUPS_PER_VREG)]
        for step in range(CHUNK):
            half, s8 = divmod(step, GROUPS_PER_VREG)
            out = None
            for g_lo in range(GROUPS_PER_VREG):
                piece = rolled[g_lo, half][(g_lo - s8) % GROUPS_PER_VREG]
                out = piece if out is None else jnp.where(grp == g_lo, piece, out)
            y_scr[col, pl.ds(r0 * CHUNK + step, rt, stride=CHUNK), :] = out


def _head_mean_sq(z, bd):
    return jnp.dot((z * z).astype(BF16), bd, preferred_element_type=F32)


def _swap16(x):
    w = x.shape[1]
    lane = lax.broadcasted_iota(jnp.int32, x.shape, 1)
    return jnp.where((lane & 16) == 0, pltpu.roll(x, w - 16, 1), pltpu.roll(x, 16, 1))


def _rope(x, cos, sin_signed):
    cols = []
    for c in range(x.shape[1] // LANES):
        xc = x[:, c * LANES:(c + 1) * LANES]
        cols.append(xc * cos + _swap16(xc) * sin_signed)
    return cols[0] if len(cols) == 1 else jnp.concatenate(cols, axis=1)


def _silu(z):
    return z * jax.nn.sigmoid(z)


def _inproj_kernel(x_ref, c_ref, shl_ref, scl_ref, shc_ref, scc_ref, pg_ref, w_ref, qg_ref, kg_ref,
                   cos_ref, sin_ref, bd_ref, qt_o, k_o, vt_o, ga_o, u_o, gs_o, v_scr, u_scr, *, n_ctx_tiles):
    is_ctx = pl.program_id(1) < n_ctx_tiles
    x = jnp.where(is_ctx, c_ref[0], x_ref[0])
    shift = jnp.where(is_ctx, shc_ref[0], shl_ref[0])
    scale = jnp.where(is_ctx, scc_ref[0], scl_ref[0])
    ms = jnp.mean(x * x, axis=-1, keepdims=True)
    xn = x * lax.rsqrt(ms + NORM_EPS) * pg_ref[...]
    h = (xn * (1.0 + scale) + shift).astype(BF16)

    def proj(a, b):
        return jnp.dot(h, w_ref[:, a:b], preferred_element_type=F32)

    u = proj(GA_END, U_END)
    for col in range(SSM_WIDTH // LANES):
        u_scr[col] = u[:, col * LANES:(col + 1) * LANES]
    _to_chunk_lanes(u_scr, u_o)

    cos = cos_ref[...]
    sin = sin_ref[...]
    zq = proj(0, Q_END)
    zk = proj(Q_END, K_END)
    gs_o[0] = _silu(proj(U_END, IN_WIDTH)).astype(BF16)
    qn = zq * lax.rsqrt(_head_mean_sq(zq, bd_ref[...]) + NORM_EPS) * qg_ref[...]
    kn = zk * lax.rsqrt(_head_mean_sq(zk, bd_ref[:KV_WIDTH, :KV_WIDTH]) + NORM_EPS) * kg_ref[...]
    ga_o[0] = _silu(proj(V_END, GA_END)).astype(BF16)
    v_scr[...] = proj(K_END, V_END)

    k_o[0] = _rope(kn, cos, sin).astype(BF16)
    q = _rope(qn, cos, sin) * (ATTN_SCALE * LOG2_E)
    group = N_Q_HEADS // N_KV_HEADS
    none = jnp.zeros((HEAD_DIM, q.shape[0]), BF16)
    for c in range(ATTN_WIDTH // LANES):
        qt = q[:, c * LANES:(c + 1) * LANES].T.astype(BF16)
        for par in range(2):
            head = 2 * c + par
            kv = head // group
            qt_o[0, head, kv * HEAD_DIM:(kv + 1) * HEAD_DIM, :] = qt[par * HEAD_DIM:(par + 1) * HEAD_DIM]
            qt_o[0, head, (1 - kv) * HEAD_DIM:(2 - kv) * HEAD_DIM, :] = none

    vt = v_scr[...].T.astype(BF16)
    tm = vt.shape[1]
    ones_row = lax.broadcasted_iota(jnp.int32, (V_ROWS - HEAD_DIM, tm), 0) == 0
    for j in range(N_KV_HEADS):
        vt_o[0, j, :HEAD_DIM, :] = vt[j * HEAD_DIM:(j + 1) * HEAD_DIM]
        vt_o[0, j, HEAD_DIM:, :] = jnp.where(ones_row, 1.0, 0.0).astype(BF16)


def _inproj(x, ctx, shift_l, scale_l, shift_c, scale_c, pre_g, w_bf, qg, kg, cos, sin, bd, tm):
    bsz, n_lat, d = x.shape
    n_ctx = ctx.shape[1]
    assert n_ctx % tm == 0 and n_lat % tm == 0 and tm % (CHUNK * 16) == 0
    nct = n_ctx // tm
    n_tot = n_ctx + n_lat
    lat_rows = lambda w: pl.BlockSpec((1, tm, w), lambda b, i: (b, jnp.maximum(i - nct, 0), 0))
    all_rows = lambda w: pl.BlockSpec((1, tm, w), lambda b, i: (b, i, 0))
    ctx_vec = pl.BlockSpec((1, 1, d), lambda b, i: (0, 0, 0))
    table = pl.BlockSpec((tm, LANES), lambda b, i: (i, 0))
    return pl.pallas_call(
        functools.partial(_inproj_kernel, n_ctx_tiles=nct),
        grid=(bsz, n_tot // tm),
        in_specs=[lat_rows(d),
                  pl.BlockSpec((1, tm, d), lambda b, i: (b, jnp.minimum(i, nct - 1), 0)),
                  _batch_vec_spec(d), _batch_vec_spec(d), ctx_vec, ctx_vec, _const_spec((1, d)),
                  _const_spec((d, IN_WIDTH)), _const_spec((1, ATTN_WIDTH)), _const_spec((1, KV_WIDTH)),
                  table, table, _const_spec((ATTN_WIDTH, ATTN_WIDTH))],
        out_specs=[pl.BlockSpec((1, N_Q_HEADS, LANES, tm), lambda b, i: (b, 0, 0, jnp.maximum(i - nct, 0))),
                   all_rows(KV_WIDTH),
                   pl.BlockSpec((1, N_KV_HEADS, V_ROWS, tm), lambda b, i: (b, 0, 0, i)),
                   lat_rows(ATTN_WIDTH),
                   pl.BlockSpec((1, N_SSM_GROUPS, tm // CHUNK, CHUNK_LANES), lambda b, i: (b, 0, i, 0)),
                   lat_rows(SSM_WIDTH)],
        out_shape=[jax.ShapeDtypeStruct((bsz, N_Q_HEADS, LANES, n_lat), BF16),
                   jax.ShapeDtypeStruct((bsz, n_tot, KV_WIDTH), BF16),
                   jax.ShapeDtypeStruct((bsz, N_KV_HEADS, V_ROWS, n_tot), BF16),
                   jax.ShapeDtypeStruct((bsz, n_lat, ATTN_WIDTH), BF16),
                   jax.ShapeDtypeStruct((bsz, N_SSM_GROUPS, n_tot // CHUNK, CHUNK_LANES), BF16),
                   jax.ShapeDtypeStruct((bsz, n_lat, SSM_WIDTH), BF16)],
        scratch_shapes=[pltpu.VMEM((tm, KV_WIDTH), F32), pltpu.VMEM((SSM_WIDTH // LANES, tm, LANES), F32)],
        compiler_params=_cparams("parallel", "arbitrary"),
        name="inproj",
    )(x, ctx, shift_l, scale_l, shift_c, scale_c, pre_g, w_bf, qg, kg, cos, sin, bd)


def _attn_kernel(qt_ref, k_ref, vt_ref, o_ref, *, tk, sub, lookahead, heads_per_pass):
    tq = qt_ref.shape[3]
    n_keys = k_ref.shape[1]
    group = N_Q_HEADS // N_KV_HEADS

    outs = []
    for h0 in range(0, N_Q_HEADS, heads_per_pass):
        heads = list(range(h0, h0 + heads_per_pass))
        q_wide = [qt_ref[0, h] for h in heads]

        def body(t, carry, heads=heads, q_wide=q_wide):
            tasks = [(j, i) for j in range(tk // sub) for i in range(len(heads))]
            state = list(carry)
            scores = {}

            def keys_at(j):
                return pl.ds(pl.multiple_of(t * tk + j * sub, sub), sub)

            def issue(n):
                j, i = tasks[n]
                scores[n] = jnp.dot(k_ref[0, keys_at(j), :], q_wide[i],
                                    preferred_element_type=F32)

            def consume(n):
                j, i = tasks[n]
                s = scores.pop(n)
                m_old, acc = state[i]
                vt = vt_ref[0, heads[i] // group, :, keys_at(j)]
                m_cols, p_cols = [], []
                for c in range(0, tq, LANES):
                    s_c = s[:, c:c + LANES]
                    m_c = jnp.maximum(m_old[:, c:c + LANES], jnp.max(s_c, axis=0, keepdims=True))
                    m_cols.append(m_c)
                    p_cols.append(jnp.exp2((s_c - m_c).astype(BF16)))
                m_new = jnp.concatenate(m_cols, axis=1)
                p = jnp.concatenate(p_cols, axis=1)
                acc = jnp.exp2(m_old - m_new) * acc + jnp.dot(vt, p, preferred_element_type=F32)
                state[i] = (m_new, acc)

            for n in range(len(tasks) + lookahead):
                if n < len(tasks):
                    issue(n)
                if n >= lookahead:
                    consume(n - lookahead)
            return tuple(state)

        init = tuple((jnp.full((1, tq), NEG_BIG, F32), jnp.zeros((V_ROWS, tq), F32)) for _ in heads)
        for _, acc in lax.fori_loop(0, n_keys // tk, body, init):
            outs.append(acc[:HEAD_DIM] / acc[HEAD_DIM:HEAD_DIM + 1])

    for c in range(N_Q_HEADS // 2):
        pair = jnp.concatenate([outs[2 * c], outs[2 * c + 1]], axis=0)
        o_ref[0, :, c * LANES:(c + 1) * LANES] = pair.T.astype(BF16)


def _attention(qt, k_all, vt_aug, tq, tk, sub, lookahead, heads_per_pass):
    bsz, _, _, n = qt.shape
    n_keys = k_all.shape[1]
    assert n_keys % tk == 0 and tk % sub == 0 and n % tq == 0
    return pl.pallas_call(
        functools.partial(_attn_kernel, tk=tk, sub=sub, lookahead=lookahead,
                          heads_per_pass=heads_per_pass),
        grid=(bsz, n // tq),
        in_specs=[pl.BlockSpec((1, N_Q_HEADS, LANES, tq), lambda b, i: (b, 0, 0, i)),
                  pl.BlockSpec((1, n_keys, KV_WIDTH), lambda b, i: (b, 0, 0)),
                  pl.BlockSpec((1, N_KV_HEADS, V_ROWS, n_keys), lambda b, i: (b, 0, 0, 0))],
        out_specs=_row_spec(tq, ATTN_WIDTH),
        out_shape=jax.ShapeDtypeStruct((bsz, n, ATTN_WIDTH), BF16),
        compiler_params=_cparams("parallel", "parallel"),
        name="attention",
    )(qt, k_all, vt_aug)


def _scan_steps(n_rows):
    return max(1, math.ceil(math.log2(n_rows)))


def _ssm_prep_kernel(lre_ref, lim_ref, ldt_ref, bt_ref, bts_ref, c_ref, cs_ref, d_ref,
                     win_o, m_o, cout_o, mult_o, *, n_steps):
    lane = lax.broadcasted_iota(jnp.int32, (1, STATE_LANES), 1)
    low = lane < SSM_STATE
    sign_lo = jnp.where(low, -1.0, 1.0)
    k_idx = lax.broadcasted_iota(jnp.int32, (CHUNK, STATE_LANES), 0)
    step_of_row = lax.broadcasted_iota(jnp.int32, (CHUNK_LANES, STATE_LANES), 0) // SSM_GROUP
    nt = (((1,), (1,)), ((), ()))

    def outer(pw, pws, mat, mats):
        full = pw[:, None, :] * mat[None, :, :] + pws[:, None, :] * mats[None, :, :]
        return full.reshape(CHUNK * mat.shape[0], STATE_LANES)

    state_in, state_out, mults, tables = {}, {}, {}, {}
    for gi in range(2):
        taps_lo, taps_hi = None, None
        for d in range(2):
            lre = lre_ref[gi, d:d + 1, :]
            lim = lim_ref[gi, d:d + 1, :]
            dt = jnp.exp(ldt_ref[gi, d:d + 1, :])

            mag = jnp.exp(dt * lre)
            a_re, a_im = mag * jnp.cos(dt * lim), mag * jnp.sin(dt * lim)
            squares = [(a_re, a_im)]
            for _ in range(MAX_POWER_BITS - 1):
                pr, pi = squares[-1]
                squares.append((pr * pr - pi * pi, 2.0 * pr * pi))

            def power(expo, squares=squares):
                re = jnp.ones(expo.shape, F32)
                im = jnp.zeros(expo.shape, F32)
                for b, (pr, pi) in enumerate(squares):
                    take = ((expo >> b) & 1) == 1
                    re, im = jnp.where(take, re * pr - im * pi, re), jnp.where(take, re * pi + im * pr, im)
                return re, im * sign_lo

            nr, ni = a_re - 1.0, a_im
            den = lre * lre + lim * lim
            cr = (nr * lre + ni * lim) / den
            ci = (ni * lre - nr * lim) / den
            cis = ci * sign_lo
            bbar = cr * bt_ref[gi, d] + cis * bts_ref[gi, d]
            bbar_s = cr * bts_ref[gi, d] - cis * bt_ref[gi, d]
            cmat, cmat_s = c_ref[gi, d], cs_ref[gi, d]

            asc, ascs = power(k_idx)
            desc, descs = power(CHUNK - 1 - k_idx)
            nxt, nxts = power(k_idx + 1)
            conj_c = lambda pw, pws: outer(pw, pws, cmat, cmat_s) * (-sign_lo)
            if d == 0:
                state_in[gi, d] = outer(desc, descs, bbar, bbar_s)
                state_out[gi, d] = conj_c(nxt, nxts)
                lag0 = jnp.where(step_of_row == CHUNK - 1, conj_c(desc, descs), 0.0)
                lags = jnp.where(step_of_row < CHUNK - 1, conj_c(nxt, nxts), 0.0)
                hi = lax.dot_general(bbar, lags, nt, preferred_element_type=F32,
                                     precision=lax.Precision.HIGHEST)
                lo = lax.dot_general(bbar, lag0, nt, preferred_element_type=F32,
                                     precision=lax.Precision.HIGHEST)
                taps_hi = hi
                taps_lo = lo if taps_lo is None else taps_lo + lo
            else:
                state_in[gi, d] = outer(asc, ascs, bbar, bbar_s)
                rev, revs = power(CHUNK - k_idx)
                state_out[gi, d] = conj_c(rev, revs)
                lo = lax.dot_general(bbar, conj_c(desc, descs), nt, preferred_element_type=F32,
                                     precision=lax.Precision.HIGHEST)
                taps_lo = lo if taps_lo is None else taps_lo + lo

            sr, sis = power(jnp.full((1, STATE_LANES), CHUNK, jnp.int32))
            chain = []
            for k in range(n_steps):
                chain.append((sr, sis))
                si = sis * sign_lo
                sr, sis = sr * sr - si * si, 2.0 * sr * si * sign_lo
            mults[gi, d] = chain
            r_idx = lax.broadcasted_iota(jnp.int32, (SUBLANES, STATE_LANES), 0)
            tables[gi, d] = power(CHUNK * (r_idx + 1) if d == 0 else CHUNK * (SUBLANES - r_idx))

        qq = lax.broadcasted_iota(jnp.int32, (SSM_GROUP, CHUNK_LANES), 0)
        ll = lax.broadcasted_iota(jnp.int32, (SSM_GROUP, CHUNK_LANES), 1)
        taps_lo = taps_lo + jnp.where(ll == qq + (CHUNK - 1) * SSM_GROUP, d_ref[gi], 0.0)
        taps = jnp.concatenate([taps_lo, taps_hi], axis=1)
        for j in range(CHUNK):
            off = (CHUNK - 1 - j) * SSM_GROUP
            m_o[gi, j * SSM_GROUP:(j + 1) * SSM_GROUP, :] = taps[:, off:off + CHUNK_LANES].astype(BF16)

    def swap(x):
        return pltpu.roll(x, SSM_STATE, 1)

    mult_o[...] = jnp.zeros(mult_o.shape, F32)
    wide_low = lax.broadcasted_iota(jnp.int32, (CHUNK_LANES, STATE_LANES), 1) < SSM_STATE
    zeros_c = jnp.zeros((SSM_STATE, CHUNK_LANES), F32)
    for d in range(2):
        s0, s1 = state_in[0, d], state_in[1, d]
        re_blk = jnp.concatenate([jnp.where(wide_low, s0, 0.0), jnp.where(wide_low, 0.0, swap(s1))], axis=0)
        im_blk = jnp.concatenate([jnp.where(wide_low, swap(s0), 0.0), jnp.where(wide_low, 0.0, s1)], axis=0)
        win_o[0, :, (2 * d) * LANES:(2 * d + 1) * LANES] = re_blk.astype(BF16)
        win_o[0, :, (2 * d + 1) * LANES:(2 * d + 2) * LANES] = im_blk.astype(BF16)
        t0, t1 = state_out[0, d].T, state_out[1, d].T
        for part in range(2):
            rows = slice(part * SSM_STATE, (part + 1) * SSM_STATE)
            blk = jnp.concatenate([jnp.concatenate([t0[rows], zeros_c], axis=1),
                                   jnp.concatenate([zeros_c, t1[rows]], axis=1)], axis=0)
            cout_o[0, (2 * d + part) * LANES:(2 * d + part + 1) * LANES, :] = blk.astype(BF16)
        for k in range(n_steps):
            (sr0, sis0), (sr1, sis1) = mults[0, d][k], mults[1, d][k]
            mult_o[0, k:k + 1, (2 * d) * LANES:(2 * d + 1) * LANES] = jnp.where(low, sr0, sr1)
            mult_o[0, k:k + 1, (2 * d + 1) * LANES:(2 * d + 2) * LANES] = jnp.where(low, -sis0, sis1)
        (tr0, tis0), (tr1, tis1) = tables[0, d], tables[1, d]
        mult_o[0, POWER_ROW0:POWER_ROW0 + SUBLANES, (2 * d) * LANES:(2 * d + 1) * LANES] = jnp.where(low, tr0, tr1)
        mult_o[0, POWER_ROW0:POWER_ROW0 + SUBLANES, (2 * d + 1) * LANES:(2 * d + 2) * LANES] = jnp.where(
            low, -tis0, tis1)


def _ssm_prep(lam_re, lam_im, log_dt, b_re, b_im, c_re, c_im, d_skip, n_steps):
    g = N_SSM_GROUPS
    dup = lambda a: jnp.concatenate([a, a], axis=-1)
    lre = dup(jnp.swapaxes(lam_re, 0, 1))
    lim = dup(jnp.swapaxes(lam_im, 0, 1))
    ldt = jnp.broadcast_to(jnp.swapaxes(log_dt, 0, 1)[..., None], (g, 2, STATE_LANES))
    btr = jnp.transpose(b_re, (1, 0, 3, 2))
    bti = jnp.transpose(b_im, (1, 0, 3, 2))
    bt, bts = jnp.concatenate([btr, bti], -1), jnp.concatenate([bti, btr], -1)
    cr, ci = jnp.swapaxes(c_re, 0, 1), jnp.swapaxes(c_im, 0, 1)
    cm, cms = jnp.concatenate([cr, ci], -1), jnp.concatenate([ci, cr], -1)
    dsk = jnp.pad(d_skip, ((0, 0), (CHUNK_LANES - SSM_GROUP, 0))).reshape(g, 1, CHUNK_LANES)
    n_pad = POWER_ROW0 + SUBLANES
    vec = pl.BlockSpec((2, 2, STATE_LANES), lambda i: (i, 0, 0))
    mat = pl.BlockSpec((2, 2, SSM_GROUP, STATE_LANES), lambda i: (i, 0, 0, 0))
    pair_sq = pl.BlockSpec((1, 2 * CHUNK_LANES, 4 * LANES), lambda i: (i, 0, 0))
    return pl.pallas_call(
        functools.partial(_ssm_prep_kernel, n_steps=n_steps),
        grid=(g // 2,),
        in_specs=[vec, vec, vec, mat, mat, mat, mat,
                  pl.BlockSpec((2, 1, CHUNK_LANES), lambda i: (i, 0, 0))],
        out_specs=[pair_sq, pl.BlockSpec((2, CHUNK_LANES, CHUNK_LANES), lambda i: (i, 0, 0)), pair_sq,
                   pl.BlockSpec((1, n_pad, 4 * LANES), lambda i: (i, 0, 0))],
        out_shape=[jax.ShapeDtypeStruct((g // 2, 2 * CHUNK_LANES, 4 * LANES), BF16),
                   jax.ShapeDtypeStruct((g, CHUNK_LANES, CHUNK_LANES), BF16),
                   jax.ShapeDtypeStruct((g // 2, 4 * LANES, 2 * CHUNK_LANES), BF16),
                   jax.ShapeDtypeStruct((g // 2, n_pad, 4 * LANES), F32)],
        compiler_params=_cparams("parallel"),
        name="ssm_prep",
    )(lre, lim, ldt, bt, bts, cm, cms, dsk)


def _complex_step(re, im, sr, si, ar, ai):
    return re + ar * sr - ai * si, im + ar * si + ai * sr


def _ssm_kernel(u_ref, win_ref, m_ref, cout_ref, mult_ref, y_o, loc_scr, car_scr, *, n_ctx_rows, n_steps):
    n_pairs = win_ref.shape[0]
    rows = u_ref.shape[2]
    n_lat_rows = rows - n_ctx_rows
    n_blk = rows // SUBLANES
    n_blk_pad = -(-n_blk // SUBLANES) * SUBLANES
    in_blk_row = lax.broadcasted_iota(jnp.int32, (SUBLANES, LANES), 0)
    blk = lax.broadcasted_iota(jnp.int32, (n_blk_pad, LANES), 0)
    tile = lambda t: jnp.broadcast_to(t[None], (n_blk, SUBLANES, LANES)).reshape(rows, LANES)
    local_steps = SUBLANES.bit_length() - 1
    to_rev = lambda a: jnp.concatenate([a[n_ctx_rows:], a[:n_ctx_rows]], axis=0)

    def shifted(a, s, idx, n, down):
        if down:
            return jnp.where(idx >= s, pltpu.roll(a, s, 0), 0.0)
        return jnp.where(idx < n - s, pltpu.roll(a, a.shape[0] - s, 0), 0.0)

    def scan(pp, x):
        def mult(k0, k1, d):
            return (mult_ref[pp, k0:k1, 2 * d * LANES:(2 * d + 1) * LANES],
                    mult_ref[pp, k0:k1, (2 * d + 1) * LANES:(2 * d + 2) * LANES])

        state = [[x[:, :LANES], x[:, LANES:2 * LANES]],
                 [to_rev(x[:, 2 * LANES:3 * LANES]), to_rev(x[:, 3 * LANES:])]]
        for d, down in ((0, True), (1, False)):
            re, im = state[d]
            for k in range(local_steps):
                s = 1 << k
                keep = in_blk_row >= s if down else in_blk_row < SUBLANES - s
                ar, ai = (tile(jnp.where(keep, m, 0.0)) for m in mult(k, k + 1, d))
                shift = s if down else rows - s
                re, im = _complex_step(re, im, pltpu.roll(re, shift, 0), pltpu.roll(im, shift, 0), ar, ai)
            loc_scr[pp, 2 * d], loc_scr[pp, 2 * d + 1] = re, im
            state[d] = [re, im]

            end_row = SUBLANES - 1 if down else 0
            pad = jnp.zeros((n_blk_pad - n_blk, LANES), F32)
            ere = jnp.concatenate([loc_scr[pp, 2 * d, pl.ds(end_row, n_blk, stride=SUBLANES), :], pad], axis=0)
            eim = jnp.concatenate([loc_scr[pp, 2 * d + 1, pl.ds(end_row, n_blk, stride=SUBLANES), :], pad],
                                  axis=0)
            for k in range(local_steps, n_steps):
                ar, ai = mult(k, k + 1, d)
                s = 1 << (k - local_steps)
                ere, eim = _complex_step(ere, eim, shifted(ere, s, blk, n_blk_pad, down),
                                         shifted(eim, s, blk, n_blk_pad, down), ar, ai)
            for c, ends in enumerate((ere, eim)):
                entering = shifted(ends, 1, blk, n_blk_pad, down)[:n_blk]
                for r in range(SUBLANES):
                    car_scr[pp, 2 * d + c, pl.ds(r, n_blk, stride=SUBLANES), :] = entering

        for d in range(2):
            tre, tim = mult(POWER_ROW0, POWER_ROW0 + SUBLANES, d)
            re, im = state[d]
            state[d] = list(_complex_step(re, im, car_scr[pp, 2 * d], car_scr[pp, 2 * d + 1],
                                          tile(tre), tile(tim)))
        return state

    xs = [jnp.dot(jnp.concatenate([u_ref[0, 2 * pp], u_ref[0, 2 * pp + 1]], axis=1), win_ref[pp],
                  preferred_element_type=F32) for pp in range(n_pairs)]
    prev = lambda a: pltpu.roll(a, 1, 0)[n_ctx_rows:]
    nxt = lambda a: pltpu.roll(a, rows - 1, 0)[:n_lat_rows]
    for pp in range(n_pairs):
        state = scan(pp, xs[pp])
        st = jnp.concatenate([prev(state[0][0]), prev(state[0][1]), nxt(state[1][0]), nxt(state[1][1])],
                             axis=1).astype(BF16)
        y = jnp.dot(st, cout_ref[pp], preferred_element_type=F32)
        for gi in range(2):
            g = 2 * pp + gi
            y_o[0, g] = (y[:, gi * CHUNK_LANES:(gi + 1) * CHUNK_LANES]
                         + jnp.dot(u_ref[0, g, n_ctx_rows:, :], m_ref[g],
                                   preferred_element_type=F32)).astype(BF16)


def _ssm(u_cf, win_pair, m, cout_pair, mult, n_ctx_rows, n_steps):
    bsz, g, rows, _ = u_cf.shape
    pairs = SSM_PAIRS_PER_STEP
    assert rows % SUBLANES == 0 and n_ctx_rows % 16 == 0 and g % (2 * pairs) == 0
    pair_spec = lambda a: pl.BlockSpec((pairs,) + a.shape[1:], lambda b, i: (i, 0, 0))
    return pl.pallas_call(
        functools.partial(_ssm_kernel, n_ctx_rows=n_ctx_rows, n_steps=n_steps),
        grid=(bsz, g // (2 * pairs)),
        in_specs=[pl.BlockSpec((1, 2 * pairs, rows, CHUNK_LANES), lambda b, i: (b, i, 0, 0)),
                  pair_spec(win_pair),
                  pl.BlockSpec((2 * pairs, CHUNK_LANES, CHUNK_LANES), lambda b, i: (i, 0, 0)),
                  pair_spec(cout_pair), pair_spec(mult)],
        out_specs=pl.BlockSpec((1, 2 * pairs, rows - n_ctx_rows, CHUNK_LANES), lambda b, i: (b, i, 0, 0)),
        out_shape=jax.ShapeDtypeStruct((bsz, g, rows - n_ctx_rows, CHUNK_LANES), BF16),
        scratch_shapes=[pltpu.VMEM((pairs, 4, rows, LANES), F32), pltpu.VMEM((pairs, 4, rows, LANES), F32)],
        compiler_params=_cparams("parallel", "parallel"),
        name="ssm",
    )(u_cf, win_pair, m, cout_pair, mult)


def _final_kernel(x_ref, at_ref, ga_ref, y_ref, gs_ref, gate_ref, wglu_ref, bglu_ref, wout_ref,
                  pg_ref, o_ref, y_scr):
    tm = x_ref.shape[1]
    sub = tm // FINAL_SUBTILES
    gain = gate_ref[0] * pg_ref[...]
    def finish(h, out):
        rows = slice(h * sub, (h + 1) * sub)
        ms = jnp.mean(out * out, axis=-1, keepdims=True)
        o_ref[0, rows] = x_ref[0, rows] + out * lax.rsqrt(ms + NORM_EPS) * gain

    pending = None
    for h in range(FINAL_SUBTILES):
        rows = slice(h * sub, (h + 1) * sub)
        _from_chunk_lanes(y_ref, y_scr, h * sub // CHUNK, sub // CHUNK)
        y = jax.nn.gelu(jnp.concatenate([y_scr[col, rows] for col in range(SSM_WIDTH // LANES)], axis=1))
        t = jnp.dot(y.astype(BF16), wglu_ref[...], preferred_element_type=F32) + bglu_ref[...]
        s = (y * jax.nn.sigmoid(t) * gs_ref[0, rows].astype(F32)).astype(BF16)
        a = at_ref[0, rows] * ga_ref[0, rows]
        out = jnp.dot(jnp.concatenate([a, s], axis=1), wout_ref[...], preferred_element_type=F32)
        if pending is not None:
            finish(*pending)
        pending = (h, out)
    finish(*pending)


def _final(x, attn, ga, y_cf, gs, gate, wglu_bf, b_glu, wout_bf, post_g, tm):
    bsz, n, d = x.shape
    assert n % tm == 0 and tm % (CHUNK * 16) == 0
    return pl.pallas_call(
        _final_kernel,
        grid=(bsz, n // tm),
        in_specs=[_row_spec(tm, d), _row_spec(tm, ATTN_WIDTH), _row_spec(tm, ATTN_WIDTH),
                  pl.BlockSpec((1, N_SSM_GROUPS, tm // CHUNK, CHUNK_LANES), lambda b, i: (b, 0, i, 0)),
                  _row_spec(tm, SSM_WIDTH), _batch_vec_spec(d),
                  _const_spec((SSM_WIDTH, SSM_WIDTH)), _const_spec((1, SSM_WIDTH)),
                  _const_spec((d, d)), _const_spec((1, d))],
        out_specs=_row_spec(tm, d),
        out_shape=jax.ShapeDtypeStruct((bsz, n, d), F32),
        scratch_shapes=[pltpu.VMEM((SSM_WIDTH // LANES, tm, LANES), F32)],
        compiler_params=_cparams("parallel", "parallel"),
        name="final",
    )(x, attn, ga, y_cf, gs, gate, wglu_bf, b_glu, wout_bf, post_g)


def _rope_tables(n_ctx, n_lat):
    t = jnp.arange(n_lat, dtype=jnp.int32)
    row_pos = (t // GRID_W).astype(F32)
    col_pos = (t % GRID_W).astype(F32)
    inv_freq = ROPE_THETA ** (-jnp.arange(ROPE_FREQS, dtype=F32) / ROPE_FREQS)
    ang_r = row_pos[:, None] * inv_freq
    ang_c = col_pos[:, None] * inv_freq
    cos = jnp.concatenate([jnp.cos(ang_r)] * 2 + [jnp.cos(ang_c)] * 2, axis=1)
    sin = jnp.concatenate([-jnp.sin(ang_r), jnp.sin(ang_r), -jnp.sin(ang_c), jnp.sin(ang_c)], axis=1)
    cos = jnp.concatenate([jnp.ones((n_ctx, HEAD_DIM), F32), cos], axis=0)
    sin = jnp.concatenate([jnp.zeros((n_ctx, HEAD_DIM), F32), sin], axis=0)
    return jnp.tile(cos, (1, LANES // HEAD_DIM)), jnp.tile(sin, (1, LANES // HEAD_DIM))


def _layer(x, ctx, c, c_ctx, w_ada, b_ada, pre_g, post_g, w_in, q_g, k_g, lam_re, lam_im, log_dt,
           b_re, b_im, c_re, c_im, d_skip, w_glu, b_glu, w_out):
    bsz, n_lat, d = x.shape
    n_ctx = ctx.shape[1]
    assert n_lat % GRID_W == 0

    cvecs = jnp.zeros((8, d), F32).at[:bsz].set(c).at[bsz].set(c_ctx)
    mod = _adaln(cvecs, w_ada, b_ada)
    shift, scale, gate = (mod[:, i * d:(i + 1) * d] for i in range(3))
    lat = lambda a: a[:bsz].reshape(bsz, 1, d)
    cvec = lambda a: a[bsz].reshape(1, 1, d)

    head = jnp.arange(ATTN_WIDTH) // HEAD_DIM
    bd = (head[:, None] == head[None, :]).astype(BF16) * (1.0 / HEAD_DIM)
    qg = jnp.tile(q_g, N_Q_HEADS).reshape(1, ATTN_WIDTH)
    kg = jnp.tile(k_g, N_KV_HEADS).reshape(1, KV_WIDTH)
    cos, sin = _rope_tables(n_ctx, n_lat)

    qt, k_all, vt_aug, ga, u_cf, gs = _inproj(x, ctx, lat(shift), lat(scale), cvec(shift), cvec(scale),
                                              pre_g.reshape(1, d), w_in.astype(BF16), qg, kg, cos, sin,
                                              bd, tm=256)
    attn = _attention(qt, k_all, vt_aug, tq=256, tk=8448, sub=256, lookahead=5, heads_per_pass=8)

    rows = (n_ctx + n_lat) // CHUNK
    n_steps = _scan_steps(rows)
    win_pair, m_op, cout_pair, mult = _ssm_prep(lam_re, lam_im, log_dt, b_re, b_im, c_re, c_im, d_skip,
                                                n_steps)
    y_cf = _ssm(u_cf, win_pair, m_op, cout_pair, mult, n_ctx // CHUNK, n_steps)

    return _final(x, attn, ga, y_cf, gs, lat(gate), w_glu.astype(BF16), b_glu.reshape(1, SSM_WIDTH),
                  w_out.astype(BF16), post_g.reshape(1, d), tm=1024)


def kernel(x, c, ctx, c_ctx, w_ada, b_ada, pre_norm, post_norm, w_in, q_norm, k_norm, ssm_lam_re,
           ssm_lam_im, ssm_log_dt, ssm_b_re, ssm_b_im, ssm_c_re, ssm_c_im, ssm_d, w_glu, b_glu, w_out):
    depth = w_ada.shape[0]
    assert depth == 1, "context stream update between layers is not implemented"
    return _layer(x, ctx, c, c_ctx, w_ada[0], b_ada[0], pre_norm[0], post_norm[0], w_in[0], q_norm[0],
                  k_norm[0], ssm_lam_re[0], ssm_lam_im[0], ssm_log_dt[0], ssm_b_re[0], ssm_b_im[0],
                  ssm_c_re[0], ssm_c_im[0], ssm_d[0], w_glu[0], b_glu[0], w_out[0])
```
